```python
import math
import jax, jax.numpy as jnp
from jax import lax
import numpy as np

D_MODEL = 1024
BATCH = 4
SEQ = 4096
DEPTH = 1

HEAD_DIM = 64
FOX_HEADS = 8
FOX_WIDTH = FOX_HEADS * HEAD_DIM
DIFF_HEADS = 4
DIFF_QK_DIM = HEAD_DIM
DIFF_QK_WIDTH = DIFF_HEADS * 2 * DIFF_QK_DIM
DIFF_V_DIM = 2 * HEAD_DIM
DIFF_WIDTH = DIFF_HEADS * DIFF_V_DIM
MEM_HEADS = 4
MEM_HEAD_DIM = 128
MEM_WIDTH = MEM_HEADS * MEM_HEAD_DIM
N_MEM = 256
N_BRANCH = 3
Q_BLOCK = 128
N_GROUPS = 4
EXPERTS_PER_GROUP = 8
N_EXPERTS = N_GROUPS * EXPERTS_PER_GROUP
TOP_K_INNER = 2
D_EXPERT = 256
LN_EPS = 1e-5
FORGET_BIAS_OFFSET = 2.0
DEEPNORM_ALPHA = (2.0 * DEPTH) ** 0.25
DEEPNORM_BETA = (8.0 * DEPTH) ** -0.25
IN_SPLITS = (FOX_WIDTH, FOX_WIDTH, FOX_WIDTH,
             DIFF_QK_WIDTH, DIFF_QK_WIDTH, DIFF_WIDTH,
             MEM_WIDTH,
             N_BRANCH * D_MODEL,
             FOX_HEADS)
IN_COLS = sum(IN_SPLITS)
IN_OFFSETS = [int(o) for o in np.cumsum(IN_SPLITS)[:-1]]

kernel_name = "hybrid_fox_diff_mem_hmoe_deepnorm"


def layer_norm(x, g, b):
    xf = x.astype(jnp.float32)
    mu = jnp.mean(xf, axis=-1, keepdims=True)
    var = jnp.mean(jnp.square(xf - mu), axis=-1, keepdims=True)
    return ((xf - mu) * lax.rsqrt(var + LN_EPS) * g + b).astype(x.dtype)


def rms_norm(x, g):
    xf = x.astype(jnp.float32)
    return (xf * lax.rsqrt(jnp.mean(jnp.square(xf), axis=-1, keepdims=True) + LN_EPS) * g).astype(x.dtype)


def _to_blocks(a):
    b, h, s = a.shape[:3]
    a = a.reshape((b, h, s // Q_BLOCK, Q_BLOCK) + a.shape[3:])
    return jnp.moveaxis(a, 2, 0)


def _from_blocks(o):
    nb, b, h, blk, dv = o.shape
    o = jnp.moveaxis(o, 0, 2).reshape(b, h, nb * blk, dv)
    return jnp.transpose(o, (0, 2, 1, 3))


def fox_attention(q, k, v, log_f):
    s_len = q.shape[2]
    c = jnp.cumsum(log_f, axis=-1)
    pos = jnp.arange(s_len)
    scale = HEAD_DIM ** -0.5

    def block(args):
        qb, cb, pb = args
        logits = jnp.einsum('bhqd,bhkd->bhqk', qb, k).astype(jnp.float32) * scale
        logits = logits + (cb[..., None] - c[:, :, None, :])
        causal = pb[:, None] >= pos[None, :]
        p = jax.nn.softmax(jnp.where(causal, logits, -jnp.inf), axis=-1)
        return jnp.einsum('bhqk,bhkd->bhqd', p.astype(v.dtype), v)

    out = lax.map(block, (_to_blocks(q), _to_blocks(c), pos.reshape(-1, Q_BLOCK)))
    return _from_blocks(out)


def diff_attention(q1, q2, k1, k2, v, lam, slopes):
    s_len = q1.shape[2]
    pos = jnp.arange(s_len)
    scale = DIFF_QK_DIM ** -0.5

    def block(args):
        q1b, q2b, pb = args
        dist = (pb[:, None] - pos[None, :]).astype(jnp.float32)
        bias = -slopes[:, None, None] * dist
        causal = dist >= 0

        def probs(qb, kk):
            lg = jnp.einsum('bhqd,bhkd->bhqk', qb, kk).astype(jnp.float32) * scale + bias
            return jax.nn.softmax(jnp.where(causal, lg, -jnp.inf), axis=-1)

        p = probs(q1b, k1) - lam * probs(q2b, k2)
        return jnp.einsum('bhqk,bhkd->bhqd', p.astype(v.dtype), v)

    out = lax.map(block, (_to_blocks(q1), _to_blocks(q2), pos.reshape(-1, Q_BLOCK)))
    return _from_blocks(out)


def memory_attention(q, km, vm):
    logits = jnp.einsum('bhqd,bhmd->bhqm', q, km).astype(jnp.float32) * (MEM_HEAD_DIM ** -0.5)
    p = jax.nn.softmax(logits, axis=-1)
    out = jnp.einsum('bhqm,bhmd->bhqd', p.astype(vm.dtype), vm)
    return jnp.transpose(out, (0, 2, 1, 3))


def hier_moe(x, w_rg, b_rg, w_re, b_re, w_gate, w_up, w_down):
    b, s, d = x.shape
    xt = x.reshape(b * s, d)
    g_prob = jax.nn.softmax((xt @ w_rg + b_rg).astype(jnp.float32), axis=-1)
    g_w, g_idx = lax.top_k(g_prob, 1)
    e_logits = (xt @ w_re + b_re).astype(jnp.float32).reshape(-1, N_GROUPS, EXPERTS_PER_GROUP)
    e_logits = jnp.take_along_axis(e_logits, g_idx[:, :, None], axis=1)[:, 0]
    e_w, e_idx = lax.top_k(jax.nn.softmax(e_logits, axis=-1), TOP_K_INNER)
    e_w = e_w / jnp.sum(e_w, axis=-1, keepdims=True)
    w = g_w * e_w
    expert_id = g_idx * EXPERTS_PER_GROUP + e_idx
    combine = jnp.sum(jax.nn.one_hot(expert_id, N_EXPERTS, dtype=jnp.float32) * w[..., None], axis=1)
    acc = jnp.zeros_like(xt)
    for e in range(N_EXPERTS):
        y = (jax.nn.silu(xt @ w_gate[e]) * (xt @ w_up[e])) @ w_down[e]
        acc = acc + combine[:, e:e + 1].astype(xt.dtype) * y
    return acc.reshape(b, s, d)


def setup_inputs(seed: int = 0) -> dict:
    key = jax.random.key(seed)
    ks = jax.random.split(key, 24)
    L, D = DEPTH, D_MODEL
    f32 = jnp.float32

    def nrm(k, shape, scale):
        return jax.random.normal(k, shape, f32) * scale

    beta = DEEPNORM_BETA
    col_scale = jnp.concatenate([
        jnp.ones((2 * FOX_WIDTH,), f32), jnp.full((FOX_WIDTH,), beta, f32),
        jnp.ones((2 * DIFF_QK_WIDTH,), f32), jnp.full((DIFF_WIDTH,), beta, f32),
        jnp.ones((MEM_WIDTH + N_BRANCH * D + FOX_HEADS,), f32)])
    mem_scale = jnp.concatenate([jnp.ones((MEM_WIDTH,), f32), jnp.full((MEM_WIDTH,), beta, f32)])
    return {
        "x": nrm(ks[0], (BATCH, SEQ, D), 1.0),
        "mem": nrm(ks[1], (BATCH, N_MEM, D), 1.0),
        "w_in": nrm(ks[2], (L, D, IN_COLS), D ** -0.5) * col_scale,
        "b_forget": FORGET_BIAS_OFFSET + nrm(ks[3], (L, FOX_HEADS), 0.5),
        "b_gates": nrm(ks[4], (L, N_BRANCH * D), 0.02),
        "lambda_q1": nrm(ks[5], (L, DIFF_QK_DIM), 0.1),
        "lambda_k1": nrm(ks[6], (L, DIFF_QK_DIM), 0.1),
        "lambda_q2": nrm(ks[7], (L, DIFF_QK_DIM), 0.1),
        "lambda_k2": nrm(ks[8], (L, DIFF_QK_DIM), 0.1),
        "diff_subln_g": 1.0 + nrm(ks[9], (L, DIFF_V_DIM), 0.02),
        "w_mem_kv": nrm(ks[10], (L, D, 2 * MEM_WIDTH), D ** -0.5) * mem_scale,
        "w_branch_fox": nrm(ks[11], (L, FOX_WIDTH, D), beta * FOX_WIDTH ** -0.5),
        "w_branch_diff": nrm(ks[12], (L, DIFF_WIDTH, D), beta * DIFF_WIDTH ** -0.5),
        "w_branch_mem": nrm(ks[13], (L, MEM_WIDTH, D), beta * MEM_WIDTH ** -0.5),
        "w_out": nrm(ks[14], (L, D, D), beta * D ** -0.5),
        "ln1_g": 1.0 + nrm(ks[15], (L, D), 0.02),
        "ln1_b": nrm(ks[16], (L, D), 0.02),
        "w_router_group": nrm(ks[17], (L, D, N_GROUPS), D ** -0.5),
        "b_router_group": nrm(ks[18], (L, N_GROUPS), 0.01),
        "w_router_expert": nrm(ks[19], (L, D, N_EXPERTS), D ** -0.5),
        "b_router_expert": nrm(ks[20], (L, N_EXPERTS), 0.01),
        "w_expert_gate": nrm(ks[21], (L, N_EXPERTS, D, D_EXPERT), beta * D ** -0.5),
        "w_expert_up": nrm(ks[22], (L, N_EXPERTS, D, D_EXPERT), beta * D ** -0.5),
        "w_expert_down": nrm(ks[23], (L, N_EXPERTS, D_EXPERT, D), beta * D_EXPERT ** -0.5),
        "ln2_g": 1.0 + nrm(jax.random.fold_in(key, 101), (L, D), 0.02),
        "ln2_b": nrm(jax.random.fold_in(key, 102), (L, D), 0.02),
    }


def reference(x, mem, w_in, b_forget, b_gates, lambda_q1, lambda_k1, lambda_q2, lambda_k2,
              diff_subln_g, w_mem_kv, w_branch_fox, w_branch_diff, w_branch_mem, w_out,
              ln1_g, ln1_b, w_router_group, b_router_group, w_router_expert, b_router_expert,
              w_expert_gate, w_expert_up, w_expert_down, ln2_g, ln2_b):
    B, S, D = x.shape
    slopes = 2.0 ** (-8.0 * jnp.arange(1, DIFF_HEADS + 1, dtype=jnp.float32) / DIFF_HEADS)
    for l in range(DEPTH):
        proj = x @ w_in[l]
        fq, fk, fv, dq, dk, dv, mq, gl, fl = jnp.split(proj, IN_OFFSETS, axis=-1)

        heads = lambda t, h, d: jnp.transpose(t.reshape(B, S, h, d), (0, 2, 1, 3))
        log_f = jax.nn.log_sigmoid((fl + b_forget[l]).astype(jnp.float32))
        y_fox = fox_attention(heads(fq, FOX_HEADS, HEAD_DIM), heads(fk, FOX_HEADS, HEAD_DIM),
                              heads(fv, FOX_HEADS, HEAD_DIM), jnp.transpose(log_f, (0, 2, 1)))
        y_fox = y_fox.reshape(B, S, FOX_WIDTH)

        lam_init = 0.8 - 0.6 * math.exp(-0.3 * l)
        lam = (jnp.exp(jnp.sum(lambda_q1[l] * lambda_k1[l]).astype(jnp.float32))
               - jnp.exp(jnp.sum(lambda_q2[l] * lambda_k2[l]).astype(jnp.float32)) + lam_init)
        dq = jnp.transpose(dq.reshape(B, S, DIFF_HEADS, 2, DIFF_QK_DIM), (0, 2, 1, 3, 4))
        dk = jnp.transpose(dk.reshape(B, S, DIFF_HEADS, 2, DIFF_QK_DIM), (0, 2, 1, 3, 4))
        y_diff = diff_attention(dq[..., 0, :], dq[..., 1, :], dk[..., 0, :], dk[..., 1, :],
                                heads(dv, DIFF_HEADS, DIFF_V_DIM), lam, slopes)
        y_diff = (rms_norm(y_diff, diff_subln_g[l]) * (1.0 - lam_init)).reshape(B, S, DIFF_WIDTH)

        mkv = mem @ w_mem_kv[l]
        mk, mv = jnp.split(mkv, 2, axis=-1)
        mem_heads = lambda t: jnp.transpose(t.reshape(B, N_MEM, MEM_HEADS, MEM_HEAD_DIM), (0, 2, 1, 3))
        y_mem = memory_attention(heads(mq, MEM_HEADS, MEM_HEAD_DIM), mem_heads(mk), mem_heads(mv))
        y_mem = y_mem.reshape(B, S, MEM_WIDTH)

        gates = jax.nn.sigmoid(gl + b_gates[l]).reshape(B, S, N_BRANCH, D)
        h = (gates[:, :, 0] * (y_fox @ w_branch_fox[l])
             + gates[:, :, 1] * (y_diff @ w_branch_diff[l])
             + gates[:, :, 2] * (y_mem @ w_branch_mem[l]))
        x = layer_norm(DEEPNORM_ALPHA * x + h @ w_out[l], ln1_g[l], ln1_b[l])

        m = hier_moe(x, w_router_group[l], b_router_group[l], w_router_expert[l], b_router_expert[l],
                     w_expert_gate[l], w_expert_up[l], w_expert_down[l])
        x = layer_norm(DEEPNORM_ALPHA * x + m, ln2_g[l], ln2_b[l])
    return x
```

```python
import functools
import math

import jax
import jax.numpy as jnp
from jax import lax
from jax.experimental import pallas as pl
from jax.experimental.pallas import tpu as pltpu

F32 = jnp.float32
BF16 = jnp.bfloat16

D_MODEL = 1024
HEAD_DIM = 64
FOX_HEADS = 8
DIFF_HEADS = 4
MEM_HEADS = 4
MEM_HEAD_DIM = 128
N_MEM = 256
N_BRANCH = 3
N_GROUPS = 4
EXPERTS_PER_GROUP = 8
N_EXPERTS = N_GROUPS * EXPERTS_PER_GROUP
D_EXPERT = 256
LN_EPS = 1e-5
DEPTH = 1
DEEPNORM_ALPHA = (2.0 * DEPTH) ** 0.25
LAM_INIT = 0.8 - 0.6 * math.exp(-0.3 * 0)

LANES = 128
QKV_COLS = 3584
N_SLABS = QKV_COLS // LANES
GATE_COL0 = QKV_COLS
FL_COL0 = QKV_COLS + N_BRANCH * D_MODEL
SLAB_FOX_Q, SLAB_FOX_K, SLAB_FOX_V = 0, 4, 8
SLAB_DIFF_Q, SLAB_DIFF_K, SLAB_DIFF_V = 12, 16, 20
SLAB_MEM_Q = 24
ROUTE_LANE0 = N_GROUPS

NEG_BIG = -1e30
VMEM_LIMIT = 56 * 1024 * 1024

PROJ_TM = 1024
PROJ_TN = 512
ATT_T = 256
MEM_TQ = 1024
MERGE_TM = 512
MOE_TM = 1024


def _cparams(sem):
    return pltpu.CompilerParams(dimension_semantics=sem, vmem_limit_bytes=VMEM_LIMIT)


def _proj_kernel(x_ref, w_ref, wfl_ref, o_ref, fl_ref, xb_ref):
    j = pl.program_id(1)

    @pl.when(j == 0)
    def _():
        xb = x_ref[...].astype(BF16)
        xb_ref[...] = xb
        fl_ref[...] = lax.dot_general(wfl_ref[...], xb, (((1,), (1,)), ((), ())),
                                      preferred_element_type=F32)

    acc = jnp.dot(xb_ref[...], w_ref[...], preferred_element_type=F32)
    for c in range(PROJ_TN // LANES):
        o_ref[c] = acc[:, c * LANES:(c + 1) * LANES].astype(BF16)


def _proj(x2, w_qkv, w_fl_t):
    t = x2.shape[0]
    spc = PROJ_TN // LANES
    return pl.pallas_call(
        _proj_kernel,
        grid=(t // PROJ_TM, QKV_COLS // PROJ_TN),
        in_specs=[
            pl.BlockSpec((PROJ_TM, D_MODEL), lambda i, j: (i, 0)),
            pl.BlockSpec((D_MODEL, PROJ_TN), lambda i, j: (0, j)),
            pl.BlockSpec((16, D_MODEL), lambda i, j: (0, 0)),
        ],
        out_specs=[
            pl.BlockSpec((spc, PROJ_TM, LANES), lambda i, j: (j, i, 0)),
            pl.BlockSpec((16, PROJ_TM), lambda i, j: (0, i)),
        ],
        out_shape=[
            jax.ShapeDtypeStruct((N_SLABS, t, LANES), BF16),
            jax.ShapeDtypeStruct((16, t), F32),
        ],
        scratch_shapes=[pltpu.VMEM((PROJ_TM, D_MODEL), BF16)],
        compiler_params=_cparams(("parallel", "arbitrary")),
        name="proj",
    )(x2, w_qkv, w_fl_t)


def _fscan_kernel(fl_ref, b_ref, c_ref):
    s = fl_ref.shape[1]
    z = fl_ref[0:FOX_HEADS, :] + b_ref[...]
    lf = jnp.minimum(z, 0.0) - jnp.log(1.0 + jnp.exp(-jnp.abs(z)))
    lane = lax.broadcasted_iota(jnp.int32, lf.shape, 1)
    sh = 1
    while sh < s:
        r = pltpu.roll(lf, sh, axis=1)
        lf = lf + jnp.where(lane >= sh, r, 0.0)
        sh *= 2
    c_ref[...] = lf


def _fscan(fl_t, b_forget_col, batch, seq):
    t = fl_t.shape[1]
    return pl.pallas_call(
        _fscan_kernel,
        grid=(batch,),
        in_specs=[
            pl.BlockSpec((16, seq), lambda b: (0, b)),
            pl.BlockSpec((FOX_HEADS, 1), lambda b: (0, 0)),
        ],
        out_specs=pl.BlockSpec((FOX_HEADS, seq), lambda b: (0, b)),
        out_shape=jax.ShapeDtypeStruct((FOX_HEADS, t), F32),
        compiler_params=_cparams(("parallel",)),
        name="fscan",
    )(fl_t, b_forget_col)


def _flash_two_chains(qi, tq, qh_scr, k_ref, v_ref, bias_fn, m_scr, l_scr, acc_scr):
    m_scr[...] = jnp.full(m_scr.shape, NEG_BIG, F32)
    l_scr[...] = jnp.zeros(l_scr.shape, F32)
    acc_scr[...] = jnp.zeros(acc_scr.shape, F32)

    def step(j, masked):
        start = pl.multiple_of(j * tq, tq)
        k = k_ref[pl.ds(start, tq), :]
        v = v_ref[pl.ds(start, tq), :]
        biases = bias_fn(start)
        if masked:
            row = lax.broadcasted_iota(jnp.int32, (tq, tq), 0)
            col = lax.broadcasted_iota(jnp.int32, (tq, tq), 1)
            keep = col <= row
        for hh in range(2):
            s = lax.dot_general(qh_scr[hh], k, (((1,), (1,)), ((), ())),
                                preferred_element_type=F32)
            s = s + biases[hh]
            if masked:
                s = jnp.where(keep, s, NEG_BIG)
            m_prev = m_scr[hh]
            m_new = jnp.maximum(m_prev, jnp.max(s, axis=1, keepdims=True))
            alpha = jnp.exp(m_prev - m_new)
            p = jnp.exp(s - m_new)
            l_scr[hh] = alpha * l_scr[hh] + jnp.sum(p, axis=1, keepdims=True)
            acc_scr[hh] = alpha * acc_scr[hh] + jnp.dot(p.astype(BF16), v,
                                                        preferred_element_type=F32)
            m_scr[hh] = m_new

    def body(j, carry):
        step(j, False)
        return carry

    lax.fori_loop(0, qi, body, 0)
    step(qi, True)


def _split_q(q_ref, qh_scr):
    q = q_ref[...].astype(F32) * (HEAD_DIM ** -0.5)
    lane = lax.broadcasted_iota(jnp.int32, q.shape, 1)
    qh_scr[0] = jnp.where(lane < HEAD_DIM, q, 0.0).astype(BF16)
    qh_scr[1] = jnp.where(lane >= HEAD_DIM, q, 0.0).astype(BF16)


def _attn_scratch(tq):
    return [
        pltpu.VMEM((2, tq, LANES), BF16),
        pltpu.VMEM((2, tq, 1), F32),
        pltpu.VMEM((2, tq, 1), F32),
        pltpu.VMEM((2, tq, LANES), F32),
    ]


def _fox_kernel(q_ref, k_ref, v_ref, c_ref, o_ref, qh_scr, m_scr, l_scr, acc_scr):
    tq = q_ref.shape[0]
    qi = pl.program_id(2)
    _split_q(q_ref, qh_scr)
    q0 = pl.multiple_of(qi * tq, tq)
    c_tile0 = c_ref[:, pl.ds(q0, tq)][:, 0:1]

    def bias_fn(start):
        b = c_tile0 - c_ref[:, pl.ds(start, tq)]
        return [b[0:1, :], b[1:2, :]]

    _flash_two_chains(qi, tq, qh_scr, k_ref, v_ref, bias_fn, m_scr, l_scr, acc_scr)
    o0 = acc_scr[0] / l_scr[0]
    o1 = acc_scr[1] / l_scr[1]
    lane = lax.broadcasted_iota(jnp.int32, o0.shape, 1)
    o_ref[...] = jnp.where(lane < HEAD_DIM, o0, o1).astype(o_ref.dtype)


def _fox(qkv4, c4, batch, seq):
    tq = ATT_T
    npairs = FOX_HEADS // 2
    return pl.pallas_call(
        _fox_kernel,
        grid=(batch, npairs, seq // tq),
        in_specs=[
            pl.BlockSpec((None, None, tq, LANES), lambda b, p, i: (SLAB_FOX_Q + p, b, i, 0)),
            pl.BlockSpec((None, None, seq, LANES), lambda b, p, i: (SLAB_FOX_K + p, b, 0, 0)),
            pl.BlockSpec((None, None, seq, LANES), lambda b, p, i: (SLAB_FOX_V + p, b, 0, 0)),
            pl.BlockSpec((None, 2, seq), lambda b, p, i: (p, 0, b)),
        ],
        out_specs=pl.BlockSpec((None, tq, LANES), lambda b, p, i: (b, i, p)),
        out_shape=jax.ShapeDtypeStruct((batch, seq, npairs * LANES), BF16),
        scratch_shapes=_attn_scratch(tq),
        compiler_params=_cparams(("parallel", "parallel", "arbitrary")),
        name="fox",
    )(qkv4, qkv4, qkv4, c4)


def _diff_kernel(slopes_ref, q_ref, k_ref, v_ref, lam_ref, g_ref, o_ref, qh_scr, m_scr, l_scr, acc_scr):
    tq = q_ref.shape[0]
    qi = pl.program_id(2)
    _split_q(q_ref, qh_scr)
    q0 = qi * tq
    slope = slopes_ref[pl.program_id(1)]

    def bias_fn(start):
        col = lax.broadcasted_iota(jnp.int32, (1, tq), 1) + (start - q0)
        b = col.astype(F32) * slope
        return [b, b]

    _flash_two_chains(qi, tq, qh_scr, k_ref, v_ref, bias_fn, m_scr, l_scr, acc_scr)
    lp = lam_ref[...]
    s1 = jnp.sum(lp[0:1, :] * lp[1:2, :], axis=1, keepdims=True)
    s2 = jnp.sum(lp[2:3, :] * lp[3:4, :], axis=1, keepdims=True)
    lam = jnp.exp(s1) - jnp.exp(s2) + LAM_INIT
    o = acc_scr[0] / l_scr[0] - lam * (acc_scr[1] / l_scr[1])
    ms = jnp.mean(o * o, axis=1, keepdims=True)
    y = o * lax.rsqrt(ms + LN_EPS) * g_ref[...]
    o_ref[...] = (y * (1.0 - LAM_INIT)).astype(o_ref.dtype)


def _diff(qkv4, slopes, lam_params, subln_g, batch, seq):
    tq = ATT_T
    grid_spec = pltpu.PrefetchScalarGridSpec(
        num_scalar_prefetch=1,
        grid=(batch, DIFF_HEADS, seq // tq),
        in_specs=[
            pl.BlockSpec((None, None, tq, LANES), lambda b, h, i, s: (SLAB_DIFF_Q + h, b, i, 0)),
            pl.BlockSpec((None, None, seq, LANES), lambda b, h, i, s: (SLAB_DIFF_K + h, b, 0, 0)),
            pl.BlockSpec((None, None, seq, LANES), lambda b, h, i, s: (SLAB_DIFF_V + h, b, 0, 0)),
            pl.BlockSpec((4, HEAD_DIM), lambda b, h, i, s: (0, 0)),
            pl.BlockSpec((1, LANES), lambda b, h, i, s: (0, 0)),
        ],
        out_specs=pl.BlockSpec((None, tq, LANES), lambda b, h, i, s: (b, i, h)),
        scratch_shapes=_attn_scratch(tq),
    )
    return pl.pallas_call(
        _diff_kernel,
        grid_spec=grid_spec,
        out_shape=jax.ShapeDtypeStruct((batch, seq, DIFF_HEADS * LANES), BF16),
        compiler_params=_cparams(("parallel", "parallel", "arbitrary")),
        name="diff",
    )(slopes, qkv4, qkv4, qkv4, lam_params, subln_g)


def _memkv_kernel(mem_ref, w_ref, o_ref):
    o_ref[...] = jnp.dot(mem_ref[...].astype(BF16), w_ref[...],
                         preferred_element_type=F32).astype(o_ref.dtype)


def _memkv(mem, w_kv):
    batch = mem.shape[0]
    width = w_kv.shape[1]
    return pl.pallas_call(
        _memkv_kernel,
        grid=(batch,),
        in_specs=[
            pl.BlockSpec((None, N_MEM, D_MODEL), lambda b: (b, 0, 0)),
            pl.BlockSpec((D_MODEL, width), lambda b: (0, 0)),
        ],
        out_specs=pl.BlockSpec((None, N_MEM, width), lambda b: (b, 0, 0)),
        out_shape=jax.ShapeDtypeStruct((batch, N_MEM, width), BF16),
        compiler_params=_cparams(("parallel",)),
        name="mem_kv",
    )(mem, w_kv)


def _mem_kernel(q_ref, mk_ref, mv_ref, o_ref):
    s = lax.dot_general(q_ref[...], mk_ref[...], (((1,), (1,)), ((), ())),
                        preferred_element_type=F32) * (MEM_HEAD_DIM ** -0.5)
    m = jnp.max(s, axis=1, keepdims=True)
    p = jnp.exp(s - m)
    l = jnp.sum(p, axis=1, keepdims=True)
    o = jnp.dot(p.astype(BF16), mv_ref[...], preferred_element_type=F32)
    o_ref[...] = (o / l).astype(o_ref.dtype)


def _mem_attn(qkv4, mkv, batch, seq):
    tq = MEM_TQ
    return pl.pallas_call(
        _mem_kernel,
        grid=(batch, MEM_HEADS, seq // tq),
        in_specs=[
            pl.BlockSpec((None, None, tq, LANES), lambda b, h, i: (SLAB_MEM_Q + h, b, i, 0)),
            pl.BlockSpec((None, N_MEM, LANES), lambda b, h, i: (b, 0, h)),
            pl.BlockSpec((None, N_MEM, LANES), lambda b, h, i: (b, 0, MEM_HEADS + h)),
        ],
        out_specs=pl.BlockSpec((None, tq, LANES), lambda b, h, i: (b, i, h)),
        out_shape=jax.ShapeDtypeStruct((batch, seq, MEM_HEADS * LANES), BF16),
        compiler_params=_cparams(("parallel", "parallel", "parallel")),
        name="mem",
    )(qkv4, mkv, mkv)


def _layer_norm(z, g, b):
    mu = jnp.mean(z, axis=1, keepdims=True)
    zc = z - mu
    var = jnp.mean(zc * zc, axis=1, keepdims=True)
    return zc * lax.rsqrt(var + LN_EPS) * g + b


def _lane_max(v):
    return jnp.max(v, axis=1, keepdims=True)


def _routing_weights(logits):
    lane = lax.broadcasted_iota(jnp.int32, logits.shape, 1)
    big = jnp.int32(2 * LANES)
    is_g = lane < N_GROUPS
    gl = jnp.where(is_g, logits, NEG_BIG)
    gmax = _lane_max(gl)
    g_w = 1.0 / jnp.sum(jnp.exp(gl - gmax), axis=1, keepdims=True)
    g_idx = jnp.min(jnp.where(gl == gmax, lane, big), axis=1, keepdims=True)
    lo = ROUTE_LANE0 + g_idx * EXPERTS_PER_GROUP
    in_grp = (lane >= lo) & (lane < lo + EXPERTS_PER_GROUP)
    el = jnp.where(in_grp, logits, NEG_BIG)
    e1 = _lane_max(el)
    i1 = jnp.min(jnp.where(el == e1, lane, big), axis=1, keepdims=True)
    el2 = jnp.where(lane == i1, NEG_BIG, el)
    e2 = _lane_max(el2)
    i2 = jnp.min(jnp.where(el2 == e2, lane, big), axis=1, keepdims=True)
    r = jnp.exp(e2 - e1)
    w1 = g_w / (1.0 + r)
    w2 = g_w * r / (1.0 + r)
    return jnp.where(lane == i1, w1, 0.0) + jnp.where(lane == i2, w2, 0.0)


def _merge_kernel(x_ref, yf_ref, yd_ref, ym_ref, wg_ref, bg_ref, wbf_ref, wbd_ref, wbm_ref,
                  wo_ref, g1_ref, b1_ref, wr_ref, br_ref, x1_ref, x1b_ref, comb_ref):
    x = x_ref[...]
    xb = x.astype(BF16)
    h = None
    for i, (y_ref, wb_ref) in enumerate(((yf_ref, wbf_ref), (yd_ref, wbd_ref), (ym_ref, wbm_ref))):
        cols = slice(i * D_MODEL, (i + 1) * D_MODEL)
        gl = jnp.dot(xb, wg_ref[:, cols], preferred_element_type=F32) + bg_ref[:, cols]
        gate = 1.0 / (1.0 + jnp.exp(-gl))
        br = jnp.dot(y_ref[...], wb_ref[...], preferred_element_type=F32)
        h = gate * br if h is None else h + gate * br
    o = jnp.dot(h.astype(BF16), wo_ref[...], preferred_element_type=F32)
    x1 = _layer_norm(DEEPNORM_ALPHA * x + o, g1_ref[...], b1_ref[...])
    x1_ref[...] = x1
    x1b = x1.astype(BF16)
    x1b_ref[...] = x1b
    logits = jnp.dot(x1b, wr_ref[...], preferred_element_type=F32) + br_ref[...]
    comb_ref[...] = _routing_weights(logits)


def _const_spec(shape):
    return pl.BlockSpec(shape, lambda i: (0,) * len(shape), pipeline_mode=pl.Buffered(1))


def _merge(x2, yf, yd, ym, wg, bg, wbf, wbd, wbm, wo, g1, b1, wr, br):
    t = x2.shape[0]
    tm = MERGE_TM
    half = yf.shape[1]
    row = lambda w: pl.BlockSpec((tm, w), lambda i: (i, 0))
    return pl.pallas_call(
        _merge_kernel,
        grid=(t // tm,),
        in_specs=[
            row(D_MODEL), row(half), row(half), row(half),
            _const_spec((D_MODEL, N_BRANCH * D_MODEL)), _const_spec((1, N_BRANCH * D_MODEL)),
            _const_spec((half, D_MODEL)), _const_spec((half, D_MODEL)), _const_spec((half, D_MODEL)),
            _const_spec((D_MODEL, D_MODEL)), _const_spec((1, D_MODEL)), _const_spec((1, D_MODEL)),
            _const_spec((D_MODEL, LANES)), _const_spec((1, LANES)),
        ],
        out_specs=[row(D_MODEL), row(D_MODEL), row(LANES)],
        out_shape=[
            jax.ShapeDtypeStruct((t, D_MODEL), F32),
            jax.ShapeDtypeStruct((t, D_MODEL), BF16),
            jax.ShapeDtypeStruct((t, LANES), F32),
        ],
        compiler_params=_cparams(("parallel",)),
        name="merge",
    )(x2, yf, yd, ym, wg, bg, wbf, wbd, wbm, wo, g1, b1, wr, br)


def _moe_kernel(x1_ref, x1b_ref, comb_ref, wg_ref, wu_ref, wd_ref, g2_ref, b2_ref, o_ref, acc_ref):
    e = pl.program_id(1)

    @pl.when(e == 0)
    def _():
        acc_ref[...] = jnp.zeros(acc_ref.shape, F32)

    xb = x1b_ref[...]
    g = jnp.dot(xb, wg_ref[...], preferred_element_type=F32)
    u = jnp.dot(xb, wu_ref[...], preferred_element_type=F32)
    hcat = (g / (1.0 + jnp.exp(-g))) * u
    y = jnp.dot(hcat.astype(BF16), wd_ref[...], preferred_element_type=F32)
    comb = comb_ref[...]
    lane = lax.broadcasted_iota(jnp.int32, comb.shape, 1)
    coef = jnp.sum(jnp.where(lane == ROUTE_LANE0 + e, comb, 0.0), axis=1, keepdims=True)
    acc_ref[...] += coef * y

    @pl.when(e == N_EXPERTS - 1)
    def _():
        z = DEEPNORM_ALPHA * x1_ref[...] + acc_ref[...]
        o_ref[...] = _layer_norm(z, g2_ref[...], b2_ref[...])


def _moe(x1, x1b, comb, wg, wu, wd, g2, b2):
    t = x1.shape[0]
    tm = MOE_TM
    return pl.pallas_call(
        _moe_kernel,
        grid=(t // tm, N_EXPERTS),
        in_specs=[
            pl.BlockSpec((tm, D_MODEL), lambda i, e: (i, 0)),
            pl.BlockSpec((tm, D_MODEL), lambda i, e: (i, 0)),
            pl.BlockSpec((tm, LANES), lambda i, e: (i, 0)),
            pl.BlockSpec((None, D_MODEL, D_EXPERT), lambda i, e: (e, 0, 0)),
            pl.BlockSpec((None, D_MODEL, D_EXPERT), lambda i, e: (e, 0, 0)),
            pl.BlockSpec((None, D_EXPERT, D_MODEL), lambda i, e: (e, 0, 0)),
            pl.BlockSpec((1, D_MODEL), lambda i, e: (0, 0)),
            pl.BlockSpec((1, D_MODEL), lambda i, e: (0, 0)),
        ],
        out_specs=pl.BlockSpec((tm, D_MODEL), lambda i, e: (i, 0)),
        out_shape=jax.ShapeDtypeStruct((t, D_MODEL), F32),
        scratch_shapes=[pltpu.VMEM((tm, D_MODEL), F32)],
        compiler_params=_cparams(("parallel", "arbitrary")),
        name="moe",
    )(x1, x1b, comb, wg, wu, wd, g2, b2)


def kernel(x, mem, w_in, b_forget, b_gates, lambda_q1, lambda_k1, lambda_q2, lambda_k2, diff_subln_g,
           w_mem_kv, w_branch_fox, w_branch_diff, w_branch_mem, w_out, ln1_g, ln1_b, w_router_group,
           b_router_group, w_router_expert, b_router_expert, w_expert_gate, w_expert_up, w_expert_down,
           ln2_g, ln2_b):
    batch, seq, d = x.shape
    t = batch * seq
    l = 0
    x2 = x.reshape(t, d)

    w_in_l = w_in[l]
    w_qkv = w_in_l[:, :QKV_COLS].astype(BF16)
    w_gates = w_in_l[:, GATE_COL0:FL_COL0].astype(BF16)
    w_fl_t = jnp.pad(w_in_l[:, FL_COL0:].T, ((0, 16 - FOX_HEADS), (0, 0))).astype(BF16)
    w_r = jnp.concatenate([w_router_group[l], w_router_expert[l]], axis=1)
    w_r = jnp.pad(w_r, ((0, 0), (0, LANES - w_r.shape[1]))).astype(BF16)
    b_r = jnp.concatenate([b_router_group[l], b_router_expert[l]])
    b_r = jnp.pad(b_r, (0, LANES - b_r.shape[0])).reshape(1, LANES)
    lam_params = jnp.stack([lambda_q1[l], lambda_k1[l], lambda_q2[l], lambda_k2[l]])
    slopes = 2.0 ** (-8.0 * jnp.arange(1, DIFF_HEADS + 1, dtype=F32) / DIFF_HEADS)

    qkv, fl_t = _proj(x2, w_qkv, w_fl_t)
    qkv4 = qkv.reshape(N_SLABS, batch, seq, LANES)
    c = _fscan(fl_t, b_forget[l].reshape(FOX_HEADS, 1), batch, seq)
    c4 = c.reshape(FOX_HEADS // 2, 2, t)

    y_fox = _fox(qkv4, c4, batch, seq)
    y_diff = _diff(qkv4, slopes, lam_params, diff_subln_g[l].reshape(1, LANES), batch, seq)
    mkv = _memkv(mem, w_mem_kv[l].astype(BF16))
    y_mem = _mem_attn(qkv4, mkv, batch, seq)

    x1, x1b, comb = _merge(
        x2, y_fox.reshape(t, -1), y_diff.reshape(t, -1), y_mem.reshape(t, -1),
        w_gates, b_gates[l].reshape(1, -1),
        w_branch_fox[l].astype(BF16), w_branch_diff[l].astype(BF16), w_branch_mem[l].astype(BF16),
        w_out[l].astype(BF16), ln1_g[l].reshape(1, d), ln1_b[l].reshape(1, d), w_r, b_r)

    out = _moe(x1, x1b, comb,
               w_expert_gate[l].astype(BF16), w_expert_up[l].astype(BF16), w_expert_down[l].astype(BF16),
               ln2_g[l].reshape(1, d), ln2_b[l].reshape(1, d))
    return out.reshape(batch, seq, d)
```

```python
import math

import jax
import jax.numpy as jnp
from jax import lax
from jax.experimental import pallas as pl
from jax.experimental.pallas import tpu as pltpu

F32 = jnp.float32
BF16 = jnp.bfloat16

D_MODEL = 1024
HEAD_DIM = 64
FOX_HEADS = 8
DIFF_HEADS = 4
MEM_HEADS = 4
MEM_HEAD_DIM = 128
N_MEM = 256
N_BRANCH = 3
N_GROUPS = 4
EXPERTS_PER_GROUP = 8
N_EXPERTS = N_GROUPS * EXPERTS_PER_GROUP
D_EXPERT = 256
LN_EPS = 1e-5
DEPTH = 1
DEEPNORM_ALPHA = (2.0 * DEPTH) ** 0.25
LAM_INIT = 0.8 - 0.6 * math.exp(-0.3 * 0)
LOG2E = math.log2(math.e)

LANES = 128
QKV_COLS = 3584
N_SLABS = QKV_COLS // LANES
GATE_COL0 = QKV_COLS
FL_COL0 = QKV_COLS + N_BRANCH * D_MODEL
SLAB2_FOX_Q, SLAB2_FOX_K, SLAB2_FOX_V = 0, 2, 4
SLAB2_DIFF_Q, SLAB2_DIFF_K, SLAB2_DIFF_V = 6, 8, 10
SLAB_MEM_Q = 24
ROUTE_LANE0 = N_GROUPS

NEG_BIG = -1e30
VMEM_LIMIT = 56 * 1024 * 1024

PROJ_TM = 1024
PROJ_TN = 512
ATT_T = 256
N_CHAINS = 4
MEM_TQ = 1024
MERGE_TM = 512
MOE_TM = 1024


def _cparams(sem):
    return pltpu.CompilerParams(dimension_semantics=sem, vmem_limit_bytes=VMEM_LIMIT)


def _proj_kernel(x_ref, w_ref, wfl_ref, sc_ref, o_ref, fl_ref, xb_ref):
    j = pl.program_id(1)

    @pl.when(j == 0)
    def _():
        xb = x_ref[...].astype(BF16)
        xb_ref[...] = xb
        fl_ref[...] = lax.dot_general(wfl_ref[...], xb, (((1,), (1,)), ((), ())),
                                      preferred_element_type=F32)

    acc = jnp.dot(xb_ref[...], w_ref[...], preferred_element_type=F32) * sc_ref[...]
    for c in range(PROJ_TN // LANES):
        o_ref[c] = acc[:, c * LANES:(c + 1) * LANES].astype(BF16)


def _proj(x2, w_qkv, w_fl_t, col_scale):
    t = x2.shape[0]
    spc = PROJ_TN // LANES
    return pl.pallas_call(
        _proj_kernel,
        grid=(t // PROJ_TM, QKV_COLS // PROJ_TN),
        in_specs=[
            pl.BlockSpec((PROJ_TM, D_MODEL), lambda i, j: (i, 0)),
            pl.BlockSpec((D_MODEL, PROJ_TN), lambda i, j: (0, j)),
            pl.BlockSpec((16, D_MODEL), lambda i, j: (0, 0)),
            pl.BlockSpec((1, PROJ_TN), lambda i, j: (0, j)),
        ],
        out_specs=[
            pl.BlockSpec((spc, PROJ_TM, LANES), lambda i, j: (j, i, 0)),
            pl.BlockSpec((16, PROJ_TM), lambda i, j: (0, i)),
        ],
        out_shape=[
            jax.ShapeDtypeStruct((N_SLABS, t, LANES), BF16),
            jax.ShapeDtypeStruct((16, t), F32),
        ],
        scratch_shapes=[pltpu.VMEM((PROJ_TM, D_MODEL), BF16)],
        compiler_params=_cparams(("parallel", "arbitrary")),
        name="proj",
    )(x2, w_qkv, w_fl_t, col_scale)


def _fscan_kernel(fl_ref, b_ref, c_ref):
    s = fl_ref.shape[1]
    z = fl_ref[0:FOX_HEADS, :] + b_ref[...]
    lf = jnp.minimum(z, 0.0) - jnp.log(1.0 + jnp.exp(-jnp.abs(z)))
    lane = lax.broadcasted_iota(jnp.int32, lf.shape, 1)
    sh = 1
    while sh < s:
        r = pltpu.roll(lf, sh, axis=1)
        lf = lf + jnp.where(lane >= sh, r, 0.0)
        sh *= 2
    c_ref[...] = lf * LOG2E


def _fscan(fl_t, b_forget_col, batch, seq):
    t = fl_t.shape[1]
    return pl.pallas_call(
        _fscan_kernel,
        grid=(batch,),
        in_specs=[
            pl.BlockSpec((16, seq), lambda b: (0, b)),
            pl.BlockSpec((FOX_HEADS, 1), lambda b: (0, 0)),
        ],
        out_specs=pl.BlockSpec((FOX_HEADS, seq), lambda b: (0, b)),
        out_shape=jax.ShapeDtypeStruct((FOX_HEADS, t), F32),
        compiler_params=_cparams(("parallel",)),
        name="fscan",
    )(fl_t, b_forget_col)


def _flash_chains(qi, tq, qh_scr, k_fn, v_fn, bias_fn, m_scr, acc_scr, s_scr):
    m_scr[...] = jnp.full(m_scr.shape, NEG_BIG, F32)
    acc_scr[...] = jnp.zeros(acc_scr.shape, F32)

    def scores(j, slot):
        start = pl.multiple_of(j * tq, tq)
        for c in range(N_CHAINS):
            s_scr[slot, c] = lax.dot_general(qh_scr[c], k_fn(c, start), (((1,), (1,)), ((), ())),
                                             preferred_element_type=F32)

    def consume(j, slot, masked):
        start = pl.multiple_of(j * tq, tq)
        if masked:
            row = lax.broadcasted_iota(jnp.int32, (tq, tq), 0)
            col = lax.broadcasted_iota(jnp.int32, (tq, tq), 1)
            keep = col <= row
        ps, alphas = [], []
        for c in range(N_CHAINS):
            s = s_scr[slot, c] + bias_fn(c, start)
            if masked:
                s = jnp.where(keep, s, NEG_BIG)
            m_prev = m_scr[c]
            m_new = jnp.maximum(m_prev, jnp.max(s, axis=1, keepdims=True))
            alphas.append(jnp.exp2(m_prev - m_new))
            ps.append(jnp.exp2(s - jnp.concatenate([m_new] * (tq // LANES), axis=1)).astype(BF16))
            m_scr[c] = m_new
        pvs = [jnp.dot(ps[c], v_fn(c, start), preferred_element_type=F32) for c in range(N_CHAINS)]
        for c in range(N_CHAINS):
            acc = acc_scr[c]
            acc_scr[c] = jnp.concatenate([alphas[c]] * (acc.shape[1] // LANES), axis=1) * acc + pvs[c]

    scores(0, 0)

    def body(jj, carry):
        j = 2 * jj
        scores(j + 1, 1)
        consume(j, 0, False)
        scores(j + 2, 0)
        consume(j + 1, 1, False)
        return carry

    lax.fori_loop(0, qi // 2, body, 0)

    @pl.when(qi % 2 == 0)
    def _():
        consume(qi, 0, True)

    @pl.when(qi % 2 == 1)
    def _():
        scores(qi, 1)
        consume(qi - 1, 0, False)
        consume(qi, 1, True)


def _split_q(q_ref, qh_scr):
    for sl in range(N_CHAINS // 2):
        q = q_ref[sl]
        lane = lax.broadcasted_iota(jnp.int32, q.shape, 1)
        zero = jnp.zeros_like(q)
        qh_scr[2 * sl] = jnp.where(lane < HEAD_DIM, q, zero)
        qh_scr[2 * sl + 1] = jnp.where(lane >= HEAD_DIM, q, zero)


def _attn_scratch(tq, acc_w):
    return [
        pltpu.VMEM((N_CHAINS, tq, LANES), BF16),
        pltpu.VMEM((N_CHAINS, tq, LANES), F32),
        pltpu.VMEM((N_CHAINS, tq, acc_w), F32),
        pltpu.VMEM((2, N_CHAINS, tq, tq), F32),
    ]


def _fox_kernel(q_ref, k_ref, v_ref, c_ref, o_ref, qh_scr, m_scr, acc_scr, s_scr):
    tq = q_ref.shape[1]
    qi = pl.program_id(2)
    _split_q(q_ref, qh_scr)
    q0 = pl.multiple_of(qi * tq, tq)
    c_tile0 = [c_ref[sl, :, pl.ds(q0, tq)][:, 0:1] for sl in range(2)]

    def k_fn(c, start):
        return k_ref[c // 2, pl.ds(start, tq), :]

    def v_fn(c, start):
        v = v_ref[c // 2, pl.ds(start, tq), :]
        lane = lax.broadcasted_iota(jnp.int32, v.shape, 1)
        own = (lane < HEAD_DIM) if c % 2 == 0 else (lane >= HEAD_DIM)
        return jnp.where(own, v, jnp.ones_like(v))

    def bias_fn(c, start):
        b = c_tile0[c // 2] - c_ref[c // 2, :, pl.ds(start, tq)]
        return b[c % 2:c % 2 + 1, :]

    _flash_chains(qi, tq, qh_scr, k_fn, v_fn, bias_fn, m_scr, acc_scr, s_scr)
    for sl in range(2):
        a0 = acc_scr[2 * sl]
        a1 = acc_scr[2 * sl + 1]
        o0 = a0 / pltpu.roll(a0, HEAD_DIM, axis=1)
        o1 = a1 / pltpu.roll(a1, HEAD_DIM, axis=1)
        lane = lax.broadcasted_iota(jnp.int32, o0.shape, 1)
        o_ref[:, sl * LANES:(sl + 1) * LANES] = jnp.where(lane < HEAD_DIM, o0, o1).astype(o_ref.dtype)


def _fox(qkv4, c4, batch, seq):
    tq = ATT_T
    return pl.pallas_call(
        _fox_kernel,
        grid=(batch, FOX_HEADS // 4, seq // tq),
        in_specs=[
            pl.BlockSpec((2, None, tq, LANES), lambda b, g, i: (SLAB2_FOX_Q + g, b, i, 0)),
            pl.BlockSpec((2, None, seq, LANES), lambda b, g, i: (SLAB2_FOX_K + g, b, 0, 0)),
            pl.BlockSpec((2, None, seq, LANES), lambda b, g, i: (SLAB2_FOX_V + g, b, 0, 0)),
            pl.BlockSpec((2, 2, seq), lambda b, g, i: (g, 0, b)),
        ],
        out_specs=pl.BlockSpec((None, tq, 2 * LANES), lambda b, g, i: (b, i, g)),
        out_shape=jax.ShapeDtypeStruct((batch, seq, FOX_HEADS * HEAD_DIM), BF16),
        scratch_shapes=_attn_scratch(tq, LANES),
        compiler_params=_cparams(("parallel", "parallel", "arbitrary")),
        name="fox",
    )(qkv4, qkv4, qkv4, c4)


def _diff_kernel(slopes_ref, q_ref, k_ref, v_ref, lam_ref, g_ref, o_ref, qh_scr, m_scr, acc_scr, s_scr):
    tq = q_ref.shape[1]
    qi = pl.program_id(2)
    _split_q(q_ref, qh_scr)
    q0 = qi * tq
    g = pl.program_id(1)
    slope = [slopes_ref[2 * g] * LOG2E, slopes_ref[2 * g + 1] * LOG2E]

    def k_fn(c, start):
        return k_ref[c // 2, pl.ds(start, tq), :]

    def v_fn(c, start):
        v = v_ref[c // 2, pl.ds(start, tq), :]
        return jnp.concatenate([v, jnp.ones_like(v)], axis=1)

    def bias_fn(c, start):
        col = lax.broadcasted_iota(jnp.int32, (1, tq), 1) + (start - q0)
        return col.astype(F32) * slope[c // 2]

    _flash_chains(qi, tq, qh_scr, k_fn, v_fn, bias_fn, m_scr, acc_scr, s_scr)
    lp = lam_ref[...]
    s1 = jnp.sum(lp[0:1, :] * lp[1:2, :], axis=1, keepdims=True)
    s2 = jnp.sum(lp[2:3, :] * lp[3:4, :], axis=1, keepdims=True)
    lam = jnp.exp(s1) - jnp.exp(s2) + LAM_INIT
    for hd in range(2):
        a1 = acc_scr[2 * hd]
        a2 = acc_scr[2 * hd + 1]
        o = a1[:, :LANES] / a1[:, LANES:] - lam * (a2[:, :LANES] / a2[:, LANES:])
        ms = jnp.mean(o * o, axis=1, keepdims=True)
        y = o * lax.rsqrt(ms + LN_EPS) * g_ref[...]
        o_ref[:, hd * LANES:(hd + 1) * LANES] = (y * (1.0 - LAM_INIT)).astype(o_ref.dtype)


def _diff(qkv4, slopes, lam_params, subln_g, batch, seq):
    tq = ATT_T
    grid_spec = pltpu.PrefetchScalarGridSpec(
        num_scalar_prefetch=1,
        grid=(batch, DIFF_HEADS // 2, seq // tq),
        in_specs=[
            pl.BlockSpec((2, None, tq, LANES), lambda b, g, i, s: (SLAB2_DIFF_Q + g, b, i, 0)),
            pl.BlockSpec((2, None, seq, LANES), lambda b, g, i, s: (SLAB2_DIFF_K + g, b, 0, 0)),
            pl.BlockSpec((2, None, seq, LANES), lambda b, g, i, s: (SLAB2_DIFF_V + g, b, 0, 0)),
            pl.BlockSpec((4, HEAD_DIM), lambda b, g, i, s: (0, 0)),
            pl.BlockSpec((1, LANES), lambda b, g, i, s: (0, 0)),
        ],
        out_specs=pl.BlockSpec((None, tq, 2 * LANES), lambda b, g, i, s: (b, i, g)),
        scratch_shapes=_attn_scratch(tq, 2 * LANES),
    )
    return pl.pallas_call(
        _diff_kernel,
        grid_spec=grid_spec,
        out_shape=jax.ShapeDtypeStruct((batch, seq, DIFF_HEADS * LANES), BF16),
        compiler_params=_cparams(("parallel", "parallel", "arbitrary")),
        name="diff",
    )(slopes, qkv4, qkv4, qkv4, lam_params, subln_g)


def _memkv_kernel(mem_ref, w_ref, o_ref):
    o_ref[...] = jnp.dot(mem_ref[...].astype(BF16), w_ref[...],
                         preferred_element_type=F32).astype(o_ref.dtype)


def _memkv(mem, w_kv):
    batch = mem.shape[0]
    width = w_kv.shape[1]
    return pl.pallas_call(
        _memkv_kernel,
        grid=(batch,),
        in_specs=[
            pl.BlockSpec((None, N_MEM, D_MODEL), lambda b: (b, 0, 0)),
            pl.BlockSpec((D_MODEL, width), lambda b: (0, 0)),
        ],
        out_specs=pl.BlockSpec((None, N_MEM, width), lambda b: (b, 0, 0)),
        out_shape=jax.ShapeDtypeStruct((batch, N_MEM, width), BF16),
        compiler_params=_cparams(("parallel",)),
        name="mem_kv",
    )(mem, w_kv)


def _mem_kernel(q_ref, mk_ref, mv_ref, o_ref):
    s = lax.dot_general(q_ref[...], mk_ref[...], (((1,), (1,)), ((), ())),
                        preferred_element_type=F32)
    m = jnp.max(s, axis=1, keepdims=True)
    p = jnp.exp2(s - m)
    l = jnp.sum(p, axis=1, keepdims=True)
    o = jnp.dot(p.astype(BF16), mv_ref[...], preferred_element_type=F32)
    o_ref[...] = (o / l).astype(o_ref.dtype)


def _mem_attn(qkv4, mkv, batch, seq):
    tq = MEM_TQ
    return pl.pallas_call(
        _mem_kernel,
        grid=(batch, MEM_HEADS, seq // tq),
        in_specs=[
            pl.BlockSpec((None, None, tq, LANES), lambda b, h, i: (SLAB_MEM_Q + h, b, i, 0)),
            pl.BlockSpec((None, N_MEM, LANES), lambda b, h, i: (b, 0, h)),
            pl.BlockSpec((None, N_MEM, LANES), lambda b, h, i: (b, 0, MEM_HEADS + h)),
        ],
        out_specs=pl.BlockSpec((None, tq, LANES), lambda b, h, i: (b, i, h)),
        out_shape=jax.ShapeDtypeStruct((batch, seq, MEM_HEADS * LANES), BF16),
        compiler_params=_cparams(("parallel", "parallel", "parallel")),
        name="mem",
    )(qkv4, mkv, mkv)


def _layer_norm(z, g, b):
    mu = jnp.mean(z, axis=1, keepdims=True)
    zc = z - mu
    var = jnp.mean(zc * zc, axis=1, keepdims=True)
    return zc * lax.rsqrt(var + LN_EPS) * g + b


def _lane_max(v):
    return jnp.max(v, axis=1, keepdims=True)


def _routing_weights(logits):
    lane = lax.broadcasted_iota(jnp.int32, logits.shape, 1)
    big = jnp.int32(2 * LANES)
    is_g = lane < N_GROUPS
    gl = jnp.where(is_g, logits, NEG_BIG)
    gmax = _lane_max(gl)
    g_w = 1.0 / jnp.sum(jnp.exp(gl - gmax), axis=1, keepdims=True)
    g_idx = jnp.min(jnp.where(gl == gmax, lane, big), axis=1, keepdims=True)
    lo = ROUTE_LANE0 + g_idx * EXPERTS_PER_GROUP
    in_grp = (lane >= lo) & (lane < lo + EXPERTS_PER_GROUP)
    el = jnp.where(in_grp, logits, NEG_BIG)
    e1 = _lane_max(el)
    i1 = jnp.min(jnp.where(el == e1, lane, big), axis=1, keepdims=True)
    el2 = jnp.where(lane == i1, NEG_BIG, el)
    e2 = _lane_max(el2)
    i2 = jnp.min(jnp.where(el2 == e2, lane, big), axis=1, keepdims=True)
    r = jnp.exp(e2 - e1)
    w1 = g_w / (1.0 + r)
    w2 = g_w * r / (1.0 + r)
    return jnp.where(lane == i1, w1, 0.0) + jnp.where(lane == i2, w2, 0.0)


def _merge_kernel(x_ref, yf_ref, yd_ref, ym_ref, wg_ref, bg_ref, wbf_ref, wbd_ref, wbm_ref,
                  wo_ref, g1_ref, b1_ref, wr_ref, br_ref, x1_ref, x1b_ref, comb_ref):
    x = x_ref[...]
    xb = x.astype(BF16)
    h = None
    for i, (y_ref, wb_ref) in enumerate(((yf_ref, wbf_ref), (yd_ref, wbd_ref), (ym_ref, wbm_ref))):
        cols = slice(i * D_MODEL, (i + 1) * D_MODEL)
        gl = jnp.dot(xb, wg_ref[:, cols], preferred_element_type=F32) + bg_ref[:, cols]
        gate = 1.0 / (1.0 + jnp.exp(-gl))
        br = jnp.dot(y_ref[...], wb_ref[...], preferred_element_type=F32)
        h = gate * br if h is None else h + gate * br
    o = jnp.dot(h.astype(BF16), wo_ref[...], preferred_element_type=F32)
    x1 = _layer_norm(DEEPNORM_ALPHA * x + o, g1_ref[...], b1_ref[...])
    x1_ref[...] = x1
    x1b = x1.astype(BF16)
    x1b_ref[...] = x1b
    logits = jnp.dot(x1b, wr_ref[...], preferred_element_type=F32) + br_ref[...]
    comb_ref[...] = _routing_weights(logits)


def _const_spec(shape):
    return pl.BlockSpec(shape, lambda i: (0,) * len(shape), pipeline_mode=pl.Buffered(1))


def _merge(x2, yf, yd, ym, wg, bg, wbf, wbd, wbm, wo, g1, b1, wr, br):
    t = x2.shape[0]
    tm = MERGE_TM
    half = yf.shape[1]
    row = lambda w: pl.BlockSpec((tm, w), lambda i: (i, 0))
    return pl.pallas_call(
        _merge_kernel,
        grid=(t // tm,),
        in_specs=[
            row(D_MODEL), row(half), row(half), row(half),
            _const_spec((D_MODEL, N_BRANCH * D_MODEL)), _const_spec((1, N_BRANCH * D_MODEL)),
            _const_spec((half, D_MODEL)), _const_spec((half, D_MODEL)), _const_spec((half, D_MODEL)),
            _const_spec((D_MODEL, D_MODEL)), _const_spec((1, D_MODEL)), _const_spec((1, D_MODEL)),
            _const_spec((D_MODEL, LANES)), _const_spec((1, LANES)),
        ],
        out_specs=[row(D_MODEL), row(D_MODEL), row(LANES)],
        out_shape=[
            jax.ShapeDtypeStruct((t, D_MODEL), F32),
            jax.ShapeDtypeStruct((t, D_MODEL), BF16),
            jax.ShapeDtypeStruct((t, LANES), F32),
        ],
        compiler_params=_cparams(("parallel",)),
        name="merge",
    )(x2, yf, yd, ym, wg, bg, wbf, wbd, wbm, wo, g1, b1, wr, br)


def _moe_kernel(x1_ref, x1b_ref, comb_ref, wg_ref, wu_ref, wd_ref, g2_ref, b2_ref, o_ref, acc_ref):
    e = pl.program_id(1)

    @pl.when(e == 0)
    def _():
        acc_ref[...] = jnp.zeros(acc_ref.shape, F32)

    xb = x1b_ref[...]
    g = jnp.dot(xb, wg_ref[...], preferred_element_type=F32)
    u = jnp.dot(xb, wu_ref[...], preferred_element_type=F32)
    hcat = (g / (1.0 + jnp.exp(-g))) * u
    y = jnp.dot(hcat.astype(BF16), wd_ref[...], preferred_element_type=F32)
    comb = comb_ref[...]
    lane = lax.broadcasted_iota(jnp.int32, comb.shape, 1)
    coef = jnp.sum(jnp.where(lane == ROUTE_LANE0 + e, comb, 0.0), axis=1, keepdims=True)
    acc_ref[...] += coef * y

    @pl.when(e == N_EXPERTS - 1)
    def _():
        z = DEEPNORM_ALPHA * x1_ref[...] + acc_ref[...]
        o_ref[...] = _layer_norm(z, g2_ref[...], b2_ref[...])


def _moe(x1, x1b, comb, wg, wu, wd, g2, b2):
    t = x1.shape[0]
    tm = MOE_TM
    return pl.pallas_call(
        _moe_kernel,
        grid=(t // tm, N_EXPERTS),
        in_specs=[
            pl.BlockSpec((tm, D_MODEL), lambda i, e: (i, 0)),
            pl.BlockSpec((tm, D_MODEL), lambda i, e: (i, 0)),
            pl.BlockSpec((tm, LANES), lambda i, e: (i, 0)),
            pl.BlockSpec((None, D_MODEL, D_EXPERT), lambda i, e: (e, 0, 0)),
            pl.BlockSpec((None, D_MODEL, D_EXPERT), lambda i, e: (e, 0, 0)),
            pl.BlockSpec((None, D_EXPERT, D_MODEL), lambda i, e: (e, 0, 0)),
            pl.BlockSpec((1, D_MODEL), lambda i, e: (0, 0)),
            pl.BlockSpec((1, D_MODEL), lambda i, e: (0, 0)),
        ],
        out_specs=pl.BlockSpec((tm, D_MODEL), lambda i, e: (i, 0)),
        out_shape=jax.ShapeDtypeStruct((t, D_MODEL), F32),
        scratch_shapes=[pltpu.VMEM((tm, D_MODEL), F32)],
        compiler_params=_cparams(("parallel", "arbitrary")),
        name="moe",
    )(x1, x1b, comb, wg, wu, wd, g2, b2)


def _qkv_col_scale():
    ones = lambda n: jnp.ones((n,), F32)
    att = FOX_HEADS * HEAD_DIM
    qs = lambda d: jnp.full((att,), LOG2E * d ** -0.5, F32)
    return jnp.concatenate([qs(HEAD_DIM), ones(2 * att), qs(HEAD_DIM), ones(2 * att),
                            qs(MEM_HEAD_DIM)]).reshape(1, QKV_COLS)


def kernel(x, mem, w_in, b_forget, b_gates, lambda_q1, lambda_k1, lambda_q2, lambda_k2, diff_subln_g,
           w_mem_kv, w_branch_fox, w_branch_diff, w_branch_mem, w_out, ln1_g, ln1_b, w_router_group,
           b_router_group, w_router_expert, b_router_expert, w_expert_gate, w_expert_up, w_expert_down,
           ln2_g, ln2_b):
    batch, seq, d = x.shape
    t = batch * seq
    l = 0
    x2 = x.reshape(t, d)

    w_in_l = w_in[l]
    w_qkv = w_in_l[:, :QKV_COLS].astype(BF16)
    w_gates = w_in_l[:, GATE_COL0:FL_COL0].astype(BF16)
    w_fl_t = jnp.pad(w_in_l[:, FL_COL0:].T, ((0, 16 - FOX_HEADS), (0, 0))).astype(BF16)
    w_r = jnp.concatenate([w_router_group[l], w_router_expert[l]], axis=1)
    w_r = jnp.pad(w_r, ((0, 0), (0, LANES - w_r.shape[1]))).astype(BF16)
    b_r = jnp.concatenate([b_router_group[l], b_router_expert[l]])
    b_r = jnp.pad(b_r, (0, LANES - b_r.shape[0])).reshape(1, LANES)
    lam_params = jnp.stack([lambda_q1[l], lambda_k1[l], lambda_q2[l], lambda_k2[l]])
    slopes = 2.0 ** (-8.0 * jnp.arange(1, DIFF_HEADS + 1, dtype=F32) / DIFF_HEADS)

    qkv, fl_t = _proj(x2, w_qkv, w_fl_t, _qkv_col_scale())
    qkv4 = qkv.reshape(N_SLABS, batch, seq, LANES)
    c = _fscan(fl_t, b_forget[l].reshape(FOX_HEADS, 1), batch, seq)
    c4 = c.reshape(FOX_HEADS // 2, 2, t)

    y_fox = _fox(qkv4, c4, batch, seq)
    y_diff = _diff(qkv4, slopes, lam_params, diff_subln_g[l].reshape(1, LANES), batch, seq)
    mkv = _memkv(mem, w_mem_kv[l].astype(BF16))
    y_mem = _mem_attn(qkv4, mkv, batch, seq)

    x1, x1b, comb = _merge(
        x2, y_fox.reshape(t, -1), y_diff.reshape(t, -1), y_mem.reshape(t, -1),
        w_gates, b_gates[l].reshape(1, -1),
        w_branch_fox[l].astype(BF16), w_branch_diff[l].astype(BF16), w_branch_mem[l].astype(BF16),
        w_out[l].astype(BF16), ln1_g[l].reshape(1, d), ln1_b[l].reshape(1, d), w_r, b_r)

    out = _moe(x1, x1b, comb,
               w_expert_gate[l].astype(BF16), w_expert_up[l].astype(BF16), w_expert_down[l].astype(BF16),
               ln2_g[l].reshape(1, d), ln2_b[l].reshape(1, d))
    return out.reshape(batch, seq, d)
```

```python
import functools
import math

import jax
import jax.numpy as jnp
from jax import lax
from jax.experimental import pallas as pl
from jax.experimental.pallas import tpu as pltpu

F32 = jnp.float32
BF16 = jnp.bfloat16

D_MODEL = 1024
HEAD_DIM = 64
FOX_HEADS = 8
DIFF_HEADS = 4
MEM_HEADS = 4
MEM_HEAD_DIM = 128
N_MEM = 256
N_BRANCH = 3
N_GROUPS = 4
EXPERTS_PER_GROUP = 8
N_EXPERTS = N_GROUPS * EXPERTS_PER_GROUP
D_EXPERT = 256
LN_EPS = 1e-5
DEPTH = 1
DEEPNORM_ALPHA = (2.0 * DEPTH) ** 0.25
LAM_INIT = 0.8 - 0.6 * math.exp(-0.3 * 0)
LOG2E = math.log2(math.e)

LANES = 128
QKV_COLS = 3584
N_SLABS = QKV_COLS // LANES
GATE_COL0 = QKV_COLS
FL_COL0 = QKV_COLS + N_BRANCH * D_MODEL
SLAB2_FOX_Q, SLAB2_FOX_K, SLAB2_FOX_V = 0, 2, 4
SLAB2_DIFF_Q, SLAB2_DIFF_K, SLAB2_DIFF_V = 6, 8, 10
SLAB_MEM_Q = 24
ROUTE_LANE0 = N_GROUPS

NEG_BIG = -1e30
VMEM_LIMIT = 56 * 1024 * 1024

PROJ_TM = 1024
PROJ_TN = 512
ATT_T = 256
N_CHAINS = 4
MEM_TQ = 1024
MERGE_TM = 512
MOE_R = 256
MOE_TD = 256
SEG_ALIGN = 8
RI_E1, RI_E2, RI_W1, RI_W2 = 0, 1, 2, 3


def _cparams(sem):
    return pltpu.CompilerParams(dimension_semantics=sem, vmem_limit_bytes=VMEM_LIMIT)


def _proj_kernel(x_ref, w_ref, wfl_ref, sc_ref, o_ref, fl_ref, xb_ref):
    j = pl.program_id(1)

    @pl.when(j == 0)
    def _():
        xb = x_ref[...].astype(BF16)
        xb_ref[...] = xb
        fl_ref[...] = lax.dot_general(wfl_ref[...], xb, (((1,), (1,)), ((), ())),
                                      preferred_element_type=F32)

    acc = jnp.dot(xb_ref[...], w_ref[...], preferred_element_type=F32) * sc_ref[...]
    for c in range(PROJ_TN // LANES):
        o_ref[c] = acc[:, c * LANES:(c + 1) * LANES].astype(BF16)


def _proj(x2, w_qkv, w_fl_t, col_scale):
    t = x2.shape[0]
    spc = PROJ_TN // LANES
    return pl.pallas_call(
        _proj_kernel,
        grid=(t // PROJ_TM, QKV_COLS // PROJ_TN),
        in_specs=[
            pl.BlockSpec((PROJ_TM, D_MODEL), lambda i, j: (i, 0)),
            pl.BlockSpec((D_MODEL, PROJ_TN), lambda i, j: (0, j)),
            pl.BlockSpec((16, D_MODEL), lambda i, j: (0, 0)),
            pl.BlockSpec((1, PROJ_TN), lambda i, j: (0, j)),
        ],
        out_specs=[
            pl.BlockSpec((spc, PROJ_TM, LANES), lambda i, j: (j, i, 0)),
            pl.BlockSpec((16, PROJ_TM), lambda i, j: (0, i)),
        ],
        out_shape=[
            jax.ShapeDtypeStruct((N_SLABS, t, LANES), BF16),
            jax.ShapeDtypeStruct((16, t), F32),
        ],
        scratch_shapes=[pltpu.VMEM((PROJ_TM, D_MODEL), BF16)],
        compiler_params=_cparams(("parallel", "arbitrary")),
        name="proj",
    )(x2, w_qkv, w_fl_t, col_scale)


def _fscan_kernel(fl_ref, b_ref, c_ref):
    s = fl_ref.shape[1]
    z = fl_ref[0:FOX_HEADS, :] + b_ref[...]
    lf = jnp.minimum(z, 0.0) - jnp.log(1.0 + jnp.exp(-jnp.abs(z)))
    lane = lax.broadcasted_iota(jnp.int32, lf.shape, 1)
    sh = 1
    while sh < s:
        r = pltpu.roll(lf, sh, axis=1)
        lf = lf + jnp.where(lane >= sh, r, 0.0)
        sh *= 2
    c_ref[...] = lf * LOG2E


def _fscan(fl_t, b_forget_col, batch, seq):
    t = fl_t.shape[1]
    return pl.pallas_call(
        _fscan_kernel,
        grid=(batch,),
        in_specs=[
            pl.BlockSpec((16, seq), lambda b: (0, b)),
            pl.BlockSpec((FOX_HEADS, 1), lambda b: (0, 0)),
        ],
        out_specs=pl.BlockSpec((FOX_HEADS, seq), lambda b: (0, b)),
        out_shape=jax.ShapeDtypeStruct((FOX_HEADS, t), F32),
        compiler_params=_cparams(("parallel",)),
        name="fscan",
    )(fl_t, b_forget_col)


def _flash_chains(qi, tq, qh_scr, k_fn, v_fn, bias_fn, m_scr, acc_scr, s_scr):
    m_scr[...] = jnp.full(m_scr.shape, NEG_BIG, F32)
    acc_scr[...] = jnp.zeros(acc_scr.shape, F32)

    def scores(j, slot):
        start = pl.multiple_of(j * tq, tq)
        for c in range(N_CHAINS):
            s_scr[slot, c] = lax.dot_general(qh_scr[c], k_fn(c, start), (((1,), (1,)), ((), ())),
                                             preferred_element_type=F32)

    def consume(j, slot, masked):
        start = pl.multiple_of(j * tq, tq)
        if masked:
            row = lax.broadcasted_iota(jnp.int32, (tq, tq), 0)
            col = lax.broadcasted_iota(jnp.int32, (tq, tq), 1)
            keep = col <= row
        ps, alphas = [], []
        for c in range(N_CHAINS):
            s = s_scr[slot, c] + bias_fn(c, start)
            if masked:
                s = jnp.where(keep, s, NEG_BIG)
            m_prev = m_scr[c]
            m_new = jnp.maximum(m_prev, jnp.max(s, axis=1, keepdims=True))
            alphas.append(jnp.exp2(m_prev - m_new))
            ps.append(jnp.exp2(s - jnp.concatenate([m_new] * (tq // LANES), axis=1)).astype(BF16))
            m_scr[c] = m_new
        pvs = [jnp.dot(ps[c], v_fn(c, start), preferred_element_type=F32) for c in range(N_CHAINS)]
        for c in range(N_CHAINS):
            acc = acc_scr[c]
            acc_scr[c] = jnp.concatenate([alphas[c]] * (acc.shape[1] // LANES), axis=1) * acc + pvs[c]

    scores(0, 0)

    def body(jj, carry):
        j = 2 * jj
        scores(j + 1, 1)
        consume(j, 0, False)
        scores(j + 2, 0)
        consume(j + 1, 1, False)
        return carry

    lax.fori_loop(0, qi // 2, body, 0)

    @pl.when(qi % 2 == 0)
    def _():
        consume(qi, 0, True)

    @pl.when(qi % 2 == 1)
    def _():
        scores(qi, 1)
        consume(qi - 1, 0, False)
        consume(qi, 1, True)


def _split_q(q_ref, qh_scr):
    for sl in range(N_CHAINS // 2):
        q = q_ref[sl]
        lane = lax.broadcasted_iota(jnp.int32, q.shape, 1)
        zero = jnp.zeros_like(q)
        qh_scr[2 * sl] = jnp.where(lane < HEAD_DIM, q, zero)
        qh_scr[2 * sl + 1] = jnp.where(lane >= HEAD_DIM, q, zero)


def _attn_scratch(tq, acc_w):
    return [
        pltpu.VMEM((N_CHAINS, tq, LANES), BF16),
        pltpu.VMEM((N_CHAINS, tq, LANES), F32),
        pltpu.VMEM((N_CHAINS, tq, acc_w), F32),
        pltpu.VMEM((2, N_CHAINS, tq, tq), F32),
    ]


def _fox_kernel(q_ref, k_ref, v_ref, c_ref, o_ref, qh_scr, m_scr, acc_scr, s_scr):
    tq = q_ref.shape[1]
    qi = pl.program_id(2)
    _split_q(q_ref, qh_scr)
    q0 = pl.multiple_of(qi * tq, tq)
    c_tile0 = [c_ref[sl, :, pl.ds(q0, tq)][:, 0:1] for sl in range(2)]

    def k_fn(c, start):
        return k_ref[c // 2, pl.ds(start, tq), :]

    def v_fn(c, start):
        v = v_ref[c // 2, pl.ds(start, tq), :]
        lane = lax.broadcasted_iota(jnp.int32, v.shape, 1)
        own = (lane < HEAD_DIM) if c % 2 == 0 else (lane >= HEAD_DIM)
        return jnp.where(own, v, jnp.ones_like(v))

    def bias_fn(c, start):
        b = c_tile0[c // 2] - c_ref[c // 2, :, pl.ds(start, tq)]
        return b[c % 2:c % 2 + 1, :]

    _flash_chains(qi, tq, qh_scr, k_fn, v_fn, bias_fn, m_scr, acc_scr, s_scr)
    for sl in range(2):
        a0 = acc_scr[2 * sl]
        a1 = acc_scr[2 * sl + 1]
        o0 = a0 / pltpu.roll(a0, HEAD_DIM, axis=1)
        o1 = a1 / pltpu.roll(a1, HEAD_DIM, axis=1)
        lane = lax.broadcasted_iota(jnp.int32, o0.shape, 1)
        o_ref[:, sl * LANES:(sl + 1) * LANES] = jnp.where(lane < HEAD_DIM, o0, o1).astype(o_ref.dtype)


def _fox(qkv4, c4, batch, seq):
    tq = ATT_T
    return pl.pallas_call(
        _fox_kernel,
        grid=(batch, FOX_HEADS // 4, seq // tq),
        in_specs=[
            pl.BlockSpec((2, None, tq, LANES), lambda b, g, i: (SLAB2_FOX_Q + g, b, i, 0)),
            pl.BlockSpec((2, None, seq, LANES), lambda b, g, i: (SLAB2_FOX_K + g, b, 0, 0)),
            pl.BlockSpec((2, None, seq, LANES), lambda b, g, i: (SLAB2_FOX_V + g, b, 0, 0)),
            pl.BlockSpec((2, 2, seq), lambda b, g, i: (g, 0, b)),
        ],
        out_specs=pl.BlockSpec((None, tq, 2 * LANES), lambda b, g, i: (b, i, g)),
        out_shape=jax.ShapeDtypeStruct((batch, seq, FOX_HEADS * HEAD_DIM), BF16),
        scratch_shapes=_attn_scratch(tq, LANES),
        compiler_params=_cparams(("parallel", "parallel", "arbitrary")),
        name="fox",
    )(qkv4, qkv4, qkv4, c4)


def _diff_kernel(slopes_ref, q_ref, k_ref, v_ref, lam_ref, g_ref, o_ref, qh_scr, m_scr, acc_scr, s_scr):
    tq = q_ref.shape[1]
    qi = pl.program_id(2)
    _split_q(q_ref, qh_scr)
    q0 = qi * tq
    g = pl.program_id(1)
    slope = [slopes_ref[2 * g] * LOG2E, slopes_ref[2 * g + 1] * LOG2E]

    def k_fn(c, start):
        return k_ref[c // 2, pl.ds(start, tq), :]

    def v_fn(c, start):
        v = v_ref[c // 2, pl.ds(start, tq), :]
        return jnp.concatenate([v, jnp.ones_like(v)], axis=1)

    def bias_fn(c, start):
        col = lax.broadcasted_iota(jnp.int32, (1, tq), 1) + (start - q0)
        return col.astype(F32) * slope[c // 2]

    _flash_chains(qi, tq, qh_scr, k_fn, v_fn, bias_fn, m_scr, acc_scr, s_scr)
    lp = lam_ref[...]
    s1 = jnp.sum(lp[0:1, :] * lp[1:2, :], axis=1, keepdims=True)
    s2 = jnp.sum(lp[2:3, :] * lp[3:4, :], axis=1, keepdims=True)
    lam = jnp.exp(s1) - jnp.exp(s2) + LAM_INIT
    for hd in range(2):
        a1 = acc_scr[2 * hd]
        a2 = acc_scr[2 * hd + 1]
        o = a1[:, :LANES] / a1[:, LANES:] - lam * (a2[:, :LANES] / a2[:, LANES:])
        ms = jnp.mean(o * o, axis=1, keepdims=True)
        y = o * lax.rsqrt(ms + LN_EPS) * g_ref[...]
        o_ref[:, hd * LANES:(hd + 1) * LANES] = (y * (1.0 - LAM_INIT)).astype(o_ref.dtype)


def _diff(qkv4, slopes, lam_params, subln_g, batch, seq):
    tq = ATT_T
    grid_spec = pltpu.PrefetchScalarGridSpec(
        num_scalar_prefetch=1,
        grid=(batch, DIFF_HEADS // 2, seq // tq),
        in_specs=[
            pl.BlockSpec((2, None, tq, LANES), lambda b, g, i, s: (SLAB2_DIFF_Q + g, b, i, 0)),
            pl.BlockSpec((2, None, seq, LANES), lambda b, g, i, s: (SLAB2_DIFF_K + g, b, 0, 0)),
            pl.BlockSpec((2, None, seq, LANES), lambda b, g, i, s: (SLAB2_DIFF_V + g, b, 0, 0)),
            pl.BlockSpec((4, HEAD_DIM), lambda b, g, i, s: (0, 0)),
            pl.BlockSpec((1, LANES), lambda b, g, i, s: (0, 0)),
        ],
        out_specs=pl.BlockSpec((None, tq, 2 * LANES), lambda b, g, i, s: (b, i, g)),
        scratch_shapes=_attn_scratch(tq, 2 * LANES),
    )
    return pl.pallas_call(
        _diff_kernel,
        grid_spec=grid_spec,
        out_shape=jax.ShapeDtypeStruct((batch, seq, DIFF_HEADS * LANES), BF16),
        compiler_params=_cparams(("parallel", "parallel", "arbitrary")),
        name="diff",
    )(slopes, qkv4, qkv4, qkv4, lam_params, subln_g)


def _memkv_kernel(mem_ref, w_ref, o_ref):
    o_ref[...] = jnp.dot(mem_ref[...].astype(BF16), w_ref[...],
                         preferred_element_type=F32).astype(o_ref.dtype)


def _memkv(mem, w_kv):
    batch = mem.shape[0]
    width = w_kv.shape[1]
    return pl.pallas_call(
        _memkv_kernel,
        grid=(batch,),
        in_specs=[
            pl.BlockSpec((None, N_MEM, D_MODEL), lambda b: (b, 0, 0)),
            pl.BlockSpec((D_MODEL, width), lambda b: (0, 0)),
        ],
        out_specs=pl.BlockSpec((None, N_MEM, width), lambda b: (b, 0, 0)),
        out_shape=jax.ShapeDtypeStruct((batch, N_MEM, width), BF16),
        compiler_params=_cparams(("parallel",)),
        name="mem_kv",
    )(mem, w_kv)


def _mem_kernel(q_ref, mk_ref, mv_ref, o_ref):
    s = lax.dot_general(q_ref[...], mk_ref[...], (((1,), (1,)), ((), ())),
                        preferred_element_type=F32)
    m = jnp.max(s, axis=1, keepdims=True)
    p = jnp.exp2(s - m)
    l = jnp.sum(p, axis=1, keepdims=True)
    o = jnp.dot(p.astype(BF16), mv_ref[...], preferred_element_type=F32)
    o_ref[...] = (o / l).astype(o_ref.dtype)


def _mem_attn(qkv4, mkv, batch, seq):
    tq = MEM_TQ
    return pl.pallas_call(
        _mem_kernel,
        grid=(batch, MEM_HEADS, seq // tq),
        in_specs=[
            pl.BlockSpec((None, None, tq, LANES), lambda b, h, i: (SLAB_MEM_Q + h, b, i, 0)),
            pl.BlockSpec((None, N_MEM, LANES), lambda b, h, i: (b, 0, h)),
            pl.BlockSpec((None, N_MEM, LANES), lambda b, h, i: (b, 0, MEM_HEADS + h)),
        ],
        out_specs=pl.BlockSpec((None, tq, LANES), lambda b, h, i: (b, i, h)),
        out_shape=jax.ShapeDtypeStruct((batch, seq, MEM_HEADS * LANES), BF16),
        compiler_params=_cparams(("parallel", "parallel", "parallel")),
        name="mem",
    )(qkv4, mkv, mkv)


def _layer_norm(z, g, b):
    mu = jnp.mean(z, axis=1, keepdims=True)
    zc = z - mu
    var = jnp.mean(zc * zc, axis=1, keepdims=True)
    return zc * lax.rsqrt(var + LN_EPS) * g + b


def _lane_max(v):
    return jnp.max(v, axis=1, keepdims=True)


def _routing_info(logits):
    lane = lax.broadcasted_iota(jnp.int32, logits.shape, 1)
    big = jnp.int32(2 * LANES)
    is_g = lane < N_GROUPS
    gl = jnp.where(is_g, logits, NEG_BIG)
    gmax = _lane_max(gl)
    g_w = 1.0 / jnp.sum(jnp.exp(gl - gmax), axis=1, keepdims=True)
    g_idx = jnp.min(jnp.where(gl == gmax, lane, big), axis=1, keepdims=True)
    lo = ROUTE_LANE0 + g_idx * EXPERTS_PER_GROUP
    in_grp = (lane >= lo) & (lane < lo + EXPERTS_PER_GROUP)
    el = jnp.where(in_grp, logits, NEG_BIG)
    e1 = _lane_max(el)
    i1 = jnp.min(jnp.where(el == e1, lane, big), axis=1, keepdims=True)
    el2 = jnp.where(lane == i1, NEG_BIG, el)
    e2 = _lane_max(el2)
    i2 = jnp.min(jnp.where(el2 == e2, lane, big), axis=1, keepdims=True)
    r = jnp.exp(e2 - e1)
    w1 = g_w / (1.0 + r)
    w2 = g_w * r / (1.0 + r)
    id1 = (i1 - ROUTE_LANE0).astype(F32)
    id2 = (i2 - ROUTE_LANE0).astype(F32)
    return jnp.where(lane == RI_E1, id1,
                     jnp.where(lane == RI_E2, id2,
                               jnp.where(lane == RI_W1, w1, jnp.where(lane == RI_W2, w2, 0.0))))


def _merge_kernel(x_ref, yf_ref, yd_ref, ym_ref, wg_ref, bg_ref, wbf_ref, wbd_ref, wbm_ref,
                  wo_ref, g1_ref, b1_ref, wr_ref, br_ref, x1_ref, ri_ref, rit_ref):
    x = x_ref[...]
    xb = x.astype(BF16)
    h = None
    for i, (y_ref, wb_ref) in enumerate(((yf_ref, wbf_ref), (yd_ref, wbd_ref), (ym_ref, wbm_ref))):
        cols = slice(i * D_MODEL, (i + 1) * D_MODEL)
        gl = jnp.dot(xb, wg_ref[:, cols], preferred_element_type=F32) + bg_ref[:, cols]
        gate = 1.0 / (1.0 + jnp.exp(-gl))
        br = jnp.dot(y_ref[...], wb_ref[...], preferred_element_type=F32)
        h = gate * br if h is None else h + gate * br
    o = jnp.dot(h.astype(BF16), wo_ref[...], preferred_element_type=F32)
    x1 = _layer_norm(DEEPNORM_ALPHA * x + o, g1_ref[...], b1_ref[...])
    x1_ref[...] = x1
    logits = jnp.dot(x1.astype(BF16), wr_ref[...], preferred_element_type=F32) + br_ref[...]
    ri = _routing_info(logits)
    ri_ref[...] = ri
    rit_ref[...] = ri.T[0:8, :]


def _const_spec(shape):
    return pl.BlockSpec(shape, lambda i: (0,) * len(shape), pipeline_mode=pl.Buffered(1))


def _merge(x2, yf, yd, ym, wg, bg, wbf, wbd, wbm, wo, g1, b1, wr, br):
    t = x2.shape[0]
    tm = MERGE_TM
    half = yf.shape[1]
    row = lambda w: pl.BlockSpec((tm, w), lambda i: (i, 0))
    return pl.pallas_call(
        _merge_kernel,
        grid=(t // tm,),
        in_specs=[
            row(D_MODEL), row(half), row(half), row(half),
            _const_spec((D_MODEL, N_BRANCH * D_MODEL)), _const_spec((1, N_BRANCH * D_MODEL)),
            _const_spec((half, D_MODEL)), _const_spec((half, D_MODEL)), _const_spec((half, D_MODEL)),
            _const_spec((D_MODEL, D_MODEL)), _const_spec((1, D_MODEL)), _const_spec((1, D_MODEL)),
            _const_spec((D_MODEL, LANES)), _const_spec((1, LANES)),
        ],
        out_specs=[row(D_MODEL), row(LANES), pl.BlockSpec((8, tm), lambda i: (0, i))],
        out_shape=[
            jax.ShapeDtypeStruct((t, D_MODEL), F32),
            jax.ShapeDtypeStruct((t, LANES), F32),
            jax.ShapeDtypeStruct((8, t), F32),
        ],
        compiler_params=_cparams(("parallel",)),
        name="merge",
    )(x2, yf, yd, ym, wg, bg, wbf, wbd, wbm, wo, g1, b1, wr, br)


def _cumsum(x, axis):
    n = x.shape[axis]
    idx = lax.broadcasted_iota(jnp.int32, x.shape, axis)
    sh = 1
    while sh < n:
        x = x + jnp.where(idx >= sh, pltpu.roll(x, sh, axis=axis), 0.0)
        sh *= 2
    return x


def _route_kernel(rit_ref, pos_ref, tiles_ref):
    s = rit_ref.shape[1]
    e1 = rit_ref[RI_E1:RI_E1 + 1, :]
    e2 = rit_ref[RI_E2:RI_E2 + 1, :]
    sub = lax.broadcasted_iota(jnp.int32, (N_EXPERTS, s), 0).astype(F32)
    oh1 = jnp.where(sub == e1, 1.0, 0.0)
    oh2 = jnp.where(sub == e2, 1.0, 0.0)
    cnt = oh1 + oh2
    incl = _cumsum(cnt, 1)
    before = incl - cnt
    total = jnp.broadcast_to(incl[:, s - 1:s], (N_EXPERTS, LANES))
    aligned = jnp.floor((total + (SEG_ALIGN - 1)) * (1.0 / SEG_ALIGN)) * SEG_ALIGN
    off = _cumsum(aligned, 0) - aligned
    slot = before + off[:, 0:1]
    pos1 = jnp.sum(oh1 * slot, axis=0, keepdims=True)
    pos2 = jnp.sum(oh2 * slot, axis=0, keepdims=True)
    pos_ref[...] = jnp.concatenate([pos1, pos2], axis=0).astype(jnp.int32)

    nt = jnp.floor((total + (MOE_R - 1)) * (1.0 / MOE_R))
    ct_incl = _cumsum(nt, 0)
    ct_excl = ct_incl - nt
    sub_l = lax.broadcasted_iota(jnp.int32, (N_EXPERTS, LANES), 0).astype(F32)
    tile_i = lax.broadcasted_iota(jnp.int32, (N_EXPERTS, LANES), 1).astype(F32)
    te = jnp.sum(jnp.where(ct_incl <= tile_i, 1.0, 0.0), axis=0, keepdims=True)
    te = jnp.minimum(te, N_EXPERTS - 1.0)
    sel = jnp.where(sub_l == te, 1.0, 0.0)
    k_in = tile_i - ct_excl
    start = jnp.sum(sel * (off + k_in * MOE_R), axis=0, keepdims=True)
    nval = jnp.sum(sel * jnp.clip(total - k_in * MOE_R, 0.0, float(MOE_R)), axis=0, keepdims=True)
    active = tile_i[0:1, :] < ct_incl[N_EXPERTS - 1:N_EXPERTS, :]
    start = jnp.where(active, start, 0.0)
    nval = jnp.where(active, nval, 0.0)
    rows = jnp.concatenate([te, start, nval, jnp.zeros((5, LANES), F32)], axis=0)
    tiles_ref[...] = rows.astype(jnp.int32)


def _route(rit, batch, seq):
    return pl.pallas_call(
        _route_kernel,
        grid=(batch,),
        in_specs=[pl.BlockSpec((8, seq), lambda b: (0, b))],
        out_specs=[
            pl.BlockSpec((None, 2, seq), lambda b: (b, 0, 0)),
            pl.BlockSpec((None, 8, LANES), lambda b: (b, 0, 0)),
        ],
        out_shape=[
            jax.ShapeDtypeStruct((batch, 2, seq), jnp.int32),
            jax.ShapeDtypeStruct((batch, 8, LANES), jnp.int32),
        ],
        compiler_params=_cparams(("parallel",)),
        name="route",
    )(rit)


def _moe_kernel(nd, nt, texp_ref, tstart_ref, tnval_ref, x1_ref, p1_ref, p2_ref, ri_ref,
                wg_ref, wu_ref, wd_ref, g2_ref, b2_ref, o_ref, buf, a_scr, b_scr):
    c = pl.program_id(0)
    j = pl.program_id(1)
    td = x1_ref.shape[0]

    @pl.when(j == 0)
    def _():
        buf[...] = jnp.zeros(buf.shape, F32)

    @pl.when(j < nd)
    def _():
        def body(t, carry):
            row = x1_ref[pl.ds(t, 1), :]
            buf[pl.ds(p1_ref[0, t], 1), :] = row
            buf[pl.ds(p2_ref[0, t], 1), :] = row
            return carry

        lax.fori_loop(0, td, body, 0, unroll=8)

    @pl.when((j >= nd) & (j < nd + nt))
    def _():
        idx = c * nt + (j - nd)
        nval = tnval_ref[idx]

        @pl.when(nval > 0)
        def _():
            start = pl.multiple_of(tstart_ref[idx], SEG_ALIGN)
            xt = buf[pl.ds(start, MOE_R), :]
            xb = xt.astype(BF16)
            g = jnp.dot(xb, wg_ref[...].astype(BF16), preferred_element_type=F32)
            u = jnp.dot(xb, wu_ref[...].astype(BF16), preferred_element_type=F32)
            hcat = (g / (1.0 + jnp.exp(-g))) * u
            y = jnp.dot(hcat.astype(BF16), wd_ref[...].astype(BF16), preferred_element_type=F32)
            rows = lax.broadcasted_iota(jnp.int32, (MOE_R, 1), 0)
            buf[pl.ds(start, MOE_R), :] = jnp.where(rows < nval, y, xt)

    @pl.when(j >= nd + nt)
    def _():
        def body(t, carry):
            a_scr[pl.ds(t, 1), :] = buf[pl.ds(p1_ref[0, t], 1), :]
            b_scr[pl.ds(t, 1), :] = buf[pl.ds(p2_ref[0, t], 1), :]
            return carry

        lax.fori_loop(0, td, body, 0, unroll=8)
        ri = ri_ref[...]
        m = ri[:, RI_W1:RI_W1 + 1] * a_scr[...] + ri[:, RI_W2:RI_W2 + 1] * b_scr[...]
        z = DEEPNORM_ALPHA * x1_ref[...] + m
        o_ref[...] = _layer_norm(z, g2_ref[...], b2_ref[...])


def _moe(x1, pos3, ri, texp, tstart, tnval, wg, wu, wd, g2, b2, batch, seq):
    t = x1.shape[0]
    td = MOE_TD
    nd = seq // td
    nt = 2 * seq // MOE_R + N_EXPERTS
    buf_rows = 2 * seq + N_EXPERTS * SEG_ALIGN + MOE_R

    def tok_tile(j):
        return jnp.where(j < nd, j, jnp.maximum(j - nd - nt, 0))

    def tile_expert(c, j, texp_ref):
        return texp_ref[c * nt + jnp.clip(j - nd, 0, nt - 1)]

    tok_spec = lambda w: pl.BlockSpec((td, w), lambda c, j, *_: (c * nd + tok_tile(j), 0))
    pos_spec = lambda k: pl.BlockSpec((None, 1, td), lambda c, j, *_: ((c * 2 + k) * nd + tok_tile(j), 0, 0),
                                      memory_space=pltpu.SMEM)
    w_spec = lambda a, b_: pl.BlockSpec((None, a, b_), lambda c, j, te, *_: (tile_expert(c, j, te), 0, 0))
    vec_spec = pl.BlockSpec((1, D_MODEL), lambda c, j, *_: (0, 0))
    grid_spec = pltpu.PrefetchScalarGridSpec(
        num_scalar_prefetch=3,
        grid=(batch, nd + nt + nd),
        in_specs=[
            tok_spec(D_MODEL), pos_spec(0), pos_spec(1), tok_spec(LANES),
            w_spec(D_MODEL, D_EXPERT), w_spec(D_MODEL, D_EXPERT), w_spec(D_EXPERT, D_MODEL),
            vec_spec, vec_spec,
        ],
        out_specs=pl.BlockSpec((td, D_MODEL),
                               lambda c, j, *_: (c * nd + jnp.maximum(j - nd - nt, 0), 0)),
        scratch_shapes=[
            pltpu.VMEM((buf_rows, D_MODEL), F32),
            pltpu.VMEM((td, D_MODEL), F32),
            pltpu.VMEM((td, D_MODEL), F32),
        ],
    )
    return pl.pallas_call(
        functools.partial(_moe_kernel, nd, nt),
        grid_spec=grid_spec,
        out_shape=jax.ShapeDtypeStruct((t, D_MODEL), F32),
        compiler_params=_cparams(("arbitrary", "arbitrary")),
        name="moe",
    )(texp, tstart, tnval, x1, pos3, pos3, ri, wg, wu, wd, g2, b2)


def _qkv_col_scale():
    ones = lambda n: jnp.ones((n,), F32)
    att = FOX_HEADS * HEAD_DIM
    qs = lambda d: jnp.full((att,), LOG2E * d ** -0.5, F32)
    return jnp.concatenate([qs(HEAD_DIM), ones(2 * att), qs(HEAD_DIM), ones(2 * att),
                            qs(MEM_HEAD_DIM)]).reshape(1, QKV_COLS)


def kernel(x, mem, w_in, b_forget, b_gates, lambda_q1, lambda_k1, lambda_q2, lambda_k2, diff_subln_g,
           w_mem_kv, w_branch_fox, w_branch_diff, w_branch_mem, w_out, ln1_g, ln1_b, w_router_group,
           b_router_group, w_router_expert, b_router_expert, w_expert_gate, w_expert_up, w_expert_down,
           ln2_g, ln2_b):
    batch, seq, d = x.shape
    t = batch * seq
    l = 0
    x2 = x.reshape(t, d)

    w_in_l = w_in[l]
    w_qkv = w_in_l[:, :QKV_COLS].astype(BF16)
    w_gates = w_in_l[:, GATE_COL0:FL_COL0].astype(BF16)
    w_fl_t = jnp.pad(w_in_l[:, FL_COL0:].T, ((0, 16 - FOX_HEADS), (0, 0))).astype(BF16)
    w_r = jnp.concatenate([w_router_group[l], w_router_expert[l]], axis=1)
    w_r = jnp.pad(w_r, ((0, 0), (0, LANES - w_r.shape[1]))).astype(BF16)
    b_r = jnp.concatenate([b_router_group[l], b_router_expert[l]])
    b_r = jnp.pad(b_r, (0, LANES - b_r.shape[0])).reshape(1, LANES)
    lam_params = jnp.stack([lambda_q1[l], lambda_k1[l], lambda_q2[l], lambda_k2[l]])
    slopes = 2.0 ** (-8.0 * jnp.arange(1, DIFF_HEADS + 1, dtype=F32) / DIFF_HEADS)

    qkv, fl_t = _proj(x2, w_qkv, w_fl_t, _qkv_col_scale())
    qkv4 = qkv.reshape(N_SLABS, batch, seq, LANES)
    c = _fscan(fl_t, b_forget[l].reshape(FOX_HEADS, 1), batch, seq)
    c4 = c.reshape(FOX_HEADS // 2, 2, t)

    y_fox = _fox(qkv4, c4, batch, seq)
    y_diff = _diff(qkv4, slopes, lam_params, diff_subln_g[l].reshape(1, LANES), batch, seq)
    mkv = _memkv(mem, w_mem_kv[l].astype(BF16))
    y_mem = _mem_attn(qkv4, mkv, batch, seq)

    x1, ri, rit = _merge(
        x2, y_fox.reshape(t, -1), y_diff.reshape(t, -1), y_mem.reshape(t, -1),
        w_gates, b_gates[l].reshape(1, -1),
        w_branch_fox[l].astype(BF16), w_branch_diff[l].astype(BF16), w_branch_mem[l].astype(BF16),
        w_out[l].astype(BF16), ln1_g[l].reshape(1, d), ln1_b[l].reshape(1, d), w_r, b_r)

    pos, tiles = _route(rit, batch, seq)
    nt = 2 * seq // MOE_R + N_EXPERTS
    texp, tstart, tnval = (tiles[:, r, :nt].reshape(-1) for r in range(3))
    pos3 = pos.reshape(batch * 2 * (seq // MOE_TD), 1, MOE_TD)
    out = _moe(x1, pos3, ri, texp, tstart, tnval,
               w_expert_gate[l], w_expert_up[l], w_expert_down[l],
               ln2_g[l].reshape(1, d), ln2_b[l].reshape(1, d), batch, seq)
    return out.reshape(batch, seq, d)
```

```python
import functools
import math

import jax
import jax.numpy as jnp
from jax import lax
from jax.experimental import pallas as pl
from jax.experimental.pallas import tpu as pltpu

F32 = jnp.float32
BF16 = jnp.bfloat16

D_MODEL = 1024
HEAD_DIM = 64
FOX_HEADS = 8
DIFF_HEADS = 4
MEM_HEADS = 4
MEM_HEAD_DIM = 128
N_MEM = 256
N_BRANCH = 3
N_GROUPS = 4
EXPERTS_PER_GROUP = 8
N_EXPERTS = N_GROUPS * EXPERTS_PER_GROUP
D_EXPERT = 256
LN_EPS = 1e-5
DEPTH = 1
DEEPNORM_ALPHA = (2.0 * DEPTH) ** 0.25
LAM_INIT = 0.8 - 0.6 * math.exp(-0.3 * 0)
LOG2E = math.log2(math.e)

LANES = 128
QKV_COLS = 3584
N_SLABS = QKV_COLS // LANES
GATE_COL0 = QKV_COLS
FL_COL0 = QKV_COLS + N_BRANCH * D_MODEL
SLAB_FOX_Q, SLAB_FOX_K, SLAB_FOX_V = 0, 4, 8
SLAB_DIFF_Q, SLAB_DIFF_K, SLAB_DIFF_V = 12, 16, 20
SLAB_MEM_Q = 24
ROUTE_LANE0 = N_GROUPS

NEG_BIG = -1e30
VMEM_LIMIT = 56 * 1024 * 1024

PROJ_TM = 1024
PROJ_TN = 512
ATT_T = 256
N_CHAINS = 4
NSL = N_CHAINS // 2
MEM_TQ = 1024
MERGE_TM = 512
MOE_R = 256
MOE_TD = 256
SEG_ALIGN = 8
RI_E1, RI_E2, RI_W1, RI_W2 = 0, 1, 2, 3


def _cparams(sem, flags=None):
    return pltpu.CompilerParams(dimension_semantics=sem, vmem_limit_bytes=VMEM_LIMIT, flags=flags)


ATT_FLAGS = None


def _proj_kernel(x_ref, w_ref, wfl_ref, sc_ref, o_ref, fl_ref, xb_ref):
    j = pl.program_id(1)

    @pl.when(j == 0)
    def _():
        xb = x_ref[...].astype(BF16)
        xb_ref[...] = xb
        fl_ref[...] = lax.dot_general(wfl_ref[...], xb, (((1,), (1,)), ((), ())),
                                      preferred_element_type=F32)

    acc = jnp.dot(xb_ref[...], w_ref[...].astype(BF16), preferred_element_type=F32) * sc_ref[...]
    for c in range(PROJ_TN // LANES):
        o_ref[c] = acc[:, c * LANES:(c + 1) * LANES].astype(BF16)


def _proj(x2, w_in, w_fl_t, col_scale):
    t = x2.shape[0]
    spc = PROJ_TN // LANES
    return pl.pallas_call(
        _proj_kernel,
        grid=(t // PROJ_TM, QKV_COLS // PROJ_TN),
        in_specs=[
            pl.BlockSpec((PROJ_TM, D_MODEL), lambda i, j: (i, 0)),
            pl.BlockSpec((None, D_MODEL, PROJ_TN), lambda i, j: (0, 0, j)),
            pl.BlockSpec((16, D_MODEL), lambda i, j: (0, 0)),
            pl.BlockSpec((1, PROJ_TN), lambda i, j: (0, j)),
        ],
        out_specs=[
            pl.BlockSpec((spc, PROJ_TM, LANES), lambda i, j: (j, i, 0)),
            pl.BlockSpec((16, PROJ_TM), lambda i, j: (0, i)),
        ],
        out_shape=[
            jax.ShapeDtypeStruct((N_SLABS, t, LANES), BF16),
            jax.ShapeDtypeStruct((16, t), F32),
        ],
        scratch_shapes=[pltpu.VMEM((PROJ_TM, D_MODEL), BF16)],
        compiler_params=_cparams(("parallel", "arbitrary")),
        name="proj",
    )(x2, w_in, w_fl_t, col_scale)


def _fscan_kernel(fl_ref, b_ref, c_ref):
    s = fl_ref.shape[1]
    z = fl_ref[0:FOX_HEADS, :] + b_ref[...]
    lf = jnp.minimum(z, 0.0) - jnp.log(1.0 + jnp.exp(-jnp.abs(z)))
    lane = lax.broadcasted_iota(jnp.int32, lf.shape, 1)
    sh = 1
    while sh < s:
        r = pltpu.roll(lf, sh, axis=1)
        lf = lf + jnp.where(lane >= sh, r, 0.0)
        sh *= 2
    c_ref[...] = lf * LOG2E


def _fscan(fl_t, b_forget_col, batch, seq):
    t = fl_t.shape[1]
    return pl.pallas_call(
        _fscan_kernel,
        grid=(batch,),
        in_specs=[
            pl.BlockSpec((16, seq), lambda b: (0, b)),
            pl.BlockSpec((FOX_HEADS, 1), lambda b: (0, 0)),
        ],
        out_specs=pl.BlockSpec((FOX_HEADS, seq), lambda b: (0, b)),
        out_shape=jax.ShapeDtypeStruct((FOX_HEADS, t), F32),
        compiler_params=_cparams(("parallel",)),
        name="fscan",
    )(fl_t, b_forget_col)


def _flash_chains(qi, tq, qh_scr, k_fn, vt_fn, bias_fn, m_scr, l_scr, acc_scr, s_scr, pm_scr, p_scr):
    m_scr[...] = jnp.full(m_scr.shape, NEG_BIG, F32)
    acc_scr[...] = jnp.zeros(acc_scr.shape, F32)
    if l_scr is not None:
        l_scr[...] = jnp.zeros(l_scr.shape, F32)
    rep = tq // LANES

    def scores(j, slot, masked):
        start = pl.multiple_of(j * tq, tq)
        for c in range(N_CHAINS):
            s = lax.dot_general(k_fn(c, start), qh_scr[c], (((1,), (1,)), ((), ())),
                                preferred_element_type=F32)
            s = s + jnp.concatenate([bias_fn(c, start)] * rep, axis=1)
            if masked:
                kv = lax.broadcasted_iota(jnp.int32, (tq, tq), 0)
                q = lax.broadcasted_iota(jnp.int32, (tq, tq), 1)
                s = jnp.where(kv <= q, s, NEG_BIG)
            s_scr[slot, c] = s
            pm_scr[slot, c] = jnp.max(s.reshape(tq // 8, 8, tq), axis=0)

    def consume(j, slot):
        start = pl.multiple_of(j * tq, tq)
        alphas = []
        for c in range(N_CHAINS):
            m_prev = m_scr[c]
            m_new = jnp.maximum(m_prev, jnp.max(pm_scr[slot, c], axis=0, keepdims=True))
            alpha = jnp.exp2(m_prev - m_new)
            p = jnp.exp2(s_scr[slot, c] - m_new)
            p_scr[c] = p.astype(BF16)
            if l_scr is not None:
                psum = jnp.sum(jnp.sum(p.reshape(tq // 8, 8, tq), axis=0), axis=0, keepdims=True)
                l_scr[c] = alpha * l_scr[c] + psum
            m_scr[c] = m_new
            alphas.append(alpha)
        for c in range(N_CHAINS):
            pv = jnp.dot(vt_fn(c, start), p_scr[c], preferred_element_type=F32)
            acc_scr[c] = acc_scr[c] * alphas[c] + pv

    @pl.when(qi == 0)
    def _():
        scores(0, 0, True)
        consume(0, 0)

    @pl.when(qi > 0)
    def _():
        scores(0, 0, False)

        def body(jj, carry):
            j = 2 * jj
            scores(j + 1, 1, False)
            consume(j, 0)
            scores(j + 2, 0, False)
            consume(j + 1, 1)
            return carry

        lax.fori_loop(0, (qi - 1) // 2, body, 0)

        @pl.when(qi % 2 == 1)
        def _():
            scores(qi, 1, True)
            consume(qi - 1, 0)
            consume(qi, 1)

        @pl.when(qi % 2 == 0)
        def _():
            scores(qi - 1, 1, False)
            consume(qi - 2, 0)
            scores(qi, 0, True)
            consume(qi - 1, 1)
            consume(qi, 0)


def _split_q(q_ref, qh_scr):
    for sl in range(NSL):
        q = q_ref[sl]
        lane = lax.broadcasted_iota(jnp.int32, q.shape, 1)
        zero = jnp.zeros_like(q)
        qh_scr[2 * sl] = jnp.where(lane < HEAD_DIM, q, zero)
        qh_scr[2 * sl + 1] = jnp.where(lane >= HEAD_DIM, q, zero)


def _transpose_bf16(a):
    return a.astype(F32).T.astype(BF16)


def _attn_scratch(tq, with_l):
    row = pltpu.VMEM((N_CHAINS, 1, tq), F32)
    return [
        pltpu.VMEM((N_CHAINS, tq, LANES), BF16),
        row,
    ] + ([row] if with_l else []) + [
        pltpu.VMEM((N_CHAINS, LANES, tq), F32),
        pltpu.VMEM((2, N_CHAINS, tq, tq), F32),
        pltpu.VMEM((2, N_CHAINS, 8, tq), F32),
        pltpu.VMEM((N_CHAINS, tq, tq), BF16),
    ]


def _fox_kernel(q_ref, k_ref, v_ref, c_ref, o_ref, qh_scr, m_scr, acc_scr, s_scr, pm_scr, p_scr, vat_scr,
                crep_scr):
    tq = q_ref.shape[1]
    seq = v_ref.shape[1]
    qi = pl.program_id(2)
    _split_q(q_ref, qh_scr)

    @pl.when(qi == 0)
    def _():
        for c in range(N_CHAINS):
            v = v_ref[c // 2]
            lane = lax.broadcasted_iota(jnp.int32, v.shape, 1)
            own = (lane < HEAD_DIM) if c % 2 == 0 else (lane >= HEAD_DIM)
            vat_scr[c] = _transpose_bf16(jnp.where(own, v, jnp.ones_like(v)))
            crow = c_ref[c // 2, c % 2:c % 2 + 1, :]
            crep_scr[c] = -jnp.broadcast_to(crow, (LANES, seq)).T

    def k_fn(c, start):
        return k_ref[c // 2, pl.ds(start, tq), :]

    def vt_fn(c, start):
        return vat_scr[c, :, pl.ds(start, tq)]

    def bias_fn(c, start):
        return crep_scr[c, pl.ds(start, tq), :]

    _flash_chains(qi, tq, qh_scr, k_fn, vt_fn, bias_fn, m_scr, None, acc_scr, s_scr, pm_scr, p_scr)
    for sl in range(NSL):
        a0 = acc_scr[2 * sl]
        a1 = acc_scr[2 * sl + 1]
        row = lax.broadcasted_iota(jnp.int32, a0.shape, 0)
        ot = jnp.where(row < HEAD_DIM, a0 / a0[HEAD_DIM:HEAD_DIM + 1, :], a1 / a1[0:1, :])
        o_ref[:, sl * LANES:(sl + 1) * LANES] = ot.T.astype(o_ref.dtype)


def _fox(qkv4, c4, batch, seq):
    tq = ATT_T
    return pl.pallas_call(
        _fox_kernel,
        grid=(batch, FOX_HEADS // N_CHAINS, seq // tq),
        in_specs=[
            pl.BlockSpec((NSL, None, tq, LANES), lambda b, g, i: (SLAB_FOX_Q // NSL + g, b, i, 0)),
            pl.BlockSpec((NSL, None, seq, LANES), lambda b, g, i: (SLAB_FOX_K // NSL + g, b, 0, 0)),
            pl.BlockSpec((NSL, None, seq, LANES), lambda b, g, i: (SLAB_FOX_V // NSL + g, b, 0, 0)),
            pl.BlockSpec((NSL, 2, seq), lambda b, g, i: (g, 0, b)),
        ],
        out_specs=pl.BlockSpec((None, tq, NSL * LANES), lambda b, g, i: (b, i, g)),
        out_shape=jax.ShapeDtypeStruct((batch, seq, FOX_HEADS * HEAD_DIM), BF16),
        scratch_shapes=_attn_scratch(tq, False) + [
            pltpu.VMEM((N_CHAINS, LANES, seq), BF16),
            pltpu.VMEM((N_CHAINS, seq, LANES), F32),
        ],
        compiler_params=_cparams(("parallel", "parallel", "arbitrary"), ATT_FLAGS),
        name="fox",
    )(qkv4, qkv4, qkv4, c4)


def _diff_kernel(slopes_ref, q_ref, k_ref, v_ref, lam_ref, g_ref, o_ref, qh_scr, m_scr, l_scr, acc_scr, s_scr,
                 pm_scr, p_scr, vat_scr, arep_scr):
    tq = q_ref.shape[1]
    seq = v_ref.shape[1]
    qi = pl.program_id(2)
    _split_q(q_ref, qh_scr)
    g = pl.program_id(1)

    @pl.when(qi == 0)
    def _():
        for hd in range(NSL):
            vat_scr[hd] = _transpose_bf16(v_ref[hd])
            pos = lax.broadcasted_iota(jnp.int32, (seq, LANES), 0).astype(F32)
            arep_scr[hd] = pos * (slopes_ref[NSL * g + hd] * LOG2E)

    def k_fn(c, start):
        return k_ref[c // 2, pl.ds(start, tq), :]

    def vt_fn(c, start):
        return vat_scr[c // 2, :, pl.ds(start, tq)]

    def bias_fn(c, start):
        return arep_scr[c // 2, pl.ds(start, tq), :]

    _flash_chains(qi, tq, qh_scr, k_fn, vt_fn, bias_fn, m_scr, l_scr, acc_scr, s_scr, pm_scr, p_scr)
    lp = lam_ref[...]
    s1 = jnp.sum(lp[0:1, :] * lp[1:2, :], axis=1, keepdims=True)
    s2 = jnp.sum(lp[2:3, :] * lp[3:4, :], axis=1, keepdims=True)
    lam = jnp.exp(s1) - jnp.exp(s2) + LAM_INIT
    for hd in range(NSL):
        ot = acc_scr[2 * hd] / l_scr[2 * hd] - lam * (acc_scr[2 * hd + 1] / l_scr[2 * hd + 1])
        ms = jnp.mean(ot * ot, axis=0, keepdims=True)
        yt = ot * lax.rsqrt(ms + LN_EPS) * g_ref[...]
        o_ref[:, hd * LANES:(hd + 1) * LANES] = (yt * (1.0 - LAM_INIT)).T.astype(o_ref.dtype)


def _diff(qkv4, slopes, lam_params, subln_g_col, batch, seq):
    tq = ATT_T
    grid_spec = pltpu.PrefetchScalarGridSpec(
        num_scalar_prefetch=1,
        grid=(batch, DIFF_HEADS // NSL, seq // tq),
        in_specs=[
            pl.BlockSpec((NSL, None, tq, LANES), lambda b, g, i, s: (SLAB_DIFF_Q // NSL + g, b, i, 0)),
            pl.BlockSpec((NSL, None, seq, LANES), lambda b, g, i, s: (SLAB_DIFF_K // NSL + g, b, 0, 0)),
            pl.BlockSpec((NSL, None, seq, LANES), lambda b, g, i, s: (SLAB_DIFF_V // NSL + g, b, 0, 0)),
            pl.BlockSpec((4, HEAD_DIM), lambda b, g, i, s: (0, 0)),
            pl.BlockSpec((LANES, 1), lambda b, g, i, s: (0, 0)),
        ],
        out_specs=pl.BlockSpec((None, tq, NSL * LANES), lambda b, g, i, s: (b, i, g)),
        scratch_shapes=_attn_scratch(tq, True) + [
            pltpu.VMEM((NSL, LANES, seq), BF16),
            pltpu.VMEM((NSL, seq, LANES), F32),
        ],
    )
    return pl.pallas_call(
        _diff_kernel,
        grid_spec=grid_spec,
        out_shape=jax.ShapeDtypeStruct((batch, seq, DIFF_HEADS * LANES), BF16),
        compiler_params=_cparams(("parallel", "parallel", "arbitrary"), ATT_FLAGS),
        name="diff",
    )(slopes, qkv4, qkv4, qkv4, lam_params, subln_g_col)


def _memkv_kernel(mem_ref, w_ref, o_ref):
    o_ref[...] = jnp.dot(mem_ref[...].astype(BF16), w_ref[...],
                         preferred_element_type=F32).astype(o_ref.dtype)


def _memkv(mem, w_kv):
    batch = mem.shape[0]
    width = w_kv.shape[1]
    return pl.pallas_call(
        _memkv_kernel,
        grid=(batch,),
        in_specs=[
            pl.BlockSpec((None, N_MEM, D_MODEL), lambda b: (b, 0, 0)),
            pl.BlockSpec((D_MODEL, width), lambda b: (0, 0)),
        ],
        out_specs=pl.BlockSpec((None, N_MEM, width), lambda b: (b, 0, 0)),
        out_shape=jax.ShapeDtypeStruct((batch, N_MEM, width), BF16),
        compiler_params=_cparams(("parallel",)),
        name="mem_kv",
    )(mem, w_kv)


def _mem_kernel(q_ref, mk_ref, mv_ref, o_ref):
    s = lax.dot_general(q_ref[...], mk_ref[...], (((1,), (1,)), ((), ())),
                        preferred_element_type=F32)
    m = jnp.max(s, axis=1, keepdims=True)
    p = jnp.exp2(s - m)
    l = jnp.sum(p, axis=1, keepdims=True)
    o = jnp.dot(p.astype(BF16), mv_ref[...], preferred_element_type=F32)
    o_ref[...] = (o / l).astype(o_ref.dtype)


def _mem_attn(qkv4, mkv, batch, seq):
    tq = MEM_TQ
    return pl.pallas_call(
        _mem_kernel,
        grid=(batch, MEM_HEADS, seq // tq),
        in_specs=[
            pl.BlockSpec((None, None, tq, LANES), lambda b, h, i: (SLAB_MEM_Q + h, b, i, 0)),
            pl.BlockSpec((None, N_MEM, LANES), lambda b, h, i: (b, 0, h)),
            pl.BlockSpec((None, N_MEM, LANES), lambda b, h, i: (b, 0, MEM_HEADS + h)),
        ],
        out_specs=pl.BlockSpec((None, tq, LANES), lambda b, h, i: (b, i, h)),
        out_shape=jax.ShapeDtypeStruct((batch, seq, MEM_HEADS * LANES), BF16),
        compiler_params=_cparams(("parallel", "parallel", "parallel")),
        name="mem",
    )(qkv4, mkv, mkv)


def _layer_norm(z, g, b):
    mu = jnp.mean(z, axis=1, keepdims=True)
    zc = z - mu
    var = jnp.mean(zc * zc, axis=1, keepdims=True)
    return zc * lax.rsqrt(var + LN_EPS) * g + b


def _lane_max(v):
    return jnp.max(v, axis=1, keepdims=True)


def _routing_info(logits):
    lane = lax.broadcasted_iota(jnp.int32, logits.shape, 1)
    big = jnp.int32(2 * LANES)
    is_g = lane < N_GROUPS
    gl = jnp.where(is_g, logits, NEG_BIG)
    gmax = _lane_max(gl)
    g_w = 1.0 / jnp.sum(jnp.exp(gl - gmax), axis=1, keepdims=True)
    g_idx = jnp.min(jnp.where(gl == gmax, lane, big), axis=1, keepdims=True)
    lo = ROUTE_LANE0 + g_idx * EXPERTS_PER_GROUP
    in_grp = (lane >= lo) & (lane < lo + EXPERTS_PER_GROUP)
    el = jnp.where(in_grp, logits, NEG_BIG)
    e1 = _lane_max(el)
    i1 = jnp.min(jnp.where(el == e1, lane, big), axis=1, keepdims=True)
    el2 = jnp.where(lane == i1, NEG_BIG, el)
    e2 = _lane_max(el2)
    i2 = jnp.min(jnp.where(el2 == e2, lane, big), axis=1, keepdims=True)
    r = jnp.exp(e2 - e1)
    w1 = g_w / (1.0 + r)
    w2 = g_w * r / (1.0 + r)
    id1 = (i1 - ROUTE_LANE0).astype(F32)
    id2 = (i2 - ROUTE_LANE0).astype(F32)
    return jnp.where(lane == RI_E1, id1,
                     jnp.where(lane == RI_E2, id2,
                               jnp.where(lane == RI_W1, w1, jnp.where(lane == RI_W2, w2, 0.0))))


def _merge_kernel(x_ref, yf_ref, yd_ref, ym_ref, wg_ref, bg_ref, wbf_ref, wbd_ref, wbm_ref,
                  wo_ref, g1_ref, b1_ref, wr_ref, br_ref, x1_ref, ri_ref, rit_ref):
    x = x_ref[...]
    xb = x.astype(BF16)
    h = None
    for i, (y_ref, wb_ref) in enumerate(((yf_ref, wbf_ref), (yd_ref, wbd_ref), (ym_ref, wbm_ref))):
        cols = slice(i * D_MODEL, (i + 1) * D_MODEL)
        gl = jnp.dot(xb, wg_ref[:, cols], preferred_element_type=F32) + bg_ref[:, cols]
        gate = 1.0 / (1.0 + jnp.exp(-gl))
        br = jnp.dot(y_ref[...], wb_ref[...], preferred_element_type=F32)
        h = gate * br if h is None else h + gate * br
    o = jnp.dot(h.astype(BF16), wo_ref[...], preferred_element_type=F32)
    x1 = _layer_norm(DEEPNORM_ALPHA * x + o, g1_ref[...], b1_ref[...])
    x1_ref[...] = x1
    logits = jnp.dot(x1.astype(BF16), wr_ref[...], preferred_element_type=F32) + br_ref[...]
    ri = _routing_info(logits)
    ri_ref[...] = ri
    rit_ref[...] = ri.T[0:8, :]


def _const_spec(shape):
    return pl.BlockSpec(shape, lambda i: (0,) * len(shape), pipeline_mode=pl.Buffered(1))


def _merge(x2, yf, yd, ym, wg, bg, wbf, wbd, wbm, wo, g1, b1, wr, br):
    t = x2.shape[0]
    tm = MERGE_TM
    half = yf.shape[1]
    row = lambda w: pl.BlockSpec((tm, w), lambda i: (i, 0))
    return pl.pallas_call(
        _merge_kernel,
        grid=(t // tm,),
        in_specs=[
            row(D_MODEL), row(half), row(half), row(half),
            _const_spec((D_MODEL, N_BRANCH * D_MODEL)), _const_spec((1, N_BRANCH * D_MODEL)),
            _const_spec((half, D_MODEL)), _const_spec((half, D_MODEL)), _const_spec((half, D_MODEL)),
            _const_spec((D_MODEL, D_MODEL)), _const_spec((1, D_MODEL)), _const_spec((1, D_MODEL)),
            _const_spec((D_MODEL, LANES)), _const_spec((1, LANES)),
        ],
        out_specs=[row(D_MODEL), row(LANES), pl.BlockSpec((8, tm), lambda i: (0, i))],
        out_shape=[
            jax.ShapeDtypeStruct((t, D_MODEL), F32),
            jax.ShapeDtypeStruct((t, LANES), F32),
            jax.ShapeDtypeStruct((8, t), F32),
        ],
        compiler_params=_cparams(("parallel",)),
        name="merge",
    )(x2, yf, yd, ym, wg, bg, wbf, wbd, wbm, wo, g1, b1, wr, br)


def _cumsum(x, axis):
    n = x.shape[axis]
    idx = lax.broadcasted_iota(jnp.int32, x.shape, axis)
    sh = 1
    while sh < n:
        x = x + jnp.where(idx >= sh, pltpu.roll(x, sh, axis=axis), 0.0)
        sh *= 2
    return x


def _route_kernel(rit_ref, pos_ref, tiles_ref):
    s = rit_ref.shape[1]
    e1 = rit_ref[RI_E1:RI_E1 + 1, :]
    e2 = rit_ref[RI_E2:RI_E2 + 1, :]
    sub = lax.broadcasted_iota(jnp.int32, (N_EXPERTS, s), 0).astype(F32)
    oh1 = jnp.where(sub == e1, 1.0, 0.0)
    oh2 = jnp.where(sub == e2, 1.0, 0.0)
    cnt = oh1 + oh2
    incl = _cumsum(cnt, 1)
    before = incl - cnt
    total = jnp.broadcast_to(incl[:, s - 1:s], (N_EXPERTS, LANES))
    aligned = jnp.floor((total + (SEG_ALIGN - 1)) * (1.0 / SEG_ALIGN)) * SEG_ALIGN
    off = _cumsum(aligned, 0) - aligned
    slot = before + off[:, 0:1]
    pos1 = jnp.sum(oh1 * slot, axis=0, keepdims=True)
    pos2 = jnp.sum(oh2 * slot, axis=0, keepdims=True)
    pos_ref[...] = jnp.concatenate([pos1, pos2], axis=0).astype(jnp.int32)

    nt = jnp.floor((total + (MOE_R - 1)) * (1.0 / MOE_R))
    ct_incl = _cumsum(nt, 0)
    ct_excl = ct_incl - nt
    sub_l = lax.broadcasted_iota(jnp.int32, (N_EXPERTS, LANES), 0).astype(F32)
    tile_i = lax.broadcasted_iota(jnp.int32, (N_EXPERTS, LANES), 1).astype(F32)
    te = jnp.sum(jnp.where(ct_incl <= tile_i, 1.0, 0.0), axis=0, keepdims=True)
    te = jnp.minimum(te, N_EXPERTS - 1.0)
    sel = jnp.where(sub_l == te, 1.0, 0.0)
    k_in = tile_i - ct_excl
    start = jnp.sum(sel * (off + k_in * MOE_R), axis=0, keepdims=True)
    nval = jnp.sum(sel * jnp.clip(total - k_in * MOE_R, 0.0, float(MOE_R)), axis=0, keepdims=True)
    active = tile_i[0:1, :] < ct_incl[N_EXPERTS - 1:N_EXPERTS, :]
    start = jnp.where(active, start, 0.0)
    nval = jnp.where(active, nval, 0.0)
    rows = jnp.concatenate([te, start, nval, jnp.zeros((5, LANES), F32)], axis=0)
    tiles_ref[...] = rows.astype(jnp.int32)


def _route(rit, batch, seq):
    return pl.pallas_call(
        _route_kernel,
        grid=(batch,),
        in_specs=[pl.BlockSpec((8, seq), lambda b: (0, b))],
        out_specs=[
            pl.BlockSpec((None, 2, seq), lambda b: (b, 0, 0)),
            pl.BlockSpec((None, 8, LANES), lambda b: (b, 0, 0)),
        ],
        out_shape=[
            jax.ShapeDtypeStruct((batch, 2, seq), jnp.int32),
            jax.ShapeDtypeStruct((batch, 8, LANES), jnp.int32),
        ],
        compiler_params=_cparams(("parallel",)),
        name="route",
    )(rit)


def _moe_kernel(nd, nt, texp_ref, tstart_ref, tnval_ref, x1_ref, p1_ref, p2_ref, ri_ref,
                wg_ref, wu_ref, wd_ref, g2_ref, b2_ref, o_ref, buf, a_scr, b_scr):
    c = pl.program_id(0)
    j = pl.program_id(1)
    td = x1_ref.shape[0]

    @pl.when(j == 0)
    def _():
        buf[...] = jnp.zeros(buf.shape, F32)

    @pl.when(j < nd)
    def _():
        def body(i, carry):
            base = pl.multiple_of(i * SEG_ALIGN, SEG_ALIGN)
            xv = x1_ref.at[pl.ds(base, SEG_ALIGN), :]
            for u in range(SEG_ALIGN):
                row = xv[u:u + 1, :]
                buf[pl.ds(p1_ref[0, base + u], 1), :] = row
                buf[pl.ds(p2_ref[0, base + u], 1), :] = row
            return carry

        lax.fori_loop(0, td // SEG_ALIGN, body, 0)

    @pl.when((j >= nd) & (j < nd + nt))
    def _():
        idx = c * nt + (j - nd)
        nval = tnval_ref[idx]

        @pl.when(nval > 0)
        def _():
            start = pl.multiple_of(tstart_ref[idx], SEG_ALIGN)
            xt = buf[pl.ds(start, MOE_R), :]
            xb = xt.astype(BF16)
            g = jnp.dot(xb, wg_ref[...].astype(BF16), preferred_element_type=F32)
            u = jnp.dot(xb, wu_ref[...].astype(BF16), preferred_element_type=F32)
            hcat = (g / (1.0 + jnp.exp(-g))) * u
            y = jnp.dot(hcat.astype(BF16), wd_ref[...].astype(BF16), preferred_element_type=F32)
            rows = lax.broadcasted_iota(jnp.int32, (MOE_R, 1), 0)
            buf[pl.ds(start, MOE_R), :] = jnp.where(rows < nval, y, xt)

    @pl.when(j >= nd + nt)
    def _():
        def body(i, carry):
            base = pl.multiple_of(i * SEG_ALIGN, SEG_ALIGN)
            av = a_scr.at[pl.ds(base, SEG_ALIGN), :]
            bv = b_scr.at[pl.ds(base, SEG_ALIGN), :]
            for u in range(SEG_ALIGN):
                av[u:u + 1, :] = buf[pl.ds(p1_ref[0, base + u], 1), :]
                bv[u:u + 1, :] = buf[pl.ds(p2_ref[0, base + u], 1), :]
            return carry

        lax.fori_loop(0, td // SEG_ALIGN, body, 0)
        ri = ri_ref[...]
        m = ri[:, RI_W1:RI_W1 + 1] * a_scr[...] + ri[:, RI_W2:RI_W2 + 1] * b_scr[...]
        z = DEEPNORM_ALPHA * x1_ref[...] + m
        o_ref[...] = _layer_norm(z, g2_ref[...], b2_ref[...])


def _moe(x1, pos3, ri, texp, tstart, tnval, wg, wu, wd, g2, b2, batch, seq):
    t = x1.shape[0]
    td = MOE_TD
    nd = seq // td
    nt = 2 * seq // MOE_R + N_EXPERTS
    buf_rows = 2 * seq + N_EXPERTS * SEG_ALIGN + MOE_R

    def tok_tile(j):
        return jnp.where(j < nd, j, jnp.maximum(j - nd - nt, 0))

    def tile_expert(c, j, texp_ref):
        return texp_ref[c * nt + jnp.clip(j - nd, 0, nt - 1)]

    tok_spec = lambda w: pl.BlockSpec((td, w), lambda c, j, *_: (c * nd + tok_tile(j), 0))
    pos_spec = lambda k: pl.BlockSpec((None, 1, td), lambda c, j, *_: ((c * 2 + k) * nd + tok_tile(j), 0, 0),
                                      memory_space=pltpu.SMEM)
    w_spec = lambda a, b_: pl.BlockSpec((None, a, b_), lambda c, j, te, *_: (tile_expert(c, j, te), 0, 0))
    vec_spec = pl.BlockSpec((1, D_MODEL), lambda c, j, *_: (0, 0))
    grid_spec = pltpu.PrefetchScalarGridSpec(
        num_scalar_prefetch=3,
        grid=(batch, nd + nt + nd),
        in_specs=[
            tok_spec(D_MODEL), pos_spec(0), pos_spec(1), tok_spec(LANES),
            w_spec(D_MODEL, D_EXPERT), w_spec(D_MODEL, D_EXPERT), w_spec(D_EXPERT, D_MODEL),
            vec_spec, vec_spec,
        ],
        out_specs=pl.BlockSpec((td, D_MODEL),
                               lambda c, j, *_: (c * nd + jnp.maximum(j - nd - nt, 0), 0)),
        scratch_shapes=[
            pltpu.VMEM((buf_rows, D_MODEL), F32),
            pltpu.VMEM((td, D_MODEL), F32),
            pltpu.VMEM((td, D_MODEL), F32),
        ],
    )
    return pl.pallas_call(
        functools.partial(_moe_kernel, nd, nt),
        grid_spec=grid_spec,
        out_shape=jax.ShapeDtypeStruct((t, D_MODEL), F32),
        compiler_params=_cparams(("arbitrary", "arbitrary")),
        name="moe",
    )(texp, tstart, tnval, x1, pos3, pos3, ri, wg, wu, wd, g2, b2)


def _qkv_col_scale():
    ones = lambda n: jnp.ones((n,), F32)
    att = FOX_HEADS * HEAD_DIM
    qs = lambda d: jnp.full((att,), LOG2E * d ** -0.5, F32)
    return jnp.concatenate([qs(HEAD_DIM), ones(2 * att), qs(HEAD_DIM), ones(2 * att),
                            qs(MEM_HEAD_DIM)]).reshape(1, QKV_COLS)


def kernel(x, mem, w_in, b_forget, b_gates, lambda_q1, lambda_k1, lambda_q2, lambda_k2, diff_subln_g,
           w_mem_kv, w_branch_fox, w_branch_diff, w_branch_mem, w_out, ln1_g, ln1_b, w_router_group,
           b_router_group, w_router_expert, b_router_expert, w_expert_gate, w_expert_up, w_expert_down,
           ln2_g, ln2_b):
    batch, seq, d = x.shape
    t = batch * seq
    l = 0
    x2 = x.reshape(t, d)

    w_gates = w_in[l, :, GATE_COL0:FL_COL0].astype(BF16)
    w_fl_t = jnp.pad(w_in[l, :, FL_COL0:].T, ((0, 16 - FOX_HEADS), (0, 0))).astype(BF16)
    w_r = jnp.concatenate([w_router_group[l], w_router_expert[l]], axis=1)
    w_r = jnp.pad(w_r, ((0, 0), (0, LANES - w_r.shape[1]))).astype(BF16)
    b_r = jnp.concatenate([b_router_group[l], b_router_expert[l]])
    b_r = jnp.pad(b_r, (0, LANES - b_r.shape[0])).reshape(1, LANES)
    lam_params = jnp.stack([lambda_q1[l], lambda_k1[l], lambda_q2[l], lambda_k2[l]])
    slopes = 2.0 ** (-8.0 * jnp.arange(1, DIFF_HEADS + 1, dtype=F32) / DIFF_HEADS)

    qkv, fl_t = _proj(x2, w_in, w_fl_t, _qkv_col_scale())
    qkv4 = qkv.reshape(N_SLABS, batch, seq, LANES)
    c = _fscan(fl_t, b_forget[l].reshape(FOX_HEADS, 1), batch, seq)
    c4 = c.reshape(FOX_HEADS // 2, 2, t)

    y_fox = _fox(qkv4, c4, batch, seq)
    y_diff = _diff(qkv4, slopes, lam_params, diff_subln_g[l].reshape(LANES, 1), batch, seq)
    mkv = _memkv(mem, w_mem_kv[l].astype(BF16))
    y_mem = _mem_attn(qkv4, mkv, batch, seq)

    x1, ri, rit = _merge(
        x2, y_fox.reshape(t, -1), y_diff.reshape(t, -1), y_mem.reshape(t, -1),
        w_gates, b_gates[l].reshape(1, -1),
        w_branch_fox[l].astype(BF16), w_branch_diff[l].astype(BF16), w_branch_mem[l].astype(BF16),
        w_out[l].astype(BF16), ln1_g[l].reshape(1, d), ln1_b[l].reshape(1, d), w_r, b_r)

    pos, tiles = _route(rit, batch, seq)
    nt = 2 * seq // MOE_R + N_EXPERTS
    texp, tstart, tnval = (tiles[:, r, :nt].reshape(-1) for r in range(3))
    pos3 = pos.reshape(batch * 2 * (seq // MOE_TD), 1, MOE_TD)
    out = _moe(x1, pos3, ri, texp, tstart, tnval,
               w_expert_gate[l], w_expert_up[l], w_expert_down[l],
               ln2_g[l].reshape(1, d), ln2_b[l].reshape(1, d), batch, seq)
    return out.reshape(batch, seq, d)
```

```python
import functools
import math

import jax
import jax.numpy as jnp
from jax import lax
from jax.experimental import pallas as pl
from jax.experimental.pallas import tpu as pltpu

F32 = jnp.float32
BF16 = jnp.bfloat16

D_MODEL = 1024
HEAD_DIM = 64
FOX_HEADS = 8
DIFF_HEADS = 4
MEM_HEADS = 4
MEM_HEAD_DIM = 128
N_MEM = 256
N_BRANCH = 3
N_GROUPS = 4
EXPERTS_PER_GROUP = 8
N_EXPERTS = N_GROUPS * EXPERTS_PER_GROUP
D_EXPERT = 256
LN_EPS = 1e-5
DEPTH = 1
DEEPNORM_ALPHA = (2.0 * DEPTH) ** 0.25
LAM_INIT = 0.8 - 0.6 * math.exp(-0.3 * 0)
LOG2E = math.log2(math.e)

LANES = 128
QKV_COLS = 3584
N_SLABS = QKV_COLS // LANES
GATE_COL0 = QKV_COLS
FL_COL0 = QKV_COLS + N_BRANCH * D_MODEL
SLAB_FOX_Q, SLAB_FOX_K, SLAB_FOX_V = 0, 4, 8
SLAB_DIFF_Q, SLAB_DIFF_K, SLAB_DIFF_V = 12, 16, 20
SLAB_MEM_Q = 24
ROUTE_LANE0 = N_GROUPS

NEG_BIG = -1e30
VMEM_LIMIT = 56 * 1024 * 1024

PROJ_TM = 1024
PROJ_TN = 512
ATT_T = 256
N_CHAINS = 4
NSL = N_CHAINS // 2
N_BIAS_LANES = 3
DIFF_ONES_ROWS = 16
SETUP_ROWS = 512
MEM_TQ = 1024
MERGE_TM = 512
MOE_R = 256
MOE_TD = 256
SEG_ALIGN = 8
RI_E1, RI_E2, RI_W1, RI_W2 = 0, 1, 2, 3


def _cparams(sem, flags=None):
    return pltpu.CompilerParams(dimension_semantics=sem, vmem_limit_bytes=VMEM_LIMIT, flags=flags)


ATT_FLAGS = None


def _proj_kernel(x_ref, w_ref, wfl_ref, sc_ref, o_ref, fl_ref, xb_ref):
    j = pl.program_id(1)

    @pl.when(j == 0)
    def _():
        xb = x_ref[...].astype(BF16)
        xb_ref[...] = xb
        fl_ref[...] = lax.dot_general(wfl_ref[...], xb, (((1,), (1,)), ((), ())),
                                      preferred_element_type=F32)

    acc = jnp.dot(xb_ref[...], w_ref[...].astype(BF16), preferred_element_type=F32) * sc_ref[...]
    for c in range(PROJ_TN // LANES):
        o_ref[c] = acc[:, c * LANES:(c + 1) * LANES].astype(BF16)


def _proj(x2, w_in, w_fl_t, col_scale):
    t = x2.shape[0]
    spc = PROJ_TN // LANES
    return pl.pallas_call(
        _proj_kernel,
        grid=(t // PROJ_TM, QKV_COLS // PROJ_TN),
        in_specs=[
            pl.BlockSpec((PROJ_TM, D_MODEL), lambda i, j: (i, 0)),
            pl.BlockSpec((None, D_MODEL, PROJ_TN), lambda i, j: (0, 0, j)),
            pl.BlockSpec((16, D_MODEL), lambda i, j: (0, 0)),
            pl.BlockSpec((1, PROJ_TN), lambda i, j: (0, j)),
        ],
        out_specs=[
            pl.BlockSpec((spc, PROJ_TM, LANES), lambda i, j: (j, i, 0)),
            pl.BlockSpec((16, PROJ_TM), lambda i, j: (0, i)),
        ],
        out_shape=[
            jax.ShapeDtypeStruct((N_SLABS, t, LANES), BF16),
            jax.ShapeDtypeStruct((16, t), F32),
        ],
        scratch_shapes=[pltpu.VMEM((PROJ_TM, D_MODEL), BF16)],
        compiler_params=_cparams(("parallel", "arbitrary")),
        name="proj",
    )(x2, w_in, w_fl_t, col_scale)


def _fscan_kernel(fl_ref, b_ref, c_ref):
    s = fl_ref.shape[1]
    z = fl_ref[0:FOX_HEADS, :] + b_ref[...]
    lf = jnp.minimum(z, 0.0) - jnp.log(1.0 + jnp.exp(-jnp.abs(z)))
    lane = lax.broadcasted_iota(jnp.int32, lf.shape, 1)
    sh = 1
    while sh < s:
        r = pltpu.roll(lf, sh, axis=1)
        lf = lf + jnp.where(lane >= sh, r, 0.0)
        sh *= 2
    c_ref[...] = lf * LOG2E


def _fscan(fl_t, b_forget_col, batch, seq):
    t = fl_t.shape[1]
    return pl.pallas_call(
        _fscan_kernel,
        grid=(batch,),
        in_specs=[
            pl.BlockSpec((16, seq), lambda b: (0, b)),
            pl.BlockSpec((FOX_HEADS, 1), lambda b: (0, 0)),
        ],
        out_specs=pl.BlockSpec((FOX_HEADS, seq), lambda b: (0, b)),
        out_shape=jax.ShapeDtypeStruct((FOX_HEADS, t), F32),
        compiler_params=_cparams(("parallel",)),
        name="fscan",
    )(fl_t, b_forget_col)


def _flash_chains(qi, tq, qh_scr, k_fn, vt_fn, m_scr, acc_scr, s_scr, pm_scr):
    m_scr[...] = jnp.full(m_scr.shape, NEG_BIG, F32)
    acc_scr[...] = jnp.zeros(acc_scr.shape, F32)

    def scores(j, slot, masked):
        start = pl.multiple_of(j * tq, tq)
        for c in range(N_CHAINS):
            s = lax.dot_general(k_fn(c, start), qh_scr[c], (((1,), (1,)), ((), ())),
                                preferred_element_type=F32)
            if masked:
                kv = lax.broadcasted_iota(jnp.int32, (tq, tq), 0)
                q = lax.broadcasted_iota(jnp.int32, (tq, tq), 1)
                s = jnp.where(kv <= q, s, NEG_BIG)
            s_scr[slot, c] = s
            pm_scr[slot, c] = jnp.max(s.reshape(tq // 8, 8, tq), axis=0)

    def consume(j, slot):
        start = pl.multiple_of(j * tq, tq)
        for c in range(N_CHAINS):
            m_prev = m_scr[c]
            m_new = jnp.maximum(m_prev, jnp.max(pm_scr[slot, c], axis=0, keepdims=True))
            alpha = jnp.exp2(m_prev - m_new)
            p = jnp.exp2(s_scr[slot, c] - m_new)
            m_scr[c] = m_new
            pv = jnp.dot(vt_fn(c, start), p.astype(BF16), preferred_element_type=F32)
            acc_scr[c] = acc_scr[c] * alpha + pv

    @pl.when(qi == 0)
    def _():
        scores(0, 0, True)
        consume(0, 0)

    @pl.when(qi > 0)
    def _():
        scores(0, 0, False)

        def body(jj, carry):
            j = 2 * jj
            scores(j + 1, 1, False)
            consume(j, 0)
            scores(j + 2, 0, False)
            consume(j + 1, 1)
            return carry

        lax.fori_loop(0, (qi - 1) // 2, body, 0)

        @pl.when(qi % 2 == 1)
        def _():
            scores(qi, 1, True)
            consume(qi - 1, 0)
            consume(qi, 1)

        @pl.when(qi % 2 == 0)
        def _():
            scores(qi - 1, 1, False)
            consume(qi - 2, 0)
            scores(qi, 0, True)
            consume(qi - 1, 1)
            consume(qi, 0)


def _own_and_spare(shape, c):
    lane = lax.broadcasted_iota(jnp.int32, shape, 1)
    own = (lane < HEAD_DIM) if c % 2 == 0 else (lane >= HEAD_DIM)
    spare = HEAD_DIM if c % 2 == 0 else 0
    return lane, own, spare


def _split_q(q_ref, qh_scr):
    for c in range(N_CHAINS):
        q = q_ref[c // 2].astype(F32)
        lane, own, spare = _own_and_spare(q.shape, c)
        ones = (lane >= spare) & (lane < spare + N_BIAS_LANES)
        qh_scr[c] = jnp.where(own, q, jnp.where(ones, 1.0, 0.0)).astype(BF16)


def _augment_k(k, bias, c):
    lane, own, spare = _own_and_spare(k.shape, c)
    hi = bias.astype(BF16).astype(F32)
    r1 = bias - hi
    mid = r1.astype(BF16).astype(F32)
    lo = (r1 - mid).astype(BF16).astype(F32)
    extra = jnp.where(lane == spare, hi, jnp.where(lane == spare + 1, mid, jnp.where(lane == spare + 2, lo, 0.0)))
    return jnp.where(own, k.astype(F32), extra).astype(BF16)


def _transpose_bf16(a):
    return a.astype(F32).T.astype(BF16)


def _attn_scratch(tq, seq, acc_rows, vt_slabs):
    return [
        pltpu.VMEM((N_CHAINS, tq, LANES), BF16),
        pltpu.VMEM((N_CHAINS, 1, tq), F32),
        pltpu.VMEM((N_CHAINS, acc_rows, tq), F32),
        pltpu.VMEM((2, N_CHAINS, tq, tq), F32),
        pltpu.VMEM((2, N_CHAINS, 8, tq), F32),
        pltpu.VMEM((N_CHAINS, seq, LANES), BF16),
        pltpu.VMEM((vt_slabs, acc_rows, seq), BF16),
    ]


def _fox_kernel(q_ref, k_ref, v_ref, c_ref, o_ref, qh_scr, m_scr, acc_scr, s_scr, pm_scr, ka_scr, vat_scr):
    tq = q_ref.shape[1]
    seq = v_ref.shape[1]
    qi = pl.program_id(2)
    _split_q(q_ref, qh_scr)

    @pl.when(qi == 0)
    def _():
        def chunk(i, carry):
            r0 = pl.multiple_of(i * SETUP_ROWS, SETUP_ROWS)
            rows = pl.ds(r0, SETUP_ROWS)
            head_row = lax.broadcasted_iota(jnp.int32, (LANES, SETUP_ROWS), 0) < HEAD_DIM
            for sl in range(NSL):
                vt = _transpose_bf16(v_ref[sl, rows, :])
                one = jnp.ones_like(vt)
                vat_scr[2 * sl, :, rows] = jnp.where(head_row, vt, one)
                vat_scr[2 * sl + 1, :, rows] = jnp.where(head_row, one, vt)
                cc = c_ref[sl, :, rows]
                stacked = jnp.where(head_row, jnp.broadcast_to(cc[1:2, :], (LANES, SETUP_ROWS)),
                                    jnp.broadcast_to(cc[0:1, :], (LANES, SETUP_ROWS)))
                bias = -stacked.T
                k = k_ref[sl, rows, :]
                ka_scr[2 * sl, rows, :] = _augment_k(k, bias, 0)
                ka_scr[2 * sl + 1, rows, :] = _augment_k(k, bias, 1)
            return carry

        lax.fori_loop(0, seq // SETUP_ROWS, chunk, 0)

    def k_fn(c, start):
        return ka_scr[c, pl.ds(start, tq), :]

    def vt_fn(c, start):
        return vat_scr[c, :, pl.ds(start, tq)]

    _flash_chains(qi, tq, qh_scr, k_fn, vt_fn, m_scr, acc_scr, s_scr, pm_scr)
    for sl in range(NSL):
        a0 = acc_scr[2 * sl]
        a1 = acc_scr[2 * sl + 1]
        row = lax.broadcasted_iota(jnp.int32, a0.shape, 0)
        ot = jnp.where(row < HEAD_DIM, a0 / a0[HEAD_DIM:HEAD_DIM + 1, :], a1 / a1[0:1, :])
        o_ref[:, sl * LANES:(sl + 1) * LANES] = ot.T.astype(o_ref.dtype)


def _fox(qkv4, c4, batch, seq):
    tq = ATT_T
    return pl.pallas_call(
        _fox_kernel,
        grid=(batch, FOX_HEADS // N_CHAINS, seq // tq),
        in_specs=[
            pl.BlockSpec((NSL, None, tq, LANES), lambda b, g, i: (SLAB_FOX_Q // NSL + g, b, i, 0)),
            pl.BlockSpec((NSL, None, seq, LANES), lambda b, g, i: (SLAB_FOX_K // NSL + g, b, 0, 0)),
            pl.BlockSpec((NSL, None, seq, LANES), lambda b, g, i: (SLAB_FOX_V // NSL + g, b, 0, 0)),
            pl.BlockSpec((NSL, 2, seq), lambda b, g, i: (g, 0, b)),
        ],
        out_specs=pl.BlockSpec((None, tq, NSL * LANES), lambda b, g, i: (b, i, g)),
        out_shape=jax.ShapeDtypeStruct((batch, seq, FOX_HEADS * HEAD_DIM), BF16),
        scratch_shapes=_attn_scratch(tq, seq, LANES, N_CHAINS),
        compiler_params=_cparams(("parallel", "parallel", "arbitrary"), ATT_FLAGS),
        name="fox",
    )(qkv4, qkv4, qkv4, c4)


def _diff_kernel(slopes_ref, q_ref, k_ref, v_ref, lam_ref, g_ref, o_ref, qh_scr, m_scr, acc_scr, s_scr, pm_scr,
                 ka_scr, vat_scr):
    tq = q_ref.shape[1]
    seq = v_ref.shape[1]
    qi = pl.program_id(2)
    _split_q(q_ref, qh_scr)
    g = pl.program_id(1)

    @pl.when(qi == 0)
    def _():
        def chunk(i, carry):
            r0 = pl.multiple_of(i * SETUP_ROWS, SETUP_ROWS)
            rows = pl.ds(r0, SETUP_ROWS)
            for hd in range(NSL):
                vt = _transpose_bf16(v_ref[hd, rows, :])
                vat_scr[hd, :, rows] = jnp.concatenate([vt, jnp.ones((DIFF_ONES_ROWS, SETUP_ROWS), BF16)], axis=0)
                pos = (lax.broadcasted_iota(jnp.int32, (SETUP_ROWS, LANES), 0) + r0).astype(F32)
                bias = pos * (slopes_ref[NSL * g + hd] * LOG2E)
                k = k_ref[hd, rows, :]
                ka_scr[2 * hd, rows, :] = _augment_k(k, bias, 0)
                ka_scr[2 * hd + 1, rows, :] = _augment_k(k, bias, 1)
            return carry

        lax.fori_loop(0, seq // SETUP_ROWS, chunk, 0)

    def k_fn(c, start):
        return ka_scr[c, pl.ds(start, tq), :]

    def vt_fn(c, start):
        return vat_scr[c // 2, :, pl.ds(start, tq)]

    _flash_chains(qi, tq, qh_scr, k_fn, vt_fn, m_scr, acc_scr, s_scr, pm_scr)
    lp = lam_ref[...]
    s1 = jnp.sum(lp[0:1, :] * lp[1:2, :], axis=1, keepdims=True)
    s2 = jnp.sum(lp[2:3, :] * lp[3:4, :], axis=1, keepdims=True)
    lam = jnp.exp(s1) - jnp.exp(s2) + LAM_INIT
    for hd in range(NSL):
        a1 = acc_scr[2 * hd]
        a2 = acc_scr[2 * hd + 1]
        ot = a1[:LANES, :] / a1[LANES:LANES + 1, :] - lam * (a2[:LANES, :] / a2[LANES:LANES + 1, :])
        ms = jnp.mean(ot * ot, axis=0, keepdims=True)
        yt = ot * lax.rsqrt(ms + LN_EPS) * g_ref[...]
        o_ref[:, hd * LANES:(hd + 1) * LANES] = (yt * (1.0 - LAM_INIT)).T.astype(o_ref.dtype)


def _diff(qkv4, slopes, lam_params, subln_g_col, batch, seq):
    tq = ATT_T
    grid_spec = pltpu.PrefetchScalarGridSpec(
        num_scalar_prefetch=1,
        grid=(batch, DIFF_HEADS // NSL, seq // tq),
        in_specs=[
            pl.BlockSpec((NSL, None, tq, LANES), lambda b, g, i, s: (SLAB_DIFF_Q // NSL + g, b, i, 0)),
            pl.BlockSpec((NSL, None, seq, LANES), lambda b, g, i, s: (SLAB_DIFF_K // NSL + g, b, 0, 0)),
            pl.BlockSpec((NSL, None, seq, LANES), lambda b, g, i, s: (SLAB_DIFF_V // NSL + g, b, 0, 0)),
            pl.BlockSpec((4, HEAD_DIM), lambda b, g, i, s: (0, 0)),
            pl.BlockSpec((LANES, 1), lambda b, g, i, s: (0, 0)),
        ],
        out_specs=pl.BlockSpec((None, tq, NSL * LANES), lambda b, g, i, s: (b, i, g)),
        scratch_shapes=_attn_scratch(tq, seq, LANES + DIFF_ONES_ROWS, NSL),
    )
    return pl.pallas_call(
        _diff_kernel,
        grid_spec=grid_spec,
        out_shape=jax.ShapeDtypeStruct((batch, seq, DIFF_HEADS * LANES), BF16),
        compiler_params=_cparams(("parallel", "parallel", "arbitrary"), ATT_FLAGS),
        name="diff",
    )(slopes, qkv4, qkv4, qkv4, lam_params, subln_g_col)


def _memkv_kernel(mem_ref, w_ref, o_ref):
    o_ref[...] = jnp.dot(mem_ref[...].astype(BF16), w_ref[...],
                         preferred_element_type=F32).astype(o_ref.dtype)


def _memkv(mem, w_kv):
    batch = mem.shape[0]
    width = w_kv.shape[1]
    return pl.pallas_call(
        _memkv_kernel,
        grid=(batch,),
        in_specs=[
            pl.BlockSpec((None, N_MEM, D_MODEL), lambda b: (b, 0, 0)),
            pl.BlockSpec((D_MODEL, width), lambda b: (0, 0)),
        ],
        out_specs=pl.BlockSpec((None, N_MEM, width), lambda b: (b, 0, 0)),
        out_shape=jax.ShapeDtypeStruct((batch, N_MEM, width), BF16),
        compiler_params=_cparams(("parallel",)),
        name="mem_kv",
    )(mem, w_kv)


def _mem_kernel(q_ref, mk_ref, mv_ref, o_ref):
    s = lax.dot_general(q_ref[...], mk_ref[...], (((1,), (1,)), ((), ())),
                        preferred_element_type=F32)
    m = jnp.max(s, axis=1, keepdims=True)
    p = jnp.exp2(s - m)
    l = jnp.sum(p, axis=1, keepdims=True)
    o = jnp.dot(p.astype(BF16), mv_ref[...], preferred_element_type=F32)
    o_ref[...] = (o / l).astype(o_ref.dtype)


def _mem_attn(qkv4, mkv, batch, seq):
    tq = MEM_TQ
    return pl.pallas_call(
        _mem_kernel,
        grid=(batch, MEM_HEADS, seq // tq),
        in_specs=[
            pl.BlockSpec((None, None, tq, LANES), lambda b, h, i: (SLAB_MEM_Q + h, b, i, 0)),
            pl.BlockSpec((None, N_MEM, LANES), lambda b, h, i: (b, 0, h)),
            pl.BlockSpec((None, N_MEM, LANES), lambda b, h, i: (b, 0, MEM_HEADS + h)),
        ],
        out_specs=pl.BlockSpec((None, tq, LANES), lambda b, h, i: (b, i, h)),
        out_shape=jax.ShapeDtypeStruct((batch, seq, MEM_HEADS * LANES), BF16),
        compiler_params=_cparams(("parallel", "parallel", "parallel")),
        name="mem",
    )(qkv4, mkv, mkv)


def _layer_norm(z, g, b):
    mu = jnp.mean(z, axis=1, keepdims=True)
    zc = z - mu
    var = jnp.mean(zc * zc, axis=1, keepdims=True)
    return zc * lax.rsqrt(var + LN_EPS) * g + b


def _lane_max(v):
    return jnp.max(v, axis=1, keepdims=True)


def _routing_info(logits):
    lane = lax.broadcasted_iota(jnp.int32, logits.shape, 1)
    big = jnp.int32(2 * LANES)
    is_g = lane < N_GROUPS
    gl = jnp.where(is_g, logits, NEG_BIG)
    gmax = _lane_max(gl)
    g_w = 1.0 / jnp.sum(jnp.exp(gl - gmax), axis=1, keepdims=True)
    g_idx = jnp.min(jnp.where(gl == gmax, lane, big), axis=1, keepdims=True)
    lo = ROUTE_LANE0 + g_idx * EXPERTS_PER_GROUP
    in_grp = (lane >= lo) & (lane < lo + EXPERTS_PER_GROUP)
    el = jnp.where(in_grp, logits, NEG_BIG)
    e1 = _lane_max(el)
    i1 = jnp.min(jnp.where(el == e1, lane, big), axis=1, keepdims=True)
    el2 = jnp.where(lane == i1, NEG_BIG, el)
    e2 = _lane_max(el2)
    i2 = jnp.min(jnp.where(el2 == e2, lane, big), axis=1, keepdims=True)
    r = jnp.exp(e2 - e1)
    w1 = g_w / (1.0 + r)
    w2 = g_w * r / (1.0 + r)
    id1 = (i1 - ROUTE_LANE0).astype(F32)
    id2 = (i2 - ROUTE_LANE0).astype(F32)
    return jnp.where(lane == RI_E1, id1,
                     jnp.where(lane == RI_E2, id2,
                               jnp.where(lane == RI_W1, w1, jnp.where(lane == RI_W2, w2, 0.0))))


def _merge_kernel(x_ref, yf_ref, yd_ref, ym_ref, wg_ref, bg_ref, wbf_ref, wbd_ref, wbm_ref,
                  wo_ref, g1_ref, b1_ref, wr_ref, br_ref, x1_ref, ri_ref, rit_ref):
    x = x_ref[...]
    xb = x.astype(BF16)
    h = None
    for i, (y_ref, wb_ref) in enumerate(((yf_ref, wbf_ref), (yd_ref, wbd_ref), (ym_ref, wbm_ref))):
        cols = slice(i * D_MODEL, (i + 1) * D_MODEL)
        gl = jnp.dot(xb, wg_ref[:, cols], preferred_element_type=F32) + bg_ref[:, cols]
        gate = 1.0 / (1.0 + jnp.exp(-gl))
        br = jnp.dot(y_ref[...], wb_ref[...], preferred_element_type=F32)
        h = gate * br if h is None else h + gate * br
    o = jnp.dot(h.astype(BF16), wo_ref[...], preferred_element_type=F32)
    x1 = _layer_norm(DEEPNORM_ALPHA * x + o, g1_ref[...], b1_ref[...])
    x1_ref[...] = x1
    logits = jnp.dot(x1.astype(BF16), wr_ref[...], preferred_element_type=F32) + br_ref[...]
    ri = _routing_info(logits)
    ri_ref[...] = ri
    rit_ref[...] = ri.T[0:8, :]


def _const_spec(shape):
    return pl.BlockSpec(shape, lambda i: (0,) * len(shape), pipeline_mode=pl.Buffered(1))


def _merge(x2, yf, yd, ym, wg, bg, wbf, wbd, wbm, wo, g1, b1, wr, br):
    t = x2.shape[0]
    tm = MERGE_TM
    half = yf.shape[1]
    row = lambda w: pl.BlockSpec((tm, w), lambda i: (i, 0))
    return pl.pallas_call(
        _merge_kernel,
        grid=(t // tm,),
        in_specs=[
            row(D_MODEL), row(half), row(half), row(half),
            _const_spec((D_MODEL, N_BRANCH * D_MODEL)), _const_spec((1, N_BRANCH * D_MODEL)),
            _const_spec((half, D_MODEL)), _const_spec((half, D_MODEL)), _const_spec((half, D_MODEL)),
            _const_spec((D_MODEL, D_MODEL)), _const_spec((1, D_MODEL)), _const_spec((1, D_MODEL)),
            _const_spec((D_MODEL, LANES)), _const_spec((1, LANES)),
        ],
        out_specs=[row(D_MODEL), row(LANES), pl.BlockSpec((8, tm), lambda i: (0, i))],
        out_shape=[
            jax.ShapeDtypeStruct((t, D_MODEL), F32),
            jax.ShapeDtypeStruct((t, LANES), F32),
            jax.ShapeDtypeStruct((8, t), F32),
        ],
        compiler_params=_cparams(("parallel",)),
        name="merge",
    )(x2, yf, yd, ym, wg, bg, wbf, wbd, wbm, wo, g1, b1, wr, br)


def _cumsum(x, axis):
    n = x.shape[axis]
    idx = lax.broadcasted_iota(jnp.int32, x.shape, axis)
    sh = 1
    while sh < n:
        x = x + jnp.where(idx >= sh, pltpu.roll(x, sh, axis=axis), 0.0)
        sh *= 2
    return x


def _route_kernel(rit_ref, pos_ref, tiles_ref):
    s = rit_ref.shape[1]
    e1 = rit_ref[RI_E1:RI_E1 + 1, :]
    e2 = rit_ref[RI_E2:RI_E2 + 1, :]
    sub = lax.broadcasted_iota(jnp.int32, (N_EXPERTS, s), 0).astype(F32)
    oh1 = jnp.where(sub == e1, 1.0, 0.0)
    oh2 = jnp.where(sub == e2, 1.0, 0.0)
    cnt = oh1 + oh2
    incl = _cumsum(cnt, 1)
    before = incl - cnt
    total = jnp.broadcast_to(incl[:, s - 1:s], (N_EXPERTS, LANES))
    aligned = jnp.floor((total + (SEG_ALIGN - 1)) * (1.0 / SEG_ALIGN)) * SEG_ALIGN
    off = _cumsum(aligned, 0) - aligned
    slot = before + off[:, 0:1]
    pos1 = jnp.sum(oh1 * slot, axis=0, keepdims=True)
    pos2 = jnp.sum(oh2 * slot, axis=0, keepdims=True)
    pos_ref[...] = jnp.concatenate([pos1, pos2], axis=0).astype(jnp.int32)

    nt = jnp.floor((total + (MOE_R - 1)) * (1.0 / MOE_R))
    ct_incl = _cumsum(nt, 0)
    ct_excl = ct_incl - nt
    sub_l = lax.broadcasted_iota(jnp.int32, (N_EXPERTS, LANES), 0).astype(F32)
    tile_i = lax.broadcasted_iota(jnp.int32, (N_EXPERTS, LANES), 1).astype(F32)
    te = jnp.sum(jnp.where(ct_incl <= tile_i, 1.0, 0.0), axis=0, keepdims=True)
    te = jnp.minimum(te, N_EXPERTS - 1.0)
    sel = jnp.where(sub_l == te, 1.0, 0.0)
    k_in = tile_i - ct_excl
    start = jnp.sum(sel * (off + k_in * MOE_R), axis=0, keepdims=True)
    nval = jnp.sum(sel * jnp.clip(total - k_in * MOE_R, 0.0, float(MOE_R)), axis=0, keepdims=True)
    active = tile_i[0:1, :] < ct_incl[N_EXPERTS - 1:N_EXPERTS, :]
    start = jnp.where(active, start, 0.0)
    nval = jnp.where(active, nval, 0.0)
    rows = jnp.concatenate([te, start, nval, jnp.zeros((5, LANES), F32)], axis=0)
    tiles_ref[...] = rows.astype(jnp.int32)


def _route(rit, batch, seq):
    return pl.pallas_call(
        _route_kernel,
        grid=(batch,),
        in_specs=[pl.BlockSpec((8, seq), lambda b: (0, b))],
        out_specs=[
            pl.BlockSpec((None, 2, seq), lambda b: (b, 0, 0)),
            pl.BlockSpec((None, 8, LANES), lambda b: (b, 0, 0)),
        ],
        out_shape=[
            jax.ShapeDtypeStruct((batch, 2, seq), jnp.int32),
            jax.ShapeDtypeStruct((batch, 8, LANES), jnp.int32),
        ],
        compiler_params=_cparams(("parallel",)),
        name="route",
    )(rit)


def _moe_kernel(nd, nt, texp_ref, tstart_ref, tnval_ref, x1_ref, p1_ref, p2_ref, ri_ref,
                wg_ref, wu_ref, wd_ref, g2_ref, b2_ref, o_ref, buf, a_scr, b_scr):
    c = pl.program_id(0)
    j = pl.program_id(1)
    td = x1_ref.shape[0]

    @pl.when(j == 0)
    def _():
        buf[...] = jnp.zeros(buf.shape, F32)

    @pl.when(j < nd)
    def _():
        def body(i, carry):
            base = pl.multiple_of(i * SEG_ALIGN, SEG_ALIGN)
            xv = x1_ref.at[pl.ds(base, SEG_ALIGN), :]
            for u in range(SEG_ALIGN):
                row = xv[u:u + 1, :]
                buf[pl.ds(p1_ref[0, base + u], 1), :] = row
                buf[pl.ds(p2_ref[0, base + u], 1), :] = row
            return carry

        lax.fori_loop(0, td // SEG_ALIGN, body, 0)

    @pl.when((j >= nd) & (j < nd + nt))
    def _():
        idx = c * nt + (j - nd)
        nval = tnval_ref[idx]

        @pl.when(nval > 0)
        def _():
            start = pl.multiple_of(tstart_ref[idx], SEG_ALIGN)
            xt = buf[pl.ds(start, MOE_R), :]
            xb = xt.astype(BF16)
            g = jnp.dot(xb, wg_ref[...].astype(BF16), preferred_element_type=F32)
            u = jnp.dot(xb, wu_ref[...].astype(BF16), preferred_element_type=F32)
            hcat = (g / (1.0 + jnp.exp(-g))) * u
            y = jnp.dot(hcat.astype(BF16), wd_ref[...].astype(BF16), preferred_element_type=F32)
            rows = lax.broadcasted_iota(jnp.int32, (MOE_R, 1), 0)
            buf[pl.ds(start, MOE_R), :] = jnp.where(rows < nval, y, xt)

    @pl.when(j >= nd + nt)
    def _():
        def body(i, carry):
            base = pl.multiple_of(i * SEG_ALIGN, SEG_ALIGN)
            av = a_scr.at[pl.ds(base, SEG_ALIGN), :]
            bv = b_scr.at[pl.ds(base, SEG_ALIGN), :]
            for u in range(SEG_ALIGN):
                av[u:u + 1, :] = buf[pl.ds(p1_ref[0, base + u], 1), :]
                bv[u:u + 1, :] = buf[pl.ds(p2_ref[0, base + u], 1), :]
            return carry

        lax.fori_loop(0, td // SEG_ALIGN, body, 0)
        ri = ri_ref[...]
        m = ri[:, RI_W1:RI_W1 + 1] * a_scr[...] + ri[:, RI_W2:RI_W2 + 1] * b_scr[...]
        z = DEEPNORM_ALPHA * x1_ref[...] + m
        o_ref[...] = _layer_norm(z, g2_ref[...], b2_ref[...])


def _moe(x1, pos3, ri, texp, tstart, tnval, wg, wu, wd, g2, b2, batch, seq):
    t = x1.shape[0]
    td = MOE_TD
    nd = seq // td
    nt = 2 * seq // MOE_R + N_EXPERTS
    buf_rows = 2 * seq + N_EXPERTS * SEG_ALIGN + MOE_R

    def tok_tile(j):
        return jnp.where(j < nd, j, jnp.maximum(j - nd - nt, 0))

    def tile_expert(c, j, texp_ref):
        return texp_ref[c * nt + jnp.clip(j - nd, 0, nt - 1)]

    tok_spec = lambda w: pl.BlockSpec((td, w), lambda c, j, *_: (c * nd + tok_tile(j), 0))
    pos_spec = lambda k: pl.BlockSpec((None, 1, td), lambda c, j, *_: ((c * 2 + k) * nd + tok_tile(j), 0, 0),
                                      memory_space=pltpu.SMEM)
    w_spec = lambda a, b_: pl.BlockSpec((None, a, b_), lambda c, j, te, *_: (tile_expert(c, j, te), 0, 0))
    vec_spec = pl.BlockSpec((1, D_MODEL), lambda c, j, *_: (0, 0))
    grid_spec = pltpu.PrefetchScalarGridSpec(
        num_scalar_prefetch=3,
        grid=(batch, nd + nt + nd),
        in_specs=[
            tok_spec(D_MODEL), pos_spec(0), pos_spec(1), tok_spec(LANES),
            w_spec(D_MODEL, D_EXPERT), w_spec(D_MODEL, D_EXPERT), w_spec(D_EXPERT, D_MODEL),
            vec_spec, vec_spec,
        ],
        out_specs=pl.BlockSpec((td, D_MODEL),
                               lambda c, j, *_: (c * nd + jnp.maximum(j - nd - nt, 0), 0)),
        scratch_shapes=[
            pltpu.VMEM((buf_rows, D_MODEL), F32),
            pltpu.VMEM((td, D_MODEL), F32),
            pltpu.VMEM((td, D_MODEL), F32),
        ],
    )
    return pl.pallas_call(
        functools.partial(_moe_kernel, nd, nt),
        grid_spec=grid_spec,
        out_shape=jax.ShapeDtypeStruct((t, D_MODEL), F32),
        compiler_params=_cparams(("arbitrary", "arbitrary")),
        name="moe",
    )(texp, tstart, tnval, x1, pos3, pos3, ri, wg, wu, wd, g2, b2)


def _qkv_col_scale():
    ones = lambda n: jnp.ones((n,), F32)
    att = FOX_HEADS * HEAD_DIM
    qs = lambda d: jnp.full((att,), LOG2E * d ** -0.5, F32)
    return jnp.concatenate([qs(HEAD_DIM), ones(2 * att), qs(HEAD_DIM), ones(2 * att),
                            qs(MEM_HEAD_DIM)]).reshape(1, QKV_COLS)


def kernel(x, mem, w_in, b_forget, b_gates, lambda_q1, lambda_k1, lambda_q2, lambda_k2, diff_subln_g,
           w_mem_kv, w_branch_fox, w_branch_diff, w_branch_mem, w_out, ln1_g, ln1_b, w_router_group,
           b_router_group, w_router_expert, b_router_expert, w_expert_gate, w_expert_up, w_expert_down,
           ln2_g, ln2_b):
    batch, seq, d = x.shape
    t = batch * seq
    l = 0
    x2 = x.reshape(t, d)

    w_gates = w_in[l, :, GATE_COL0:FL_COL0].astype(BF16)
    w_fl_t = jnp.pad(w_in[l, :, FL_COL0:].T, ((0, 16 - FOX_HEADS), (0, 0))).astype(BF16)
    w_r = jnp.concatenate([w_router_group[l], w_router_expert[l]], axis=1)
    w_r = jnp.pad(w_r, ((0, 0), (0, LANES - w_r.shape[1]))).astype(BF16)
    b_r = jnp.concatenate([b_router_group[l], b_router_expert[l]])
    b_r = jnp.pad(b_r, (0, LANES - b_r.shape[0])).reshape(1, LANES)
    lam_params = jnp.stack([lambda_q1[l], lambda_k1[l], lambda_q2[l], lambda_k2[l]])
    slopes = 2.0 ** (-8.0 * jnp.arange(1, DIFF_HEADS + 1, dtype=F32) / DIFF_HEADS)

    qkv, fl_t = _proj(x2, w_in, w_fl_t, _qkv_col_scale())
    qkv4 = qkv.reshape(N_SLABS, batch, seq, LANES)
    c = _fscan(fl_t, b_forget[l].reshape(FOX_HEADS, 1), batch, seq)
    c4 = c.reshape(FOX_HEADS // 2, 2, t)

    y_fox = _fox(qkv4, c4, batch, seq)
    y_diff = _diff(qkv4, slopes, lam_params, diff_subln_g[l].reshape(LANES, 1), batch, seq)
    mkv = _memkv(mem, w_mem_kv[l].astype(BF16))
    y_mem = _mem_attn(qkv4, mkv, batch, seq)

    x1, ri, rit = _merge(
        x2, y_fox.reshape(t, -1), y_diff.reshape(t, -1), y_mem.reshape(t, -1),
        w_gates, b_gates[l].reshape(1, -1),
        w_branch_fox[l].astype(BF16), w_branch_diff[l].astype(BF16), w_branch_mem[l].astype(BF16),
        w_out[l].astype(BF16), ln1_g[l].reshape(1, d), ln1_b[l].reshape(1, d), w_r, b_r)

    pos, tiles = _route(rit, batch, seq)
    nt = 2 * seq // MOE_R + N_EXPERTS
    texp, tstart, tnval = (tiles[:, r, :nt].reshape(-1) for r in range(3))
    pos3 = pos.reshape(batch * 2 * (seq // MOE_TD), 1, MOE_TD)
    out = _moe(x1, pos3, ri, texp, tstart, tnval,
               w_expert_gate[l], w_expert_up[l], w_expert_down[l],
               ln2_g[l].reshape(1, d), ln2_b[l].reshape(1, d), batch, seq)
    return out.reshape(batch, seq, d)
```

```python
import functools
import math

import jax
import jax.numpy as jnp
from jax import lax
from jax.experimental import pallas as pl
from jax.experimental.pallas import tpu as pltpu

F32 = jnp.float32
BF16 = jnp.bfloat16

D_MODEL = 1024
HEAD_DIM = 64
FOX_HEADS = 8
DIFF_HEADS = 4
MEM_HEADS = 4
MEM_HEAD_DIM = 128
N_MEM = 256
N_BRANCH = 3
N_GROUPS = 4
EXPERTS_PER_GROUP = 8
N_EXPERTS = N_GROUPS * EXPERTS_PER_GROUP
D_EXPERT = 256
LN_EPS = 1e-5
DEPTH = 1
DEEPNORM_ALPHA = (2.0 * DEPTH) ** 0.25
LAM_INIT = 0.8 - 0.6 * math.exp(-0.3 * 0)
LOG2E = math.log2(math.e)

LANES = 128
QKV_COLS = 3584
N_SLABS = QKV_COLS // LANES
GATE_COL0 = QKV_COLS
FL_COL0 = QKV_COLS + N_BRANCH * D_MODEL
SLAB_FOX_Q, SLAB_FOX_K, SLAB_FOX_V = 0, 4, 8
SLAB_DIFF_Q, SLAB_DIFF_K, SLAB_DIFF_V = 12, 16, 20
SLAB_MEM_Q = 24
ROUTE_LANE0 = N_GROUPS

NEG_BIG = -1e30
VMEM_LIMIT = 56 * 1024 * 1024

PROJ_TM = 1024
PROJ_TN = 512
ATT_T = 256
N_CHAINS = 4
NSL = N_CHAINS // 2
N_BIAS_LANES = 3
DIFF_ONES_ROWS = 16
SETUP_ROWS = 512
MEM_TQ = 4096
MERGE_TM = 512
MOE_R = 256
MOE_TD = 256
SEG_ALIGN = 8
RI_E1, RI_E2, RI_W1, RI_W2 = 0, 1, 2, 3


def _cparams(sem, flags=None):
    return pltpu.CompilerParams(dimension_semantics=sem, vmem_limit_bytes=VMEM_LIMIT, flags=flags)


ATT_FLAGS = None


def _proj_kernel(x_ref, w_ref, wfl_ref, sc_ref, o_ref, fl_ref, xb_ref):
    j = pl.program_id(1)

    @pl.when(j == 0)
    def _():
        xb = x_ref[...].astype(BF16)
        xb_ref[...] = xb
        fl_ref[...] = lax.dot_general(wfl_ref[...], xb, (((1,), (1,)), ((), ())),
                                      preferred_element_type=F32)

    acc = lax.dot_general(xb_ref[...], w_ref[...].astype(BF16), (((1,), (1,)), ((), ())),
                          preferred_element_type=F32) * sc_ref[...]
    for c in range(PROJ_TN // LANES):
        o_ref[c] = acc[:, c * LANES:(c + 1) * LANES].astype(BF16)


def _proj(x2, w_in_t, w_fl_t, col_scale):
    t = x2.shape[0]
    spc = PROJ_TN // LANES
    return pl.pallas_call(
        _proj_kernel,
        grid=(t // PROJ_TM, QKV_COLS // PROJ_TN),
        in_specs=[
            pl.BlockSpec((PROJ_TM, D_MODEL), lambda i, j: (i, 0)),
            pl.BlockSpec((None, PROJ_TN, D_MODEL), lambda i, j: (0, j, 0)),
            pl.BlockSpec((16, D_MODEL), lambda i, j: (0, 0)),
            pl.BlockSpec((1, PROJ_TN), lambda i, j: (0, j)),
        ],
        out_specs=[
            pl.BlockSpec((spc, PROJ_TM, LANES), lambda i, j: (j, i, 0)),
            pl.BlockSpec((16, PROJ_TM), lambda i, j: (0, i)),
        ],
        out_shape=[
            jax.ShapeDtypeStruct((N_SLABS, t, LANES), BF16),
            jax.ShapeDtypeStruct((16, t), F32),
        ],
        scratch_shapes=[pltpu.VMEM((PROJ_TM, D_MODEL), BF16)],
        compiler_params=_cparams(("parallel", "arbitrary")),
        name="proj",
    )(x2, w_in_t, w_fl_t, col_scale)


def _fscan_kernel(fl_ref, b_ref, c_ref):
    s = fl_ref.shape[1]
    z = fl_ref[0:FOX_HEADS, :] + b_ref[...]
    lf = jnp.minimum(z, 0.0) - jnp.log(1.0 + jnp.exp(-jnp.abs(z)))
    lane = lax.broadcasted_iota(jnp.int32, lf.shape, 1)
    sh = 1
    while sh < s:
        r = pltpu.roll(lf, sh, axis=1)
        lf = lf + jnp.where(lane >= sh, r, 0.0)
        sh *= 2
    c_ref[...] = lf * LOG2E


def _fscan(fl_t, b_forget_col, batch, seq):
    t = fl_t.shape[1]
    return pl.pallas_call(
        _fscan_kernel,
        grid=(batch,),
        in_specs=[
            pl.BlockSpec((16, seq), lambda b: (0, b)),
            pl.BlockSpec((FOX_HEADS, 1), lambda b: (0, 0)),
        ],
        out_specs=pl.BlockSpec((FOX_HEADS, seq), lambda b: (0, b)),
        out_shape=jax.ShapeDtypeStruct((FOX_HEADS, t), F32),
        compiler_params=_cparams(("parallel",)),
        name="fscan",
    )(fl_t, b_forget_col)


def _flash_chains(qi, tq, qh_scr, k_fn, vt_fn, m_scr, acc_scr, s_scr, pm_scr):
    m_scr[...] = jnp.full(m_scr.shape, NEG_BIG, F32)
    acc_scr[...] = jnp.zeros(acc_scr.shape, F32)

    def scores(j, slot, masked):
        start = pl.multiple_of(j * tq, tq)
        for c in range(N_CHAINS):
            s = lax.dot_general(k_fn(c, start), qh_scr[c], (((1,), (1,)), ((), ())),
                                preferred_element_type=F32)
            if masked:
                kv = lax.broadcasted_iota(jnp.int32, (tq, tq), 0)
                q = lax.broadcasted_iota(jnp.int32, (tq, tq), 1)
                s = jnp.where(kv <= q, s, NEG_BIG)
            s_scr[slot, c] = s
            pm_scr[slot, c] = jnp.max(s.reshape(tq // 8, 8, tq), axis=0)

    def consume(j, slot):
        start = pl.multiple_of(j * tq, tq)
        for c in range(N_CHAINS):
            m_prev = m_scr[c]
            m_new = jnp.maximum(m_prev, jnp.max(pm_scr[slot, c], axis=0, keepdims=True))
            alpha = jnp.exp2(m_prev - m_new)
            p = jnp.exp2(s_scr[slot, c] - m_new)
            m_scr[c] = m_new
            pv = jnp.dot(vt_fn(c, start), p.astype(BF16), preferred_element_type=F32)
            acc_scr[c] = acc_scr[c] * alpha + pv

    @pl.when(qi == 0)
    def _():
        scores(0, 0, True)
        consume(0, 0)

    @pl.when(qi > 0)
    def _():
        scores(0, 0, False)

        def body(jj, carry):
            j = 2 * jj
            scores(j + 1, 1, False)
            consume(j, 0)
            scores(j + 2, 0, False)
            consume(j + 1, 1)
            return carry

        lax.fori_loop(0, (qi - 1) // 2, body, 0)

        @pl.when(qi % 2 == 1)
        def _():
            scores(qi, 1, True)
            consume(qi - 1, 0)
            consume(qi, 1)

        @pl.when(qi % 2 == 0)
        def _():
            scores(qi - 1, 1, False)
            consume(qi - 2, 0)
            scores(qi, 0, True)
            consume(qi - 1, 1)
            consume(qi, 0)


def _own_and_spare(shape, c):
    lane = lax.broadcasted_iota(jnp.int32, shape, 1)
    own = (lane < HEAD_DIM) if c % 2 == 0 else (lane >= HEAD_DIM)
    spare = HEAD_DIM if c % 2 == 0 else 0
    return lane, own, spare


def _split_q(q_ref, qh_scr):
    for c in range(N_CHAINS):
        q = q_ref[c // 2].astype(F32)
        lane, own, spare = _own_and_spare(q.shape, c)
        ones = (lane >= spare) & (lane < spare + N_BIAS_LANES)
        qh_scr[c] = jnp.where(own, q, jnp.where(ones, 1.0, 0.0)).astype(BF16)


def _augment_k(k, bias, c):
    lane, own, spare = _own_and_spare(k.shape, c)
    hi = bias.astype(BF16).astype(F32)
    r1 = bias - hi
    mid = r1.astype(BF16).astype(F32)
    lo = (r1 - mid).astype(BF16).astype(F32)
    extra = jnp.where(lane == spare, hi, jnp.where(lane == spare + 1, mid, jnp.where(lane == spare + 2, lo, 0.0)))
    return jnp.where(own, k.astype(F32), extra).astype(BF16)


def _transpose_bf16(a):
    return a.astype(F32).T.astype(BF16)


def _attn_scratch(tq, seq, acc_rows, vt_slabs):
    return [
        pltpu.VMEM((N_CHAINS, tq, LANES), BF16),
        pltpu.VMEM((N_CHAINS, 1, tq), F32),
        pltpu.VMEM((N_CHAINS, acc_rows, tq), F32),
        pltpu.VMEM((2, N_CHAINS, tq, tq), F32),
        pltpu.VMEM((2, N_CHAINS, 8, tq), F32),
        pltpu.VMEM((N_CHAINS, seq, LANES), BF16),
        pltpu.VMEM((vt_slabs, acc_rows, seq), BF16),
    ]


def _fox_kernel(q_ref, k_ref, v_ref, c_ref, o_ref, qh_scr, m_scr, acc_scr, s_scr, pm_scr, ka_scr, vat_scr):
    tq = q_ref.shape[1]
    seq = v_ref.shape[1]
    qi = pl.program_id(2)
    _split_q(q_ref, qh_scr)

    @pl.when(qi == 0)
    def _():
        def chunk(i, carry):
            r0 = pl.multiple_of(i * SETUP_ROWS, SETUP_ROWS)
            rows = pl.ds(r0, SETUP_ROWS)
            head_row = lax.broadcasted_iota(jnp.int32, (LANES, SETUP_ROWS), 0) < HEAD_DIM
            for sl in range(NSL):
                vt = _transpose_bf16(v_ref[sl, rows, :])
                one = jnp.ones_like(vt)
                vat_scr[2 * sl, :, rows] = jnp.where(head_row, vt, one)
                vat_scr[2 * sl + 1, :, rows] = jnp.where(head_row, one, vt)
                cc = c_ref[sl, :, rows]
                stacked = jnp.where(head_row, jnp.broadcast_to(cc[1:2, :], (LANES, SETUP_ROWS)),
                                    jnp.broadcast_to(cc[0:1, :], (LANES, SETUP_ROWS)))
                bias = -stacked.T
                k = k_ref[sl, rows, :]
                ka_scr[2 * sl, rows, :] = _augment_k(k, bias, 0)
                ka_scr[2 * sl + 1, rows, :] = _augment_k(k, bias, 1)
            return carry

        lax.fori_loop(0, seq // SETUP_ROWS, chunk, 0)

    def k_fn(c, start):
        return ka_scr[c, pl.ds(start, tq), :]

    def vt_fn(c, start):
        return vat_scr[c, :, pl.ds(start, tq)]

    _flash_chains(qi, tq, qh_scr, k_fn, vt_fn, m_scr, acc_scr, s_scr, pm_scr)
    for sl in range(NSL):
        a0 = acc_scr[2 * sl]
        a1 = acc_scr[2 * sl + 1]
        row = lax.broadcasted_iota(jnp.int32, a0.shape, 0)
        ot = jnp.where(row < HEAD_DIM, a0 / a0[HEAD_DIM:HEAD_DIM + 1, :], a1 / a1[0:1, :])
        o_ref[:, sl * LANES:(sl + 1) * LANES] = ot.T.astype(o_ref.dtype)


def _fox(qkv4, c4, batch, seq):
    tq = ATT_T
    return pl.pallas_call(
        _fox_kernel,
        grid=(batch, FOX_HEADS // N_CHAINS, seq // tq),
        in_specs=[
            pl.BlockSpec((NSL, None, tq, LANES), lambda b, g, i: (SLAB_FOX_Q // NSL + g, b, i, 0)),
            pl.BlockSpec((NSL, None, seq, LANES), lambda b, g, i: (SLAB_FOX_K // NSL + g, b, 0, 0)),
            pl.BlockSpec((NSL, None, seq, LANES), lambda b, g, i: (SLAB_FOX_V // NSL + g, b, 0, 0)),
            pl.BlockSpec((NSL, 2, seq), lambda b, g, i: (g, 0, b)),
        ],
        out_specs=pl.BlockSpec((None, tq, NSL * LANES), lambda b, g, i: (b, i, g)),
        out_shape=jax.ShapeDtypeStruct((batch, seq, FOX_HEADS * HEAD_DIM), BF16),
        scratch_shapes=_attn_scratch(tq, seq, LANES, N_CHAINS),
        compiler_params=_cparams(("parallel", "parallel", "arbitrary"), ATT_FLAGS),
        name="fox",
    )(qkv4, qkv4, qkv4, c4)


def _diff_kernel(slopes_ref, q_ref, k_ref, v_ref, lam_ref, g_ref, o_ref, qh_scr, m_scr, acc_scr, s_scr, pm_scr,
                 ka_scr, vat_scr):
    tq = q_ref.shape[1]
    seq = v_ref.shape[1]
    qi = pl.program_id(2)
    _split_q(q_ref, qh_scr)
    g = pl.program_id(1)

    @pl.when(qi == 0)
    def _():
        def chunk(i, carry):
            r0 = pl.multiple_of(i * SETUP_ROWS, SETUP_ROWS)
            rows = pl.ds(r0, SETUP_ROWS)
            for hd in range(NSL):
                vt = _transpose_bf16(v_ref[hd, rows, :])
                vat_scr[hd, :, rows] = jnp.concatenate([vt, jnp.ones((DIFF_ONES_ROWS, SETUP_ROWS), BF16)], axis=0)
                pos = (lax.broadcasted_iota(jnp.int32, (SETUP_ROWS, LANES), 0) + r0).astype(F32)
                bias = pos * (slopes_ref[NSL * g + hd] * LOG2E)
                k = k_ref[hd, rows, :]
                ka_scr[2 * hd, rows, :] = _augment_k(k, bias, 0)
                ka_scr[2 * hd + 1, rows, :] = _augment_k(k, bias, 1)
            return carry

        lax.fori_loop(0, seq // SETUP_ROWS, chunk, 0)

    def k_fn(c, start):
        return ka_scr[c, pl.ds(start, tq), :]

    def vt_fn(c, start):
        return vat_scr[c // 2, :, pl.ds(start, tq)]

    _flash_chains(qi, tq, qh_scr, k_fn, vt_fn, m_scr, acc_scr, s_scr, pm_scr)
    lp = lam_ref[...]
    s1 = jnp.sum(lp[0:1, :] * lp[1:2, :], axis=1, keepdims=True)
    s2 = jnp.sum(lp[2:3, :] * lp[3:4, :], axis=1, keepdims=True)
    lam = jnp.exp(s1) - jnp.exp(s2) + LAM_INIT
    for hd in range(NSL):
        a1 = acc_scr[2 * hd]
        a2 = acc_scr[2 * hd + 1]
        ot = a1[:LANES, :] / a1[LANES:LANES + 1, :] - lam * (a2[:LANES, :] / a2[LANES:LANES + 1, :])
        ms = jnp.mean(ot * ot, axis=0, keepdims=True)
        yt = ot * lax.rsqrt(ms + LN_EPS) * g_ref[...]
        o_ref[:, hd * LANES:(hd + 1) * LANES] = (yt * (1.0 - LAM_INIT)).T.astype(o_ref.dtype)


def _diff(qkv4, slopes, lam_params, subln_g_col, batch, seq):
    tq = ATT_T
    grid_spec = pltpu.PrefetchScalarGridSpec(
        num_scalar_prefetch=1,
        grid=(batch, DIFF_HEADS // NSL, seq // tq),
        in_specs=[
            pl.BlockSpec((NSL, None, tq, LANES), lambda b, g, i, s: (SLAB_DIFF_Q // NSL + g, b, i, 0)),
            pl.BlockSpec((NSL, None, seq, LANES), lambda b, g, i, s: (SLAB_DIFF_K // NSL + g, b, 0, 0)),
            pl.BlockSpec((NSL, None, seq, LANES), lambda b, g, i, s: (SLAB_DIFF_V // NSL + g, b, 0, 0)),
            pl.BlockSpec((4, HEAD_DIM), lambda b, g, i, s: (0, 0)),
            pl.BlockSpec((LANES, 1), lambda b, g, i, s: (0, 0)),
        ],
        out_specs=pl.BlockSpec((None, tq, NSL * LANES), lambda b, g, i, s: (b, i, g)),
        scratch_shapes=_attn_scratch(tq, seq, LANES + DIFF_ONES_ROWS, NSL),
    )
    return pl.pallas_call(
        _diff_kernel,
        grid_spec=grid_spec,
        out_shape=jax.ShapeDtypeStruct((batch, seq, DIFF_HEADS * LANES), BF16),
        compiler_params=_cparams(("parallel", "parallel", "arbitrary"), ATT_FLAGS),
        name="diff",
    )(slopes, qkv4, qkv4, qkv4, lam_params, subln_g_col)


def _memkv_kernel(mem_ref, w_ref, o_ref):
    o_ref[...] = jnp.dot(mem_ref[...].astype(BF16), w_ref[...],
                         preferred_element_type=F32).astype(o_ref.dtype)


def _memkv(mem, w_kv):
    batch = mem.shape[0]
    width = w_kv.shape[1]
    return pl.pallas_call(
        _memkv_kernel,
        grid=(batch,),
        in_specs=[
            pl.BlockSpec((None, N_MEM, D_MODEL), lambda b: (b, 0, 0)),
            pl.BlockSpec((D_MODEL, width), lambda b: (0, 0)),
        ],
        out_specs=pl.BlockSpec((None, N_MEM, width), lambda b: (b, 0, 0)),
        out_shape=jax.ShapeDtypeStruct((batch, N_MEM, width), BF16),
        compiler_params=_cparams(("parallel",)),
        name="mem_kv",
    )(mem, w_kv)


def _mem_kernel(q_ref, mk_ref, mv_ref, o_ref):
    s = lax.dot_general(q_ref[...], mk_ref[...], (((1,), (1,)), ((), ())),
                        preferred_element_type=F32)
    m = jnp.max(s, axis=1, keepdims=True)
    p = jnp.exp2(s - m)
    l = jnp.sum(p, axis=1, keepdims=True)
    o = jnp.dot(p.astype(BF16), mv_ref[...], preferred_element_type=F32)
    o_ref[...] = (o / l).astype(o_ref.dtype)


def _mem_attn(qkv4, mkv, batch, seq):
    tq = MEM_TQ
    return pl.pallas_call(
        _mem_kernel,
        grid=(batch, MEM_HEADS, seq // tq),
        in_specs=[
            pl.BlockSpec((None, None, tq, LANES), lambda b, h, i: (SLAB_MEM_Q + h, b, i, 0)),
            pl.BlockSpec((None, N_MEM, LANES), lambda b, h, i: (b, 0, h)),
            pl.BlockSpec((None, N_MEM, LANES), lambda b, h, i: (b, 0, MEM_HEADS + h)),
        ],
        out_specs=pl.BlockSpec((None, tq, LANES), lambda b, h, i: (b, i, h)),
        out_shape=jax.ShapeDtypeStruct((batch, seq, MEM_HEADS * LANES), BF16),
        compiler_params=_cparams(("parallel", "parallel", "parallel")),
        name="mem",
    )(qkv4, mkv, mkv)


def _layer_norm(z, g, b):
    mu = jnp.mean(z, axis=1, keepdims=True)
    zc = z - mu
    var = jnp.mean(zc * zc, axis=1, keepdims=True)
    return zc * lax.rsqrt(var + LN_EPS) * g + b


def _lane_max(v):
    return jnp.max(v, axis=1, keepdims=True)


def _routing_info(logits):
    lane = lax.broadcasted_iota(jnp.int32, logits.shape, 1)
    big = jnp.int32(2 * LANES)
    is_g = lane < N_GROUPS
    gl = jnp.where(is_g, logits, NEG_BIG)
    gmax = _lane_max(gl)
    g_w = 1.0 / jnp.sum(jnp.exp(gl - gmax), axis=1, keepdims=True)
    g_idx = jnp.min(jnp.where(gl == gmax, lane, big), axis=1, keepdims=True)
    lo = ROUTE_LANE0 + g_idx * EXPERTS_PER_GROUP
    in_grp = (lane >= lo) & (lane < lo + EXPERTS_PER_GROUP)
    el = jnp.where(in_grp, logits, NEG_BIG)
    e1 = _lane_max(el)
    i1 = jnp.min(jnp.where(el == e1, lane, big), axis=1, keepdims=True)
    el2 = jnp.where(lane == i1, NEG_BIG, el)
    e2 = _lane_max(el2)
    i2 = jnp.min(jnp.where(el2 == e2, lane, big), axis=1, keepdims=True)
    r = jnp.exp(e2 - e1)
    w1 = g_w / (1.0 + r)
    w2 = g_w * r / (1.0 + r)
    id1 = (i1 - ROUTE_LANE0).astype(F32)
    id2 = (i2 - ROUTE_LANE0).astype(F32)
    return jnp.where(lane == RI_E1, id1,
                     jnp.where(lane == RI_E2, id2,
                               jnp.where(lane == RI_W1, w1, jnp.where(lane == RI_W2, w2, 0.0))))


def _merge_kernel(x_ref, yf_ref, yd_ref, ym_ref, wg_ref, bg_ref, wbf_ref, wbd_ref, wbm_ref,
                  wo_ref, g1_ref, b1_ref, wr_ref, br_ref, x1_ref, ri_ref, rit_ref):
    x = x_ref[...]
    xb = x.astype(BF16)
    h = None
    for i, (y_ref, wb_ref) in enumerate(((yf_ref, wbf_ref), (yd_ref, wbd_ref), (ym_ref, wbm_ref))):
        cols = slice(i * D_MODEL, (i + 1) * D_MODEL)
        gl = lax.dot_general(xb, wg_ref[cols, :], (((1,), (1,)), ((), ())),
                             preferred_element_type=F32) + bg_ref[:, cols]
        gate = 1.0 / (1.0 + jnp.exp(-gl))
        br = jnp.dot(y_ref[...], wb_ref[...], preferred_element_type=F32)
        h = gate * br if h is None else h + gate * br
    o = jnp.dot(h.astype(BF16), wo_ref[...], preferred_element_type=F32)
    x1 = _layer_norm(DEEPNORM_ALPHA * x + o, g1_ref[...], b1_ref[...])
    x1_ref[...] = x1
    logits = jnp.dot(x1.astype(BF16), wr_ref[...], preferred_element_type=F32) + br_ref[...]
    ri = _routing_info(logits)
    ri_ref[...] = ri
    rit_ref[...] = ri.T[0:8, :]


def _const_spec(shape):
    return pl.BlockSpec(shape, lambda i: (0,) * len(shape), pipeline_mode=pl.Buffered(1))


def _merge(x2, yf, yd, ym, wg, bg, wbf, wbd, wbm, wo, g1, b1, wr, br):
    t = x2.shape[0]
    tm = MERGE_TM
    half = yf.shape[1]
    row = lambda w: pl.BlockSpec((tm, w), lambda i: (i, 0))
    return pl.pallas_call(
        _merge_kernel,
        grid=(t // tm,),
        in_specs=[
            row(D_MODEL), row(half), row(half), row(half),
            _const_spec((N_BRANCH * D_MODEL, D_MODEL)), _const_spec((1, N_BRANCH * D_MODEL)),
            _const_spec((half, D_MODEL)), _const_spec((half, D_MODEL)), _const_spec((half, D_MODEL)),
            _const_spec((D_MODEL, D_MODEL)), _const_spec((1, D_MODEL)), _const_spec((1, D_MODEL)),
            _const_spec((D_MODEL, LANES)), _const_spec((1, LANES)),
        ],
        out_specs=[row(D_MODEL), row(LANES), pl.BlockSpec((8, tm), lambda i: (0, i))],
        out_shape=[
            jax.ShapeDtypeStruct((t, D_MODEL), F32),
            jax.ShapeDtypeStruct((t, LANES), F32),
            jax.ShapeDtypeStruct((8, t), F32),
        ],
        compiler_params=_cparams(("parallel",)),
        name="merge",
    )(x2, yf, yd, ym, wg, bg, wbf, wbd, wbm, wo, g1, b1, wr, br)


def _cumsum(x, axis):
    n = x.shape[axis]
    idx = lax.broadcasted_iota(jnp.int32, x.shape, axis)
    sh = 1
    while sh < n:
        x = x + jnp.where(idx >= sh, pltpu.roll(x, sh, axis=axis), 0.0)
        sh *= 2
    return x


def _route_kernel(rit_ref, pos_ref, tiles_ref):
    s = rit_ref.shape[1]
    e1 = rit_ref[RI_E1:RI_E1 + 1, :]
    e2 = rit_ref[RI_E2:RI_E2 + 1, :]
    sub = lax.broadcasted_iota(jnp.int32, (N_EXPERTS, s), 0).astype(F32)
    oh1 = jnp.where(sub == e1, 1.0, 0.0)
    oh2 = jnp.where(sub == e2, 1.0, 0.0)
    cnt = oh1 + oh2
    incl = _cumsum(cnt, 1)
    before = incl - cnt
    total = jnp.broadcast_to(incl[:, s - 1:s], (N_EXPERTS, LANES))
    aligned = jnp.floor((total + (SEG_ALIGN - 1)) * (1.0 / SEG_ALIGN)) * SEG_ALIGN
    off = _cumsum(aligned, 0) - aligned
    slot = before + off[:, 0:1]
    pos1 = jnp.sum(oh1 * slot, axis=0, keepdims=True)
    pos2 = jnp.sum(oh2 * slot, axis=0, keepdims=True)
    pos_ref[...] = jnp.concatenate([pos1, pos2], axis=0).astype(jnp.int32)

    nt = jnp.floor((total + (MOE_R - 1)) * (1.0 / MOE_R))
    ct_incl = _cumsum(nt, 0)
    ct_excl = ct_incl - nt
    sub_l = lax.broadcasted_iota(jnp.int32, (N_EXPERTS, LANES), 0).astype(F32)
    tile_i = lax.broadcasted_iota(jnp.int32, (N_EXPERTS, LANES), 1).astype(F32)
    te = jnp.sum(jnp.where(ct_incl <= tile_i, 1.0, 0.0), axis=0, keepdims=True)
    te = jnp.minimum(te, N_EXPERTS - 1.0)
    sel = jnp.where(sub_l == te, 1.0, 0.0)
    k_in = tile_i - ct_excl
    start = jnp.sum(sel * (off + k_in * MOE_R), axis=0, keepdims=True)
    nval = jnp.sum(sel * jnp.clip(total - k_in * MOE_R, 0.0, float(MOE_R)), axis=0, keepdims=True)
    active = tile_i[0:1, :] < ct_incl[N_EXPERTS - 1:N_EXPERTS, :]
    start = jnp.where(active, start, 0.0)
    nval = jnp.where(active, nval, 0.0)
    rows = jnp.concatenate([te, start, nval, jnp.zeros((5, LANES), F32)], axis=0)
    tiles_ref[...] = rows.astype(jnp.int32)


def _route(rit, batch, seq):
    return pl.pallas_call(
        _route_kernel,
        grid=(batch,),
        in_specs=[pl.BlockSpec((8, seq), lambda b: (0, b))],
        out_specs=[
            pl.BlockSpec((None, 2, seq), lambda b: (b, 0, 0)),
            pl.BlockSpec((None, 8, LANES), lambda b: (b, 0, 0)),
        ],
        out_shape=[
            jax.ShapeDtypeStruct((batch, 2, seq), jnp.int32),
            jax.ShapeDtypeStruct((batch, 8, LANES), jnp.int32),
        ],
        compiler_params=_cparams(("parallel",)),
        name="route",
    )(rit)


def _moe_kernel(nd, nt, texp_ref, tstart_ref, tnval_ref, x1_ref, p1_ref, p2_ref, ri_ref,
                wg_ref, wu_ref, wd_ref, g2_ref, b2_ref, o_ref, buf, a_scr, b_scr):
    c = pl.program_id(0)
    j = pl.program_id(1)
    td = x1_ref.shape[0]

    @pl.when(j == 0)
    def _():
        buf[...] = jnp.zeros(buf.shape, F32)

    @pl.when(j < nd)
    def _():
        def body(i, carry):
            base = pl.multiple_of(i * SEG_ALIGN, SEG_ALIGN)
            xv = x1_ref.at[pl.ds(base, SEG_ALIGN), :]
            for u in range(SEG_ALIGN):
                row = xv[u:u + 1, :]
                buf[pl.ds(p1_ref[0, base + u], 1), :] = row
                buf[pl.ds(p2_ref[0, base + u], 1), :] = row
            return carry

        lax.fori_loop(0, td // SEG_ALIGN, body, 0)

    @pl.when((j >= nd) & (j < nd + nt))
    def _():
        idx = c * nt + (j - nd)
        nval = tnval_ref[idx]

        @pl.when(nval > 0)
        def _():
            start = pl.multiple_of(tstart_ref[idx], SEG_ALIGN)
            xt = buf[pl.ds(start, MOE_R), :]
            xb = xt.astype(BF16)
            g = jnp.dot(xb, wg_ref[...].astype(BF16), preferred_element_type=F32)
            u = jnp.dot(xb, wu_ref[...].astype(BF16), preferred_element_type=F32)
            hcat = (g / (1.0 + jnp.exp(-g))) * u
            y = jnp.dot(hcat.astype(BF16), wd_ref[...].astype(BF16), preferred_element_type=F32)
            rows = lax.broadcasted_iota(jnp.int32, (MOE_R, 1), 0)
            buf[pl.ds(start, MOE_R), :] = jnp.where(rows < nval, y, xt)

    @pl.when(j >= nd + nt)
    def _():
        def body(i, carry):
            base = pl.multiple_of(i * SEG_ALIGN, SEG_ALIGN)
            av = a_scr.at[pl.ds(base, SEG_ALIGN), :]
            bv = b_scr.at[pl.ds(base, SEG_ALIGN), :]
            for u in range(SEG_ALIGN):
                av[u:u + 1, :] = buf[pl.ds(p1_ref[0, base + u], 1), :]
                bv[u:u + 1, :] = buf[pl.ds(p2_ref[0, base + u], 1), :]
            return carry

        lax.fori_loop(0, td // SEG_ALIGN, body, 0)
        ri = ri_ref[...]
        m = ri[:, RI_W1:RI_W1 + 1] * a_scr[...] + ri[:, RI_W2:RI_W2 + 1] * b_scr[...]
        z = DEEPNORM_ALPHA * x1_ref[...] + m
        o_ref[...] = _layer_norm(z, g2_ref[...], b2_ref[...])


def _moe(x1, pos3, ri, texp, tstart, tnval, wg, wu, wd, g2, b2, batch, seq):
    t = x1.shape[0]
    td = MOE_TD
    nd = seq // td
    nt = 2 * seq // MOE_R + N_EXPERTS
    buf_rows = 2 * seq + N_EXPERTS * SEG_ALIGN + MOE_R

    def tok_tile(j):
        return jnp.where(j < nd, j, jnp.maximum(j - nd - nt, 0))

    def tile_expert(c, j, texp_ref):
        return texp_ref[c * nt + jnp.clip(j - nd, 0, nt - 1)]

    tok_spec = lambda w: pl.BlockSpec((td, w), lambda c, j, *_: (c * nd + tok_tile(j), 0))
    pos_spec = lambda k: pl.BlockSpec((None, 1, td), lambda c, j, *_: ((c * 2 + k) * nd + tok_tile(j), 0, 0),
                                      memory_space=pltpu.SMEM)
    w_spec = lambda a, b_: pl.BlockSpec((None, a, b_), lambda c, j, te, *_: (tile_expert(c, j, te), 0, 0))
    vec_spec = pl.BlockSpec((1, D_MODEL), lambda c, j, *_: (0, 0))
    grid_spec = pltpu.PrefetchScalarGridSpec(
        num_scalar_prefetch=3,
        grid=(batch, nd + nt + nd),
        in_specs=[
            tok_spec(D_MODEL), pos_spec(0), pos_spec(1), tok_spec(LANES),
            w_spec(D_MODEL, D_EXPERT), w_spec(D_MODEL, D_EXPERT), w_spec(D_EXPERT, D_MODEL),
            vec_spec, vec_spec,
        ],
        out_specs=pl.BlockSpec((td, D_MODEL),
                               lambda c, j, *_: (c * nd + jnp.maximum(j - nd - nt, 0), 0)),
        scratch_shapes=[
            pltpu.VMEM((buf_rows, D_MODEL), F32),
            pltpu.VMEM((td, D_MODEL), F32),
            pltpu.VMEM((td, D_MODEL), F32),
        ],
    )
    return pl.pallas_call(
        functools.partial(_moe_kernel, nd, nt),
        grid_spec=grid_spec,
        out_shape=jax.ShapeDtypeStruct((t, D_MODEL), F32),
        compiler_params=_cparams(("arbitrary", "arbitrary")),
        name="moe",
    )(texp, tstart, tnval, x1, pos3, pos3, ri, wg, wu, wd, g2, b2)


def _qkv_col_scale():
    ones = lambda n: jnp.ones((n,), F32)
    att = FOX_HEADS * HEAD_DIM
    qs = lambda d: jnp.full((att,), LOG2E * d ** -0.5, F32)
    return jnp.concatenate([qs(HEAD_DIM), ones(2 * att), qs(HEAD_DIM), ones(2 * att),
                            qs(MEM_HEAD_DIM)]).reshape(1, QKV_COLS)


def kernel(x, mem, w_in, b_forget, b_gates, lambda_q1, lambda_k1, lambda_q2, lambda_k2, diff_subln_g,
           w_mem_kv, w_branch_fox, w_branch_diff, w_branch_mem, w_out, ln1_g, ln1_b, w_router_group,
           b_router_group, w_router_expert, b_router_expert, w_expert_gate, w_expert_up, w_expert_down,
           ln2_g, ln2_b):
    batch, seq, d = x.shape
    t = batch * seq
    l = 0
    x2 = x.reshape(t, d)

    w_in_t = jnp.swapaxes(w_in, 1, 2)
    w_gates = w_in_t[l, GATE_COL0:FL_COL0, :].astype(BF16)
    w_fl_t = jnp.pad(w_in_t[l, FL_COL0:, :], ((0, 16 - FOX_HEADS), (0, 0))).astype(BF16)
    w_r = jnp.concatenate([w_router_group[l], w_router_expert[l]], axis=1)
    w_r = jnp.pad(w_r, ((0, 0), (0, LANES - w_r.shape[1]))).astype(BF16)
    b_r = jnp.concatenate([b_router_group[l], b_router_expert[l]])
    b_r = jnp.pad(b_r, (0, LANES - b_r.shape[0])).reshape(1, LANES)
    lam_params = jnp.stack([lambda_q1[l], lambda_k1[l], lambda_q2[l], lambda_k2[l]])
    slopes = 2.0 ** (-8.0 * jnp.arange(1, DIFF_HEADS + 1, dtype=F32) / DIFF_HEADS)

    qkv, fl_t = _proj(x2, w_in_t, w_fl_t, _qkv_col_scale())
    qkv4 = qkv.reshape(N_SLABS, batch, seq, LANES)
    c = _fscan(fl_t, b_forget[l].reshape(FOX_HEADS, 1), batch, seq)
    c4 = c.reshape(FOX_HEADS // 2, 2, t)

    y_fox = _fox(qkv4, c4, batch, seq)
    y_diff = _diff(qkv4, slopes, lam_params, diff_subln_g[l].reshape(LANES, 1), batch, seq)
    mkv = _memkv(mem, w_mem_kv[l].astype(BF16))
    y_mem = _mem_attn(qkv4, mkv, batch, seq)

    x1, ri, rit = _merge(
        x2, y_fox.reshape(t, -1), y_diff.reshape(t, -1), y_mem.reshape(t, -1),
        w_gates, b_gates[l].reshape(1, -1),
        w_branch_fox[l].astype(BF16), w_branch_diff[l].astype(BF16), w_branch_mem[l].astype(BF16),
        w_out[l].astype(BF16), ln1_g[l].reshape(1, d), ln1_b[l].reshape(1, d), w_r, b_r)

    pos, tiles = _route(rit, batch, seq)
    nt = 2 * seq // MOE_R + N_EXPERTS
    texp, tstart, tnval = (tiles[:, r, :nt].reshape(-1) for r in range(3))
    pos3 = pos.reshape(batch * 2 * (seq // MOE_TD), 1, MOE_TD)
    out = _moe(x1, pos3, ri, texp, tstart, tnval,
               w_expert_gate[l], w_expert_up[l], w_expert_down[l],
               ln2_g[l].reshape(1, d), ln2_b[l].reshape(1, d), batch, seq)
    return out.reshape(batch, seq, d)
```

```python
import functools
import math

import jax
import jax.numpy as jnp
from jax import lax
from jax.experimental import pallas as pl
from jax.experimental.pallas import tpu as pltpu

F32 = jnp.float32
BF16 = jnp.bfloat16

D_MODEL = 1024
HEAD_DIM = 64
FOX_HEADS = 8
DIFF_HEADS = 4
MEM_HEADS = 4
MEM_HEAD_DIM = 128
N_MEM = 256
N_BRANCH = 3
N_GROUPS = 4
EXPERTS_PER_GROUP = 8
N_EXPERTS = N_GROUPS * EXPERTS_PER_GROUP
D_EXPERT = 256
LN_EPS = 1e-5
DEPTH = 1
DEEPNORM_ALPHA = (2.0 * DEPTH) ** 0.25
LAM_INIT = 0.8 - 0.6 * math.exp(-0.3 * 0)
LOG2E = math.log2(math.e)

LANES = 128
QKV_COLS = 3584
N_SLABS = QKV_COLS // LANES
GATE_COL0 = QKV_COLS
FL_COL0 = QKV_COLS + N_BRANCH * D_MODEL
SLAB_FOX_Q, SLAB_FOX_K, SLAB_FOX_V = 0, 4, 8
SLAB_DIFF_Q, SLAB_DIFF_K, SLAB_DIFF_V = 12, 16, 20
SLAB_MEM_Q = 24
ROUTE_LANE0 = N_GROUPS

NEG_BIG = -1e30
VMEM_LIMIT = 56 * 1024 * 1024

PROJ_TM = 2048
PROJ_TN = 512
ATT_T = 256
N_CHAINS = 4
NSL = N_CHAINS // 2
N_BIAS_LANES = 3
DIFF_ONES_ROWS = 16
SETUP_ROWS = 512
MEM_TQ = 4096
MERGE_TM = 1024
MOE_R = 256
MOE_TD = 256
SEG_ALIGN = 8
RI_E1, RI_E2, RI_W1, RI_W2 = 0, 1, 2, 3


def _cparams(sem, flags=None):
    return pltpu.CompilerParams(dimension_semantics=sem, vmem_limit_bytes=VMEM_LIMIT, flags=flags)


ATT_FLAGS = None


def _proj_kernel(x_ref, w_ref, wfl_ref, sc_ref, o_ref, fl_ref, xb_ref):
    j = pl.program_id(1)

    @pl.when(j == 0)
    def _():
        xb = x_ref[...].astype(BF16)
        xb_ref[...] = xb
        fl_ref[...] = lax.dot_general(wfl_ref[...], xb, (((1,), (1,)), ((), ())),
                                      preferred_element_type=F32)

    acc = lax.dot_general(xb_ref[...], w_ref[...].astype(BF16), (((1,), (1,)), ((), ())),
                          preferred_element_type=F32) * sc_ref[...]
    for c in range(PROJ_TN // LANES):
        o_ref[c] = acc[:, c * LANES:(c + 1) * LANES].astype(BF16)


def _proj(x2, w_in_t, w_fl_t, col_scale):
    t = x2.shape[0]
    spc = PROJ_TN // LANES
    return pl.pallas_call(
        _proj_kernel,
        grid=(t // PROJ_TM, QKV_COLS // PROJ_TN),
        in_specs=[
            pl.BlockSpec((PROJ_TM, D_MODEL), lambda i, j: (i, 0)),
            pl.BlockSpec((None, PROJ_TN, D_MODEL), lambda i, j: (0, j, 0)),
            pl.BlockSpec((16, D_MODEL), lambda i, j: (0, 0)),
            pl.BlockSpec((1, PROJ_TN), lambda i, j: (0, j)),
        ],
        out_specs=[
            pl.BlockSpec((spc, PROJ_TM, LANES), lambda i, j: (j, i, 0)),
            pl.BlockSpec((16, PROJ_TM), lambda i, j: (0, i)),
        ],
        out_shape=[
            jax.ShapeDtypeStruct((N_SLABS, t, LANES), BF16),
            jax.ShapeDtypeStruct((16, t), F32),
        ],
        scratch_shapes=[pltpu.VMEM((PROJ_TM, D_MODEL), BF16)],
        compiler_params=_cparams(("parallel", "arbitrary")),
        name="proj",
    )(x2, w_in_t, w_fl_t, col_scale)


def _fscan_kernel(fl_ref, b_ref, c_ref):
    s = fl_ref.shape[1]
    z = fl_ref[0:FOX_HEADS, :] + b_ref[...]
    lf = jnp.minimum(z, 0.0) - jnp.log(1.0 + jnp.exp(-jnp.abs(z)))
    lane = lax.broadcasted_iota(jnp.int32, lf.shape, 1)
    sh = 1
    while sh < s:
        r = pltpu.roll(lf, sh, axis=1)
        lf = lf + jnp.where(lane >= sh, r, 0.0)
        sh *= 2
    c_ref[...] = lf * LOG2E


def _fscan(fl_t, b_forget_col, batch, seq):
    t = fl_t.shape[1]
    return pl.pallas_call(
        _fscan_kernel,
        grid=(batch,),
        in_specs=[
            pl.BlockSpec((16, seq), lambda b: (0, b)),
            pl.BlockSpec((FOX_HEADS, 1), lambda b: (0, 0)),
        ],
        out_specs=pl.BlockSpec((FOX_HEADS, seq), lambda b: (0, b)),
        out_shape=jax.ShapeDtypeStruct((FOX_HEADS, t), F32),
        compiler_params=_cparams(("parallel",)),
        name="fscan",
    )(fl_t, b_forget_col)


def _flash_chains(qi, tq, qh_scr, k_fn, vt_fn, m_scr, acc_scr, s_scr, pm_scr):
    m_scr[...] = jnp.full(m_scr.shape, NEG_BIG, F32)
    acc_scr[...] = jnp.zeros(acc_scr.shape, F32)

    def scores(j, slot, masked):
        start = pl.multiple_of(j * tq, tq)
        for c in range(N_CHAINS):
            s = lax.dot_general(k_fn(c, start), qh_scr[c], (((1,), (1,)), ((), ())),
                                preferred_element_type=F32)
            if masked:
                kv = lax.broadcasted_iota(jnp.int32, (tq, tq), 0)
                q = lax.broadcasted_iota(jnp.int32, (tq, tq), 1)
                s = jnp.where(kv <= q, s, NEG_BIG)
            s_scr[slot, c] = s
            pm_scr[slot, c] = jnp.max(s.reshape(tq // 8, 8, tq), axis=0)

    def consume(j, slot):
        start = pl.multiple_of(j * tq, tq)
        for c in range(N_CHAINS):
            m_prev = m_scr[c]
            m_new = jnp.maximum(m_prev, jnp.max(pm_scr[slot, c], axis=0, keepdims=True))
            alpha = jnp.exp2(m_prev - m_new)
            p = jnp.exp2(s_scr[slot, c] - m_new)
            m_scr[c] = m_new
            pv = jnp.dot(vt_fn(c, start), p.astype(BF16), preferred_element_type=F32)
            acc_scr[c] = acc_scr[c] * alpha + pv

    @pl.when(qi == 0)
    def _():
        scores(0, 0, True)
        consume(0, 0)

    @pl.when(qi > 0)
    def _():
        scores(0, 0, False)

        def body(jj, carry):
            j = 2 * jj
            scores(j + 1, 1, False)
            consume(j, 0)
            scores(j + 2, 0, False)
            consume(j + 1, 1)
            return carry

        lax.fori_loop(0, (qi - 1) // 2, body, 0)

        @pl.when(qi % 2 == 1)
        def _():
            scores(qi, 1, True)
            consume(qi - 1, 0)
            consume(qi, 1)

        @pl.when(qi % 2 == 0)
        def _():
            scores(qi - 1, 1, False)
            consume(qi - 2, 0)
            scores(qi, 0, True)
            consume(qi - 1, 1)
            consume(qi, 0)


def _own_and_spare(shape, c):
    lane = lax.broadcasted_iota(jnp.int32, shape, 1)
    own = (lane < HEAD_DIM) if c % 2 == 0 else (lane >= HEAD_DIM)
    spare = HEAD_DIM if c % 2 == 0 else 0
    return lane, own, spare


def _split_q(q_ref, qh_scr):
    for c in range(N_CHAINS):
        q = q_ref[c // 2].astype(F32)
        lane, own, spare = _own_and_spare(q.shape, c)
        ones = (lane >= spare) & (lane < spare + N_BIAS_LANES)
        qh_scr[c] = jnp.where(own, q, jnp.where(ones, 1.0, 0.0)).astype(BF16)


def _augment_k(k, bias, c):
    lane, own, spare = _own_and_spare(k.shape, c)
    hi = bias.astype(BF16).astype(F32)
    r1 = bias - hi
    mid = r1.astype(BF16).astype(F32)
    lo = (r1 - mid).astype(BF16).astype(F32)
    extra = jnp.where(lane == spare, hi, jnp.where(lane == spare + 1, mid, jnp.where(lane == spare + 2, lo, 0.0)))
    return jnp.where(own, k.astype(F32), extra).astype(BF16)


def _transpose_bf16(a):
    return a.astype(F32).T.astype(BF16)


def _attn_scratch(tq, seq, acc_rows, vt_slabs):
    return [
        pltpu.VMEM((N_CHAINS, tq, LANES), BF16),
        pltpu.VMEM((N_CHAINS, 1, tq), F32),
        pltpu.VMEM((N_CHAINS, acc_rows, tq), F32),
        pltpu.VMEM((2, N_CHAINS, tq, tq), F32),
        pltpu.VMEM((2, N_CHAINS, 8, tq), F32),
        pltpu.VMEM((N_CHAINS, seq, LANES), BF16),
        pltpu.VMEM((vt_slabs, acc_rows, seq), BF16),
    ]


def _fox_kernel(q_ref, k_ref, v_ref, c_ref, o_ref, qh_scr, m_scr, acc_scr, s_scr, pm_scr, ka_scr, vat_scr):
    tq = q_ref.shape[1]
    seq = v_ref.shape[1]
    qi = pl.program_id(2)
    _split_q(q_ref, qh_scr)

    @pl.when(qi == 0)
    def _():
        def chunk(i, carry):
            r0 = pl.multiple_of(i * SETUP_ROWS, SETUP_ROWS)
            rows = pl.ds(r0, SETUP_ROWS)
            head_row = lax.broadcasted_iota(jnp.int32, (LANES, SETUP_ROWS), 0) < HEAD_DIM
            for sl in range(NSL):
                vt = _transpose_bf16(v_ref[sl, rows, :])
                one = jnp.ones_like(vt)
                vat_scr[2 * sl, :, rows] = jnp.where(head_row, vt, one)
                vat_scr[2 * sl + 1, :, rows] = jnp.where(head_row, one, vt)
                cc = c_ref[sl, :, rows]
                stacked = jnp.where(head_row, jnp.broadcast_to(cc[1:2, :], (LANES, SETUP_ROWS)),
                                    jnp.broadcast_to(cc[0:1, :], (LANES, SETUP_ROWS)))
                bias = -stacked.T
                k = k_ref[sl, rows, :]
                ka_scr[2 * sl, rows, :] = _augment_k(k, bias, 0)
                ka_scr[2 * sl + 1, rows, :] = _augment_k(k, bias, 1)
            return carry

        lax.fori_loop(0, seq // SETUP_ROWS, chunk, 0)

    def k_fn(c, start):
        return ka_scr[c, pl.ds(start, tq), :]

    def vt_fn(c, start):
        return vat_scr[c, :, pl.ds(start, tq)]

    _flash_chains(qi, tq, qh_scr, k_fn, vt_fn, m_scr, acc_scr, s_scr, pm_scr)
    for sl in range(NSL):
        a0 = acc_scr[2 * sl]
        a1 = acc_scr[2 * sl + 1]
        row = lax.broadcasted_iota(jnp.int32, a0.shape, 0)
        ot = jnp.where(row < HEAD_DIM, a0 / a0[HEAD_DIM:HEAD_DIM + 1, :], a1 / a1[0:1, :])
        o_ref[:, sl * LANES:(sl + 1) * LANES] = ot.T.astype(o_ref.dtype)


def _fox(qkv4, c4, batch, seq):
    tq = ATT_T
    return pl.pallas_call(
        _fox_kernel,
        grid=(batch, FOX_HEADS // N_CHAINS, seq // tq),
        in_specs=[
            pl.BlockSpec((NSL, None, tq, LANES), lambda b, g, i: (SLAB_FOX_Q // NSL + g, b, i, 0)),
            pl.BlockSpec((NSL, None, seq, LANES), lambda b, g, i: (SLAB_FOX_K // NSL + g, b, 0, 0)),
            pl.BlockSpec((NSL, None, seq, LANES), lambda b, g, i: (SLAB_FOX_V // NSL + g, b, 0, 0)),
            pl.BlockSpec((NSL, 2, seq), lambda b, g, i: (g, 0, b)),
        ],
        out_specs=pl.BlockSpec((None, tq, NSL * LANES), lambda b, g, i: (b, i, g)),
        out_shape=jax.ShapeDtypeStruct((batch, seq, FOX_HEADS * HEAD_DIM), BF16),
        scratch_shapes=_attn_scratch(tq, seq, LANES, N_CHAINS),
        compiler_params=_cparams(("parallel", "parallel", "arbitrary"), ATT_FLAGS),
        name="fox",
    )(qkv4, qkv4, qkv4, c4)


def _diff_kernel(slopes_ref, q_ref, k_ref, v_ref, lam_ref, g_ref, o_ref, qh_scr, m_scr, acc_scr, s_scr, pm_scr,
                 ka_scr, vat_scr):
    tq = q_ref.shape[1]
    seq = v_ref.shape[1]
    qi = pl.program_id(2)
    _split_q(q_ref, qh_scr)
    g = pl.program_id(1)

    @pl.when(qi == 0)
    def _():
        def chunk(i, carry):
            r0 = pl.multiple_of(i * SETUP_ROWS, SETUP_ROWS)
            rows = pl.ds(r0, SETUP_ROWS)
            for hd in range(NSL):
                vt = _transpose_bf16(v_ref[hd, rows, :])
                vat_scr[hd, :, rows] = jnp.concatenate([vt, jnp.ones((DIFF_ONES_ROWS, SETUP_ROWS), BF16)], axis=0)
                pos = (lax.broadcasted_iota(jnp.int32, (SETUP_ROWS, LANES), 0) + r0).astype(F32)
                bias = pos * (slopes_ref[NSL * g + hd] * LOG2E)
                k = k_ref[hd, rows, :]
                ka_scr[2 * hd, rows, :] = _augment_k(k, bias, 0)
                ka_scr[2 * hd + 1, rows, :] = _augment_k(k, bias, 1)
            return carry

        lax.fori_loop(0, seq // SETUP_ROWS, chunk, 0)

    def k_fn(c, start):
        return ka_scr[c, pl.ds(start, tq), :]

    def vt_fn(c, start):
        return vat_scr[c // 2, :, pl.ds(start, tq)]

    _flash_chains(qi, tq, qh_scr, k_fn, vt_fn, m_scr, acc_scr, s_scr, pm_scr)
    lp = lam_ref[...]
    s1 = jnp.sum(lp[0:1, :] * lp[1:2, :], axis=1, keepdims=True)
    s2 = jnp.sum(lp[2:3, :] * lp[3:4, :], axis=1, keepdims=True)
    lam = jnp.exp(s1) - jnp.exp(s2) + LAM_INIT
    for hd in range(NSL):
        a1 = acc_scr[2 * hd]
        a2 = acc_scr[2 * hd + 1]
        ot = a1[:LANES, :] / a1[LANES:LANES + 1, :] - lam * (a2[:LANES, :] / a2[LANES:LANES + 1, :])
        ms = jnp.mean(ot * ot, axis=0, keepdims=True)
        yt = ot * lax.rsqrt(ms + LN_EPS) * g_ref[...]
        o_ref[:, hd * LANES:(hd + 1) * LANES] = (yt * (1.0 - LAM_INIT)).T.astype(o_ref.dtype)


def _diff(qkv4, slopes, lam_params, subln_g_col, batch, seq):
    tq = ATT_T
    grid_spec = pltpu.PrefetchScalarGridSpec(
        num_scalar_prefetch=1,
        grid=(batch, DIFF_HEADS // NSL, seq // tq),
        in_specs=[
            pl.BlockSpec((NSL, None, tq, LANES), lambda b, g, i, s: (SLAB_DIFF_Q // NSL + g, b, i, 0)),
            pl.BlockSpec((NSL, None, seq, LANES), lambda b, g, i, s: (SLAB_DIFF_K // NSL + g, b, 0, 0)),
            pl.BlockSpec((NSL, None, seq, LANES), lambda b, g, i, s: (SLAB_DIFF_V // NSL + g, b, 0, 0)),
            pl.BlockSpec((4, HEAD_DIM), lambda b, g, i, s: (0, 0)),
            pl.BlockSpec((LANES, 1), lambda b, g, i, s: (0, 0)),
        ],
        out_specs=pl.BlockSpec((None, tq, NSL * LANES), lambda b, g, i, s: (b, i, g)),
        scratch_shapes=_attn_scratch(tq, seq, LANES + DIFF_ONES_ROWS, NSL),
    )
    return pl.pallas_call(
        _diff_kernel,
        grid_spec=grid_spec,
        out_shape=jax.ShapeDtypeStruct((batch, seq, DIFF_HEADS * LANES), BF16),
        compiler_params=_cparams(("parallel", "parallel", "arbitrary"), ATT_FLAGS),
        name="diff",
    )(slopes, qkv4, qkv4, qkv4, lam_params, subln_g_col)


def _memkv_kernel(mem_ref, w_ref, o_ref):
    o_ref[...] = jnp.dot(mem_ref[...].astype(BF16), w_ref[...],
                         preferred_element_type=F32).astype(o_ref.dtype)


def _memkv(mem, w_kv):
    batch = mem.shape[0]
    width = w_kv.shape[1]
    return pl.pallas_call(
        _memkv_kernel,
        grid=(batch,),
        in_specs=[
            pl.BlockSpec((None, N_MEM, D_MODEL), lambda b: (b, 0, 0)),
            pl.BlockSpec((D_MODEL, width), lambda b: (0, 0)),
        ],
        out_specs=pl.BlockSpec((None, N_MEM, width), lambda b: (b, 0, 0)),
        out_shape=jax.ShapeDtypeStruct((batch, N_MEM, width), BF16),
        compiler_params=_cparams(("parallel",)),
        name="mem_kv",
    )(mem, w_kv)


def _mem_kernel(q_ref, mk_ref, mv_ref, o_ref):
    s = lax.dot_general(q_ref[...], mk_ref[...], (((1,), (1,)), ((), ())),
                        preferred_element_type=F32)
    m = jnp.max(s, axis=1, keepdims=True)
    p = jnp.exp2(s - m)
    l = jnp.sum(p, axis=1, keepdims=True)
    o = jnp.dot(p.astype(BF16), mv_ref[...], preferred_element_type=F32)
    o_ref[...] = (o / l).astype(o_ref.dtype)


def _mem_attn(qkv4, mkv, batch, seq):
    tq = MEM_TQ
    return pl.pallas_call(
        _mem_kernel,
        grid=(batch, MEM_HEADS, seq // tq),
        in_specs=[
            pl.BlockSpec((None, None, tq, LANES), lambda b, h, i: (SLAB_MEM_Q + h, b, i, 0)),
            pl.BlockSpec((None, N_MEM, LANES), lambda b, h, i: (b, 0, h)),
            pl.BlockSpec((None, N_MEM, LANES), lambda b, h, i: (b, 0, MEM_HEADS + h)),
        ],
        out_specs=pl.BlockSpec((None, tq, LANES), lambda b, h, i: (b, i, h)),
        out_shape=jax.ShapeDtypeStruct((batch, seq, MEM_HEADS * LANES), BF16),
        compiler_params=_cparams(("parallel", "parallel", "parallel")),
        name="mem",
    )(qkv4, mkv, mkv)


def _layer_norm(z, g, b):
    mu = jnp.mean(z, axis=1, keepdims=True)
    zc = z - mu
    var = jnp.mean(zc * zc, axis=1, keepdims=True)
    return zc * lax.rsqrt(var + LN_EPS) * g + b


def _lane_max(v):
    return jnp.max(v, axis=1, keepdims=True)


def _routing_info(logits):
    lane = lax.broadcasted_iota(jnp.int32, logits.shape, 1)
    big = jnp.int32(2 * LANES)
    is_g = lane < N_GROUPS
    gl = jnp.where(is_g, logits, NEG_BIG)
    gmax = _lane_max(gl)
    g_w = 1.0 / jnp.sum(jnp.exp(gl - gmax), axis=1, keepdims=True)
    g_idx = jnp.min(jnp.where(gl == gmax, lane, big), axis=1, keepdims=True)
    lo = ROUTE_LANE0 + g_idx * EXPERTS_PER_GROUP
    in_grp = (lane >= lo) & (lane < lo + EXPERTS_PER_GROUP)
    el = jnp.where(in_grp, logits, NEG_BIG)
    e1 = _lane_max(el)
    i1 = jnp.min(jnp.where(el == e1, lane, big), axis=1, keepdims=True)
    el2 = jnp.where(lane == i1, NEG_BIG, el)
    e2 = _lane_max(el2)
    i2 = jnp.min(jnp.where(el2 == e2, lane, big), axis=1, keepdims=True)
    r = jnp.exp(e2 - e1)
    w1 = g_w / (1.0 + r)
    w2 = g_w * r / (1.0 + r)
    id1 = (i1 - ROUTE_LANE0).astype(F32)
    id2 = (i2 - ROUTE_LANE0).astype(F32)
    return jnp.where(lane == RI_E1, id1,
                     jnp.where(lane == RI_E2, id2,
                               jnp.where(lane == RI_W1, w1, jnp.where(lane == RI_W2, w2, 0.0))))


def _merge_kernel(x_ref, yf_ref, yd_ref, ym_ref, wg_ref, bg_ref, wbf_ref, wbd_ref, wbm_ref,
                  wo_ref, g1_ref, b1_ref, wr_ref, br_ref, x1_ref, ri_ref, rit_ref):
    x = x_ref[...]
    xb = x.astype(BF16)
    h = None
    for i, (y_ref, wb_ref) in enumerate(((yf_ref, wbf_ref), (yd_ref, wbd_ref), (ym_ref, wbm_ref))):
        cols = slice(i * D_MODEL, (i + 1) * D_MODEL)
        gl = lax.dot_general(xb, wg_ref[cols, :], (((1,), (1,)), ((), ())),
                             preferred_element_type=F32) + bg_ref[:, cols]
        gate = 1.0 / (1.0 + jnp.exp(-gl))
        br = jnp.dot(y_ref[...], wb_ref[...], preferred_element_type=F32)
        h = gate * br if h is None else h + gate * br
    o = jnp.dot(h.astype(BF16), wo_ref[...], preferred_element_type=F32)
    x1 = _layer_norm(DEEPNORM_ALPHA * x + o, g1_ref[...], b1_ref[...])
    x1_ref[...] = x1
    logits = jnp.dot(x1.astype(BF16), wr_ref[...], preferred_element_type=F32) + br_ref[...]
    ri = _routing_info(logits)
    ri_ref[...] = ri
    rit_ref[...] = ri.T[0:8, :]


def _const_spec(shape):
    return pl.BlockSpec(shape, lambda i: (0,) * len(shape), pipeline_mode=pl.Buffered(1))


def _merge(x2, yf, yd, ym, wg, bg, wbf, wbd, wbm, wo, g1, b1, wr, br):
    t = x2.shape[0]
    tm = MERGE_TM
    half = yf.shape[1]
    row = lambda w: pl.BlockSpec((tm, w), lambda i: (i, 0))
    return pl.pallas_call(
        _merge_kernel,
        grid=(t // tm,),
        in_specs=[
            row(D_MODEL), row(half), row(half), row(half),
            _const_spec((N_BRANCH * D_MODEL, D_MODEL)), _const_spec((1, N_BRANCH * D_MODEL)),
            _const_spec((half, D_MODEL)), _const_spec((half, D_MODEL)), _const_spec((half, D_MODEL)),
            _const_spec((D_MODEL, D_MODEL)), _const_spec((1, D_MODEL)), _const_spec((1, D_MODEL)),
            _const_spec((D_MODEL, LANES)), _const_spec((1, LANES)),
        ],
        out_specs=[row(D_MODEL), row(LANES), pl.BlockSpec((8, tm), lambda i: (0, i))],
        out_shape=[
            jax.ShapeDtypeStruct((t, D_MODEL), F32),
            jax.ShapeDtypeStruct((t, LANES), F32),
            jax.ShapeDtypeStruct((8, t), F32),
        ],
        compiler_params=_cparams(("parallel",)),
        name="merge",
    )(x2, yf, yd, ym, wg, bg, wbf, wbd, wbm, wo, g1, b1, wr, br)


def _cumsum(x, axis):
    n = x.shape[axis]
    idx = lax.broadcasted_iota(jnp.int32, x.shape, axis)
    sh = 1
    while sh < n:
        x = x + jnp.where(idx >= sh, pltpu.roll(x, sh, axis=axis), 0.0)
        sh *= 2
    return x


def _route_kernel(rit_ref, pos_ref, tiles_ref):
    s = rit_ref.shape[1]
    e1 = rit_ref[RI_E1:RI_E1 + 1, :]
    e2 = rit_ref[RI_E2:RI_E2 + 1, :]
    sub = lax.broadcasted_iota(jnp.int32, (N_EXPERTS, s), 0).astype(F32)
    oh1 = jnp.where(sub == e1, 1.0, 0.0)
    oh2 = jnp.where(sub == e2, 1.0, 0.0)
    cnt = oh1 + oh2
    incl = _cumsum(cnt, 1)
    before = incl - cnt
    total = jnp.broadcast_to(incl[:, s - 1:s], (N_EXPERTS, LANES))
    aligned = jnp.floor((total + (SEG_ALIGN - 1)) * (1.0 / SEG_ALIGN)) * SEG_ALIGN
    off = _cumsum(aligned, 0) - aligned
    slot = before + off[:, 0:1]
    pos1 = jnp.sum(oh1 * slot, axis=0, keepdims=True)
    pos2 = jnp.sum(oh2 * slot, axis=0, keepdims=True)
    pos_ref[...] = jnp.concatenate([pos1, pos2], axis=0).astype(jnp.int32)

    nt = jnp.floor((total + (MOE_R - 1)) * (1.0 / MOE_R))
    ct_incl = _cumsum(nt, 0)
    ct_excl = ct_incl - nt
    sub_l = lax.broadcasted_iota(jnp.int32, (N_EXPERTS, LANES), 0).astype(F32)
    tile_i = lax.broadcasted_iota(jnp.int32, (N_EXPERTS, LANES), 1).astype(F32)
    te = jnp.sum(jnp.where(ct_incl <= tile_i, 1.0, 0.0), axis=0, keepdims=True)
    te = jnp.minimum(te, N_EXPERTS - 1.0)
    sel = jnp.where(sub_l == te, 1.0, 0.0)
    k_in = tile_i - ct_excl
    start = jnp.sum(sel * (off + k_in * MOE_R), axis=0, keepdims=True)
    nval = jnp.sum(sel * jnp.clip(total - k_in * MOE_R, 0.0, float(MOE_R)), axis=0, keepdims=True)
    active = tile_i[0:1, :] < ct_incl[N_EXPERTS - 1:N_EXPERTS, :]
    start = jnp.where(active, start, 0.0)
    nval = jnp.where(active, nval, 0.0)
    rows = jnp.concatenate([te, start, nval, jnp.zeros((5, LANES), F32)], axis=0)
    tiles_ref[...] = rows.astype(jnp.int32)


def _route(rit, batch, seq):
    return pl.pallas_call(
        _route_kernel,
        grid=(batch,),
        in_specs=[pl.BlockSpec((8, seq), lambda b: (0, b))],
        out_specs=[
            pl.BlockSpec((None, 2, seq), lambda b: (b, 0, 0)),
            pl.BlockSpec((None, 8, LANES), lambda b: (b, 0, 0)),
        ],
        out_shape=[
            jax.ShapeDtypeStruct((batch, 2, seq), jnp.int32),
            jax.ShapeDtypeStruct((batch, 8, LANES), jnp.int32),
        ],
        compiler_params=_cparams(("parallel",)),
        name="route",
    )(rit)


def _moe_kernel(nd, nt, texp_ref, tstart_ref, tnval_ref, x1_ref, p1_ref, p2_ref, ri_ref,
                wg_ref, wu_ref, wd_ref, g2_ref, b2_ref, o_ref, buf, a_scr, b_scr):
    c = pl.program_id(0)
    j = pl.program_id(1)
    td = x1_ref.shape[0]

    @pl.when(j == 0)
    def _():
        buf[...] = jnp.zeros(buf.shape, F32)

    @pl.when(j < nd)
    def _():
        def body(i, carry):
            base = pl.multiple_of(i * SEG_ALIGN, SEG_ALIGN)
            xv = x1_ref.at[pl.ds(base, SEG_ALIGN), :]
            for u in range(SEG_ALIGN):
                row = xv[u:u + 1, :]
                buf[pl.ds(p1_ref[0, base + u], 1), :] = row
                buf[pl.ds(p2_ref[0, base + u], 1), :] = row
            return carry

        lax.fori_loop(0, td // SEG_ALIGN, body, 0)

    @pl.when((j >= nd) & (j < nd + nt))
    def _():
        idx = c * nt + (j - nd)
        nval = tnval_ref[idx]

        @pl.when(nval > 0)
        def _():
            start = pl.multiple_of(tstart_ref[idx], SEG_ALIGN)
            xt = buf[pl.ds(start, MOE_R), :]
            xb = xt.astype(BF16)
            g = jnp.dot(xb, wg_ref[...].astype(BF16), preferred_element_type=F32)
            u = jnp.dot(xb, wu_ref[...].astype(BF16), preferred_element_type=F32)
            hcat = (g / (1.0 + jnp.exp(-g))) * u
            y = jnp.dot(hcat.astype(BF16), wd_ref[...].astype(BF16), preferred_element_type=F32)
            rows = lax.broadcasted_iota(jnp.int32, (MOE_R, 1), 0)
            buf[pl.ds(start, MOE_R), :] = jnp.where(rows < nval, y, xt)

    @pl.when(j >= nd + nt)
    def _():
        def body(i, carry):
            base = pl.multiple_of(i * SEG_ALIGN, SEG_ALIGN)
            av = a_scr.at[pl.ds(base, SEG_ALIGN), :]
            bv = b_scr.at[pl.ds(base, SEG_ALIGN), :]
            for u in range(SEG_ALIGN):
                av[u:u + 1, :] = buf[pl.ds(p1_ref[0, base + u], 1), :]
                bv[u:u + 1, :] = buf[pl.ds(p2_ref[0, base + u], 1), :]
            return carry

        lax.fori_loop(0, td // SEG_ALIGN, body, 0)
        ri = ri_ref[...]
        m = ri[:, RI_W1:RI_W1 + 1] * a_scr[...] + ri[:, RI_W2:RI_W2 + 1] * b_scr[...]
        z = DEEPNORM_ALPHA * x1_ref[...] + m
        o_ref[...] = _layer_norm(z, g2_ref[...], b2_ref[...])


def _moe(x1, pos3, ri, texp, tstart, tnval, wg, wu, wd, g2, b2, batch, seq):
    t = x1.shape[0]
    td = MOE_TD
    nd = seq // td
    nt = 2 * seq // MOE_R + N_EXPERTS
    buf_rows = 2 * seq + N_EXPERTS * SEG_ALIGN + MOE_R

    def tok_tile(j):
        return jnp.where(j < nd, j, jnp.maximum(j - nd - nt, 0))

    def tile_expert(c, j, texp_ref):
        return texp_ref[c * nt + jnp.clip(j - nd, 0, nt - 1)]

    tok_spec = lambda w: pl.BlockSpec((td, w), lambda c, j, *_: (c * nd + tok_tile(j), 0))
    pos_spec = lambda k: pl.BlockSpec((None, 1, td), lambda c, j, *_: ((c * 2 + k) * nd + tok_tile(j), 0, 0),
                                      memory_space=pltpu.SMEM)
    w_spec = lambda a, b_: pl.BlockSpec((None, a, b_), lambda c, j, te, *_: (tile_expert(c, j, te), 0, 0))
    vec_spec = pl.BlockSpec((1, D_MODEL), lambda c, j, *_: (0, 0))
    grid_spec = pltpu.PrefetchScalarGridSpec(
        num_scalar_prefetch=3,
        grid=(batch, nd + nt + nd),
        in_specs=[
            tok_spec(D_MODEL), pos_spec(0), pos_spec(1), tok_spec(LANES),
            w_spec(D_MODEL, D_EXPERT), w_spec(D_MODEL, D_EXPERT), w_spec(D_EXPERT, D_MODEL),
            vec_spec, vec_spec,
        ],
        out_specs=pl.BlockSpec((td, D_MODEL),
                               lambda c, j, *_: (c * nd + jnp.maximum(j - nd - nt, 0), 0)),
        scratch_shapes=[
            pltpu.VMEM((buf_rows, D_MODEL), F32),
            pltpu.VMEM((td, D_MODEL), F32),
            pltpu.VMEM((td, D_MODEL), F32),
        ],
    )
    return pl.pallas_call(
        functools.partial(_moe_kernel, nd, nt),
        grid_spec=grid_spec,
        out_shape=jax.ShapeDtypeStruct((t, D_MODEL), F32),
        compiler_params=_cparams(("arbitrary", "arbitrary")),
        name="moe",
    )(texp, tstart, tnval, x1, pos3, pos3, ri, wg, wu, wd, g2, b2)


def _qkv_col_scale():
    ones = lambda n: jnp.ones((n,), F32)
    att = FOX_HEADS * HEAD_DIM
    qs = lambda d: jnp.full((att,), LOG2E * d ** -0.5, F32)
    return jnp.concatenate([qs(HEAD_DIM), ones(2 * att), qs(HEAD_DIM), ones(2 * att),
                            qs(MEM_HEAD_DIM)]).reshape(1, QKV_COLS)


def kernel(x, mem, w_in, b_forget, b_gates, lambda_q1, lambda_k1, lambda_q2, lambda_k2, diff_subln_g,
           w_mem_kv, w_branch_fox, w_branch_diff, w_branch_mem, w_out, ln1_g, ln1_b, w_router_group,
           b_router_group, w_router_expert, b_router_expert, w_expert_gate, w_expert_up, w_expert_down,
           ln2_g, ln2_b):
    batch, seq, d = x.shape
    t = batch * seq
    l = 0
    x2 = x.reshape(t, d)

    w_in_t = jnp.swapaxes(w_in, 1, 2)
    w_gates = w_in_t[l, GATE_COL0:FL_COL0, :].astype(BF16)
    w_fl_t = jnp.pad(w_in_t[l, FL_COL0:, :], ((0, 16 - FOX_HEADS), (0, 0))).astype(BF16)
    w_r = jnp.concatenate([w_router_group[l], w_router_expert[l]], axis=1)
    w_r = jnp.pad(w_r, ((0, 0), (0, LANES - w_r.shape[1]))).astype(BF16)
    b_r = jnp.concatenate([b_router_group[l], b_router_expert[l]])
    b_r = jnp.pad(b_r, (0, LANES - b_r.shape[0])).reshape(1, LANES)
    lam_params = jnp.stack([lambda_q1[l], lambda_k1[l], lambda_q2[l], lambda_k2[l]])
    slopes = 2.0 ** (-8.0 * jnp.arange(1, DIFF_HEADS + 1, dtype=F32) / DIFF_HEADS)

    qkv, fl_t = _proj(x2, w_in_t, w_fl_t, _qkv_col_scale())
    qkv4 = qkv.reshape(N_SLABS, batch, seq, LANES)
    c = _fscan(fl_t, b_forget[l].reshape(FOX_HEADS, 1), batch, seq)
    c4 = c.reshape(FOX_HEADS // 2, 2, t)

    y_fox = _fox(qkv4, c4, batch, seq)
    y_diff = _diff(qkv4, slopes, lam_params, diff_subln_g[l].reshape(LANES, 1), batch, seq)
    mkv = _memkv(mem, w_mem_kv[l].astype(BF16))
    y_mem = _mem_attn(qkv4, mkv, batch, seq)

    x1, ri, rit = _merge(
        x2, y_fox.reshape(t, -1), y_diff.reshape(t, -1), y_mem.reshape(t, -1),
        w_gates, b_gates[l].reshape(1, -1),
        w_branch_fox[l].astype(BF16), w_branch_diff[l].astype(BF16), w_branch_mem[l].astype(BF16),
        w_out[l].astype(BF16), ln1_g[l].reshape(1, d), ln1_b[l].reshape(1, d), w_r, b_r)

    pos, tiles = _route(rit, batch, seq)
    nt = 2 * seq // MOE_R + N_EXPERTS
    texp, tstart, tnval = (tiles[:, r, :nt].reshape(-1) for r in range(3))
    pos3 = pos.reshape(batch * 2 * (seq // MOE_TD), 1, MOE_TD)
    out = _moe(x1, pos3, ri, texp, tstart, tnval,
               w_expert_gate[l], w_expert_up[l], w_expert_down[l],
               ln2_g[l].reshape(1, d), ln2_b[l].reshape(1, d), batch, seq)
    return out.reshape(batch, seq, d)
```

```python
import functools
import math

import jax
import jax.numpy as jnp
from jax import lax
from jax.experimental import pallas as pl
from jax.experimental.pallas import tpu as pltpu

F32 = jnp.float32
BF16 = jnp.bfloat16

D_MODEL = 1024
HEAD_DIM = 64
FOX_HEADS = 8
DIFF_HEADS = 4
MEM_HEADS = 4
MEM_HEAD_DIM = 128
N_MEM = 256
N_BRANCH = 3
N_GROUPS = 4
EXPERTS_PER_GROUP = 8
N_EXPERTS = N_GROUPS * EXPERTS_PER_GROUP
D_EXPERT = 256
LN_EPS = 1e-5
DEPTH = 1
DEEPNORM_ALPHA = (2.0 * DEPTH) ** 0.25
LAM_INIT = 0.8 - 0.6 * math.exp(-0.3 * 0)
LOG2E = math.log2(math.e)

LANES = 128
QKV_COLS = 3584
N_SLABS = QKV_COLS // LANES
GATE_COL0 = QKV_COLS
FL_COL0 = QKV_COLS + N_BRANCH * D_MODEL
SLAB_FOX_Q, SLAB_FOX_K, SLAB_FOX_V = 0, 4, 8
SLAB_DIFF_Q, SLAB_DIFF_K, SLAB_DIFF_V = 12, 16, 20
SLAB_MEM_Q = 24
ROUTE_LANE0 = N_GROUPS

NEG_BIG = -1e30
VMEM_LIMIT = 56 * 1024 * 1024

PROJ_TM = 2048
PROJ_TN = 512
ATT_T = 256
N_CHAINS = 4
NSL = N_CHAINS // 2
N_BIAS_LANES = 3
DIFF_ONES_ROWS = 16
SETUP_ROWS = 512
MEM_TQ = 4096
MERGE_TM = 1024
MOE_R = 256
MOE_TD = 256
SEG_ALIGN = 8
ROW_TILE = D_MODEL // LANES
RI_E1, RI_E2, RI_W1, RI_W2 = 0, 1, 2, 3


def _cparams(sem, flags=None):
    return pltpu.CompilerParams(dimension_semantics=sem, vmem_limit_bytes=VMEM_LIMIT, flags=flags)


ATT_FLAGS = None


def _proj_kernel(x_ref, w_ref, wfl_ref, sc_ref, o_ref, fl_ref, xb_ref):
    j = pl.program_id(1)

    @pl.when(j == 0)
    def _():
        xb = x_ref[...].astype(BF16)
        xb_ref[...] = xb
        fl_ref[...] = lax.dot_general(wfl_ref[...], xb, (((1,), (1,)), ((), ())),
                                      preferred_element_type=F32)

    acc = lax.dot_general(xb_ref[...], w_ref[...].astype(BF16), (((1,), (1,)), ((), ())),
                          preferred_element_type=F32) * sc_ref[...]
    for c in range(PROJ_TN // LANES):
        o_ref[c] = acc[:, c * LANES:(c + 1) * LANES].astype(BF16)


def _proj(x2, w_in_t, w_fl_t, col_scale):
    t = x2.shape[0]
    spc = PROJ_TN // LANES
    return pl.pallas_call(
        _proj_kernel,
        grid=(t // PROJ_TM, QKV_COLS // PROJ_TN),
        in_specs=[
            pl.BlockSpec((PROJ_TM, D_MODEL), lambda i, j: (i, 0)),
            pl.BlockSpec((None, PROJ_TN, D_MODEL), lambda i, j: (0, j, 0)),
            pl.BlockSpec((16, D_MODEL), lambda i, j: (0, 0)),
            pl.BlockSpec((1, PROJ_TN), lambda i, j: (0, j)),
        ],
        out_specs=[
            pl.BlockSpec((spc, PROJ_TM, LANES), lambda i, j: (j, i, 0)),
            pl.BlockSpec((16, PROJ_TM), lambda i, j: (0, i)),
        ],
        out_shape=[
            jax.ShapeDtypeStruct((N_SLABS, t, LANES), BF16),
            jax.ShapeDtypeStruct((16, t), F32),
        ],
        scratch_shapes=[pltpu.VMEM((PROJ_TM, D_MODEL), BF16)],
        compiler_params=_cparams(("parallel", "arbitrary")),
        name="proj",
    )(x2, w_in_t, w_fl_t, col_scale)


def _fscan_kernel(fl_ref, b_ref, c_ref):
    s = fl_ref.shape[1]
    z = fl_ref[0:FOX_HEADS, :] + b_ref[...]
    lf = jnp.minimum(z, 0.0) - jnp.log(1.0 + jnp.exp(-jnp.abs(z)))
    lane = lax.broadcasted_iota(jnp.int32, lf.shape, 1)
    sh = 1
    while sh < s:
        r = pltpu.roll(lf, sh, axis=1)
        lf = lf + jnp.where(lane >= sh, r, 0.0)
        sh *= 2
    c_ref[...] = lf * LOG2E


def _fscan(fl_t, b_forget_col, batch, seq):
    t = fl_t.shape[1]
    return pl.pallas_call(
        _fscan_kernel,
        grid=(batch,),
        in_specs=[
            pl.BlockSpec((16, seq), lambda b: (0, b)),
            pl.BlockSpec((FOX_HEADS, 1), lambda b: (0, 0)),
        ],
        out_specs=pl.BlockSpec((FOX_HEADS, seq), lambda b: (0, b)),
        out_shape=jax.ShapeDtypeStruct((FOX_HEADS, t), F32),
        compiler_params=_cparams(("parallel",)),
        name="fscan",
    )(fl_t, b_forget_col)


def _flash_chains(qi, tq, qh_scr, k_fn, vt_fn, m_scr, acc_scr, s_scr, pm_scr):
    m_scr[...] = jnp.full(m_scr.shape, NEG_BIG, F32)
    acc_scr[...] = jnp.zeros(acc_scr.shape, F32)

    def scores(j, slot, masked):
        start = pl.multiple_of(j * tq, tq)
        for c in range(N_CHAINS):
            s = lax.dot_general(k_fn(c, start), qh_scr[c], (((1,), (1,)), ((), ())),
                                preferred_element_type=F32)
            if masked:
                kv = lax.broadcasted_iota(jnp.int32, (tq, tq), 0)
                q = lax.broadcasted_iota(jnp.int32, (tq, tq), 1)
                s = jnp.where(kv <= q, s, NEG_BIG)
            s_scr[slot, c] = s
            pm_scr[slot, c] = jnp.max(s.reshape(tq // 8, 8, tq), axis=0)

    def consume(j, slot):
        start = pl.multiple_of(j * tq, tq)
        for c in range(N_CHAINS):
            m_prev = m_scr[c]
            m_new = jnp.maximum(m_prev, jnp.max(pm_scr[slot, c], axis=0, keepdims=True))
            alpha = jnp.exp2(m_prev - m_new)
            p = jnp.exp2(s_scr[slot, c] - m_new)
            m_scr[c] = m_new
            pv = jnp.dot(vt_fn(c, start), p.astype(BF16), preferred_element_type=F32)
            acc_scr[c] = acc_scr[c] * alpha + pv

    @pl.when(qi == 0)
    def _():
        scores(0, 0, True)
        consume(0, 0)

    @pl.when(qi > 0)
    def _():
        scores(0, 0, False)

        def body(jj, carry):
            j = 2 * jj
            scores(j + 1, 1, False)
            consume(j, 0)
            scores(j + 2, 0, False)
            consume(j + 1, 1)
            return carry

        lax.fori_loop(0, (qi - 1) // 2, body, 0)

        @pl.when(qi % 2 == 1)
        def _():
            scores(qi, 1, True)
            consume(qi - 1, 0)
            consume(qi, 1)

        @pl.when(qi % 2 == 0)
        def _():
            scores(qi - 1, 1, False)
            consume(qi - 2, 0)
            scores(qi, 0, True)
            consume(qi - 1, 1)
            consume(qi, 0)


def _own_and_spare(shape, c):
    lane = lax.broadcasted_iota(jnp.int32, shape, 1)
    own = (lane < HEAD_DIM) if c % 2 == 0 else (lane >= HEAD_DIM)
    spare = HEAD_DIM if c % 2 == 0 else 0
    return lane, own, spare


def _split_q(q_ref, qh_scr):
    for c in range(N_CHAINS):
        q = q_ref[c // 2].astype(F32)
        lane, own, spare = _own_and_spare(q.shape, c)
        ones = (lane >= spare) & (lane < spare + N_BIAS_LANES)
        qh_scr[c] = jnp.where(own, q, jnp.where(ones, 1.0, 0.0)).astype(BF16)


def _augment_k(k, bias, c):
    lane, own, spare = _own_and_spare(k.shape, c)
    hi = bias.astype(BF16).astype(F32)
    r1 = bias - hi
    mid = r1.astype(BF16).astype(F32)
    lo = (r1 - mid).astype(BF16).astype(F32)
    extra = jnp.where(lane == spare, hi, jnp.where(lane == spare + 1, mid, jnp.where(lane == spare + 2, lo, 0.0)))
    return jnp.where(own, k.astype(F32), extra).astype(BF16)


def _transpose_bf16(a):
    return a.astype(F32).T.astype(BF16)


def _attn_scratch(tq, seq, acc_rows, vt_slabs):
    return [
        pltpu.VMEM((N_CHAINS, tq, LANES), BF16),
        pltpu.VMEM((N_CHAINS, 1, tq), F32),
        pltpu.VMEM((N_CHAINS, acc_rows, tq), F32),
        pltpu.VMEM((2, N_CHAINS, tq, tq), F32),
        pltpu.VMEM((2, N_CHAINS, 8, tq), F32),
        pltpu.VMEM((N_CHAINS, seq, LANES), BF16),
        pltpu.VMEM((vt_slabs, acc_rows, seq), BF16),
    ]


def _fox_kernel(q_ref, k_ref, v_ref, c_ref, o_ref, qh_scr, m_scr, acc_scr, s_scr, pm_scr, ka_scr, vat_scr):
    tq = q_ref.shape[1]
    seq = v_ref.shape[1]
    qi = pl.program_id(2)
    _split_q(q_ref, qh_scr)

    @pl.when(qi == 0)
    def _():
        def chunk(i, carry):
            r0 = pl.multiple_of(i * SETUP_ROWS, SETUP_ROWS)
            rows = pl.ds(r0, SETUP_ROWS)
            head_row = lax.broadcasted_iota(jnp.int32, (LANES, SETUP_ROWS), 0) < HEAD_DIM
            for sl in range(NSL):
                vt = _transpose_bf16(v_ref[sl, rows, :])
                one = jnp.ones_like(vt)
                vat_scr[2 * sl, :, rows] = jnp.where(head_row, vt, one)
                vat_scr[2 * sl + 1, :, rows] = jnp.where(head_row, one, vt)
                cc = c_ref[sl, :, rows]
                stacked = jnp.where(head_row, jnp.broadcast_to(cc[1:2, :], (LANES, SETUP_ROWS)),
                                    jnp.broadcast_to(cc[0:1, :], (LANES, SETUP_ROWS)))
                bias = -stacked.T
                k = k_ref[sl, rows, :]
                ka_scr[2 * sl, rows, :] = _augment_k(k, bias, 0)
                ka_scr[2 * sl + 1, rows, :] = _augment_k(k, bias, 1)
            return carry

        lax.fori_loop(0, seq // SETUP_ROWS, chunk, 0)

    def k_fn(c, start):
        return ka_scr[c, pl.ds(start, tq), :]

    def vt_fn(c, start):
        return vat_scr[c, :, pl.ds(start, tq)]

    _flash_chains(qi, tq, qh_scr, k_fn, vt_fn, m_scr, acc_scr, s_scr, pm_scr)
    for sl in range(NSL):
        a0 = acc_scr[2 * sl]
        a1 = acc_scr[2 * sl + 1]
        row = lax.broadcasted_iota(jnp.int32, a0.shape, 0)
        ot = jnp.where(row < HEAD_DIM, a0 / a0[HEAD_DIM:HEAD_DIM + 1, :], a1 / a1[0:1, :])
        o_ref[:, sl * LANES:(sl + 1) * LANES] = ot.T.astype(o_ref.dtype)


def _fox(qkv4, c4, batch, seq):
    tq = ATT_T
    return pl.pallas_call(
        _fox_kernel,
        grid=(batch, FOX_HEADS // N_CHAINS, seq // tq),
        in_specs=[
            pl.BlockSpec((NSL, None, tq, LANES), lambda b, g, i: (SLAB_FOX_Q // NSL + g, b, i, 0)),
            pl.BlockSpec((NSL, None, seq, LANES), lambda b, g, i: (SLAB_FOX_K // NSL + g, b, 0, 0)),
            pl.BlockSpec((NSL, None, seq, LANES), lambda b, g, i: (SLAB_FOX_V // NSL + g, b, 0, 0)),
            pl.BlockSpec((NSL, 2, seq), lambda b, g, i: (g, 0, b)),
        ],
        out_specs=pl.BlockSpec((None, tq, NSL * LANES), lambda b, g, i: (b, i, g)),
        out_shape=jax.ShapeDtypeStruct((batch, seq, FOX_HEADS * HEAD_DIM), BF16),
        scratch_shapes=_attn_scratch(tq, seq, LANES, N_CHAINS),
        compiler_params=_cparams(("parallel", "parallel", "arbitrary"), ATT_FLAGS),
        name="fox",
    )(qkv4, qkv4, qkv4, c4)


def _diff_kernel(slopes_ref, q_ref, k_ref, v_ref, lam_ref, g_ref, o_ref, qh_scr, m_scr, acc_scr, s_scr, pm_scr,
                 ka_scr, vat_scr):
    tq = q_ref.shape[1]
    seq = v_ref.shape[1]
    qi = pl.program_id(2)
    _split_q(q_ref, qh_scr)
    g = pl.program_id(1)

    @pl.when(qi == 0)
    def _():
        def chunk(i, carry):
            r0 = pl.multiple_of(i * SETUP_ROWS, SETUP_ROWS)
            rows = pl.ds(r0, SETUP_ROWS)
            for hd in range(NSL):
                vt = _transpose_bf16(v_ref[hd, rows, :])
                vat_scr[hd, :, rows] = jnp.concatenate([vt, jnp.ones((DIFF_ONES_ROWS, SETUP_ROWS), BF16)], axis=0)
                pos = (lax.broadcasted_iota(jnp.int32, (SETUP_ROWS, LANES), 0) + r0).astype(F32)
                bias = pos * (slopes_ref[NSL * g + hd] * LOG2E)
                k = k_ref[hd, rows, :]
                ka_scr[2 * hd, rows, :] = _augment_k(k, bias, 0)
                ka_scr[2 * hd + 1, rows, :] = _augment_k(k, bias, 1)
            return carry

        lax.fori_loop(0, seq // SETUP_ROWS, chunk, 0)

    def k_fn(c, start):
        return ka_scr[c, pl.ds(start, tq), :]

    def vt_fn(c, start):
        return vat_scr[c // 2, :, pl.ds(start, tq)]

    _flash_chains(qi, tq, qh_scr, k_fn, vt_fn, m_scr, acc_scr, s_scr, pm_scr)
    lp = lam_ref[...]
    s1 = jnp.sum(lp[0:1, :] * lp[1:2, :], axis=1, keepdims=True)
    s2 = jnp.sum(lp[2:3, :] * lp[3:4, :], axis=1, keepdims=True)
    lam = jnp.exp(s1) - jnp.exp(s2) + LAM_INIT
    for hd in range(NSL):
        a1 = acc_scr[2 * hd]
        a2 = acc_scr[2 * hd + 1]
        ot = a1[:LANES, :] / a1[LANES:LANES + 1, :] - lam * (a2[:LANES, :] / a2[LANES:LANES + 1, :])
        ms = jnp.mean(ot * ot, axis=0, keepdims=True)
        yt = ot * lax.rsqrt(ms + LN_EPS) * g_ref[...]
        o_ref[:, hd * LANES:(hd + 1) * LANES] = (yt * (1.0 - LAM_INIT)).T.astype(o_ref.dtype)


def _diff(qkv4, slopes, lam_params, subln_g_col, batch, seq):
    tq = ATT_T
    grid_spec = pltpu.PrefetchScalarGridSpec(
        num_scalar_prefetch=1,
        grid=(batch, DIFF_HEADS // NSL, seq // tq),
        in_specs=[
            pl.BlockSpec((NSL, None, tq, LANES), lambda b, g, i, s: (SLAB_DIFF_Q // NSL + g, b, i, 0)),
            pl.BlockSpec((NSL, None, seq, LANES), lambda b, g, i, s: (SLAB_DIFF_K // NSL + g, b, 0, 0)),
            pl.BlockSpec((NSL, None, seq, LANES), lambda b, g, i, s: (SLAB_DIFF_V // NSL + g, b, 0, 0)),
            pl.BlockSpec((4, HEAD_DIM), lambda b, g, i, s: (0, 0)),
            pl.BlockSpec((LANES, 1), lambda b, g, i, s: (0, 0)),
        ],
        out_specs=pl.BlockSpec((None, tq, NSL * LANES), lambda b, g, i, s: (b, i, g)),
        scratch_shapes=_attn_scratch(tq, seq, LANES + DIFF_ONES_ROWS, NSL),
    )
    return pl.pallas_call(
        _diff_kernel,
        grid_spec=grid_spec,
        out_shape=jax.ShapeDtypeStruct((batch, seq, DIFF_HEADS * LANES), BF16),
        compiler_params=_cparams(("parallel", "parallel", "arbitrary"), ATT_FLAGS),
        name="diff",
    )(slopes, qkv4, qkv4, qkv4, lam_params, subln_g_col)


def _memkv_kernel(mem_ref, w_ref, o_ref):
    o_ref[...] = jnp.dot(mem_ref[...].astype(BF16), w_ref[...],
                         preferred_element_type=F32).astype(o_ref.dtype)


def _memkv(mem, w_kv):
    batch = mem.shape[0]
    width = w_kv.shape[1]
    return pl.pallas_call(
        _memkv_kernel,
        grid=(batch,),
        in_specs=[
            pl.BlockSpec((None, N_MEM, D_MODEL), lambda b: (b, 0, 0)),
            pl.BlockSpec((D_MODEL, width), lambda b: (0, 0)),
        ],
        out_specs=pl.BlockSpec((None, N_MEM, width), lambda b: (b, 0, 0)),
        out_shape=jax.ShapeDtypeStruct((batch, N_MEM, width), BF16),
        compiler_params=_cparams(("parallel",)),
        name="mem_kv",
    )(mem, w_kv)


def _mem_kernel(q_ref, mk_ref, mv_ref, o_ref):
    s = lax.dot_general(q_ref[...], mk_ref[...], (((1,), (1,)), ((), ())),
                        preferred_element_type=F32)
    m = jnp.max(s, axis=1, keepdims=True)
    p = jnp.exp2(s - m)
    l = jnp.sum(p, axis=1, keepdims=True)
    o = jnp.dot(p.astype(BF16), mv_ref[...], preferred_element_type=F32)
    o_ref[...] = (o / l).astype(o_ref.dtype)


def _mem_attn(qkv4, mkv, batch, seq):
    tq = MEM_TQ
    return pl.pallas_call(
        _mem_kernel,
        grid=(batch, MEM_HEADS, seq // tq),
        in_specs=[
            pl.BlockSpec((None, None, tq, LANES), lambda b, h, i: (SLAB_MEM_Q + h, b, i, 0)),
            pl.BlockSpec((None, N_MEM, LANES), lambda b, h, i: (b, 0, h)),
            pl.BlockSpec((None, N_MEM, LANES), lambda b, h, i: (b, 0, MEM_HEADS + h)),
        ],
        out_specs=pl.BlockSpec((None, tq, LANES), lambda b, h, i: (b, i, h)),
        out_shape=jax.ShapeDtypeStruct((batch, seq, MEM_HEADS * LANES), BF16),
        compiler_params=_cparams(("parallel", "parallel", "parallel")),
        name="mem",
    )(qkv4, mkv, mkv)


def _layer_norm(z, g, b):
    mu = jnp.mean(z, axis=1, keepdims=True)
    zc = z - mu
    var = jnp.mean(zc * zc, axis=1, keepdims=True)
    return zc * lax.rsqrt(var + LN_EPS) * g + b


def _lane_max(v):
    return jnp.max(v, axis=1, keepdims=True)


def _routing_info(logits):
    lane = lax.broadcasted_iota(jnp.int32, logits.shape, 1)
    big = jnp.int32(2 * LANES)
    is_g = lane < N_GROUPS
    gl = jnp.where(is_g, logits, NEG_BIG)
    gmax = _lane_max(gl)
    g_w = 1.0 / jnp.sum(jnp.exp(gl - gmax), axis=1, keepdims=True)
    g_idx = jnp.min(jnp.where(gl == gmax, lane, big), axis=1, keepdims=True)
    lo = ROUTE_LANE0 + g_idx * EXPERTS_PER_GROUP
    in_grp = (lane >= lo) & (lane < lo + EXPERTS_PER_GROUP)
    el = jnp.where(in_grp, logits, NEG_BIG)
    e1 = _lane_max(el)
    i1 = jnp.min(jnp.where(el == e1, lane, big), axis=1, keepdims=True)
    el2 = jnp.where(lane == i1, NEG_BIG, el)
    e2 = _lane_max(el2)
    i2 = jnp.min(jnp.where(el2 == e2, lane, big), axis=1, keepdims=True)
    r = jnp.exp(e2 - e1)
    w1 = g_w / (1.0 + r)
    w2 = g_w * r / (1.0 + r)
    id1 = (i1 - ROUTE_LANE0).astype(F32)
    id2 = (i2 - ROUTE_LANE0).astype(F32)
    return jnp.where(lane == RI_E1, id1,
                     jnp.where(lane == RI_E2, id2,
                               jnp.where(lane == RI_W1, w1, jnp.where(lane == RI_W2, w2, 0.0))))


def _merge_kernel(x_ref, yf_ref, yd_ref, ym_ref, wg_ref, bg_ref, wbf_ref, wbd_ref, wbm_ref,
                  wo_ref, g1_ref, b1_ref, wr_ref, br_ref, x1_ref, ri_ref, rit_ref):
    x = x_ref[...]
    xb = x.astype(BF16)
    h = None
    for i, (y_ref, wb_ref) in enumerate(((yf_ref, wbf_ref), (yd_ref, wbd_ref), (ym_ref, wbm_ref))):
        cols = slice(i * D_MODEL, (i + 1) * D_MODEL)
        gl = lax.dot_general(xb, wg_ref[cols, :], (((1,), (1,)), ((), ())),
                             preferred_element_type=F32) + bg_ref[:, cols]
        gate = 1.0 / (1.0 + jnp.exp(-gl))
        br = jnp.dot(y_ref[...], wb_ref[...], preferred_element_type=F32)
        h = gate * br if h is None else h + gate * br
    o = jnp.dot(h.astype(BF16), wo_ref[...], preferred_element_type=F32)
    x1 = _layer_norm(DEEPNORM_ALPHA * x + o, g1_ref[...], b1_ref[...])
    x1_ref[...] = x1
    logits = jnp.dot(x1.astype(BF16), wr_ref[...], preferred_element_type=F32) + br_ref[...]
    ri = _routing_info(logits)
    ri_ref[...] = ri
    rit_ref[...] = ri.T[0:8, :]


def _const_spec(shape):
    return pl.BlockSpec(shape, lambda i: (0,) * len(shape), pipeline_mode=pl.Buffered(1))


def _merge(x2, yf, yd, ym, wg, bg, wbf, wbd, wbm, wo, g1, b1, wr, br):
    t = x2.shape[0]
    tm = MERGE_TM
    half = yf.shape[1]
    row = lambda w: pl.BlockSpec((tm, w), lambda i: (i, 0))
    return pl.pallas_call(
        _merge_kernel,
        grid=(t // tm,),
        in_specs=[
            row(D_MODEL), row(half), row(half), row(half),
            _const_spec((N_BRANCH * D_MODEL, D_MODEL)), _const_spec((1, N_BRANCH * D_MODEL)),
            _const_spec((half, D_MODEL)), _const_spec((half, D_MODEL)), _const_spec((half, D_MODEL)),
            _const_spec((D_MODEL, D_MODEL)), _const_spec((1, D_MODEL)), _const_spec((1, D_MODEL)),
            _const_spec((D_MODEL, LANES)), _const_spec((1, LANES)),
        ],
        out_specs=[row(D_MODEL), row(LANES), pl.BlockSpec((8, tm), lambda i: (0, i))],
        out_shape=[
            jax.ShapeDtypeStruct((t, D_MODEL), F32),
            jax.ShapeDtypeStruct((t, LANES), F32),
            jax.ShapeDtypeStruct((8, t), F32),
        ],
        compiler_params=_cparams(("parallel",)),
        name="merge",
    )(x2, yf, yd, ym, wg, bg, wbf, wbd, wbm, wo, g1, b1, wr, br)


def _cumsum(x, axis):
    n = x.shape[axis]
    idx = lax.broadcasted_iota(jnp.int32, x.shape, axis)
    sh = 1
    while sh < n:
        x = x + jnp.where(idx >= sh, pltpu.roll(x, sh, axis=axis), 0.0)
        sh *= 2
    return x


def _route_kernel(rit_ref, pos_ref, tiles_ref):
    s = rit_ref.shape[1]
    e1 = rit_ref[RI_E1:RI_E1 + 1, :]
    e2 = rit_ref[RI_E2:RI_E2 + 1, :]
    sub = lax.broadcasted_iota(jnp.int32, (N_EXPERTS, s), 0).astype(F32)
    oh1 = jnp.where(sub == e1, 1.0, 0.0)
    oh2 = jnp.where(sub == e2, 1.0, 0.0)
    cnt = oh1 + oh2
    incl = _cumsum(cnt, 1)
    before = incl - cnt
    total = jnp.broadcast_to(incl[:, s - 1:s], (N_EXPERTS, LANES))
    aligned = jnp.floor((total + (SEG_ALIGN - 1)) * (1.0 / SEG_ALIGN)) * SEG_ALIGN
    off = _cumsum(aligned, 0) - aligned
    slot = before + off[:, 0:1]
    pos1 = jnp.sum(oh1 * slot, axis=0, keepdims=True)
    pos2 = jnp.sum(oh2 * slot, axis=0, keepdims=True)
    pos_ref[...] = (jnp.concatenate([pos1, pos2], axis=0) * ROW_TILE).astype(jnp.int32)

    nt = jnp.floor((total + (MOE_R - 1)) * (1.0 / MOE_R))
    ct_incl = _cumsum(nt, 0)
    ct_excl = ct_incl - nt
    sub_l = lax.broadcasted_iota(jnp.int32, (N_EXPERTS, LANES), 0).astype(F32)
    tile_i = lax.broadcasted_iota(jnp.int32, (N_EXPERTS, LANES), 1).astype(F32)
    te = jnp.sum(jnp.where(ct_incl <= tile_i, 1.0, 0.0), axis=0, keepdims=True)
    te = jnp.minimum(te, N_EXPERTS - 1.0)
    sel = jnp.where(sub_l == te, 1.0, 0.0)
    k_in = tile_i - ct_excl
    start = jnp.sum(sel * (off + k_in * MOE_R), axis=0, keepdims=True)
    nval = jnp.sum(sel * jnp.clip(total - k_in * MOE_R, 0.0, float(MOE_R)), axis=0, keepdims=True)
    active = tile_i[0:1, :] < ct_incl[N_EXPERTS - 1:N_EXPERTS, :]
    start = jnp.where(active, start, 0.0)
    nval = jnp.where(active, nval, 0.0)
    rows = jnp.concatenate([te, start * ROW_TILE, nval, jnp.zeros((5, LANES), F32)], axis=0)
    tiles_ref[...] = rows.astype(jnp.int32)


def _route(rit, batch, seq):
    return pl.pallas_call(
        _route_kernel,
        grid=(batch,),
        in_specs=[pl.BlockSpec((8, seq), lambda b: (0, b))],
        out_specs=[
            pl.BlockSpec((None, 2, seq), lambda b: (b, 0, 0)),
            pl.BlockSpec((None, 8, LANES), lambda b: (b, 0, 0)),
        ],
        out_shape=[
            jax.ShapeDtypeStruct((batch, 2, seq), jnp.int32),
            jax.ShapeDtypeStruct((batch, 8, LANES), jnp.int32),
        ],
        compiler_params=_cparams(("parallel",)),
        name="route",
    )(rit)


def _read_rows(view, n_rows):
    return jnp.concatenate([view[pl.ds(k, n_rows, stride=ROW_TILE), :] for k in range(ROW_TILE)], axis=1)


def _write_rows(view, n_rows, val):
    for k in range(ROW_TILE):
        view[pl.ds(k, n_rows, stride=ROW_TILE), :] = val[:, k * LANES:(k + 1) * LANES]


def _moe_kernel(nd, nt, texp_ref, tstart_ref, tnval_ref, x1_ref, p1_ref, p2_ref, ri_ref,
                wg_ref, wu_ref, wd_ref, g2_ref, b2_ref, o_ref, buf, xs_scr, as_scr, bs_scr):
    c = pl.program_id(0)
    j = pl.program_id(1)
    td = x1_ref.shape[0]

    def tile_at(ref, off):
        return ref.at[pl.ds(pl.multiple_of(off, ROW_TILE), ROW_TILE), :]

    @pl.when(j == 0)
    def _():
        buf[...] = jnp.zeros(buf.shape, F32)

    @pl.when(j < nd)
    def _():
        _write_rows(xs_scr, td, x1_ref[...])

        def body(i, carry):
            for u in range(SEG_ALIGN):
                t = i * SEG_ALIGN + u
                row = tile_at(xs_scr, t * ROW_TILE)[...]
                tile_at(buf, p1_ref[0, t])[...] = row
                tile_at(buf, p2_ref[0, t])[...] = row
            return carry

        lax.fori_loop(0, td // SEG_ALIGN, body, 0)

    @pl.when((j >= nd) & (j < nd + nt))
    def _():
        idx = c * nt + (j - nd)
        nval = tnval_ref[idx]

        @pl.when(nval > 0)
        def _():
            start = pl.multiple_of(tstart_ref[idx], ROW_TILE * SEG_ALIGN)
            view = buf.at[pl.ds(start, MOE_R * ROW_TILE), :]
            xt = _read_rows(view, MOE_R)
            xb = xt.astype(BF16)
            g = jnp.dot(xb, wg_ref[...].astype(BF16), preferred_element_type=F32)
            u = jnp.dot(xb, wu_ref[...].astype(BF16), preferred_element_type=F32)
            hcat = (g / (1.0 + jnp.exp(-g))) * u
            y = jnp.dot(hcat.astype(BF16), wd_ref[...].astype(BF16), preferred_element_type=F32)
            rows = lax.broadcasted_iota(jnp.int32, (MOE_R, 1), 0)
            _write_rows(view, MOE_R, jnp.where(rows < nval, y, xt))

    @pl.when(j >= nd + nt)
    def _():
        def body(i, carry):
            for u in range(SEG_ALIGN):
                t = i * SEG_ALIGN + u
                tile_at(as_scr, t * ROW_TILE)[...] = tile_at(buf, p1_ref[0, t])[...]
                tile_at(bs_scr, t * ROW_TILE)[...] = tile_at(buf, p2_ref[0, t])[...]
            return carry

        lax.fori_loop(0, td // SEG_ALIGN, body, 0)
        ri = ri_ref[...]
        m = ri[:, RI_W1:RI_W1 + 1] * _read_rows(as_scr, td) + ri[:, RI_W2:RI_W2 + 1] * _read_rows(bs_scr, td)
        z = DEEPNORM_ALPHA * x1_ref[...] + m
        o_ref[...] = _layer_norm(z, g2_ref[...], b2_ref[...])


def _moe(x1, pos3, ri, texp, tstart, tnval, wg, wu, wd, g2, b2, batch, seq):
    t = x1.shape[0]
    td = MOE_TD
    nd = seq // td
    nt = 2 * seq // MOE_R + N_EXPERTS
    buf_rows = 2 * seq + N_EXPERTS * SEG_ALIGN + MOE_R

    def tok_tile(j):
        return jnp.where(j < nd, j, jnp.maximum(j - nd - nt, 0))

    def tile_expert(c, j, texp_ref):
        return texp_ref[c * nt + jnp.clip(j - nd, 0, nt - 1)]

    tok_spec = lambda w: pl.BlockSpec((td, w), lambda c, j, *_: (c * nd + tok_tile(j), 0))
    pos_spec = lambda k: pl.BlockSpec((None, 1, td), lambda c, j, *_: ((c * 2 + k) * nd + tok_tile(j), 0, 0),
                                      memory_space=pltpu.SMEM)
    w_spec = lambda a, b_: pl.BlockSpec((None, a, b_), lambda c, j, te, *_: (tile_expert(c, j, te), 0, 0))
    vec_spec = pl.BlockSpec((1, D_MODEL), lambda c, j, *_: (0, 0))
    grid_spec = pltpu.PrefetchScalarGridSpec(
        num_scalar_prefetch=3,
        grid=(batch, nd + nt + nd),
        in_specs=[
            tok_spec(D_MODEL), pos_spec(0), pos_spec(1), tok_spec(LANES),
            w_spec(D_MODEL, D_EXPERT), w_spec(D_MODEL, D_EXPERT), w_spec(D_EXPERT, D_MODEL),
            vec_spec, vec_spec,
        ],
        out_specs=pl.BlockSpec((td, D_MODEL),
                               lambda c, j, *_: (c * nd + jnp.maximum(j - nd - nt, 0), 0)),
        scratch_shapes=[
            pltpu.VMEM((buf_rows * ROW_TILE, LANES), F32),
            pltpu.VMEM((td * ROW_TILE, LANES), F32),
            pltpu.VMEM((td * ROW_TILE, LANES), F32),
            pltpu.VMEM((td * ROW_TILE, LANES), F32),
        ],
    )
    return pl.pallas_call(
        functools.partial(_moe_kernel, nd, nt),
        grid_spec=grid_spec,
        out_shape=jax.ShapeDtypeStruct((t, D_MODEL), F32),
        compiler_params=_cparams(("arbitrary", "arbitrary")),
        name="moe",
    )(texp, tstart, tnval, x1, pos3, pos3, ri, wg, wu, wd, g2, b2)


def _qkv_col_scale():
    ones = lambda n: jnp.ones((n,), F32)
    att = FOX_HEADS * HEAD_DIM
    qs = lambda d: jnp.full((att,), LOG2E * d ** -0.5, F32)
    return jnp.concatenate([qs(HEAD_DIM), ones(2 * att), qs(HEAD_DIM), ones(2 * att),
                            qs(MEM_HEAD_DIM)]).reshape(1, QKV_COLS)


def kernel(x, mem, w_in, b_forget, b_gates, lambda_q1, lambda_k1, lambda_q2, lambda_k2, diff_subln_g,
           w_mem_kv, w_branch_fox, w_branch_diff, w_branch_mem, w_out, ln1_g, ln1_b, w_router_group,
           b_router_group, w_router_expert, b_router_expert, w_expert_gate, w_expert_up, w_expert_down,
           ln2_g, ln2_b):
    batch, seq, d = x.shape
    t = batch * seq
    l = 0
    x2 = x.reshape(t, d)

    w_in_t = jnp.swapaxes(w_in, 1, 2)
    w_gates = w_in_t[l, GATE_COL0:FL_COL0, :].astype(BF16)
    w_fl_t = jnp.pad(w_in_t[l, FL_COL0:, :], ((0, 16 - FOX_HEADS), (0, 0))).astype(BF16)
    w_r = jnp.concatenate([w_router_group[l], w_router_expert[l]], axis=1)
    w_r = jnp.pad(w_r, ((0, 0), (0, LANES - w_r.shape[1]))).astype(BF16)
    b_r = jnp.concatenate([b_router_group[l], b_router_expert[l]])
    b_r = jnp.pad(b_r, (0, LANES - b_r.shape[0])).reshape(1, LANES)
    lam_params = jnp.stack([lambda_q1[l], lambda_k1[l], lambda_q2[l], lambda_k2[l]])
    slopes = 2.0 ** (-8.0 * jnp.arange(1, DIFF_HEADS + 1, dtype=F32) / DIFF_HEADS)

    qkv, fl_t = _proj(x2, w_in_t, w_fl_t, _qkv_col_scale())
    qkv4 = qkv.reshape(N_SLABS, batch, seq, LANES)
    c = _fscan(fl_t, b_forget[l].reshape(FOX_HEADS, 1), batch, seq)
    c4 = c.reshape(FOX_HEADS // 2, 2, t)

    y_fox = _fox(qkv4, c4, batch, seq)
    y_diff = _diff(qkv4, slopes, lam_params, diff_subln_g[l].reshape(LANES, 1), batch, seq)
    mkv = _memkv(mem, w_mem_kv[l].astype(BF16))
    y_mem = _mem_attn(qkv4, mkv, batch, seq)

    x1, ri, rit = _merge(
        x2, y_fox.reshape(t, -1), y_diff.reshape(t, -1), y_mem.reshape(t, -1),
        w_gates, b_gates[l].reshape(1, -1),
        w_branch_fox[l].astype(BF16), w_branch_diff[l].astype(BF16), w_branch_mem[l].astype(BF16),
        w_out[l].astype(BF16), ln1_g[l].reshape(1, d), ln1_b[l].reshape(1, d), w_r, b_r)

    pos, tiles = _route(rit, batch, seq)
    nt = 2 * seq // MOE_R + N_EXPERTS
    texp, tstart, tnval = (tiles[:, r, :nt].reshape(-1) for r in range(3))
    pos3 = pos.reshape(batch * 2 * (seq // MOE_TD), 1, MOE_TD)
    out = _moe(x1, pos3, ri, texp, tstart, tnval,
               w_expert_gate[l], w_expert_up[l], w_expert_down[l],
               ln2_g[l].reshape(1, d), ln2_b[l].reshape(1, d), batch, seq)
    return out.reshape(batch, seq, d)
```

```python
import functools
import math

import jax
import jax.numpy as jnp
from jax import lax
from jax.experimental import pallas as pl
from jax.experimental.pallas import tpu as pltpu

F32 = jnp.float32
BF16 = jnp.bfloat16

D_MODEL = 1024
HEAD_DIM = 64
FOX_HEADS = 8
DIFF_HEADS = 4
MEM_HEADS = 4
MEM_HEAD_DIM = 128
N_MEM = 256
N_BRANCH = 3
N_GROUPS = 4
EXPERTS_PER_GROUP = 8
N_EXPERTS = N_GROUPS * EXPERTS_PER_GROUP
D_EXPERT = 256
LN_EPS = 1e-5
DEPTH = 1
DEEPNORM_ALPHA = (2.0 * DEPTH) ** 0.25
LAM_INIT = 0.8 - 0.6 * math.exp(-0.3 * 0)
LOG2E = math.log2(math.e)

LANES = 128
QKV_COLS = 3584
N_SLABS = QKV_COLS // LANES
GATE_COL0 = QKV_COLS
FL_COL0 = QKV_COLS + N_BRANCH * D_MODEL
SLAB_FOX_Q, SLAB_FOX_K, SLAB_FOX_V = 0, 4, 8
SLAB_DIFF_Q, SLAB_DIFF_K, SLAB_DIFF_V = 12, 16, 20
SLAB_MEM_Q = 24
ROUTE_LANE0 = N_GROUPS

NEG_BIG = -1e30
VMEM_LIMIT = 56 * 1024 * 1024

PROJ_TM = 2048
PROJ_TN = 512
PROJ_J_FOX_V = SLAB_FOX_V * LANES // PROJ_TN
PROJ_J_DIFF_V = SLAB_DIFF_V * LANES // PROJ_TN
VT_FOX, VT_DIFF = 0, PROJ_TN // LANES
ATT_T = 256
N_CHAINS = 4
NSL = N_CHAINS // 2
N_BIAS_LANES = 3
DIFF_ONES_ROWS = 16
SETUP_ROWS = 512
MEM_TQ = 4096
MERGE_TM = 1024
MOE_R = 256
MOE_TD = 256
SEG_ALIGN = 8
ROW_TILE = D_MODEL // LANES
RI_E1, RI_E2, RI_W1, RI_W2 = 0, 1, 2, 3


def _cparams(sem, flags=None):
    return pltpu.CompilerParams(dimension_semantics=sem, vmem_limit_bytes=VMEM_LIMIT, flags=flags)


ATT_FLAGS = None


def _proj_kernel(x_ref, w_ref, wfl_ref, sc_ref, o_ref, vt_ref, fl_ref, xb_ref):
    j = pl.program_id(1)
    is_v = (j == PROJ_J_FOX_V) | (j == PROJ_J_DIFF_V)
    spc = PROJ_TN // LANES

    @pl.when(j == 0)
    def _():
        xb = x_ref[...].astype(BF16)
        xb_ref[...] = xb
        fl_ref[...] = lax.dot_general(wfl_ref[...], xb, (((1,), (1,)), ((), ())),
                                      preferred_element_type=F32)

    @pl.when(jnp.logical_not(is_v))
    def _():
        acc = lax.dot_general(xb_ref[...], w_ref[...].astype(BF16), (((1,), (1,)), ((), ())),
                              preferred_element_type=F32) * sc_ref[...]
        for c in range(spc):
            o_ref[c] = acc[:, c * LANES:(c + 1) * LANES].astype(BF16)

    @pl.when(is_v)
    def _():
        acc_t = lax.dot_general(w_ref[...].astype(BF16), xb_ref[...], (((1,), (1,)), ((), ())),
                                preferred_element_type=F32)
        for c in range(spc):
            vt_ref[c] = acc_t[c * LANES:(c + 1) * LANES, :].astype(BF16)
        o_ref[...] = jnp.zeros(o_ref.shape, BF16)


def _proj(x2, w_in_t, w_fl_t, col_scale, batch, seq):
    t = x2.shape[0]
    spc = PROJ_TN // LANES
    per_seq = seq // PROJ_TM
    return pl.pallas_call(
        _proj_kernel,
        grid=(t // PROJ_TM, QKV_COLS // PROJ_TN),
        in_specs=[
            pl.BlockSpec((PROJ_TM, D_MODEL), lambda i, j: (i, 0)),
            pl.BlockSpec((None, PROJ_TN, D_MODEL), lambda i, j: (0, j, 0)),
            pl.BlockSpec((16, D_MODEL), lambda i, j: (0, 0)),
            pl.BlockSpec((1, PROJ_TN), lambda i, j: (0, j)),
        ],
        out_specs=[
            pl.BlockSpec((spc, PROJ_TM, LANES), lambda i, j: (j, i, 0)),
            pl.BlockSpec((spc, None, LANES, PROJ_TM),
                         lambda i, j: (jnp.where(j < PROJ_J_DIFF_V, 0, 1), i // per_seq, 0, i % per_seq)),
            pl.BlockSpec((16, PROJ_TM), lambda i, j: (0, i)),
        ],
        out_shape=[
            jax.ShapeDtypeStruct((N_SLABS, t, LANES), BF16),
            jax.ShapeDtypeStruct((2 * spc, batch, LANES, seq), BF16),
            jax.ShapeDtypeStruct((16, t), F32),
        ],
        scratch_shapes=[pltpu.VMEM((PROJ_TM, D_MODEL), BF16)],
        compiler_params=_cparams(("parallel", "arbitrary")),
        name="proj",
    )(x2, w_in_t, w_fl_t, col_scale)


def _fscan_kernel(fl_ref, b_ref, c_ref):
    s = fl_ref.shape[1]
    z = fl_ref[0:FOX_HEADS, :] + b_ref[...]
    lf = jnp.minimum(z, 0.0) - jnp.log(1.0 + jnp.exp(-jnp.abs(z)))
    lane = lax.broadcasted_iota(jnp.int32, lf.shape, 1)
    sh = 1
    while sh < s:
        r = pltpu.roll(lf, sh, axis=1)
        lf = lf + jnp.where(lane >= sh, r, 0.0)
        sh *= 2
    c_ref[...] = lf * LOG2E


def _fscan(fl_t, b_forget_col, batch, seq):
    t = fl_t.shape[1]
    return pl.pallas_call(
        _fscan_kernel,
        grid=(batch,),
        in_specs=[
            pl.BlockSpec((16, seq), lambda b: (0, b)),
            pl.BlockSpec((FOX_HEADS, 1), lambda b: (0, 0)),
        ],
        out_specs=pl.BlockSpec((FOX_HEADS, seq), lambda b: (0, b)),
        out_shape=jax.ShapeDtypeStruct((FOX_HEADS, t), F32),
        compiler_params=_cparams(("parallel",)),
        name="fscan",
    )(fl_t, b_forget_col)


def _flash_chains(qi, tq, qh_scr, k_fn, vt_fn, m_scr, acc_scr, s_scr, pm_scr):
    m_scr[...] = jnp.full(m_scr.shape, NEG_BIG, F32)
    acc_scr[...] = jnp.zeros(acc_scr.shape, F32)

    def scores(j, slot, masked):
        start = pl.multiple_of(j * tq, tq)
        for c in range(N_CHAINS):
            s = lax.dot_general(k_fn(c, start), qh_scr[c], (((1,), (1,)), ((), ())),
                                preferred_element_type=F32)
            if masked:
                kv = lax.broadcasted_iota(jnp.int32, (tq, tq), 0)
                q = lax.broadcasted_iota(jnp.int32, (tq, tq), 1)
                s = jnp.where(kv <= q, s, NEG_BIG)
            s_scr[slot, c] = s
            pm_scr[slot, c] = jnp.max(s.reshape(tq // 8, 8, tq), axis=0)

    def consume(j, slot):
        start = pl.multiple_of(j * tq, tq)
        for c in range(N_CHAINS):
            m_prev = m_scr[c]
            m_new = jnp.maximum(m_prev, jnp.max(pm_scr[slot, c], axis=0, keepdims=True))
            alpha = jnp.exp2(m_prev - m_new)
            p = jnp.exp2(s_scr[slot, c] - m_new)
            m_scr[c] = m_new
            pv = jnp.dot(vt_fn(c, start), p.astype(BF16), preferred_element_type=F32)
            acc_scr[c] = acc_scr[c] * alpha + pv

    @pl.when(qi == 0)
    def _():
        scores(0, 0, True)
        consume(0, 0)

    @pl.when(qi > 0)
    def _():
        scores(0, 0, False)

        def body(jj, carry):
            j = 2 * jj
            scores(j + 1, 1, False)
            consume(j, 0)
            scores(j + 2, 0, False)
            consume(j + 1, 1)
            return carry

        lax.fori_loop(0, (qi - 1) // 2, body, 0)

        @pl.when(qi % 2 == 1)
        def _():
            scores(qi, 1, True)
            consume(qi - 1, 0)
            consume(qi, 1)

        @pl.when(qi % 2 == 0)
        def _():
            scores(qi - 1, 1, False)
            consume(qi - 2, 0)
            scores(qi, 0, True)
            consume(qi - 1, 1)
            consume(qi, 0)


def _own_and_spare(shape, c):
    lane = lax.broadcasted_iota(jnp.int32, shape, 1)
    own = (lane < HEAD_DIM) if c % 2 == 0 else (lane >= HEAD_DIM)
    spare = HEAD_DIM if c % 2 == 0 else 0
    return lane, own, spare


def _split_q(q_ref, qh_scr):
    for c in range(N_CHAINS):
        q = q_ref[c // 2].astype(F32)
        lane, own, spare = _own_and_spare(q.shape, c)
        ones = (lane >= spare) & (lane < spare + N_BIAS_LANES)
        qh_scr[c] = jnp.where(own, q, jnp.where(ones, 1.0, 0.0)).astype(BF16)


def _augment_k(k, bias, c):
    lane, own, spare = _own_and_spare(k.shape, c)
    hi = bias.astype(BF16).astype(F32)
    r1 = bias - hi
    mid = r1.astype(BF16).astype(F32)
    lo = (r1 - mid).astype(BF16).astype(F32)
    extra = jnp.where(lane == spare, hi, jnp.where(lane == spare + 1, mid, jnp.where(lane == spare + 2, lo, 0.0)))
    return jnp.where(own, k.astype(F32), extra).astype(BF16)


def _transpose_bf16(a):
    return a.astype(F32).T.astype(BF16)


def _attn_scratch(tq, seq, acc_rows, vt_slabs):
    return [
        pltpu.VMEM((N_CHAINS, tq, LANES), BF16),
        pltpu.VMEM((N_CHAINS, 1, tq), F32),
        pltpu.VMEM((N_CHAINS, acc_rows, tq), F32),
        pltpu.VMEM((2, N_CHAINS, tq, tq), F32),
        pltpu.VMEM((2, N_CHAINS, 8, tq), F32),
        pltpu.VMEM((N_CHAINS, seq, LANES), BF16),
        pltpu.VMEM((vt_slabs, acc_rows, seq), BF16),
    ]


def _fox_kernel(q_ref, k_ref, vt_ref, c_ref, o_ref, qh_scr, m_scr, acc_scr, s_scr, pm_scr, ka_scr, vat_scr):
    tq = q_ref.shape[1]
    seq = k_ref.shape[1]
    qi = pl.program_id(2)
    _split_q(q_ref, qh_scr)

    @pl.when(qi == 0)
    def _():
        def chunk(i, carry):
            r0 = pl.multiple_of(i * SETUP_ROWS, SETUP_ROWS)
            rows = pl.ds(r0, SETUP_ROWS)
            head_row = lax.broadcasted_iota(jnp.int32, (LANES, SETUP_ROWS), 0) < HEAD_DIM
            for sl in range(NSL):
                vt = vt_ref[sl, :, rows]
                one = jnp.ones_like(vt)
                vat_scr[2 * sl, :, rows] = jnp.where(head_row, vt, one)
                vat_scr[2 * sl + 1, :, rows] = jnp.where(head_row, one, vt)
                cc = c_ref[sl, :, rows]
                stacked = jnp.where(head_row, jnp.broadcast_to(cc[1:2, :], (LANES, SETUP_ROWS)),
                                    jnp.broadcast_to(cc[0:1, :], (LANES, SETUP_ROWS)))
                bias = -stacked.T
                k = k_ref[sl, rows, :]
                ka_scr[2 * sl, rows, :] = _augment_k(k, bias, 0)
                ka_scr[2 * sl + 1, rows, :] = _augment_k(k, bias, 1)
            return carry

        lax.fori_loop(0, seq // SETUP_ROWS, chunk, 0)

    def k_fn(c, start):
        return ka_scr[c, pl.ds(start, tq), :]

    def vt_fn(c, start):
        return vat_scr[c, :, pl.ds(start, tq)]

    _flash_chains(qi, tq, qh_scr, k_fn, vt_fn, m_scr, acc_scr, s_scr, pm_scr)
    for sl in range(NSL):
        a0 = acc_scr[2 * sl]
        a1 = acc_scr[2 * sl + 1]
        row = lax.broadcasted_iota(jnp.int32, a0.shape, 0)
        ot = jnp.where(row < HEAD_DIM, a0 / a0[HEAD_DIM:HEAD_DIM + 1, :], a1 / a1[0:1, :])
        o_ref[:, sl * LANES:(sl + 1) * LANES] = ot.T.astype(o_ref.dtype)


def _fox(qkv4, vt4, c4, batch, seq):
    tq = ATT_T
    return pl.pallas_call(
        _fox_kernel,
        grid=(batch, FOX_HEADS // N_CHAINS, seq // tq),
        in_specs=[
            pl.BlockSpec((NSL, None, tq, LANES), lambda b, g, i: (SLAB_FOX_Q // NSL + g, b, i, 0)),
            pl.BlockSpec((NSL, None, seq, LANES), lambda b, g, i: (SLAB_FOX_K // NSL + g, b, 0, 0)),
            pl.BlockSpec((NSL, None, LANES, seq), lambda b, g, i: (VT_FOX // NSL + g, b, 0, 0)),
            pl.BlockSpec((NSL, 2, seq), lambda b, g, i: (g, 0, b)),
        ],
        out_specs=pl.BlockSpec((None, tq, NSL * LANES), lambda b, g, i: (b, i, g)),
        out_shape=jax.ShapeDtypeStruct((batch, seq, FOX_HEADS * HEAD_DIM), BF16),
        scratch_shapes=_attn_scratch(tq, seq, LANES, N_CHAINS),
        compiler_params=_cparams(("parallel", "parallel", "arbitrary"), ATT_FLAGS),
        name="fox",
    )(qkv4, qkv4, vt4, c4)


def _diff_kernel(slopes_ref, q_ref, k_ref, vt_ref, lam_ref, g_ref, o_ref, qh_scr, m_scr, acc_scr, s_scr, pm_scr,
                 ka_scr, vat_scr):
    tq = q_ref.shape[1]
    seq = k_ref.shape[1]
    qi = pl.program_id(2)
    _split_q(q_ref, qh_scr)
    g = pl.program_id(1)

    @pl.when(qi == 0)
    def _():
        def chunk(i, carry):
            r0 = pl.multiple_of(i * SETUP_ROWS, SETUP_ROWS)
            rows = pl.ds(r0, SETUP_ROWS)
            for hd in range(NSL):
                vt = vt_ref[hd, :, rows]
                vat_scr[hd, :, rows] = jnp.concatenate([vt, jnp.ones((DIFF_ONES_ROWS, SETUP_ROWS), BF16)], axis=0)
                pos = (lax.broadcasted_iota(jnp.int32, (SETUP_ROWS, LANES), 0) + r0).astype(F32)
                bias = pos * (slopes_ref[NSL * g + hd] * LOG2E)
                k = k_ref[hd, rows, :]
                ka_scr[2 * hd, rows, :] = _augment_k(k, bias, 0)
                ka_scr[2 * hd + 1, rows, :] = _augment_k(k, bias, 1)
            return carry

        lax.fori_loop(0, seq // SETUP_ROWS, chunk, 0)

    def k_fn(c, start):
        return ka_scr[c, pl.ds(start, tq), :]

    def vt_fn(c, start):
        return vat_scr[c // 2, :, pl.ds(start, tq)]

    _flash_chains(qi, tq, qh_scr, k_fn, vt_fn, m_scr, acc_scr, s_scr, pm_scr)
    lp = lam_ref[...]
    s1 = jnp.sum(lp[0:1, :] * lp[1:2, :], axis=1, keepdims=True)
    s2 = jnp.sum(lp[2:3, :] * lp[3:4, :], axis=1, keepdims=True)
    lam = jnp.exp(s1) - jnp.exp(s2) + LAM_INIT
    for hd in range(NSL):
        a1 = acc_scr[2 * hd]
        a2 = acc_scr[2 * hd + 1]
        ot = a1[:LANES, :] / a1[LANES:LANES + 1, :] - lam * (a2[:LANES, :] / a2[LANES:LANES + 1, :])
        ms = jnp.mean(ot * ot, axis=0, keepdims=True)
        yt = ot * lax.rsqrt(ms + LN_EPS) * g_ref[...]
        o_ref[:, hd * LANES:(hd + 1) * LANES] = (yt * (1.0 - LAM_INIT)).T.astype(o_ref.dtype)


def _diff(qkv4, vt4, slopes, lam_params, subln_g_col, batch, seq):
    tq = ATT_T
    grid_spec = pltpu.PrefetchScalarGridSpec(
        num_scalar_prefetch=1,
        grid=(batch, DIFF_HEADS // NSL, seq // tq),
        in_specs=[
            pl.BlockSpec((NSL, None, tq, LANES), lambda b, g, i, s: (SLAB_DIFF_Q // NSL + g, b, i, 0)),
            pl.BlockSpec((NSL, None, seq, LANES), lambda b, g, i, s: (SLAB_DIFF_K // NSL + g, b, 0, 0)),
            pl.BlockSpec((NSL, None, LANES, seq), lambda b, g, i, s: (VT_DIFF // NSL + g, b, 0, 0)),
            pl.BlockSpec((4, HEAD_DIM), lambda b, g, i, s: (0, 0)),
            pl.BlockSpec((LANES, 1), lambda b, g, i, s: (0, 0)),
        ],
        out_specs=pl.BlockSpec((None, tq, NSL * LANES), lambda b, g, i, s: (b, i, g)),
        scratch_shapes=_attn_scratch(tq, seq, LANES + DIFF_ONES_ROWS, NSL),
    )
    return pl.pallas_call(
        _diff_kernel,
        grid_spec=grid_spec,
        out_shape=jax.ShapeDtypeStruct((batch, seq, DIFF_HEADS * LANES), BF16),
        compiler_params=_cparams(("parallel", "parallel", "arbitrary"), ATT_FLAGS),
        name="diff",
    )(slopes, qkv4, qkv4, vt4, lam_params, subln_g_col)


def _memkv_kernel(mem_ref, w_ref, o_ref):
    o_ref[...] = jnp.dot(mem_ref[...].astype(BF16), w_ref[...],
                         preferred_element_type=F32).astype(o_ref.dtype)


def _memkv(mem, w_kv):
    batch = mem.shape[0]
    width = w_kv.shape[1]
    return pl.pallas_call(
        _memkv_kernel,
        grid=(batch,),
        in_specs=[
            pl.BlockSpec((None, N_MEM, D_MODEL), lambda b: (b, 0, 0)),
            pl.BlockSpec((D_MODEL, width), lambda b: (0, 0)),
        ],
        out_specs=pl.BlockSpec((None, N_MEM, width), lambda b: (b, 0, 0)),
        out_shape=jax.ShapeDtypeStruct((batch, N_MEM, width), BF16),
        compiler_params=_cparams(("parallel",)),
        name="mem_kv",
    )(mem, w_kv)


def _mem_kernel(q_ref, mk_ref, mv_ref, o_ref):
    s = lax.dot_general(q_ref[...], mk_ref[...], (((1,), (1,)), ((), ())),
                        preferred_element_type=F32)
    m = jnp.max(s, axis=1, keepdims=True)
    p = jnp.exp2(s - m)
    l = jnp.sum(p, axis=1, keepdims=True)
    o = jnp.dot(p.astype(BF16), mv_ref[...], preferred_element_type=F32)
    o_ref[...] = (o / l).astype(o_ref.dtype)


def _mem_attn(qkv4, mkv, batch, seq):
    tq = MEM_TQ
    return pl.pallas_call(
        _mem_kernel,
        grid=(batch, MEM_HEADS, seq // tq),
        in_specs=[
            pl.BlockSpec((None, None, tq, LANES), lambda b, h, i: (SLAB_MEM_Q + h, b, i, 0)),
            pl.BlockSpec((None, N_MEM, LANES), lambda b, h, i: (b, 0, h)),
            pl.BlockSpec((None, N_MEM, LANES), lambda b, h, i: (b, 0, MEM_HEADS + h)),
        ],
        out_specs=pl.BlockSpec((None, tq, LANES), lambda b, h, i: (b, i, h)),
        out_shape=jax.ShapeDtypeStruct((batch, seq, MEM_HEADS * LANES), BF16),
        compiler_params=_cparams(("parallel", "parallel", "parallel")),
        name="mem",
    )(qkv4, mkv, mkv)


def _layer_norm(z, g, b):
    mu = jnp.mean(z, axis=1, keepdims=True)
    zc = z - mu
    var = jnp.mean(zc * zc, axis=1, keepdims=True)
    return zc * lax.rsqrt(var + LN_EPS) * g + b


def _lane_max(v):
    return jnp.max(v, axis=1, keepdims=True)


def _routing_info(logits):
    lane = lax.broadcasted_iota(jnp.int32, logits.shape, 1)
    big = jnp.int32(2 * LANES)
    is_g = lane < N_GROUPS
    gl = jnp.where(is_g, logits, NEG_BIG)
    gmax = _lane_max(gl)
    g_w = 1.0 / jnp.sum(jnp.exp(gl - gmax), axis=1, keepdims=True)
    g_idx = jnp.min(jnp.where(gl == gmax, lane, big), axis=1, keepdims=True)
    lo = ROUTE_LANE0 + g_idx * EXPERTS_PER_GROUP
    in_grp = (lane >= lo) & (lane < lo + EXPERTS_PER_GROUP)
    el = jnp.where(in_grp, logits, NEG_BIG)
    e1 = _lane_max(el)
    i1 = jnp.min(jnp.where(el == e1, lane, big), axis=1, keepdims=True)
    el2 = jnp.where(lane == i1, NEG_BIG, el)
    e2 = _lane_max(el2)
    i2 = jnp.min(jnp.where(el2 == e2, lane, big), axis=1, keepdims=True)
    r = jnp.exp(e2 - e1)
    w1 = g_w / (1.0 + r)
    w2 = g_w * r / (1.0 + r)
    id1 = (i1 - ROUTE_LANE0).astype(F32)
    id2 = (i2 - ROUTE_LANE0).astype(F32)
    return jnp.where(lane == RI_E1, id1,
                     jnp.where(lane == RI_E2, id2,
                               jnp.where(lane == RI_W1, w1, jnp.where(lane == RI_W2, w2, 0.0))))


def _merge_kernel(x_ref, yf_ref, yd_ref, ym_ref, wg_ref, bg_ref, wbf_ref, wbd_ref, wbm_ref,
                  wo_ref, g1_ref, b1_ref, wr_ref, br_ref, x1_ref, ri_ref, rit_ref):
    x = x_ref[...]
    xb = x.astype(BF16)
    h = None
    for i, (y_ref, wb_ref) in enumerate(((yf_ref, wbf_ref), (yd_ref, wbd_ref), (ym_ref, wbm_ref))):
        cols = slice(i * D_MODEL, (i + 1) * D_MODEL)
        gl = lax.dot_general(xb, wg_ref[cols, :], (((1,), (1,)), ((), ())),
                             preferred_element_type=F32) + bg_ref[:, cols]
        gate = 1.0 / (1.0 + jnp.exp(-gl))
        br = jnp.dot(y_ref[...], wb_ref[...], preferred_element_type=F32)
        h = gate * br if h is None else h + gate * br
    o = jnp.dot(h.astype(BF16), wo_ref[...], preferred_element_type=F32)
    x1 = _layer_norm(DEEPNORM_ALPHA * x + o, g1_ref[...], b1_ref[...])
    x1_ref[...] = x1
    logits = jnp.dot(x1.astype(BF16), wr_ref[...], preferred_element_type=F32) + br_ref[...]
    ri = _routing_info(logits)
    ri_ref[...] = ri
    rit_ref[...] = ri.T[0:8, :]


def _const_spec(shape):
    return pl.BlockSpec(shape, lambda i: (0,) * len(shape), pipeline_mode=pl.Buffered(1))


def _merge(x2, yf, yd, ym, wg, bg, wbf, wbd, wbm, wo, g1, b1, wr, br):
    t = x2.shape[0]
    tm = MERGE_TM
    half = yf.shape[1]
    row = lambda w: pl.BlockSpec((tm, w), lambda i: (i, 0))
    return pl.pallas_call(
        _merge_kernel,
        grid=(t // tm,),
        in_specs=[
            row(D_MODEL), row(half), row(half), row(half),
            _const_spec((N_BRANCH * D_MODEL, D_MODEL)), _const_spec((1, N_BRANCH * D_MODEL)),
            _const_spec((half, D_MODEL)), _const_spec((half, D_MODEL)), _const_spec((half, D_MODEL)),
            _const_spec((D_MODEL, D_MODEL)), _const_spec((1, D_MODEL)), _const_spec((1, D_MODEL)),
            _const_spec((D_MODEL, LANES)), _const_spec((1, LANES)),
        ],
        out_specs=[row(D_MODEL), row(LANES), pl.BlockSpec((8, tm), lambda i: (0, i))],
        out_shape=[
            jax.ShapeDtypeStruct((t, D_MODEL), F32),
            jax.ShapeDtypeStruct((t, LANES), F32),
            jax.ShapeDtypeStruct((8, t), F32),
        ],
        compiler_params=_cparams(("parallel",)),
        name="merge",
    )(x2, yf, yd, ym, wg, bg, wbf, wbd, wbm, wo, g1, b1, wr, br)


def _cumsum(x, axis):
    n = x.shape[axis]
    idx = lax.broadcasted_iota(jnp.int32, x.shape, axis)
    sh = 1
    while sh < n:
        x = x + jnp.where(idx >= sh, pltpu.roll(x, sh, axis=axis), 0.0)
        sh *= 2
    return x


def _route_kernel(rit_ref, pos_ref, tiles_ref):
    s = rit_ref.shape[1]
    e1 = rit_ref[RI_E1:RI_E1 + 1, :]
    e2 = rit_ref[RI_E2:RI_E2 + 1, :]
    sub = lax.broadcasted_iota(jnp.int32, (N_EXPERTS, s), 0).astype(F32)
    oh1 = jnp.where(sub == e1, 1.0, 0.0)
    oh2 = jnp.where(sub == e2, 1.0, 0.0)
    cnt = oh1 + oh2
    incl = _cumsum(cnt, 1)
    before = incl - cnt
    total = jnp.broadcast_to(incl[:, s - 1:s], (N_EXPERTS, LANES))
    aligned = jnp.floor((total + (SEG_ALIGN - 1)) * (1.0 / SEG_ALIGN)) * SEG_ALIGN
    off = _cumsum(aligned, 0) - aligned
    slot = before + off[:, 0:1]
    pos1 = jnp.sum(oh1 * slot, axis=0, keepdims=True)
    pos2 = jnp.sum(oh2 * slot, axis=0, keepdims=True)
    pos_ref[...] = (jnp.concatenate([pos1, pos2], axis=0) * ROW_TILE).astype(jnp.int32)

    nt = jnp.floor((total + (MOE_R - 1)) * (1.0 / MOE_R))
    ct_incl = _cumsum(nt, 0)
    ct_excl = ct_incl - nt
    sub_l = lax.broadcasted_iota(jnp.int32, (N_EXPERTS, LANES), 0).astype(F32)
    tile_i = lax.broadcasted_iota(jnp.int32, (N_EXPERTS, LANES), 1).astype(F32)
    te = jnp.sum(jnp.where(ct_incl <= tile_i, 1.0, 0.0), axis=0, keepdims=True)
    te = jnp.minimum(te, N_EXPERTS - 1.0)
    sel = jnp.where(sub_l == te, 1.0, 0.0)
    k_in = tile_i - ct_excl
    start = jnp.sum(sel * (off + k_in * MOE_R), axis=0, keepdims=True)
    nval = jnp.sum(sel * jnp.clip(total - k_in * MOE_R, 0.0, float(MOE_R)), axis=0, keepdims=True)
    active = tile_i[0:1, :] < ct_incl[N_EXPERTS - 1:N_EXPERTS, :]
    start = jnp.where(active, start, 0.0)
    nval = jnp.where(active, nval, 0.0)
    rows = jnp.concatenate([te, start * ROW_TILE, nval, jnp.zeros((5, LANES), F32)], axis=0)
    tiles_ref[...] = rows.astype(jnp.int32)


def _route(rit, batch, seq):
    return pl.pallas_call(
        _route_kernel,
        grid=(batch,),
        in_specs=[pl.BlockSpec((8, seq), lambda b: (0, b))],
        out_specs=[
            pl.BlockSpec((None, 2, seq), lambda b: (b, 0, 0)),
            pl.BlockSpec((None, 8, LANES), lambda b: (b, 0, 0)),
        ],
        out_shape=[
            jax.ShapeDtypeStruct((batch, 2, seq), jnp.int32),
            jax.ShapeDtypeStruct((batch, 8, LANES), jnp.int32),
        ],
        compiler_params=_cparams(("parallel",)),
        name="route",
    )(rit)


def _read_rows(view, n_rows):
    return jnp.concatenate([view[pl.ds(k, n_rows, stride=ROW_TILE), :] for k in range(ROW_TILE)], axis=1)


def _write_rows(view, n_rows, val):
    for k in range(ROW_TILE):
        view[pl.ds(k, n_rows, stride=ROW_TILE), :] = val[:, k * LANES:(k + 1) * LANES]


def _moe_kernel(nd, nt, texp_ref, tstart_ref, tnval_ref, x1_ref, p1_ref, p2_ref, ri_ref,
                wg_ref, wu_ref, wd_ref, g2_ref, b2_ref, o_ref, buf, xs_scr, as_scr, bs_scr):
    c = pl.program_id(0)
    j = pl.program_id(1)
    td = x1_ref.shape[0]

    def tile_at(ref, off):
        return ref.at[pl.ds(pl.multiple_of(off, ROW_TILE), ROW_TILE), :]

    @pl.when(j == 0)
    def _():
        buf[...] = jnp.zeros(buf.shape, F32)

    @pl.when(j < nd)
    def _():
        _write_rows(xs_scr, td, x1_ref[...])

        def body(i, carry):
            for u in range(SEG_ALIGN):
                t = i * SEG_ALIGN + u
                row = tile_at(xs_scr, t * ROW_TILE)[...]
                tile_at(buf, p1_ref[0, t])[...] = row
                tile_at(buf, p2_ref[0, t])[...] = row
            return carry

        lax.fori_loop(0, td // SEG_ALIGN, body, 0)

    @pl.when((j >= nd) & (j < nd + nt))
    def _():
        idx = c * nt + (j - nd)
        nval = tnval_ref[idx]

        @pl.when(nval > 0)
        def _():
            start = pl.multiple_of(tstart_ref[idx], ROW_TILE * SEG_ALIGN)
            view = buf.at[pl.ds(start, MOE_R * ROW_TILE), :]
            xt = _read_rows(view, MOE_R)
            xb = xt.astype(BF16)
            g = jnp.dot(xb, wg_ref[...].astype(BF16), preferred_element_type=F32)
            u = jnp.dot(xb, wu_ref[...].astype(BF16), preferred_element_type=F32)
            hcat = (g / (1.0 + jnp.exp(-g))) * u
            y = jnp.dot(hcat.astype(BF16), wd_ref[...].astype(BF16), preferred_element_type=F32)
            rows = lax.broadcasted_iota(jnp.int32, (MOE_R, 1), 0)
            _write_rows(view, MOE_R, jnp.where(rows < nval, y, xt))

    @pl.when(j >= nd + nt)
    def _():
        def body(i, carry):
            for u in range(SEG_ALIGN):
                t = i * SEG_ALIGN + u
                tile_at(as_scr, t * ROW_TILE)[...] = tile_at(buf, p1_ref[0, t])[...]
                tile_at(bs_scr, t * ROW_TILE)[...] = tile_at(buf, p2_ref[0, t])[...]
            return carry

        lax.fori_loop(0, td // SEG_ALIGN, body, 0)
        ri = ri_ref[...]
        m = ri[:, RI_W1:RI_W1 + 1] * _read_rows(as_scr, td) + ri[:, RI_W2:RI_W2 + 1] * _read_rows(bs_scr, td)
        z = DEEPNORM_ALPHA * x1_ref[...] + m
        o_ref[...] = _layer_norm(z, g2_ref[...], b2_ref[...])


def _moe(x1, pos3, ri, texp, tstart, tnval, wg, wu, wd, g2, b2, batch, seq):
    t = x1.shape[0]
    td = MOE_TD
    nd = seq // td
    nt = 2 * seq // MOE_R + N_EXPERTS
    buf_rows = 2 * seq + N_EXPERTS * SEG_ALIGN + MOE_R

    def tok_tile(j):
        return jnp.where(j < nd, j, jnp.maximum(j - nd - nt, 0))

    def tile_expert(c, j, texp_ref):
        return texp_ref[c * nt + jnp.clip(j - nd, 0, nt - 1)]

    tok_spec = lambda w: pl.BlockSpec((td, w), lambda c, j, *_: (c * nd + tok_tile(j), 0))
    pos_spec = lambda k: pl.BlockSpec((None, 1, td), lambda c, j, *_: ((c * 2 + k) * nd + tok_tile(j), 0, 0),
                                      memory_space=pltpu.SMEM)
    w_spec = lambda a, b_: pl.BlockSpec((None, a, b_), lambda c, j, te, *_: (tile_expert(c, j, te), 0, 0))
    vec_spec = pl.BlockSpec((1, D_MODEL), lambda c, j, *_: (0, 0))
    grid_spec = pltpu.PrefetchScalarGridSpec(
        num_scalar_prefetch=3,
        grid=(batch, nd + nt + nd),
        in_specs=[
            tok_spec(D_MODEL), pos_spec(0), pos_spec(1), tok_spec(LANES),
            w_spec(D_MODEL, D_EXPERT), w_spec(D_MODEL, D_EXPERT), w_spec(D_EXPERT, D_MODEL),
            vec_spec, vec_spec,
        ],
        out_specs=pl.BlockSpec((td, D_MODEL),
                               lambda c, j, *_: (c * nd + jnp.maximum(j - nd - nt, 0), 0)),
        scratch_shapes=[
            pltpu.VMEM((buf_rows * ROW_TILE, LANES), F32),
            pltpu.VMEM((td * ROW_TILE, LANES), F32),
            pltpu.VMEM((td * ROW_TILE, LANES), F32),
            pltpu.VMEM((td * ROW_TILE, LANES), F32),
        ],
    )
    return pl.pallas_call(
        functools.partial(_moe_kernel, nd, nt),
        grid_spec=grid_spec,
        out_shape=jax.ShapeDtypeStruct((t, D_MODEL), F32),
        compiler_params=_cparams(("arbitrary", "arbitrary")),
        name="moe",
    )(texp, tstart, tnval, x1, pos3, pos3, ri, wg, wu, wd, g2, b2)


def _qkv_col_scale():
    ones = lambda n: jnp.ones((n,), F32)
    att = FOX_HEADS * HEAD_DIM
    qs = lambda d: jnp.full((att,), LOG2E * d ** -0.5, F32)
    return jnp.concatenate([qs(HEAD_DIM), ones(2 * att), qs(HEAD_DIM), ones(2 * att),
                            qs(MEM_HEAD_DIM)]).reshape(1, QKV_COLS)


def kernel(x, mem, w_in, b_forget, b_gates, lambda_q1, lambda_k1, lambda_q2, lambda_k2, diff_subln_g,
           w_mem_kv, w_branch_fox, w_branch_diff, w_branch_mem, w_out, ln1_g, ln1_b, w_router_group,
           b_router_group, w_router_expert, b_router_expert, w_expert_gate, w_expert_up, w_expert_down,
           ln2_g, ln2_b):
    batch, seq, d = x.shape
    t = batch * seq
    l = 0
    x2 = x.reshape(t, d)

    w_in_t = jnp.swapaxes(w_in, 1, 2)
    w_gates = w_in_t[l, GATE_COL0:FL_COL0, :].astype(BF16)
    w_fl_t = jnp.pad(w_in_t[l, FL_COL0:, :], ((0, 16 - FOX_HEADS), (0, 0))).astype(BF16)
    w_r = jnp.concatenate([w_router_group[l], w_router_expert[l]], axis=1)
    w_r = jnp.pad(w_r, ((0, 0), (0, LANES - w_r.shape[1]))).astype(BF16)
    b_r = jnp.concatenate([b_router_group[l], b_router_expert[l]])
    b_r = jnp.pad(b_r, (0, LANES - b_r.shape[0])).reshape(1, LANES)
    lam_params = jnp.stack([lambda_q1[l], lambda_k1[l], lambda_q2[l], lambda_k2[l]])
    slopes = 2.0 ** (-8.0 * jnp.arange(1, DIFF_HEADS + 1, dtype=F32) / DIFF_HEADS)

    qkv, vt4, fl_t = _proj(x2, w_in_t, w_fl_t, _qkv_col_scale(), batch, seq)
    qkv4 = qkv.reshape(N_SLABS, batch, seq, LANES)
    c = _fscan(fl_t, b_forget[l].reshape(FOX_HEADS, 1), batch, seq)
    c4 = c.reshape(FOX_HEADS // 2, 2, t)

    y_fox = _fox(qkv4, vt4, c4, batch, seq)
    y_diff = _diff(qkv4, vt4, slopes, lam_params, diff_subln_g[l].reshape(LANES, 1), batch, seq)
    mkv = _memkv(mem, w_mem_kv[l].astype(BF16))
    y_mem = _mem_attn(qkv4, mkv, batch, seq)

    x1, ri, rit = _merge(
        x2, y_fox.reshape(t, -1), y_diff.reshape(t, -1), y_mem.reshape(t, -1),
        w_gates, b_gates[l].reshape(1, -1),
        w_branch_fox[l].astype(BF16), w_branch_diff[l].astype(BF16), w_branch_mem[l].astype(BF16),
        w_out[l].astype(BF16), ln1_g[l].reshape(1, d), ln1_b[l].reshape(1, d), w_r, b_r)

    pos, tiles = _route(rit, batch, seq)
    nt = 2 * seq // MOE_R + N_EXPERTS
    texp, tstart, tnval = (tiles[:, r, :nt].reshape(-1) for r in range(3))
    pos3 = pos.reshape(batch * 2 * (seq // MOE_TD), 1, MOE_TD)
    out = _moe(x1, pos3, ri, texp, tstart, tnval,
               w_expert_gate[l], w_expert_up[l], w_expert_down[l],
               ln2_g[l].reshape(1, d), ln2_b[l].reshape(1, d), batch, seq)
    return out.reshape(batch, seq, d)
```

```python
import functools
import math

import jax
import jax.numpy as jnp
from jax import lax
from jax.experimental import pallas as pl
from jax.experimental.pallas import tpu as pltpu

F32 = jnp.float32
BF16 = jnp.bfloat16

D_MODEL = 1024
HEAD_DIM = 64
FOX_HEADS = 8
DIFF_HEADS = 4
MEM_HEADS = 4
MEM_HEAD_DIM = 128
N_MEM = 256
N_BRANCH = 3
N_GROUPS = 4
EXPERTS_PER_GROUP = 8
N_EXPERTS = N_GROUPS * EXPERTS_PER_GROUP
D_EXPERT = 256
LN_EPS = 1e-5
DEPTH = 1
DEEPNORM_ALPHA = (2.0 * DEPTH) ** 0.25
LAM_INIT = 0.8 - 0.6 * math.exp(-0.3 * 0)
LOG2E = math.log2(math.e)

LANES = 128
QKV_COLS = 3584
N_SLABS = QKV_COLS // LANES
GATE_COL0 = QKV_COLS
FL_COL0 = QKV_COLS + N_BRANCH * D_MODEL
SLAB_FOX_Q, SLAB_FOX_K, SLAB_FOX_V = 0, 4, 8
SLAB_DIFF_Q, SLAB_DIFF_K, SLAB_DIFF_V = 12, 16, 20
SLAB_MEM_Q = 24
ROUTE_LANE0 = N_GROUPS

NEG_BIG = -1e30
VMEM_LIMIT = 56 * 1024 * 1024

PROJ_TM = 2048
PROJ_TN = 512
PROJ_J_FOX_V = SLAB_FOX_V * LANES // PROJ_TN
PROJ_J_DIFF_V = SLAB_DIFF_V * LANES // PROJ_TN
VT_FOX, VT_DIFF = 0, PROJ_TN // LANES
ATT_T = 256
ATT_UNROLL = 4
N_CHAINS = 4
NSL = N_CHAINS // 2
N_BIAS_LANES = 3
DIFF_ONES_ROWS = 16
SETUP_ROWS = 512
MEM_TQ = 4096
MERGE_TM = 1024
MOE_R = 256
MOE_TD = 256
SEG_ALIGN = 8
ROW_TILE = D_MODEL // LANES
RI_E1, RI_E2, RI_W1, RI_W2 = 0, 1, 2, 3


def _cparams(sem, flags=None):
    return pltpu.CompilerParams(dimension_semantics=sem, vmem_limit_bytes=VMEM_LIMIT, flags=flags)


ATT_FLAGS = None


def _proj_kernel(x_ref, w_ref, wfl_ref, sc_ref, o_ref, vt_ref, fl_ref, xb_ref):
    j = pl.program_id(1)
    is_v = (j == PROJ_J_FOX_V) | (j == PROJ_J_DIFF_V)
    spc = PROJ_TN // LANES

    @pl.when(j == 0)
    def _():
        xb = x_ref[...].astype(BF16)
        xb_ref[...] = xb
        fl_ref[...] = lax.dot_general(wfl_ref[...], xb, (((1,), (1,)), ((), ())),
                                      preferred_element_type=F32)

    @pl.when(jnp.logical_not(is_v))
    def _():
        acc = lax.dot_general(xb_ref[...], w_ref[...].astype(BF16), (((1,), (1,)), ((), ())),
                              preferred_element_type=F32) * sc_ref[...]
        for c in range(spc):
            o_ref[c] = acc[:, c * LANES:(c + 1) * LANES].astype(BF16)

    @pl.when(is_v)
    def _():
        acc_t = lax.dot_general(w_ref[...].astype(BF16), xb_ref[...], (((1,), (1,)), ((), ())),
                                preferred_element_type=F32)
        for c in range(spc):
            vt_ref[c] = acc_t[c * LANES:(c + 1) * LANES, :].astype(BF16)
        o_ref[...] = jnp.zeros(o_ref.shape, BF16)


def _proj(x2, w_in_t, w_fl_t, col_scale, batch, seq):
    t = x2.shape[0]
    spc = PROJ_TN // LANES
    per_seq = seq // PROJ_TM
    return pl.pallas_call(
        _proj_kernel,
        grid=(t // PROJ_TM, QKV_COLS // PROJ_TN),
        in_specs=[
            pl.BlockSpec((PROJ_TM, D_MODEL), lambda i, j: (i, 0)),
            pl.BlockSpec((None, PROJ_TN, D_MODEL), lambda i, j: (0, j, 0)),
            pl.BlockSpec((16, D_MODEL), lambda i, j: (0, 0)),
            pl.BlockSpec((1, PROJ_TN), lambda i, j: (0, j)),
        ],
        out_specs=[
            pl.BlockSpec((spc, PROJ_TM, LANES), lambda i, j: (j, i, 0)),
            pl.BlockSpec((spc, None, LANES, PROJ_TM),
                         lambda i, j: (jnp.where(j < PROJ_J_DIFF_V, 0, 1), i // per_seq, 0, i % per_seq)),
            pl.BlockSpec((16, PROJ_TM), lambda i, j: (0, i)),
        ],
        out_shape=[
            jax.ShapeDtypeStruct((N_SLABS, t, LANES), BF16),
            jax.ShapeDtypeStruct((2 * spc, batch, LANES, seq), BF16),
            jax.ShapeDtypeStruct((16, t), F32),
        ],
        scratch_shapes=[pltpu.VMEM((PROJ_TM, D_MODEL), BF16)],
        compiler_params=_cparams(("parallel", "arbitrary")),
        name="proj",
    )(x2, w_in_t, w_fl_t, col_scale)


def _fscan_kernel(fl_ref, b_ref, c_ref):
    s = fl_ref.shape[1]
    z = fl_ref[0:FOX_HEADS, :] + b_ref[...]
    lf = jnp.minimum(z, 0.0) - jnp.log(1.0 + jnp.exp(-jnp.abs(z)))
    lane = lax.broadcasted_iota(jnp.int32, lf.shape, 1)
    sh = 1
    while sh < s:
        r = pltpu.roll(lf, sh, axis=1)
        lf = lf + jnp.where(lane >= sh, r, 0.0)
        sh *= 2
    c_ref[...] = lf * LOG2E


def _fscan(fl_t, b_forget_col, batch, seq):
    t = fl_t.shape[1]
    return pl.pallas_call(
        _fscan_kernel,
        grid=(batch,),
        in_specs=[
            pl.BlockSpec((16, seq), lambda b: (0, b)),
            pl.BlockSpec((FOX_HEADS, 1), lambda b: (0, 0)),
        ],
        out_specs=pl.BlockSpec((FOX_HEADS, seq), lambda b: (0, b)),
        out_shape=jax.ShapeDtypeStruct((FOX_HEADS, t), F32),
        compiler_params=_cparams(("parallel",)),
        name="fscan",
    )(fl_t, b_forget_col)


def _flash_chains(qi, tq, qh_scr, k_fn, vt_fn, m_scr, acc_scr, s_scr, pm_scr):
    m_scr[...] = jnp.full(m_scr.shape, NEG_BIG, F32)
    acc_scr[...] = jnp.zeros(acc_scr.shape, F32)

    def scores(j, slot, masked):
        start = pl.multiple_of(j * tq, tq)
        for c in range(N_CHAINS):
            s = lax.dot_general(k_fn(c, start), qh_scr[c], (((1,), (1,)), ((), ())),
                                preferred_element_type=F32)
            if masked:
                kv = lax.broadcasted_iota(jnp.int32, (tq, tq), 0)
                q = lax.broadcasted_iota(jnp.int32, (tq, tq), 1)
                s = jnp.where(kv <= q, s, NEG_BIG)
            s_scr[slot, c] = s
            pm_scr[slot, c] = jnp.max(s.reshape(tq // 8, 8, tq), axis=0)

    def consume(j, slot):
        start = pl.multiple_of(j * tq, tq)
        for c in range(N_CHAINS):
            m_prev = m_scr[c]
            m_new = jnp.maximum(m_prev, jnp.max(pm_scr[slot, c], axis=0, keepdims=True))
            alpha = jnp.exp2(m_prev - m_new)
            p = jnp.exp2(s_scr[slot, c] - m_new)
            m_scr[c] = m_new
            pv = jnp.dot(vt_fn(c, start), p.astype(BF16), preferred_element_type=F32)
            acc_scr[c] = acc_scr[c] * alpha + pv

    @pl.when(qi == 0)
    def _():
        scores(0, 0, True)
        consume(0, 0)

    def run(j0, count, last_masked, issue_beyond):
        for t in range(count):
            if t + 1 < count or issue_beyond:
                scores(j0 + t + 1, (t + 1) % 2, last_masked and t + 1 == count - 1)
            consume(j0 + t, t % 2)

    @pl.when(qi > 0)
    def _():
        scores(0, 0, False)
        groups = (qi - 1) // ATT_UNROLL

        def body(g, carry):
            run(g * ATT_UNROLL, ATT_UNROLL, False, True)
            return carry

        lax.fori_loop(0, groups, body, 0)
        j0 = groups * ATT_UNROLL
        for rem in range(1, ATT_UNROLL + 1):
            @pl.when(qi - j0 == rem)
            def _(rem=rem):
                run(j0, rem + 1, True, False)


def _own_and_spare(shape, c):
    lane = lax.broadcasted_iota(jnp.int32, shape, 1)
    own = (lane < HEAD_DIM) if c % 2 == 0 else (lane >= HEAD_DIM)
    spare = HEAD_DIM if c % 2 == 0 else 0
    return lane, own, spare


def _split_q(q_ref, qh_scr):
    for c in range(N_CHAINS):
        q = q_ref[c // 2].astype(F32)
        lane, own, spare = _own_and_spare(q.shape, c)
        ones = (lane >= spare) & (lane < spare + N_BIAS_LANES)
        qh_scr[c] = jnp.where(own, q, jnp.where(ones, 1.0, 0.0)).astype(BF16)


def _augment_k(k, bias, c):
    lane, own, spare = _own_and_spare(k.shape, c)
    hi = bias.astype(BF16).astype(F32)
    r1 = bias - hi
    mid = r1.astype(BF16).astype(F32)
    lo = (r1 - mid).astype(BF16).astype(F32)
    extra = jnp.where(lane == spare, hi, jnp.where(lane == spare + 1, mid, jnp.where(lane == spare + 2, lo, 0.0)))
    return jnp.where(own, k.astype(F32), extra).astype(BF16)


def _transpose_bf16(a):
    return a.astype(F32).T.astype(BF16)


def _attn_scratch(tq, seq, acc_rows, vt_slabs):
    return [
        pltpu.VMEM((N_CHAINS, tq, LANES), BF16),
        pltpu.VMEM((N_CHAINS, 1, tq), F32),
        pltpu.VMEM((N_CHAINS, acc_rows, tq), F32),
        pltpu.VMEM((2, N_CHAINS, tq, tq), F32),
        pltpu.VMEM((2, N_CHAINS, 8, tq), F32),
        pltpu.VMEM((N_CHAINS, seq, LANES), BF16),
        pltpu.VMEM((vt_slabs, acc_rows, seq), BF16),
    ]


def _fox_kernel(q_ref, k_ref, vt_ref, c_ref, o_ref, qh_scr, m_scr, acc_scr, s_scr, pm_scr, ka_scr, vat_scr):
    tq = q_ref.shape[1]
    seq = k_ref.shape[1]
    qi = pl.program_id(2)
    _split_q(q_ref, qh_scr)

    @pl.when(qi == 0)
    def _():
        def chunk(i, carry):
            r0 = pl.multiple_of(i * SETUP_ROWS, SETUP_ROWS)
            rows = pl.ds(r0, SETUP_ROWS)
            head_row = lax.broadcasted_iota(jnp.int32, (LANES, SETUP_ROWS), 0) < HEAD_DIM
            for sl in range(NSL):
                vt = vt_ref[sl, :, rows]
                one = jnp.ones_like(vt)
                vat_scr[2 * sl, :, rows] = jnp.where(head_row, vt, one)
                vat_scr[2 * sl + 1, :, rows] = jnp.where(head_row, one, vt)
                cc = c_ref[sl, :, rows]
                stacked = jnp.where(head_row, jnp.broadcast_to(cc[1:2, :], (LANES, SETUP_ROWS)),
                                    jnp.broadcast_to(cc[0:1, :], (LANES, SETUP_ROWS)))
                bias = -stacked.T
                k = k_ref[sl, rows, :]
                ka_scr[2 * sl, rows, :] = _augment_k(k, bias, 0)
                ka_scr[2 * sl + 1, rows, :] = _augment_k(k, bias, 1)
            return carry

        lax.fori_loop(0, seq // SETUP_ROWS, chunk, 0)

    def k_fn(c, start):
        return ka_scr[c, pl.ds(start, tq), :]

    def vt_fn(c, start):
        return vat_scr[c, :, pl.ds(start, tq)]

    _flash_chains(qi, tq, qh_scr, k_fn, vt_fn, m_scr, acc_scr, s_scr, pm_scr)
    for sl in range(NSL):
        a0 = acc_scr[2 * sl]
        a1 = acc_scr[2 * sl + 1]
        row = lax.broadcasted_iota(jnp.int32, a0.shape, 0)
        ot = jnp.where(row < HEAD_DIM, a0 / a0[HEAD_DIM:HEAD_DIM + 1, :], a1 / a1[0:1, :])
        o_ref[:, sl * LANES:(sl + 1) * LANES] = ot.T.astype(o_ref.dtype)


def _fox(qkv4, vt4, c4, batch, seq):
    tq = ATT_T
    return pl.pallas_call(
        _fox_kernel,
        grid=(batch, FOX_HEADS // N_CHAINS, seq // tq),
        in_specs=[
            pl.BlockSpec((NSL, None, tq, LANES), lambda b, g, i: (SLAB_FOX_Q // NSL + g, b, i, 0)),
            pl.BlockSpec((NSL, None, seq, LANES), lambda b, g, i: (SLAB_FOX_K // NSL + g, b, 0, 0)),
            pl.BlockSpec((NSL, None, LANES, seq), lambda b, g, i: (VT_FOX // NSL + g, b, 0, 0)),
            pl.BlockSpec((NSL, 2, seq), lambda b, g, i: (g, 0, b)),
        ],
        out_specs=pl.BlockSpec((None, tq, NSL * LANES), lambda b, g, i: (b, i, g)),
        out_shape=jax.ShapeDtypeStruct((batch, seq, FOX_HEADS * HEAD_DIM), BF16),
        scratch_shapes=_attn_scratch(tq, seq, LANES, N_CHAINS),
        compiler_params=_cparams(("parallel", "parallel", "arbitrary"), ATT_FLAGS),
        name="fox",
    )(qkv4, qkv4, vt4, c4)


def _diff_kernel(slopes_ref, q_ref, k_ref, vt_ref, lam_ref, g_ref, o_ref, qh_scr, m_scr, acc_scr, s_scr, pm_scr,
                 ka_scr, vat_scr):
    tq = q_ref.shape[1]
    seq = k_ref.shape[1]
    qi = pl.program_id(2)
    _split_q(q_ref, qh_scr)
    g = pl.program_id(1)

    @pl.when(qi == 0)
    def _():
        def chunk(i, carry):
            r0 = pl.multiple_of(i * SETUP_ROWS, SETUP_ROWS)
            rows = pl.ds(r0, SETUP_ROWS)
            for hd in range(NSL):
                vt = vt_ref[hd, :, rows]
                vat_scr[hd, :, rows] = jnp.concatenate([vt, jnp.ones((DIFF_ONES_ROWS, SETUP_ROWS), BF16)], axis=0)
                pos = (lax.broadcasted_iota(jnp.int32, (SETUP_ROWS, LANES), 0) + r0).astype(F32)
                bias = pos * (slopes_ref[NSL * g + hd] * LOG2E)
                k = k_ref[hd, rows, :]
                ka_scr[2 * hd, rows, :] = _augment_k(k, bias, 0)
                ka_scr[2 * hd + 1, rows, :] = _augment_k(k, bias, 1)
            return carry

        lax.fori_loop(0, seq // SETUP_ROWS, chunk, 0)

    def k_fn(c, start):
        return ka_scr[c, pl.ds(start, tq), :]

    def vt_fn(c, start):
        return vat_scr[c // 2, :, pl.ds(start, tq)]

    _flash_chains(qi, tq, qh_scr, k_fn, vt_fn, m_scr, acc_scr, s_scr, pm_scr)
    lp = lam_ref[...]
    s1 = jnp.sum(lp[0:1, :] * lp[1:2, :], axis=1, keepdims=True)
    s2 = jnp.sum(lp[2:3, :] * lp[3:4, :], axis=1, keepdims=True)
    lam = jnp.exp(s1) - jnp.exp(s2) + LAM_INIT
    for hd in range(NSL):
        a1 = acc_scr[2 * hd]
        a2 = acc_scr[2 * hd + 1]
        ot = a1[:LANES, :] / a1[LANES:LANES + 1, :] - lam * (a2[:LANES, :] / a2[LANES:LANES + 1, :])
        ms = jnp.mean(ot * ot, axis=0, keepdims=True)
        yt = ot * lax.rsqrt(ms + LN_EPS) * g_ref[...]
        o_ref[:, hd * LANES:(hd + 1) * LANES] = (yt * (1.0 - LAM_INIT)).T.astype(o_ref.dtype)


def _diff(qkv4, vt4, slopes, lam_params, subln_g_col, batch, seq):
    tq = ATT_T
    grid_spec = pltpu.PrefetchScalarGridSpec(
        num_scalar_prefetch=1,
        grid=(batch, DIFF_HEADS // NSL, seq // tq),
        in_specs=[
            pl.BlockSpec((NSL, None, tq, LANES), lambda b, g, i, s: (SLAB_DIFF_Q // NSL + g, b, i, 0)),
            pl.BlockSpec((NSL, None, seq, LANES), lambda b, g, i, s: (SLAB_DIFF_K // NSL + g, b, 0, 0)),
            pl.BlockSpec((NSL, None, LANES, seq), lambda b, g, i, s: (VT_DIFF // NSL + g, b, 0, 0)),
            pl.BlockSpec((4, HEAD_DIM), lambda b, g, i, s: (0, 0)),
            pl.BlockSpec((LANES, 1), lambda b, g, i, s: (0, 0)),
        ],
        out_specs=pl.BlockSpec((None, tq, NSL * LANES), lambda b, g, i, s: (b, i, g)),
        scratch_shapes=_attn_scratch(tq, seq, LANES + DIFF_ONES_ROWS, NSL),
    )
    return pl.pallas_call(
        _diff_kernel,
        grid_spec=grid_spec,
        out_shape=jax.ShapeDtypeStruct((batch, seq, DIFF_HEADS * LANES), BF16),
        compiler_params=_cparams(("parallel", "parallel", "arbitrary"), ATT_FLAGS),
        name="diff",
    )(slopes, qkv4, qkv4, vt4, lam_params, subln_g_col)


def _memkv_kernel(mem_ref, w_ref, o_ref):
    o_ref[...] = jnp.dot(mem_ref[...].astype(BF16), w_ref[...],
                         preferred_element_type=F32).astype(o_ref.dtype)


def _memkv(mem, w_kv):
    batch = mem.shape[0]
    width = w_kv.shape[1]
    return pl.pallas_call(
        _memkv_kernel,
        grid=(batch,),
        in_specs=[
            pl.BlockSpec((None, N_MEM, D_MODEL), lambda b: (b, 0, 0)),
            pl.BlockSpec((D_MODEL, width), lambda b: (0, 0)),
        ],
        out_specs=pl.BlockSpec((None, N_MEM, width), lambda b: (b, 0, 0)),
        out_shape=jax.ShapeDtypeStruct((batch, N_MEM, width), BF16),
        compiler_params=_cparams(("parallel",)),
        name="mem_kv",
    )(mem, w_kv)


def _mem_kernel(q_ref, mk_ref, mv_ref, o_ref):
    s = lax.dot_general(q_ref[...], mk_ref[...], (((1,), (1,)), ((), ())),
                        preferred_element_type=F32)
    m = jnp.max(s, axis=1, keepdims=True)
    p = jnp.exp2(s - m)
    l = jnp.sum(p, axis=1, keepdims=True)
    o = jnp.dot(p.astype(BF16), mv_ref[...], preferred_element_type=F32)
    o_ref[...] = (o / l).astype(o_ref.dtype)


def _mem_attn(qkv4, mkv, batch, seq):
    tq = MEM_TQ
    return pl.pallas_call(
        _mem_kernel,
        grid=(batch, MEM_HEADS, seq // tq),
        in_specs=[
            pl.BlockSpec((None, None, tq, LANES), lambda b, h, i: (SLAB_MEM_Q + h, b, i, 0)),
            pl.BlockSpec((None, N_MEM, LANES), lambda b, h, i: (b, 0, h)),
            pl.BlockSpec((None, N_MEM, LANES), lambda b, h, i: (b, 0, MEM_HEADS + h)),
        ],
        out_specs=pl.BlockSpec((None, tq, LANES), lambda b, h, i: (b, i, h)),
        out_shape=jax.ShapeDtypeStruct((batch, seq, MEM_HEADS * LANES), BF16),
        compiler_params=_cparams(("parallel", "parallel", "parallel")),
        name="mem",
    )(qkv4, mkv, mkv)


def _layer_norm(z, g, b):
    mu = jnp.mean(z, axis=1, keepdims=True)
    zc = z - mu
    var = jnp.mean(zc * zc, axis=1, keepdims=True)
    return zc * lax.rsqrt(var + LN_EPS) * g + b


def _lane_max(v):
    return jnp.max(v, axis=1, keepdims=True)


def _routing_info(logits):
    lane = lax.broadcasted_iota(jnp.int32, logits.shape, 1)
    big = jnp.int32(2 * LANES)
    is_g = lane < N_GROUPS
    gl = jnp.where(is_g, logits, NEG_BIG)
    gmax = _lane_max(gl)
    g_w = 1.0 / jnp.sum(jnp.exp(gl - gmax), axis=1, keepdims=True)
    g_idx = jnp.min(jnp.where(gl == gmax, lane, big), axis=1, keepdims=True)
    lo = ROUTE_LANE0 + g_idx * EXPERTS_PER_GROUP
    in_grp = (lane >= lo) & (lane < lo + EXPERTS_PER_GROUP)
    el = jnp.where(in_grp, logits, NEG_BIG)
    e1 = _lane_max(el)
    i1 = jnp.min(jnp.where(el == e1, lane, big), axis=1, keepdims=True)
    el2 = jnp.where(lane == i1, NEG_BIG, el)
    e2 = _lane_max(el2)
    i2 = jnp.min(jnp.where(el2 == e2, lane, big), axis=1, keepdims=True)
    r = jnp.exp(e2 - e1)
    w1 = g_w / (1.0 + r)
    w2 = g_w * r / (1.0 + r)
    id1 = (i1 - ROUTE_LANE0).astype(F32)
    id2 = (i2 - ROUTE_LANE0).astype(F32)
    return jnp.where(lane == RI_E1, id1,
                     jnp.where(lane == RI_E2, id2,
                               jnp.where(lane == RI_W1, w1, jnp.where(lane == RI_W2, w2, 0.0))))


def _merge_kernel(x_ref, yf_ref, yd_ref, ym_ref, wg_ref, bg_ref, wbf_ref, wbd_ref, wbm_ref,
                  wo_ref, g1_ref, b1_ref, wr_ref, br_ref, x1_ref, ri_ref, rit_ref):
    x = x_ref[...]
    xb = x.astype(BF16)
    h = None
    for i, (y_ref, wb_ref) in enumerate(((yf_ref, wbf_ref), (yd_ref, wbd_ref), (ym_ref, wbm_ref))):
        cols = slice(i * D_MODEL, (i + 1) * D_MODEL)
        gl = lax.dot_general(xb, wg_ref[cols, :], (((1,), (1,)), ((), ())),
                             preferred_element_type=F32) + bg_ref[:, cols]
        gate = 1.0 / (1.0 + jnp.exp(-gl))
        br = jnp.dot(y_ref[...], wb_ref[...], preferred_element_type=F32)
        h = gate * br if h is None else h + gate * br
    o = jnp.dot(h.astype(BF16), wo_ref[...], preferred_element_type=F32)
    x1 = _layer_norm(DEEPNORM_ALPHA * x + o, g1_ref[...], b1_ref[...])
    x1_ref[...] = x1
    logits = jnp.dot(x1.astype(BF16), wr_ref[...], preferred_element_type=F32) + br_ref[...]
    ri = _routing_info(logits)
    ri_ref[...] = ri
    rit_ref[...] = ri.T[0:8, :]


def _const_spec(shape):
    return pl.BlockSpec(shape, lambda i: (0,) * len(shape), pipeline_mode=pl.Buffered(1))


def _merge(x2, yf, yd, ym, wg, bg, wbf, wbd, wbm, wo, g1, b1, wr, br):
    t = x2.shape[0]
    tm = MERGE_TM
    half = yf.shape[1]
    row = lambda w: pl.BlockSpec((tm, w), lambda i: (i, 0))
    return pl.pallas_call(
        _merge_kernel,
        grid=(t // tm,),
        in_specs=[
            row(D_MODEL), row(half), row(half), row(half),
            _const_spec((N_BRANCH * D_MODEL, D_MODEL)), _const_spec((1, N_BRANCH * D_MODEL)),
            _const_spec((half, D_MODEL)), _const_spec((half, D_MODEL)), _const_spec((half, D_MODEL)),
            _const_spec((D_MODEL, D_MODEL)), _const_spec((1, D_MODEL)), _const_spec((1, D_MODEL)),
            _const_spec((D_MODEL, LANES)), _const_spec((1, LANES)),
        ],
        out_specs=[row(D_MODEL), row(LANES), pl.BlockSpec((8, tm), lambda i: (0, i))],
        out_shape=[
            jax.ShapeDtypeStruct((t, D_MODEL), F32),
            jax.ShapeDtypeStruct((t, LANES), F32),
            jax.ShapeDtypeStruct((8, t), F32),
        ],
        compiler_params=_cparams(("parallel",)),
        name="merge",
    )(x2, yf, yd, ym, wg, bg, wbf, wbd, wbm, wo, g1, b1, wr, br)


def _cumsum(x, axis):
    n = x.shape[axis]
    idx = lax.broadcasted_iota(jnp.int32, x.shape, axis)
    sh = 1
    while sh < n:
        x = x + jnp.where(idx >= sh, pltpu.roll(x, sh, axis=axis), 0.0)
        sh *= 2
    return x


def _route_kernel(rit_ref, pos_ref, tiles_ref):
    s = rit_ref.shape[1]
    e1 = rit_ref[RI_E1:RI_E1 + 1, :]
    e2 = rit_ref[RI_E2:RI_E2 + 1, :]
    sub = lax.broadcasted_iota(jnp.int32, (N_EXPERTS, s), 0).astype(F32)
    oh1 = jnp.where(sub == e1, 1.0, 0.0)
    oh2 = jnp.where(sub == e2, 1.0, 0.0)
    cnt = oh1 + oh2
    incl = _cumsum(cnt, 1)
    before = incl - cnt
    total = jnp.broadcast_to(incl[:, s - 1:s], (N_EXPERTS, LANES))
    aligned = jnp.floor((total + (SEG_ALIGN - 1)) * (1.0 / SEG_ALIGN)) * SEG_ALIGN
    off = _cumsum(aligned, 0) - aligned
    slot = before + off[:, 0:1]
    pos1 = jnp.sum(oh1 * slot, axis=0, keepdims=True)
    pos2 = jnp.sum(oh2 * slot, axis=0, keepdims=True)
    pos_ref[...] = (jnp.concatenate([pos1, pos2], axis=0) * ROW_TILE).astype(jnp.int32)

    nt = jnp.floor((total + (MOE_R - 1)) * (1.0 / MOE_R))
    ct_incl = _cumsum(nt, 0)
    ct_excl = ct_incl - nt
    sub_l = lax.broadcasted_iota(jnp.int32, (N_EXPERTS, LANES), 0).astype(F32)
    tile_i = lax.broadcasted_iota(jnp.int32, (N_EXPERTS, LANES), 1).astype(F32)
    te = jnp.sum(jnp.where(ct_incl <= tile_i, 1.0, 0.0), axis=0, keepdims=True)
    te = jnp.minimum(te, N_EXPERTS - 1.0)
    sel = jnp.where(sub_l == te, 1.0, 0.0)
    k_in = tile_i - ct_excl
    start = jnp.sum(sel * (off + k_in * MOE_R), axis=0, keepdims=True)
    nval = jnp.sum(sel * jnp.clip(total - k_in * MOE_R, 0.0, float(MOE_R)), axis=0, keepdims=True)
    active = tile_i[0:1, :] < ct_incl[N_EXPERTS - 1:N_EXPERTS, :]
    start = jnp.where(active, start, 0.0)
    nval = jnp.where(active, nval, 0.0)
    rows = jnp.concatenate([te, start * ROW_TILE, nval, jnp.zeros((5, LANES), F32)], axis=0)
    tiles_ref[...] = rows.astype(jnp.int32)


def _route(rit, batch, seq):
    return pl.pallas_call(
        _route_kernel,
        grid=(batch,),
        in_specs=[pl.BlockSpec((8, seq), lambda b: (0, b))],
        out_specs=[
            pl.BlockSpec((None, 2, seq), lambda b: (b, 0, 0)),
            pl.BlockSpec((None, 8, LANES), lambda b: (b, 0, 0)),
        ],
        out_shape=[
            jax.ShapeDtypeStruct((batch, 2, seq), jnp.int32),
            jax.ShapeDtypeStruct((batch, 8, LANES), jnp.int32),
        ],
        compiler_params=_cparams(("parallel",)),
        name="route",
    )(rit)


def _read_rows(view, n_rows):
    return jnp.concatenate([view[pl.ds(k, n_rows, stride=ROW_TILE), :] for k in range(ROW_TILE)], axis=1)


def _write_rows(view, n_rows, val):
    for k in range(ROW_TILE):
        view[pl.ds(k, n_rows, stride=ROW_TILE), :] = val[:, k * LANES:(k + 1) * LANES]


def _moe_kernel(nd, nt, texp_ref, tstart_ref, tnval_ref, x1_ref, p1_ref, p2_ref, ri_ref,
                wg_ref, wu_ref, wd_ref, g2_ref, b2_ref, o_ref, buf, xs_scr, as_scr, bs_scr):
    c = pl.program_id(0)
    j = pl.program_id(1)
    td = x1_ref.shape[0]

    def tile_at(ref, off):
        return ref.at[pl.ds(pl.multiple_of(off, ROW_TILE), ROW_TILE), :]

    @pl.when(j == 0)
    def _():
        buf[...] = jnp.zeros(buf.shape, F32)

    @pl.when(j < nd)
    def _():
        _write_rows(xs_scr, td, x1_ref[...])

        def body(i, carry):
            for u in range(SEG_ALIGN):
                t = i * SEG_ALIGN + u
                row = tile_at(xs_scr, t * ROW_TILE)[...]
                tile_at(buf, p1_ref[0, t])[...] = row
                tile_at(buf, p2_ref[0, t])[...] = row
            return carry

        lax.fori_loop(0, td // SEG_ALIGN, body, 0)

    @pl.when((j >= nd) & (j < nd + nt))
    def _():
        idx = c * nt + (j - nd)
        nval = tnval_ref[idx]

        @pl.when(nval > 0)
        def _():
            start = pl.multiple_of(tstart_ref[idx], ROW_TILE * SEG_ALIGN)
            view = buf.at[pl.ds(start, MOE_R * ROW_TILE), :]
            xt = _read_rows(view, MOE_R)
            xb = xt.astype(BF16)
            g = jnp.dot(xb, wg_ref[...].astype(BF16), preferred_element_type=F32)
            u = jnp.dot(xb, wu_ref[...].astype(BF16), preferred_element_type=F32)
            hcat = (g / (1.0 + jnp.exp(-g))) * u
            y = jnp.dot(hcat.astype(BF16), wd_ref[...].astype(BF16), preferred_element_type=F32)
            rows = lax.broadcasted_iota(jnp.int32, (MOE_R, 1), 0)
            _write_rows(view, MOE_R, jnp.where(rows < nval, y, xt))

    @pl.when(j >= nd + nt)
    def _():
        def body(i, carry):
            for u in range(SEG_ALIGN):
                t = i * SEG_ALIGN + u
                tile_at(as_scr, t * ROW_TILE)[...] = tile_at(buf, p1_ref[0, t])[...]
                tile_at(bs_scr, t * ROW_TILE)[...] = tile_at(buf, p2_ref[0, t])[...]
            return carry

        lax.fori_loop(0, td // SEG_ALIGN, body, 0)
        ri = ri_ref[...]
        m = ri[:, RI_W1:RI_W1 + 1] * _read_rows(as_scr, td) + ri[:, RI_W2:RI_W2 + 1] * _read_rows(bs_scr, td)
        z = DEEPNORM_ALPHA * x1_ref[...] + m
        o_ref[...] = _layer_norm(z, g2_ref[...], b2_ref[...])


def _moe(x1, pos3, ri, texp, tstart, tnval, wg, wu, wd, g2, b2, batch, seq):
    t = x1.shape[0]
    td = MOE_TD
    nd = seq // td
    nt = 2 * seq // MOE_R + N_EXPERTS
    buf_rows = 2 * seq + N_EXPERTS * SEG_ALIGN + MOE_R

    def tok_tile(j):
        return jnp.where(j < nd, j, jnp.maximum(j - nd - nt, 0))

    def tile_expert(c, j, texp_ref):
        return texp_ref[c * nt + jnp.clip(j - nd, 0, nt - 1)]

    tok_spec = lambda w: pl.BlockSpec((td, w), lambda c, j, *_: (c * nd + tok_tile(j), 0))
    pos_spec = lambda k: pl.BlockSpec((None, 1, td), lambda c, j, *_: ((c * 2 + k) * nd + tok_tile(j), 0, 0),
                                      memory_space=pltpu.SMEM)
    w_spec = lambda a, b_: pl.BlockSpec((None, a, b_), lambda c, j, te, *_: (tile_expert(c, j, te), 0, 0))
    vec_spec = pl.BlockSpec((1, D_MODEL), lambda c, j, *_: (0, 0))
    grid_spec = pltpu.PrefetchScalarGridSpec(
        num_scalar_prefetch=3,
        grid=(batch, nd + nt + nd),
        in_specs=[
            tok_spec(D_MODEL), pos_spec(0), pos_spec(1), tok_spec(LANES),
            w_spec(D_MODEL, D_EXPERT), w_spec(D_MODEL, D_EXPERT), w_spec(D_EXPERT, D_MODEL),
            vec_spec, vec_spec,
        ],
        out_specs=pl.BlockSpec((td, D_MODEL),
                               lambda c, j, *_: (c * nd + jnp.maximum(j - nd - nt, 0), 0)),
        scratch_shapes=[
            pltpu.VMEM((buf_rows * ROW_TILE, LANES), F32),
            pltpu.VMEM((td * ROW_TILE, LANES), F32),
            pltpu.VMEM((td * ROW_TILE, LANES), F32),
            pltpu.VMEM((td * ROW_TILE, LANES), F32),
        ],
    )
    return pl.pallas_call(
        functools.partial(_moe_kernel, nd, nt),
        grid_spec=grid_spec,
        out_shape=jax.ShapeDtypeStruct((t, D_MODEL), F32),
        compiler_params=_cparams(("arbitrary", "arbitrary")),
        name="moe",
    )(texp, tstart, tnval, x1, pos3, pos3, ri, wg, wu, wd, g2, b2)


def _qkv_col_scale():
    ones = lambda n: jnp.ones((n,), F32)
    att = FOX_HEADS * HEAD_DIM
    qs = lambda d: jnp.full((att,), LOG2E * d ** -0.5, F32)
    return jnp.concatenate([qs(HEAD_DIM), ones(2 * att), qs(HEAD_DIM), ones(2 * att),
                            qs(MEM_HEAD_DIM)]).reshape(1, QKV_COLS)


def kernel(x, mem, w_in, b_forget, b_gates, lambda_q1, lambda_k1, lambda_q2, lambda_k2, diff_subln_g,
           w_mem_kv, w_branch_fox, w_branch_diff, w_branch_mem, w_out, ln1_g, ln1_b, w_router_group,
           b_router_group, w_router_expert, b_router_expert, w_expert_gate, w_expert_up, w_expert_down,
           ln2_g, ln2_b):
    batch, seq, d = x.shape
    t = batch * seq
    l = 0
    x2 = x.reshape(t, d)

    w_in_t = jnp.swapaxes(w_in, 1, 2)
    w_gates = w_in_t[l, GATE_COL0:FL_COL0, :].astype(BF16)
    w_fl_t = jnp.pad(w_in_t[l, FL_COL0:, :], ((0, 16 - FOX_HEADS), (0, 0))).astype(BF16)
    w_r = jnp.concatenate([w_router_group[l], w_router_expert[l]], axis=1)
    w_r = jnp.pad(w_r, ((0, 0), (0, LANES - w_r.shape[1]))).astype(BF16)
    b_r = jnp.concatenate([b_router_group[l], b_router_expert[l]])
    b_r = jnp.pad(b_r, (0, LANES - b_r.shape[0])).reshape(1, LANES)
    lam_params = jnp.stack([lambda_q1[l], lambda_k1[l], lambda_q2[l], lambda_k2[l]])
    slopes = 2.0 ** (-8.0 * jnp.arange(1, DIFF_HEADS + 1, dtype=F32) / DIFF_HEADS)

    qkv, vt4, fl_t = _proj(x2, w_in_t, w_fl_t, _qkv_col_scale(), batch, seq)
    qkv4 = qkv.reshape(N_SLABS, batch, seq, LANES)
    c = _fscan(fl_t, b_forget[l].reshape(FOX_HEADS, 1), batch, seq)
    c4 = c.reshape(FOX_HEADS // 2, 2, t)

    y_fox = _fox(qkv4, vt4, c4, batch, seq)
    y_diff = _diff(qkv4, vt4, slopes, lam_params, diff_subln_g[l].reshape(LANES, 1), batch, seq)
    mkv = _memkv(mem, w_mem_kv[l].astype(BF16))
    y_mem = _mem_attn(qkv4, mkv, batch, seq)

    x1, ri, rit = _merge(
        x2, y_fox.reshape(t, -1), y_diff.reshape(t, -1), y_mem.reshape(t, -1),
        w_gates, b_gates[l].reshape(1, -1),
        w_branch_fox[l].astype(BF16), w_branch_diff[l].astype(BF16), w_branch_mem[l].astype(BF16),
        w_out[l].astype(BF16), ln1_g[l].reshape(1, d), ln1_b[l].reshape(1, d), w_r, b_r)

    pos, tiles = _route(rit, batch, seq)
    nt = 2 * seq // MOE_R + N_EXPERTS
    texp, tstart, tnval = (tiles[:, r, :nt].reshape(-1) for r in range(3))
    pos3 = pos.reshape(batch * 2 * (seq // MOE_TD), 1, MOE_TD)
    out = _moe(x1, pos3, ri, texp, tstart, tnval,
               w_expert_gate[l], w_expert_up[l], w_expert_down[l],
               ln2_g[l].reshape(1, d), ln2_b[l].reshape(1, d), batch, seq)
    return out.reshape(batch, seq, d)
```

```python
import functools
import math

import jax
import jax.numpy as jnp
from jax import lax
from jax.experimental import pallas as pl
from jax.experimental.pallas import tpu as pltpu

F32 = jnp.float32
BF16 = jnp.bfloat16

D_MODEL = 1024
HEAD_DIM = 64
FOX_HEADS = 8
DIFF_HEADS = 4
MEM_HEADS = 4
MEM_HEAD_DIM = 128
N_MEM = 256
N_BRANCH = 3
N_GROUPS = 4
EXPERTS_PER_GROUP = 8
N_EXPERTS = N_GROUPS * EXPERTS_PER_GROUP
D_EXPERT = 256
LN_EPS = 1e-5
DEPTH = 1
DEEPNORM_ALPHA = (2.0 * DEPTH) ** 0.25
LAM_INIT = 0.8 - 0.6 * math.exp(-0.3 * 0)
LOG2E = math.log2(math.e)

LANES = 128
QKV_COLS = 3584
N_SLABS = QKV_COLS // LANES
GATE_COL0 = QKV_COLS
FL_COL0 = QKV_COLS + N_BRANCH * D_MODEL
SLAB_FOX_Q, SLAB_FOX_K, SLAB_FOX_V = 0, 4, 8
SLAB_DIFF_Q, SLAB_DIFF_K, SLAB_DIFF_V = 12, 16, 20
SLAB_MEM_Q = 24
ROUTE_LANE0 = N_GROUPS

NEG_BIG = -1e30
VMEM_LIMIT = 56 * 1024 * 1024

PROJ_TM = 2048
PROJ_TN = 512
PROJ_J_FOX_V = SLAB_FOX_V * LANES // PROJ_TN
PROJ_J_DIFF_V = SLAB_DIFF_V * LANES // PROJ_TN
VT_FOX, VT_DIFF = 0, PROJ_TN // LANES
ATT_T = 256
ATT_UNROLL = 4
N_CHAINS = 4
NSL = N_CHAINS // 2
N_BIAS_LANES = 3
DIFF_ONES_ROWS = 16
SETUP_ROWS = 512
MEM_TQ = 4096
MERGE_TM = 1024
MOE_R = 256
MOE_TD = 256
SEG_ALIGN = 8
ROW_TILE = D_MODEL // LANES
RI_E1, RI_E2, RI_W1, RI_W2 = 0, 1, 2, 3


def _cparams(sem, flags=None):
    return pltpu.CompilerParams(dimension_semantics=sem, vmem_limit_bytes=VMEM_LIMIT, flags=flags)


ATT_FLAGS = None


def _proj_kernel(x_ref, w_ref, wfl_ref, sc_ref, o_ref, vt_ref, fl_ref, xb_ref):
    j = pl.program_id(1)
    is_v = (j == PROJ_J_FOX_V) | (j == PROJ_J_DIFF_V)
    spc = PROJ_TN // LANES

    @pl.when(j == 0)
    def _():
        xb = x_ref[...].astype(BF16)
        xb_ref[...] = xb
        fl_ref[...] = lax.dot_general(wfl_ref[...], xb, (((1,), (1,)), ((), ())),
                                      preferred_element_type=F32)

    @pl.when(jnp.logical_not(is_v))
    def _():
        acc = lax.dot_general(xb_ref[...], w_ref[...].astype(BF16), (((1,), (1,)), ((), ())),
                              preferred_element_type=F32) * sc_ref[...]
        for c in range(spc):
            o_ref[c] = acc[:, c * LANES:(c + 1) * LANES].astype(BF16)

    @pl.when(is_v)
    def _():
        acc_t = lax.dot_general(w_ref[...].astype(BF16), xb_ref[...], (((1,), (1,)), ((), ())),
                                preferred_element_type=F32)
        for c in range(spc):
            vt_ref[c] = acc_t[c * LANES:(c + 1) * LANES, :].astype(BF16)
        o_ref[...] = jnp.zeros(o_ref.shape, BF16)


def _proj(x2, w_in_t, w_fl_t, col_scale, batch, seq):
    t = x2.shape[0]
    spc = PROJ_TN // LANES
    per_seq = seq // PROJ_TM
    return pl.pallas_call(
        _proj_kernel,
        grid=(t // PROJ_TM, QKV_COLS // PROJ_TN),
        in_specs=[
            pl.BlockSpec((PROJ_TM, D_MODEL), lambda i, j: (i, 0)),
            pl.BlockSpec((None, PROJ_TN, D_MODEL), lambda i, j: (0, j, 0)),
            pl.BlockSpec((16, D_MODEL), lambda i, j: (0, 0)),
            pl.BlockSpec((1, PROJ_TN), lambda i, j: (0, j)),
        ],
        out_specs=[
            pl.BlockSpec((spc, PROJ_TM, LANES), lambda i, j: (j, i, 0)),
            pl.BlockSpec((spc, None, LANES, PROJ_TM),
                         lambda i, j: (jnp.where(j < PROJ_J_DIFF_V, 0, 1), i // per_seq, 0, i % per_seq)),
            pl.BlockSpec((16, PROJ_TM), lambda i, j: (0, i)),
        ],
        out_shape=[
            jax.ShapeDtypeStruct((N_SLABS, t, LANES), BF16),
            jax.ShapeDtypeStruct((2 * spc, batch, LANES, seq), BF16),
            jax.ShapeDtypeStruct((16, t), F32),
        ],
        scratch_shapes=[pltpu.VMEM((PROJ_TM, D_MODEL), BF16)],
        compiler_params=_cparams(("parallel", "arbitrary")),
        name="proj",
    )(x2, w_in_t, w_fl_t, col_scale)


def _fscan_kernel(fl_ref, b_ref, c_ref):
    s = fl_ref.shape[1]
    z = fl_ref[0:FOX_HEADS, :] + b_ref[...]
    lf = jnp.minimum(z, 0.0) - jnp.log(1.0 + jnp.exp(-jnp.abs(z)))
    lane = lax.broadcasted_iota(jnp.int32, lf.shape, 1)
    sh = 1
    while sh < s:
        r = pltpu.roll(lf, sh, axis=1)
        lf = lf + jnp.where(lane >= sh, r, 0.0)
        sh *= 2
    c_ref[...] = lf * LOG2E


def _fscan(fl_t, b_forget_col, batch, seq):
    t = fl_t.shape[1]
    return pl.pallas_call(
        _fscan_kernel,
        grid=(batch,),
        in_specs=[
            pl.BlockSpec((16, seq), lambda b: (0, b)),
            pl.BlockSpec((FOX_HEADS, 1), lambda b: (0, 0)),
        ],
        out_specs=pl.BlockSpec((FOX_HEADS, seq), lambda b: (0, b)),
        out_shape=jax.ShapeDtypeStruct((FOX_HEADS, t), F32),
        compiler_params=_cparams(("parallel",)),
        name="fscan",
    )(fl_t, b_forget_col)


def _flash_chains(qi, tq, qh_scr, k_fn, vt_fn, m_scr, acc_scr, s_scr, pm_scr):
    m_scr[...] = jnp.full(m_scr.shape, NEG_BIG, F32)
    acc_scr[...] = jnp.zeros(acc_scr.shape, F32)

    def scores(j, slot, masked):
        start = pl.multiple_of(j * tq, tq)
        for c in range(N_CHAINS):
            s = lax.dot_general(k_fn(c, start), qh_scr[c], (((1,), (1,)), ((), ())),
                                preferred_element_type=F32)
            if masked:
                kv = lax.broadcasted_iota(jnp.int32, (tq, tq), 0)
                q = lax.broadcasted_iota(jnp.int32, (tq, tq), 1)
                s = jnp.where(kv <= q, s, NEG_BIG)
            s_scr[slot, c] = s
            pm_scr[slot, c] = jnp.max(s.reshape(tq // 8, 8, tq), axis=0)

    def consume(j, slot):
        start = pl.multiple_of(j * tq, tq)
        for c in range(N_CHAINS):
            m_prev = m_scr[c]
            m_new = jnp.maximum(m_prev, jnp.max(pm_scr[slot, c], axis=0, keepdims=True))
            alpha = jnp.exp2(m_prev - m_new)
            p = jnp.exp2(s_scr[slot, c] - m_new)
            m_scr[c] = m_new
            pv = jnp.dot(vt_fn(c, start), p.astype(BF16), preferred_element_type=F32)
            acc_scr[c] = acc_scr[c] * alpha + pv

    @pl.when(qi == 0)
    def _():
        scores(0, 0, True)
        consume(0, 0)

    def run(j0, count, last_masked, issue_beyond):
        for t in range(count):
            if t + 1 < count or issue_beyond:
                scores(j0 + t + 1, (t + 1) % 2, last_masked and t + 1 == count - 1)
            consume(j0 + t, t % 2)

    @pl.when(qi > 0)
    def _():
        scores(0, 0, False)
        groups = (qi - 1) // ATT_UNROLL

        def body(g, carry):
            run(g * ATT_UNROLL, ATT_UNROLL, False, True)
            return carry

        lax.fori_loop(0, groups, body, 0)
        j0 = groups * ATT_UNROLL
        for rem in range(1, ATT_UNROLL + 1):
            @pl.when(qi - j0 == rem)
            def _(rem=rem):
                run(j0, rem + 1, True, False)


def _own_and_spare(shape, c):
    lane = lax.broadcasted_iota(jnp.int32, shape, 1)
    own = (lane < HEAD_DIM) if c % 2 == 0 else (lane >= HEAD_DIM)
    spare = HEAD_DIM if c % 2 == 0 else 0
    return lane, own, spare


def _split_q(q_ref, qh_scr):
    for c in range(N_CHAINS):
        q = q_ref[c // 2].astype(F32)
        lane, own, spare = _own_and_spare(q.shape, c)
        ones = (lane >= spare) & (lane < spare + N_BIAS_LANES)
        qh_scr[c] = jnp.where(own, q, jnp.where(ones, 1.0, 0.0)).astype(BF16)


def _augment_k(k, bias, c):
    lane, own, spare = _own_and_spare(k.shape, c)
    hi = bias.astype(BF16).astype(F32)
    r1 = bias - hi
    mid = r1.astype(BF16).astype(F32)
    lo = (r1 - mid).astype(BF16).astype(F32)
    extra = jnp.where(lane == spare, hi, jnp.where(lane == spare + 1, mid, jnp.where(lane == spare + 2, lo, 0.0)))
    return jnp.where(own, k.astype(F32), extra).astype(BF16)


def _transpose_bf16(a):
    return a.astype(F32).T.astype(BF16)


def _attn_scratch(tq, seq, acc_rows, vt_slabs):
    return [
        pltpu.VMEM((N_CHAINS, tq, LANES), BF16),
        pltpu.VMEM((N_CHAINS, 1, tq), F32),
        pltpu.VMEM((N_CHAINS, acc_rows, tq), F32),
        pltpu.VMEM((2, N_CHAINS, tq, tq), F32),
        pltpu.VMEM((2, N_CHAINS, 8, tq), F32),
        pltpu.VMEM((N_CHAINS, seq, LANES), BF16),
        pltpu.VMEM((vt_slabs, acc_rows, seq), BF16),
    ]


def _fox_kernel(q_ref, k_ref, vt_ref, c_ref, wg_ref, wu_ref, wd_ref, o_ref, wgb_ref, wub_ref, wdb_ref,
                qh_scr, m_scr, acc_scr, s_scr, pm_scr, ka_scr, vat_scr):
    tq = q_ref.shape[1]
    seq = k_ref.shape[1]
    qi = pl.program_id(2)
    _split_q(q_ref, qh_scr)

    n = (pl.program_id(0) * pl.num_programs(1) + pl.program_id(1)) * pl.num_programs(2) + qi
    for k, (src, dst) in enumerate(((wg_ref, wgb_ref), (wu_ref, wub_ref), (wd_ref, wdb_ref))):
        @pl.when((n >= k * N_EXPERTS) & (n < (k + 1) * N_EXPERTS))
        def _(src=src, dst=dst):
            dst[...] = src[...].astype(BF16)

    @pl.when(qi == 0)
    def _():
        def chunk(i, carry):
            r0 = pl.multiple_of(i * SETUP_ROWS, SETUP_ROWS)
            rows = pl.ds(r0, SETUP_ROWS)
            head_row = lax.broadcasted_iota(jnp.int32, (LANES, SETUP_ROWS), 0) < HEAD_DIM
            for sl in range(NSL):
                vt = vt_ref[sl, :, rows]
                one = jnp.ones_like(vt)
                vat_scr[2 * sl, :, rows] = jnp.where(head_row, vt, one)
                vat_scr[2 * sl + 1, :, rows] = jnp.where(head_row, one, vt)
                cc = c_ref[sl, :, rows]
                stacked = jnp.where(head_row, jnp.broadcast_to(cc[1:2, :], (LANES, SETUP_ROWS)),
                                    jnp.broadcast_to(cc[0:1, :], (LANES, SETUP_ROWS)))
                bias = -stacked.T
                k = k_ref[sl, rows, :]
                ka_scr[2 * sl, rows, :] = _augment_k(k, bias, 0)
                ka_scr[2 * sl + 1, rows, :] = _augment_k(k, bias, 1)
            return carry

        lax.fori_loop(0, seq // SETUP_ROWS, chunk, 0)

    def k_fn(c, start):
        return ka_scr[c, pl.ds(start, tq), :]

    def vt_fn(c, start):
        return vat_scr[c, :, pl.ds(start, tq)]

    _flash_chains(qi, tq, qh_scr, k_fn, vt_fn, m_scr, acc_scr, s_scr, pm_scr)
    for sl in range(NSL):
        a0 = acc_scr[2 * sl]
        a1 = acc_scr[2 * sl + 1]
        row = lax.broadcasted_iota(jnp.int32, a0.shape, 0)
        ot = jnp.where(row < HEAD_DIM, a0 / a0[HEAD_DIM:HEAD_DIM + 1, :], a1 / a1[0:1, :])
        o_ref[:, sl * LANES:(sl + 1) * LANES] = ot.T.astype(o_ref.dtype)


def _fox(qkv4, vt4, c4, wg, wu, wd, batch, seq):
    tq = ATT_T
    groups = FOX_HEADS // N_CHAINS
    nq = seq // tq
    assert batch * groups * nq >= 3 * N_EXPERTS

    def w_spec(k, rows, cols):
        def index(b, g, i):
            return (jnp.clip((b * groups + g) * nq + i - k * N_EXPERTS, 0, N_EXPERTS - 1), 0, 0)
        return pl.BlockSpec((None, rows, cols), index)

    w_specs = [w_spec(0, D_MODEL, D_EXPERT), w_spec(1, D_MODEL, D_EXPERT), w_spec(2, D_EXPERT, D_MODEL)]
    return pl.pallas_call(
        _fox_kernel,
        grid=(batch, groups, nq),
        in_specs=[
            pl.BlockSpec((NSL, None, tq, LANES), lambda b, g, i: (SLAB_FOX_Q // NSL + g, b, i, 0)),
            pl.BlockSpec((NSL, None, seq, LANES), lambda b, g, i: (SLAB_FOX_K // NSL + g, b, 0, 0)),
            pl.BlockSpec((NSL, None, LANES, seq), lambda b, g, i: (VT_FOX // NSL + g, b, 0, 0)),
            pl.BlockSpec((NSL, 2, seq), lambda b, g, i: (g, 0, b)),
        ] + w_specs,
        out_specs=[pl.BlockSpec((None, tq, NSL * LANES), lambda b, g, i: (b, i, g))] + w_specs,
        out_shape=[jax.ShapeDtypeStruct((batch, seq, FOX_HEADS * HEAD_DIM), BF16)]
        + [jax.ShapeDtypeStruct(w.shape, BF16) for w in (wg, wu, wd)],
        scratch_shapes=_attn_scratch(tq, seq, LANES, N_CHAINS),
        compiler_params=_cparams(("arbitrary", "arbitrary", "arbitrary"), ATT_FLAGS),
        name="fox",
    )(qkv4, qkv4, vt4, c4, wg, wu, wd)


def _diff_kernel(slopes_ref, q_ref, k_ref, vt_ref, lam_ref, g_ref, o_ref, qh_scr, m_scr, acc_scr, s_scr, pm_scr,
                 ka_scr, vat_scr):
    tq = q_ref.shape[1]
    seq = k_ref.shape[1]
    qi = pl.program_id(2)
    _split_q(q_ref, qh_scr)
    g = pl.program_id(1)

    @pl.when(qi == 0)
    def _():
        def chunk(i, carry):
            r0 = pl.multiple_of(i * SETUP_ROWS, SETUP_ROWS)
            rows = pl.ds(r0, SETUP_ROWS)
            for hd in range(NSL):
                vt = vt_ref[hd, :, rows]
                vat_scr[hd, :, rows] = jnp.concatenate([vt, jnp.ones((DIFF_ONES_ROWS, SETUP_ROWS), BF16)], axis=0)
                pos = (lax.broadcasted_iota(jnp.int32, (SETUP_ROWS, LANES), 0) + r0).astype(F32)
                bias = pos * (slopes_ref[NSL * g + hd] * LOG2E)
                k = k_ref[hd, rows, :]
                ka_scr[2 * hd, rows, :] = _augment_k(k, bias, 0)
                ka_scr[2 * hd + 1, rows, :] = _augment_k(k, bias, 1)
            return carry

        lax.fori_loop(0, seq // SETUP_ROWS, chunk, 0)

    def k_fn(c, start):
        return ka_scr[c, pl.ds(start, tq), :]

    def vt_fn(c, start):
        return vat_scr[c // 2, :, pl.ds(start, tq)]

    _flash_chains(qi, tq, qh_scr, k_fn, vt_fn, m_scr, acc_scr, s_scr, pm_scr)
    lp = lam_ref[...]
    s1 = jnp.sum(lp[0:1, :] * lp[1:2, :], axis=1, keepdims=True)
    s2 = jnp.sum(lp[2:3, :] * lp[3:4, :], axis=1, keepdims=True)
    lam = jnp.exp(s1) - jnp.exp(s2) + LAM_INIT
    for hd in range(NSL):
        a1 = acc_scr[2 * hd]
        a2 = acc_scr[2 * hd + 1]
        ot = a1[:LANES, :] / a1[LANES:LANES + 1, :] - lam * (a2[:LANES, :] / a2[LANES:LANES + 1, :])
        ms = jnp.mean(ot * ot, axis=0, keepdims=True)
        yt = ot * lax.rsqrt(ms + LN_EPS) * g_ref[...]
        o_ref[:, hd * LANES:(hd + 1) * LANES] = (yt * (1.0 - LAM_INIT)).T.astype(o_ref.dtype)


def _diff(qkv4, vt4, slopes, lam_params, subln_g_col, batch, seq):
    tq = ATT_T
    grid_spec = pltpu.PrefetchScalarGridSpec(
        num_scalar_prefetch=1,
        grid=(batch, DIFF_HEADS // NSL, seq // tq),
        in_specs=[
            pl.BlockSpec((NSL, None, tq, LANES), lambda b, g, i, s: (SLAB_DIFF_Q // NSL + g, b, i, 0)),
            pl.BlockSpec((NSL, None, seq, LANES), lambda b, g, i, s: (SLAB_DIFF_K // NSL + g, b, 0, 0)),
            pl.BlockSpec((NSL, None, LANES, seq), lambda b, g, i, s: (VT_DIFF // NSL + g, b, 0, 0)),
            pl.BlockSpec((4, HEAD_DIM), lambda b, g, i, s: (0, 0)),
            pl.BlockSpec((LANES, 1), lambda b, g, i, s: (0, 0)),
        ],
        out_specs=pl.BlockSpec((None, tq, NSL * LANES), lambda b, g, i, s: (b, i, g)),
        scratch_shapes=_attn_scratch(tq, seq, LANES + DIFF_ONES_ROWS, NSL),
    )
    return pl.pallas_call(
        _diff_kernel,
        grid_spec=grid_spec,
        out_shape=jax.ShapeDtypeStruct((batch, seq, DIFF_HEADS * LANES), BF16),
        compiler_params=_cparams(("parallel", "parallel", "arbitrary"), ATT_FLAGS),
        name="diff",
    )(slopes, qkv4, qkv4, vt4, lam_params, subln_g_col)


def _memkv_kernel(mem_ref, w_ref, o_ref):
    o_ref[...] = jnp.dot(mem_ref[...].astype(BF16), w_ref[...],
                         preferred_element_type=F32).astype(o_ref.dtype)


def _memkv(mem, w_kv):
    batch = mem.shape[0]
    width = w_kv.shape[1]
    return pl.pallas_call(
        _memkv_kernel,
        grid=(batch,),
        in_specs=[
            pl.BlockSpec((None, N_MEM, D_MODEL), lambda b: (b, 0, 0)),
            pl.BlockSpec((D_MODEL, width), lambda b: (0, 0)),
        ],
        out_specs=pl.BlockSpec((None, N_MEM, width), lambda b: (b, 0, 0)),
        out_shape=jax.ShapeDtypeStruct((batch, N_MEM, width), BF16),
        compiler_params=_cparams(("parallel",)),
        name="mem_kv",
    )(mem, w_kv)


def _mem_kernel(q_ref, mk_ref, mv_ref, o_ref):
    s = lax.dot_general(q_ref[...], mk_ref[...], (((1,), (1,)), ((), ())),
                        preferred_element_type=F32)
    m = jnp.max(s, axis=1, keepdims=True)
    p = jnp.exp2(s - m)
    l = jnp.sum(p, axis=1, keepdims=True)
    o = jnp.dot(p.astype(BF16), mv_ref[...], preferred_element_type=F32)
    o_ref[...] = (o / l).astype(o_ref.dtype)


def _mem_attn(qkv4, mkv, batch, seq):
    tq = MEM_TQ
    return pl.pallas_call(
        _mem_kernel,
        grid=(batch, MEM_HEADS, seq // tq),
        in_specs=[
            pl.BlockSpec((None, None, tq, LANES), lambda b, h, i: (SLAB_MEM_Q + h, b, i, 0)),
            pl.BlockSpec((None, N_MEM, LANES), lambda b, h, i: (b, 0, h)),
            pl.BlockSpec((None, N_MEM, LANES), lambda b, h, i: (b, 0, MEM_HEADS + h)),
        ],
        out_specs=pl.BlockSpec((None, tq, LANES), lambda b, h, i: (b, i, h)),
        out_shape=jax.ShapeDtypeStruct((batch, seq, MEM_HEADS * LANES), BF16),
        compiler_params=_cparams(("parallel", "parallel", "parallel")),
        name="mem",
    )(qkv4, mkv, mkv)


def _layer_norm(z, g, b):
    mu = jnp.mean(z, axis=1, keepdims=True)
    zc = z - mu
    var = jnp.mean(zc * zc, axis=1, keepdims=True)
    return zc * lax.rsqrt(var + LN_EPS) * g + b


def _lane_max(v):
    return jnp.max(v, axis=1, keepdims=True)


def _routing_info(logits):
    lane = lax.broadcasted_iota(jnp.int32, logits.shape, 1)
    big = jnp.int32(2 * LANES)
    is_g = lane < N_GROUPS
    gl = jnp.where(is_g, logits, NEG_BIG)
    gmax = _lane_max(gl)
    g_w = 1.0 / jnp.sum(jnp.exp(gl - gmax), axis=1, keepdims=True)
    g_idx = jnp.min(jnp.where(gl == gmax, lane, big), axis=1, keepdims=True)
    lo = ROUTE_LANE0 + g_idx * EXPERTS_PER_GROUP
    in_grp = (lane >= lo) & (lane < lo + EXPERTS_PER_GROUP)
    el = jnp.where(in_grp, logits, NEG_BIG)
    e1 = _lane_max(el)
    i1 = jnp.min(jnp.where(el == e1, lane, big), axis=1, keepdims=True)
    el2 = jnp.where(lane == i1, NEG_BIG, el)
    e2 = _lane_max(el2)
    i2 = jnp.min(jnp.where(el2 == e2, lane, big), axis=1, keepdims=True)
    r = jnp.exp(e2 - e1)
    w1 = g_w / (1.0 + r)
    w2 = g_w * r / (1.0 + r)
    id1 = (i1 - ROUTE_LANE0).astype(F32)
    id2 = (i2 - ROUTE_LANE0).astype(F32)
    return jnp.where(lane == RI_E1, id1,
                     jnp.where(lane == RI_E2, id2,
                               jnp.where(lane == RI_W1, w1, jnp.where(lane == RI_W2, w2, 0.0))))


def _merge_kernel(x_ref, yf_ref, yd_ref, ym_ref, wg_ref, bg_ref, wbf_ref, wbd_ref, wbm_ref,
                  wo_ref, g1_ref, b1_ref, wr_ref, br_ref, x1_ref, ri_ref, rit_ref):
    x = x_ref[...]
    xb = x.astype(BF16)
    h = None
    for i, (y_ref, wb_ref) in enumerate(((yf_ref, wbf_ref), (yd_ref, wbd_ref), (ym_ref, wbm_ref))):
        cols = slice(i * D_MODEL, (i + 1) * D_MODEL)
        gl = lax.dot_general(xb, wg_ref[cols, :], (((1,), (1,)), ((), ())),
                             preferred_element_type=F32) + bg_ref[:, cols]
        gate = 1.0 / (1.0 + jnp.exp(-gl))
        br = jnp.dot(y_ref[...], wb_ref[...], preferred_element_type=F32)
        h = gate * br if h is None else h + gate * br
    o = jnp.dot(h.astype(BF16), wo_ref[...], preferred_element_type=F32)
    x1 = _layer_norm(DEEPNORM_ALPHA * x + o, g1_ref[...], b1_ref[...])
    x1_ref[...] = x1
    logits = jnp.dot(x1.astype(BF16), wr_ref[...], preferred_element_type=F32) + br_ref[...]
    ri = _routing_info(logits)
    ri_ref[...] = ri
    rit_ref[...] = ri.T[0:8, :]


def _const_spec(shape):
    return pl.BlockSpec(shape, lambda i: (0,) * len(shape), pipeline_mode=pl.Buffered(1))


def _merge(x2, yf, yd, ym, wg, bg, wbf, wbd, wbm, wo, g1, b1, wr, br):
    t = x2.shape[0]
    tm = MERGE_TM
    half = yf.shape[1]
    row = lambda w: pl.BlockSpec((tm, w), lambda i: (i, 0))
    return pl.pallas_call(
        _merge_kernel,
        grid=(t // tm,),
        in_specs=[
            row(D_MODEL), row(half), row(half), row(half),
            _const_spec((N_BRANCH * D_MODEL, D_MODEL)), _const_spec((1, N_BRANCH * D_MODEL)),
            _const_spec((half, D_MODEL)), _const_spec((half, D_MODEL)), _const_spec((half, D_MODEL)),
            _const_spec((D_MODEL, D_MODEL)), _const_spec((1, D_MODEL)), _const_spec((1, D_MODEL)),
            _const_spec((D_MODEL, LANES)), _const_spec((1, LANES)),
        ],
        out_specs=[row(D_MODEL), row(LANES), pl.BlockSpec((8, tm), lambda i: (0, i))],
        out_shape=[
            jax.ShapeDtypeStruct((t, D_MODEL), F32),
            jax.ShapeDtypeStruct((t, LANES), F32),
            jax.ShapeDtypeStruct((8, t), F32),
        ],
        compiler_params=_cparams(("parallel",)),
        name="merge",
    )(x2, yf, yd, ym, wg, bg, wbf, wbd, wbm, wo, g1, b1, wr, br)


def _cumsum(x, axis):
    n = x.shape[axis]
    idx = lax.broadcasted_iota(jnp.int32, x.shape, axis)
    sh = 1
    while sh < n:
        x = x + jnp.where(idx >= sh, pltpu.roll(x, sh, axis=axis), 0.0)
        sh *= 2
    return x


def _route_kernel(rit_ref, pos_ref, tiles_ref):
    s = rit_ref.shape[1]
    e1 = rit_ref[RI_E1:RI_E1 + 1, :]
    e2 = rit_ref[RI_E2:RI_E2 + 1, :]
    sub = lax.broadcasted_iota(jnp.int32, (N_EXPERTS, s), 0).astype(F32)
    oh1 = jnp.where(sub == e1, 1.0, 0.0)
    oh2 = jnp.where(sub == e2, 1.0, 0.0)
    cnt = oh1 + oh2
    incl = _cumsum(cnt, 1)
    before = incl - cnt
    total = jnp.broadcast_to(incl[:, s - 1:s], (N_EXPERTS, LANES))
    aligned = jnp.floor((total + (SEG_ALIGN - 1)) * (1.0 / SEG_ALIGN)) * SEG_ALIGN
    off = _cumsum(aligned, 0) - aligned
    slot = before + off[:, 0:1]
    pos1 = jnp.sum(oh1 * slot, axis=0, keepdims=True)
    pos2 = jnp.sum(oh2 * slot, axis=0, keepdims=True)
    pos_ref[...] = (jnp.concatenate([pos1, pos2], axis=0) * ROW_TILE).astype(jnp.int32)

    nt = jnp.floor((total + (MOE_R - 1)) * (1.0 / MOE_R))
    ct_incl = _cumsum(nt, 0)
    ct_excl = ct_incl - nt
    sub_l = lax.broadcasted_iota(jnp.int32, (N_EXPERTS, LANES), 0).astype(F32)
    tile_i = lax.broadcasted_iota(jnp.int32, (N_EXPERTS, LANES), 1).astype(F32)
    te = jnp.sum(jnp.where(ct_incl <= tile_i, 1.0, 0.0), axis=0, keepdims=True)
    te = jnp.minimum(te, N_EXPERTS - 1.0)
    sel = jnp.where(sub_l == te, 1.0, 0.0)
    k_in = tile_i - ct_excl
    start = jnp.sum(sel * (off + k_in * MOE_R), axis=0, keepdims=True)
    nval = jnp.sum(sel * jnp.clip(total - k_in * MOE_R, 0.0, float(MOE_R)), axis=0, keepdims=True)
    active = tile_i[0:1, :] < ct_incl[N_EXPERTS - 1:N_EXPERTS, :]
    start = jnp.where(active, start, 0.0)
    nval = jnp.where(active, nval, 0.0)
    rows = jnp.concatenate([te, start * ROW_TILE, nval, jnp.zeros((5, LANES), F32)], axis=0)
    tiles_ref[...] = rows.astype(jnp.int32)


def _route(rit, batch, seq):
    return pl.pallas_call(
        _route_kernel,
        grid=(batch,),
        in_specs=[pl.BlockSpec((8, seq), lambda b: (0, b))],
        out_specs=[
            pl.BlockSpec((None, 2, seq), lambda b: (b, 0, 0)),
            pl.BlockSpec((None, 8, LANES), lambda b: (b, 0, 0)),
        ],
        out_shape=[
            jax.ShapeDtypeStruct((batch, 2, seq), jnp.int32),
            jax.ShapeDtypeStruct((batch, 8, LANES), jnp.int32),
        ],
        compiler_params=_cparams(("parallel",)),
        name="route",
    )(rit)


def _read_rows(view, n_rows):
    return jnp.concatenate([view[pl.ds(k, n_rows, stride=ROW_TILE), :] for k in range(ROW_TILE)], axis=1)


def _write_rows(view, n_rows, val):
    for k in range(ROW_TILE):
        view[pl.ds(k, n_rows, stride=ROW_TILE), :] = val[:, k * LANES:(k + 1) * LANES]


def _moe_kernel(nd, nt, texp_ref, tstart_ref, tnval_ref, x1_ref, p1_ref, p2_ref, ri_ref,
                wg_ref, wu_ref, wd_ref, g2_ref, b2_ref, o_ref, buf, xs_scr, as_scr, bs_scr):
    c = pl.program_id(0)
    j = pl.program_id(1)
    td = x1_ref.shape[0]

    def tile_at(ref, off):
        return ref.at[pl.ds(pl.multiple_of(off, ROW_TILE), ROW_TILE), :]

    @pl.when(j == 0)
    def _():
        buf[...] = jnp.zeros(buf.shape, F32)

    @pl.when(j < nd)
    def _():
        _write_rows(xs_scr, td, x1_ref[...])

        def body(i, carry):
            for u in range(SEG_ALIGN):
                t = i * SEG_ALIGN + u
                row = tile_at(xs_scr, t * ROW_TILE)[...]
                tile_at(buf, p1_ref[0, t])[...] = row
                tile_at(buf, p2_ref[0, t])[...] = row
            return carry

        lax.fori_loop(0, td // SEG_ALIGN, body, 0)

    @pl.when((j >= nd) & (j < nd + nt))
    def _():
        idx = c * nt + (j - nd)
        nval = tnval_ref[idx]

        @pl.when(nval > 0)
        def _():
            start = pl.multiple_of(tstart_ref[idx], ROW_TILE * SEG_ALIGN)
            view = buf.at[pl.ds(start, MOE_R * ROW_TILE), :]
            xt = _read_rows(view, MOE_R)
            xb = xt.astype(BF16)
            g = jnp.dot(xb, wg_ref[...], preferred_element_type=F32)
            u = jnp.dot(xb, wu_ref[...], preferred_element_type=F32)
            hcat = (g / (1.0 + jnp.exp(-g))) * u
            y = jnp.dot(hcat.astype(BF16), wd_ref[...], preferred_element_type=F32)
            rows = lax.broadcasted_iota(jnp.int32, (MOE_R, 1), 0)
            _write_rows(view, MOE_R, jnp.where(rows < nval, y, xt))

    @pl.when(j >= nd + nt)
    def _():
        def body(i, carry):
            for u in range(SEG_ALIGN):
                t = i * SEG_ALIGN + u
                tile_at(as_scr, t * ROW_TILE)[...] = tile_at(buf, p1_ref[0, t])[...]
                tile_at(bs_scr, t * ROW_TILE)[...] = tile_at(buf, p2_ref[0, t])[...]
            return carry

        lax.fori_loop(0, td // SEG_ALIGN, body, 0)
        ri = ri_ref[...]
        m = ri[:, RI_W1:RI_W1 + 1] * _read_rows(as_scr, td) + ri[:, RI_W2:RI_W2 + 1] * _read_rows(bs_scr, td)
        z = DEEPNORM_ALPHA * x1_ref[...] + m
        o_ref[...] = _layer_norm(z, g2_ref[...], b2_ref[...])


def _moe(x1, pos3, ri, texp, tstart, tnval, wg, wu, wd, g2, b2, batch, seq):
    t = x1.shape[0]
    td = MOE_TD
    nd = seq // td
    nt = 2 * seq // MOE_R + N_EXPERTS
    buf_rows = 2 * seq + N_EXPERTS * SEG_ALIGN + MOE_R

    def tok_tile(j):
        return jnp.where(j < nd, j, jnp.maximum(j - nd - nt, 0))

    def tile_expert(c, j, texp_ref):
        return texp_ref[c * nt + jnp.clip(j - nd, 0, nt - 1)]

    tok_spec = lambda w: pl.BlockSpec((td, w), lambda c, j, *_: (c * nd + tok_tile(j), 0))
    pos_spec = lambda k: pl.BlockSpec((None, 1, td), lambda c, j, *_: ((c * 2 + k) * nd + tok_tile(j), 0, 0),
                                      memory_space=pltpu.SMEM)
    w_spec = lambda a, b_: pl.BlockSpec((None, a, b_), lambda c, j, te, *_: (tile_expert(c, j, te), 0, 0))
    vec_spec = pl.BlockSpec((1, D_MODEL), lambda c, j, *_: (0, 0))
    grid_spec = pltpu.PrefetchScalarGridSpec(
        num_scalar_prefetch=3,
        grid=(batch, nd + nt + nd),
        in_specs=[
            tok_spec(D_MODEL), pos_spec(0), pos_spec(1), tok_spec(LANES),
            w_spec(D_MODEL, D_EXPERT), w_spec(D_MODEL, D_EXPERT), w_spec(D_EXPERT, D_MODEL),
            vec_spec, vec_spec,
        ],
        out_specs=pl.BlockSpec((td, D_MODEL),
                               lambda c, j, *_: (c * nd + jnp.maximum(j - nd - nt, 0), 0)),
        scratch_shapes=[
            pltpu.VMEM((buf_rows * ROW_TILE, LANES), F32),
            pltpu.VMEM((td * ROW_TILE, LANES), F32),
            pltpu.VMEM((td * ROW_TILE, LANES), F32),
            pltpu.VMEM((td * ROW_TILE, LANES), F32),
        ],
    )
    return pl.pallas_call(
        functools.partial(_moe_kernel, nd, nt),
        grid_spec=grid_spec,
        out_shape=jax.ShapeDtypeStruct((t, D_MODEL), F32),
        compiler_params=_cparams(("arbitrary", "arbitrary")),
        name="moe",
    )(texp, tstart, tnval, x1, pos3, pos3, ri, wg, wu, wd, g2, b2)


def _qkv_col_scale():
    ones = lambda n: jnp.ones((n,), F32)
    att = FOX_HEADS * HEAD_DIM
    qs = lambda d: jnp.full((att,), LOG2E * d ** -0.5, F32)
    return jnp.concatenate([qs(HEAD_DIM), ones(2 * att), qs(HEAD_DIM), ones(2 * att),
                            qs(MEM_HEAD_DIM)]).reshape(1, QKV_COLS)


def kernel(x, mem, w_in, b_forget, b_gates, lambda_q1, lambda_k1, lambda_q2, lambda_k2, diff_subln_g,
           w_mem_kv, w_branch_fox, w_branch_diff, w_branch_mem, w_out, ln1_g, ln1_b, w_router_group,
           b_router_group, w_router_expert, b_router_expert, w_expert_gate, w_expert_up, w_expert_down,
           ln2_g, ln2_b):
    batch, seq, d = x.shape
    t = batch * seq
    l = 0
    x2 = x.reshape(t, d)

    w_in_t = jnp.swapaxes(w_in, 1, 2)
    w_gates = w_in_t[l, GATE_COL0:FL_COL0, :].astype(BF16)
    w_fl_t = jnp.pad(w_in_t[l, FL_COL0:, :], ((0, 16 - FOX_HEADS), (0, 0))).astype(BF16)
    w_r = jnp.concatenate([w_router_group[l], w_router_expert[l]], axis=1)
    w_r = jnp.pad(w_r, ((0, 0), (0, LANES - w_r.shape[1]))).astype(BF16)
    b_r = jnp.concatenate([b_router_group[l], b_router_expert[l]])
    b_r = jnp.pad(b_r, (0, LANES - b_r.shape[0])).reshape(1, LANES)
    lam_params = jnp.stack([lambda_q1[l], lambda_k1[l], lambda_q2[l], lambda_k2[l]])
    slopes = 2.0 ** (-8.0 * jnp.arange(1, DIFF_HEADS + 1, dtype=F32) / DIFF_HEADS)

    qkv, vt4, fl_t = _proj(x2, w_in_t, w_fl_t, _qkv_col_scale(), batch, seq)
    qkv4 = qkv.reshape(N_SLABS, batch, seq, LANES)
    c = _fscan(fl_t, b_forget[l].reshape(FOX_HEADS, 1), batch, seq)
    c4 = c.reshape(FOX_HEADS // 2, 2, t)

    y_fox, wg_b, wu_b, wd_b = _fox(qkv4, vt4, c4, w_expert_gate[l], w_expert_up[l], w_expert_down[l],
                                   batch, seq)
    y_diff = _diff(qkv4, vt4, slopes, lam_params, diff_subln_g[l].reshape(LANES, 1), batch, seq)
    mkv = _memkv(mem, w_mem_kv[l].astype(BF16))
    y_mem = _mem_attn(qkv4, mkv, batch, seq)

    x1, ri, rit = _merge(
        x2, y_fox.reshape(t, -1), y_diff.reshape(t, -1), y_mem.reshape(t, -1),
        w_gates, b_gates[l].reshape(1, -1),
        w_branch_fox[l].astype(BF16), w_branch_diff[l].astype(BF16), w_branch_mem[l].astype(BF16),
        w_out[l].astype(BF16), ln1_g[l].reshape(1, d), ln1_b[l].reshape(1, d), w_r, b_r)

    pos, tiles = _route(rit, batch, seq)
    nt = 2 * seq // MOE_R + N_EXPERTS
    texp, tstart, tnval = (tiles[:, r, :nt].reshape(-1) for r in range(3))
    pos3 = pos.reshape(batch * 2 * (seq // MOE_TD), 1, MOE_TD)
    out = _moe(x1, pos3, ri, texp, tstart, tnval,
               wg_b, wu_b, wd_b,
               ln2_g[l].reshape(1, d), ln2_b[l].reshape(1, d), batch, seq)
    return out.reshape(batch, seq, d)
```

```python
import functools
import math

import jax
import jax.numpy as jnp
from jax import lax
from jax.experimental import pallas as pl
from jax.experimental.pallas import tpu as pltpu

F32 = jnp.float32
BF16 = jnp.bfloat16

D_MODEL = 1024
HEAD_DIM = 64
FOX_HEADS = 8
DIFF_HEADS = 4
MEM_HEADS = 4
MEM_HEAD_DIM = 128
N_MEM = 256
N_BRANCH = 3
N_GROUPS = 4
EXPERTS_PER_GROUP = 8
N_EXPERTS = N_GROUPS * EXPERTS_PER_GROUP
D_EXPERT = 256
LN_EPS = 1e-5
DEPTH = 1
DEEPNORM_ALPHA = (2.0 * DEPTH) ** 0.25
LAM_INIT = 0.8 - 0.6 * math.exp(-0.3 * 0)
LOG2E = math.log2(math.e)

LANES = 128
QKV_COLS = 3584
N_SLABS = QKV_COLS // LANES
GATE_COL0 = QKV_COLS
FL_COL0 = QKV_COLS + N_BRANCH * D_MODEL
SLAB_FOX_Q, SLAB_FOX_K, SLAB_FOX_V = 0, 4, 8
SLAB_DIFF_Q, SLAB_DIFF_K, SLAB_DIFF_V = 12, 16, 20
SLAB_MEM_Q = 24
ROUTE_LANE0 = N_GROUPS

NEG_BIG = -1e30
VMEM_LIMIT = 56 * 1024 * 1024

PROJ_TM = 2048
PROJ_TN = 512
PROJ_J_FOX_V = SLAB_FOX_V * LANES // PROJ_TN
PROJ_J_DIFF_V = SLAB_DIFF_V * LANES // PROJ_TN
VT_FOX, VT_DIFF = 0, PROJ_TN // LANES
ATT_T = 256
ATT_UNROLL = 8
N_CHAINS = 4
NSL = N_CHAINS // 2
N_BIAS_LANES = 3
DIFF_ONES_ROWS = 16
SETUP_ROWS = 512
MEM_TQ = 4096
MERGE_TM = 1024
MOE_R = 256
MOE_TD = 256
SEG_ALIGN = 8
ROW_TILE = D_MODEL // LANES
RI_E1, RI_E2, RI_W1, RI_W2 = 0, 1, 2, 3


def _cparams(sem, flags=None):
    return pltpu.CompilerParams(dimension_semantics=sem, vmem_limit_bytes=VMEM_LIMIT, flags=flags)


ATT_FLAGS = None


def _proj_kernel(x_ref, w_ref, wfl_ref, sc_ref, o_ref, vt_ref, fl_ref, xb_ref):
    j = pl.program_id(1)
    is_v = (j == PROJ_J_FOX_V) | (j == PROJ_J_DIFF_V)
    spc = PROJ_TN // LANES

    @pl.when(j == 0)
    def _():
        xb = x_ref[...].astype(BF16)
        xb_ref[...] = xb
        fl_ref[...] = lax.dot_general(wfl_ref[...], xb, (((1,), (1,)), ((), ())),
                                      preferred_element_type=F32)

    @pl.when(jnp.logical_not(is_v))
    def _():
        acc = lax.dot_general(xb_ref[...], w_ref[...].astype(BF16), (((1,), (1,)), ((), ())),
                              preferred_element_type=F32) * sc_ref[...]
        for c in range(spc):
            o_ref[c] = acc[:, c * LANES:(c + 1) * LANES].astype(BF16)

    @pl.when(is_v)
    def _():
        acc_t = lax.dot_general(w_ref[...].astype(BF16), xb_ref[...], (((1,), (1,)), ((), ())),
                                preferred_element_type=F32)
        for c in range(spc):
            vt_ref[c] = acc_t[c * LANES:(c + 1) * LANES, :].astype(BF16)
        o_ref[...] = jnp.zeros(o_ref.shape, BF16)


def _proj(x2, w_in_t, w_fl_t, col_scale, batch, seq):
    t = x2.shape[0]
    spc = PROJ_TN // LANES
    per_seq = seq // PROJ_TM
    return pl.pallas_call(
        _proj_kernel,
        grid=(t // PROJ_TM, QKV_COLS // PROJ_TN),
        in_specs=[
            pl.BlockSpec((PROJ_TM, D_MODEL), lambda i, j: (i, 0)),
            pl.BlockSpec((None, PROJ_TN, D_MODEL), lambda i, j: (0, j, 0)),
            pl.BlockSpec((16, D_MODEL), lambda i, j: (0, 0)),
            pl.BlockSpec((1, PROJ_TN), lambda i, j: (0, j)),
        ],
        out_specs=[
            pl.BlockSpec((spc, PROJ_TM, LANES), lambda i, j: (j, i, 0)),
            pl.BlockSpec((spc, None, LANES, PROJ_TM),
                         lambda i, j: (jnp.where(j < PROJ_J_DIFF_V, 0, 1), i // per_seq, 0, i % per_seq)),
            pl.BlockSpec((16, PROJ_TM), lambda i, j: (0, i)),
        ],
        out_shape=[
            jax.ShapeDtypeStruct((N_SLABS, t, LANES), BF16),
            jax.ShapeDtypeStruct((2 * spc, batch, LANES, seq), BF16),
            jax.ShapeDtypeStruct((16, t), F32),
        ],
        scratch_shapes=[pltpu.VMEM((PROJ_TM, D_MODEL), BF16)],
        compiler_params=_cparams(("parallel", "arbitrary")),
        name="proj",
    )(x2, w_in_t, w_fl_t, col_scale)


def _fscan_kernel(fl_ref, b_ref, c_ref):
    s = fl_ref.shape[1]
    z = fl_ref[0:FOX_HEADS, :] + b_ref[...]
    lf = jnp.minimum(z, 0.0) - jnp.log(1.0 + jnp.exp(-jnp.abs(z)))
    lane = lax.broadcasted_iota(jnp.int32, lf.shape, 1)
    sh = 1
    while sh < s:
        r = pltpu.roll(lf, sh, axis=1)
        lf = lf + jnp.where(lane >= sh, r, 0.0)
        sh *= 2
    c_ref[...] = lf * LOG2E


def _fscan(fl_t, b_forget_col, batch, seq):
    t = fl_t.shape[1]
    return pl.pallas_call(
        _fscan_kernel,
        grid=(batch,),
        in_specs=[
            pl.BlockSpec((16, seq), lambda b: (0, b)),
            pl.BlockSpec((FOX_HEADS, 1), lambda b: (0, 0)),
        ],
        out_specs=pl.BlockSpec((FOX_HEADS, seq), lambda b: (0, b)),
        out_shape=jax.ShapeDtypeStruct((FOX_HEADS, t), F32),
        compiler_params=_cparams(("parallel",)),
        name="fscan",
    )(fl_t, b_forget_col)


def _flash_chains(qi, tq, qh_scr, k_fn, vt_fn, m_scr, acc_scr, s_scr, pm_scr):
    m_scr[...] = jnp.full(m_scr.shape, NEG_BIG, F32)
    acc_scr[...] = jnp.zeros(acc_scr.shape, F32)

    def scores(j, slot, masked):
        start = pl.multiple_of(j * tq, tq)
        for c in range(N_CHAINS):
            s = lax.dot_general(k_fn(c, start), qh_scr[c], (((1,), (1,)), ((), ())),
                                preferred_element_type=F32)
            if masked:
                kv = lax.broadcasted_iota(jnp.int32, (tq, tq), 0)
                q = lax.broadcasted_iota(jnp.int32, (tq, tq), 1)
                s = jnp.where(kv <= q, s, NEG_BIG)
            s_scr[slot, c] = s
            pm_scr[slot, c] = jnp.max(s.reshape(tq // 8, 8, tq), axis=0)

    def consume(j, slot):
        start = pl.multiple_of(j * tq, tq)
        for c in range(N_CHAINS):
            m_prev = m_scr[c]
            m_new = jnp.maximum(m_prev, jnp.max(pm_scr[slot, c], axis=0, keepdims=True))
            alpha = jnp.exp2(m_prev - m_new)
            p = jnp.exp2(s_scr[slot, c] - m_new)
            m_scr[c] = m_new
            pv = jnp.dot(vt_fn(c, start), p.astype(BF16), preferred_element_type=F32)
            acc_scr[c] = acc_scr[c] * alpha + pv

    @pl.when(qi == 0)
    def _():
        scores(0, 0, True)
        consume(0, 0)

    def run(j0, count, last_masked, issue_beyond):
        for t in range(count):
            if t + 1 < count or issue_beyond:
                scores(j0 + t + 1, (t + 1) % 2, last_masked and t + 1 == count - 1)
            consume(j0 + t, t % 2)

    @pl.when(qi > 0)
    def _():
        scores(0, 0, False)
        groups = (qi - 1) // ATT_UNROLL

        def body(g, carry):
            run(g * ATT_UNROLL, ATT_UNROLL, False, True)
            return carry

        lax.fori_loop(0, groups, body, 0)
        j0 = groups * ATT_UNROLL
        for rem in range(1, ATT_UNROLL + 1):
            @pl.when(qi - j0 == rem)
            def _(rem=rem):
                run(j0, rem + 1, True, False)


def _own_and_spare(shape, c):
    lane = lax.broadcasted_iota(jnp.int32, shape, 1)
    own = (lane < HEAD_DIM) if c % 2 == 0 else (lane >= HEAD_DIM)
    spare = HEAD_DIM if c % 2 == 0 else 0
    return lane, own, spare


def _split_q(q_ref, qh_scr):
    for c in range(N_CHAINS):
        q = q_ref[c // 2].astype(F32)
        lane, own, spare = _own_and_spare(q.shape, c)
        ones = (lane >= spare) & (lane < spare + N_BIAS_LANES)
        qh_scr[c] = jnp.where(own, q, jnp.where(ones, 1.0, 0.0)).astype(BF16)


def _augment_k(k, bias, c):
    lane, own, spare = _own_and_spare(k.shape, c)
    hi = bias.astype(BF16).astype(F32)
    r1 = bias - hi
    mid = r1.astype(BF16).astype(F32)
    lo = (r1 - mid).astype(BF16).astype(F32)
    extra = jnp.where(lane == spare, hi, jnp.where(lane == spare + 1, mid, jnp.where(lane == spare + 2, lo, 0.0)))
    return jnp.where(own, k.astype(F32), extra).astype(BF16)


def _transpose_bf16(a):
    return a.astype(F32).T.astype(BF16)


def _attn_scratch(tq, seq, acc_rows, vt_slabs):
    return [
        pltpu.VMEM((N_CHAINS, tq, LANES), BF16),
        pltpu.VMEM((N_CHAINS, 1, tq), F32),
        pltpu.VMEM((N_CHAINS, acc_rows, tq), F32),
        pltpu.VMEM((2, N_CHAINS, tq, tq), F32),
        pltpu.VMEM((2, N_CHAINS, 8, tq), F32),
        pltpu.VMEM((N_CHAINS, seq, LANES), BF16),
        pltpu.VMEM((vt_slabs, acc_rows, seq), BF16),
    ]


def _fox_kernel(q_ref, k_ref, vt_ref, c_ref, wg_ref, wu_ref, wd_ref, o_ref, wgb_ref, wub_ref, wdb_ref,
                qh_scr, m_scr, acc_scr, s_scr, pm_scr, ka_scr, vat_scr):
    tq = q_ref.shape[1]
    seq = k_ref.shape[1]
    qi = pl.program_id(2)
    _split_q(q_ref, qh_scr)

    n = (pl.program_id(0) * pl.num_programs(1) + pl.program_id(1)) * pl.num_programs(2) + qi
    for k, (src, dst) in enumerate(((wg_ref, wgb_ref), (wu_ref, wub_ref), (wd_ref, wdb_ref))):
        @pl.when((n >= k * N_EXPERTS) & (n < (k + 1) * N_EXPERTS))
        def _(src=src, dst=dst):
            dst[...] = src[...].astype(BF16)

    @pl.when(qi == 0)
    def _():
        def chunk(i, carry):
            r0 = pl.multiple_of(i * SETUP_ROWS, SETUP_ROWS)
            rows = pl.ds(r0, SETUP_ROWS)
            head_row = lax.broadcasted_iota(jnp.int32, (LANES, SETUP_ROWS), 0) < HEAD_DIM
            for sl in range(NSL):
                vt = vt_ref[sl, :, rows]
                one = jnp.ones_like(vt)
                vat_scr[2 * sl, :, rows] = jnp.where(head_row, vt, one)
                vat_scr[2 * sl + 1, :, rows] = jnp.where(head_row, one, vt)
                cc = c_ref[sl, :, rows]
                stacked = jnp.where(head_row, jnp.broadcast_to(cc[1:2, :], (LANES, SETUP_ROWS)),
                                    jnp.broadcast_to(cc[0:1, :], (LANES, SETUP_ROWS)))
                bias = -stacked.T
                k = k_ref[sl, rows, :]
                ka_scr[2 * sl, rows, :] = _augment_k(k, bias, 0)
                ka_scr[2 * sl + 1, rows, :] = _augment_k(k, bias, 1)
            return carry

        lax.fori_loop(0, seq // SETUP_ROWS, chunk, 0)

    def k_fn(c, start):
        return ka_scr[c, pl.ds(start, tq), :]

    def vt_fn(c, start):
        return vat_scr[c, :, pl.ds(start, tq)]

    _flash_chains(qi, tq, qh_scr, k_fn, vt_fn, m_scr, acc_scr, s_scr, pm_scr)
    for sl in range(NSL):
        a0 = acc_scr[2 * sl]
        a1 = acc_scr[2 * sl + 1]
        row = lax.broadcasted_iota(jnp.int32, a0.shape, 0)
        ot = jnp.where(row < HEAD_DIM, a0 / a0[HEAD_DIM:HEAD_DIM + 1, :], a1 / a1[0:1, :])
        o_ref[:, sl * LANES:(sl + 1) * LANES] = ot.T.astype(o_ref.dtype)


def _fox(qkv4, vt4, c4, wg, wu, wd, batch, seq):
    tq = ATT_T
    groups = FOX_HEADS // N_CHAINS
    nq = seq // tq
    assert batch * groups * nq >= 3 * N_EXPERTS

    def w_spec(k, rows, cols):
        def index(b, g, i):
            return (jnp.clip((b * groups + g) * nq + i - k * N_EXPERTS, 0, N_EXPERTS - 1), 0, 0)
        return pl.BlockSpec((None, rows, cols), index)

    w_specs = [w_spec(0, D_MODEL, D_EXPERT), w_spec(1, D_MODEL, D_EXPERT), w_spec(2, D_EXPERT, D_MODEL)]
    return pl.pallas_call(
        _fox_kernel,
        grid=(batch, groups, nq),
        in_specs=[
            pl.BlockSpec((NSL, None, tq, LANES), lambda b, g, i: (SLAB_FOX_Q // NSL + g, b, i, 0)),
            pl.BlockSpec((NSL, None, seq, LANES), lambda b, g, i: (SLAB_FOX_K // NSL + g, b, 0, 0)),
            pl.BlockSpec((NSL, None, LANES, seq), lambda b, g, i: (VT_FOX // NSL + g, b, 0, 0)),
            pl.BlockSpec((NSL, 2, seq), lambda b, g, i: (g, 0, b)),
        ] + w_specs,
        out_specs=[pl.BlockSpec((None, tq, NSL * LANES), lambda b, g, i: (b, i, g))] + w_specs,
        out_shape=[jax.ShapeDtypeStruct((batch, seq, FOX_HEADS * HEAD_DIM), BF16)]
        + [jax.ShapeDtypeStruct(w.shape, BF16) for w in (wg, wu, wd)],
        scratch_shapes=_attn_scratch(tq, seq, LANES, N_CHAINS),
        compiler_params=_cparams(("arbitrary", "arbitrary", "arbitrary"), ATT_FLAGS),
        name="fox",
    )(qkv4, qkv4, vt4, c4, wg, wu, wd)


def _diff_kernel(slopes_ref, q_ref, k_ref, vt_ref, lam_ref, g_ref, o_ref, qh_scr, m_scr, acc_scr, s_scr, pm_scr,
                 ka_scr, vat_scr):
    tq = q_ref.shape[1]
    seq = k_ref.shape[1]
    qi = pl.program_id(2)
    _split_q(q_ref, qh_scr)
    g = pl.program_id(1)

    @pl.when(qi == 0)
    def _():
        def chunk(i, carry):
            r0 = pl.multiple_of(i * SETUP_ROWS, SETUP_ROWS)
            rows = pl.ds(r0, SETUP_ROWS)
            for hd in range(NSL):
                vt = vt_ref[hd, :, rows]
                vat_scr[hd, :, rows] = jnp.concatenate([vt, jnp.ones((DIFF_ONES_ROWS, SETUP_ROWS), BF16)], axis=0)
                pos = (lax.broadcasted_iota(jnp.int32, (SETUP_ROWS, LANES), 0) + r0).astype(F32)
                bias = pos * (slopes_ref[NSL * g + hd] * LOG2E)
                k = k_ref[hd, rows, :]
                ka_scr[2 * hd, rows, :] = _augment_k(k, bias, 0)
                ka_scr[2 * hd + 1, rows, :] = _augment_k(k, bias, 1)
            return carry

        lax.fori_loop(0, seq // SETUP_ROWS, chunk, 0)

    def k_fn(c, start):
        return ka_scr[c, pl.ds(start, tq), :]

    def vt_fn(c, start):
        return vat_scr[c // 2, :, pl.ds(start, tq)]

    _flash_chains(qi, tq, qh_scr, k_fn, vt_fn, m_scr, acc_scr, s_scr, pm_scr)
    lp = lam_ref[...]
    s1 = jnp.sum(lp[0:1, :] * lp[1:2, :], axis=1, keepdims=True)
    s2 = jnp.sum(lp[2:3, :] * lp[3:4, :], axis=1, keepdims=True)
    lam = jnp.exp(s1) - jnp.exp(s2) + LAM_INIT
    for hd in range(NSL):
        a1 = acc_scr[2 * hd]
        a2 = acc_scr[2 * hd + 1]
        ot = a1[:LANES, :] / a1[LANES:LANES + 1, :] - lam * (a2[:LANES, :] / a2[LANES:LANES + 1, :])
        ms = jnp.mean(ot * ot, axis=0, keepdims=True)
        yt = ot * lax.rsqrt(ms + LN_EPS) * g_ref[...]
        o_ref[:, hd * LANES:(hd + 1) * LANES] = (yt * (1.0 - LAM_INIT)).T.astype(o_ref.dtype)


def _diff(qkv4, vt4, slopes, lam_params, subln_g_col, batch, seq):
    tq = ATT_T
    grid_spec = pltpu.PrefetchScalarGridSpec(
        num_scalar_prefetch=1,
        grid=(batch, DIFF_HEADS // NSL, seq // tq),
        in_specs=[
            pl.BlockSpec((NSL, None, tq, LANES), lambda b, g, i, s: (SLAB_DIFF_Q // NSL + g, b, i, 0)),
            pl.BlockSpec((NSL, None, seq, LANES), lambda b, g, i, s: (SLAB_DIFF_K // NSL + g, b, 0, 0)),
            pl.BlockSpec((NSL, None, LANES, seq), lambda b, g, i, s: (VT_DIFF // NSL + g, b, 0, 0)),
            pl.BlockSpec((4, HEAD_DIM), lambda b, g, i, s: (0, 0)),
            pl.BlockSpec((LANES, 1), lambda b, g, i, s: (0, 0)),
        ],
        out_specs=pl.BlockSpec((None, tq, NSL * LANES), lambda b, g, i, s: (b, i, g)),
        scratch_shapes=_attn_scratch(tq, seq, LANES + DIFF_ONES_ROWS, NSL),
    )
    return pl.pallas_call(
        _diff_kernel,
        grid_spec=grid_spec,
        out_shape=jax.ShapeDtypeStruct((batch, seq, DIFF_HEADS * LANES), BF16),
        compiler_params=_cparams(("parallel", "parallel", "arbitrary"), ATT_FLAGS),
        name="diff",
    )(slopes, qkv4, qkv4, vt4, lam_params, subln_g_col)


def _memkv_kernel(mem_ref, w_ref, o_ref):
    o_ref[...] = jnp.dot(mem_ref[...].astype(BF16), w_ref[...],
                         preferred_element_type=F32).astype(o_ref.dtype)


def _memkv(mem, w_kv):
    batch = mem.shape[0]
    width = w_kv.shape[1]
    return pl.pallas_call(
        _memkv_kernel,
        grid=(batch,),
        in_specs=[
            pl.BlockSpec((None, N_MEM, D_MODEL), lambda b: (b, 0, 0)),
            pl.BlockSpec((D_MODEL, width), lambda b: (0, 0)),
        ],
        out_specs=pl.BlockSpec((None, N_MEM, width), lambda b: (b, 0, 0)),
        out_shape=jax.ShapeDtypeStruct((batch, N_MEM, width), BF16),
        compiler_params=_cparams(("parallel",)),
        name="mem_kv",
    )(mem, w_kv)


def _mem_kernel(q_ref, mk_ref, mv_ref, o_ref):
    s = lax.dot_general(q_ref[...], mk_ref[...], (((1,), (1,)), ((), ())),
                        preferred_element_type=F32)
    m = jnp.max(s, axis=1, keepdims=True)
    p = jnp.exp2(s - m)
    l = jnp.sum(p, axis=1, keepdims=True)
    o = jnp.dot(p.astype(BF16), mv_ref[...], preferred_element_type=F32)
    o_ref[...] = (o / l).astype(o_ref.dtype)


def _mem_attn(qkv4, mkv, batch, seq):
    tq = MEM_TQ
    return pl.pallas_call(
        _mem_kernel,
        grid=(batch, MEM_HEADS, seq // tq),
        in_specs=[
            pl.BlockSpec((None, None, tq, LANES), lambda b, h, i: (SLAB_MEM_Q + h, b, i, 0)),
            pl.BlockSpec((None, N_MEM, LANES), lambda b, h, i: (b, 0, h)),
            pl.BlockSpec((None, N_MEM, LANES), lambda b, h, i: (b, 0, MEM_HEADS + h)),
        ],
        out_specs=pl.BlockSpec((None, tq, LANES), lambda b, h, i: (b, i, h)),
        out_shape=jax.ShapeDtypeStruct((batch, seq, MEM_HEADS * LANES), BF16),
        compiler_params=_cparams(("parallel", "parallel", "parallel")),
        name="mem",
    )(qkv4, mkv, mkv)


def _layer_norm(z, g, b):
    mu = jnp.mean(z, axis=1, keepdims=True)
    zc = z - mu
    var = jnp.mean(zc * zc, axis=1, keepdims=True)
    return zc * lax.rsqrt(var + LN_EPS) * g + b


def _lane_max(v):
    return jnp.max(v, axis=1, keepdims=True)


def _routing_info(logits):
    lane = lax.broadcasted_iota(jnp.int32, logits.shape, 1)
    big = jnp.int32(2 * LANES)
    is_g = lane < N_GROUPS
    gl = jnp.where(is_g, logits, NEG_BIG)
    gmax = _lane_max(gl)
    g_w = 1.0 / jnp.sum(jnp.exp(gl - gmax), axis=1, keepdims=True)
    g_idx = jnp.min(jnp.where(gl == gmax, lane, big), axis=1, keepdims=True)
    lo = ROUTE_LANE0 + g_idx * EXPERTS_PER_GROUP
    in_grp = (lane >= lo) & (lane < lo + EXPERTS_PER_GROUP)
    el = jnp.where(in_grp, logits, NEG_BIG)
    e1 = _lane_max(el)
    i1 = jnp.min(jnp.where(el == e1, lane, big), axis=1, keepdims=True)
    el2 = jnp.where(lane == i1, NEG_BIG, el)
    e2 = _lane_max(el2)
    i2 = jnp.min(jnp.where(el2 == e2, lane, big), axis=1, keepdims=True)
    r = jnp.exp(e2 - e1)
    w1 = g_w / (1.0 + r)
    w2 = g_w * r / (1.0 + r)
    id1 = (i1 - ROUTE_LANE0).astype(F32)
    id2 = (i2 - ROUTE_LANE0).astype(F32)
    return jnp.where(lane == RI_E1, id1,
                     jnp.where(lane == RI_E2, id2,
                               jnp.where(lane == RI_W1, w1, jnp.where(lane == RI_W2, w2, 0.0))))


def _merge_kernel(x_ref, yf_ref, yd_ref, ym_ref, wg_ref, bg_ref, wbf_ref, wbd_ref, wbm_ref,
                  wo_ref, g1_ref, b1_ref, wr_ref, br_ref, x1_ref, ri_ref, rit_ref):
    x = x_ref[...]
    xb = x.astype(BF16)
    h = None
    for i, (y_ref, wb_ref) in enumerate(((yf_ref, wbf_ref), (yd_ref, wbd_ref), (ym_ref, wbm_ref))):
        cols = slice(i * D_MODEL, (i + 1) * D_MODEL)
        gl = lax.dot_general(xb, wg_ref[cols, :], (((1,), (1,)), ((), ())),
                             preferred_element_type=F32) + bg_ref[:, cols]
        gate = 1.0 / (1.0 + jnp.exp(-gl))
        br = jnp.dot(y_ref[...], wb_ref[...], preferred_element_type=F32)
        h = gate * br if h is None else h + gate * br
    o = jnp.dot(h.astype(BF16), wo_ref[...], preferred_element_type=F32)
    x1 = _layer_norm(DEEPNORM_ALPHA * x + o, g1_ref[...], b1_ref[...])
    x1_ref[...] = x1
    logits = jnp.dot(x1.astype(BF16), wr_ref[...], preferred_element_type=F32) + br_ref[...]
    ri = _routing_info(logits)
    ri_ref[...] = ri
    rit_ref[...] = ri.T[0:8, :]


def _const_spec(shape):
    return pl.BlockSpec(shape, lambda i: (0,) * len(shape), pipeline_mode=pl.Buffered(1))


def _merge(x2, yf, yd, ym, wg, bg, wbf, wbd, wbm, wo, g1, b1, wr, br):
    t = x2.shape[0]
    tm = MERGE_TM
    half = yf.shape[1]
    row = lambda w: pl.BlockSpec((tm, w), lambda i: (i, 0))
    return pl.pallas_call(
        _merge_kernel,
        grid=(t // tm,),
        in_specs=[
            row(D_MODEL), row(half), row(half), row(half),
            _const_spec((N_BRANCH * D_MODEL, D_MODEL)), _const_spec((1, N_BRANCH * D_MODEL)),
            _const_spec((half, D_MODEL)), _const_spec((half, D_MODEL)), _const_spec((half, D_MODEL)),
            _const_spec((D_MODEL, D_MODEL)), _const_spec((1, D_MODEL)), _const_spec((1, D_MODEL)),
            _const_spec((D_MODEL, LANES)), _const_spec((1, LANES)),
        ],
        out_specs=[row(D_MODEL), row(LANES), pl.BlockSpec((8, tm), lambda i: (0, i))],
        out_shape=[
            jax.ShapeDtypeStruct((t, D_MODEL), F32),
            jax.ShapeDtypeStruct((t, LANES), F32),
            jax.ShapeDtypeStruct((8, t), F32),
        ],
        compiler_params=_cparams(("parallel",)),
        name="merge",
    )(x2, yf, yd, ym, wg, bg, wbf, wbd, wbm, wo, g1, b1, wr, br)


def _cumsum(x, axis):
    n = x.shape[axis]
    idx = lax.broadcasted_iota(jnp.int32, x.shape, axis)
    sh = 1
    while sh < n:
        x = x + jnp.where(idx >= sh, pltpu.roll(x, sh, axis=axis), 0.0)
        sh *= 2
    return x


def _route_kernel(rit_ref, pos_ref, tiles_ref):
    s = rit_ref.shape[1]
    e1 = rit_ref[RI_E1:RI_E1 + 1, :]
    e2 = rit_ref[RI_E2:RI_E2 + 1, :]
    sub = lax.broadcasted_iota(jnp.int32, (N_EXPERTS, s), 0).astype(F32)
    oh1 = jnp.where(sub == e1, 1.0, 0.0)
    oh2 = jnp.where(sub == e2, 1.0, 0.0)
    cnt = oh1 + oh2
    incl = _cumsum(cnt, 1)
    before = incl - cnt
    total = jnp.broadcast_to(incl[:, s - 1:s], (N_EXPERTS, LANES))
    aligned = jnp.floor((total + (SEG_ALIGN - 1)) * (1.0 / SEG_ALIGN)) * SEG_ALIGN
    off = _cumsum(aligned, 0) - aligned
    slot = before + off[:, 0:1]
    pos1 = jnp.sum(oh1 * slot, axis=0, keepdims=True)
    pos2 = jnp.sum(oh2 * slot, axis=0, keepdims=True)
    pos_ref[...] = (jnp.concatenate([pos1, pos2], axis=0) * ROW_TILE).astype(jnp.int32)

    nt = jnp.floor((total + (MOE_R - 1)) * (1.0 / MOE_R))
    ct_incl = _cumsum(nt, 0)
    ct_excl = ct_incl - nt
    sub_l = lax.broadcasted_iota(jnp.int32, (N_EXPERTS, LANES), 0).astype(F32)
    tile_i = lax.broadcasted_iota(jnp.int32, (N_EXPERTS, LANES), 1).astype(F32)
    te = jnp.sum(jnp.where(ct_incl <= tile_i, 1.0, 0.0), axis=0, keepdims=True)
    te = jnp.minimum(te, N_EXPERTS - 1.0)
    sel = jnp.where(sub_l == te, 1.0, 0.0)
    k_in = tile_i - ct_excl
    start = jnp.sum(sel * (off + k_in * MOE_R), axis=0, keepdims=True)
    nval = jnp.sum(sel * jnp.clip(total - k_in * MOE_R, 0.0, float(MOE_R)), axis=0, keepdims=True)
    active = tile_i[0:1, :] < ct_incl[N_EXPERTS - 1:N_EXPERTS, :]
    start = jnp.where(active, start, 0.0)
    nval = jnp.where(active, nval, 0.0)
    rows = jnp.concatenate([te, start * ROW_TILE, nval, jnp.zeros((5, LANES), F32)], axis=0)
    tiles_ref[...] = rows.astype(jnp.int32)


def _route(rit, batch, seq):
    return pl.pallas_call(
        _route_kernel,
        grid=(batch,),
        in_specs=[pl.BlockSpec((8, seq), lambda b: (0, b))],
        out_specs=[
            pl.BlockSpec((None, 2, seq), lambda b: (b, 0, 0)),
            pl.BlockSpec((None, 8, LANES), lambda b: (b, 0, 0)),
        ],
        out_shape=[
            jax.ShapeDtypeStruct((batch, 2, seq), jnp.int32),
            jax.ShapeDtypeStruct((batch, 8, LANES), jnp.int32),
        ],
        compiler_params=_cparams(("parallel",)),
        name="route",
    )(rit)


def _read_rows(view, n_rows):
    return jnp.concatenate([view[pl.ds(k, n_rows, stride=ROW_TILE), :] for k in range(ROW_TILE)], axis=1)


def _write_rows(view, n_rows, val):
    for k in range(ROW_TILE):
        view[pl.ds(k, n_rows, stride=ROW_TILE), :] = val[:, k * LANES:(k + 1) * LANES]


def _moe_kernel(nd, nt, texp_ref, tstart_ref, tnval_ref, x1_ref, p1_ref, p2_ref, ri_ref,
                wg_ref, wu_ref, wd_ref, g2_ref, b2_ref, o_ref, buf, xs_scr, as_scr, bs_scr):
    c = pl.program_id(0)
    j = pl.program_id(1)
    td = x1_ref.shape[0]

    def tile_at(ref, off):
        return ref.at[pl.ds(pl.multiple_of(off, ROW_TILE), ROW_TILE), :]

    @pl.when(j == 0)
    def _():
        buf[...] = jnp.zeros(buf.shape, F32)

    @pl.when(j < nd)
    def _():
        _write_rows(xs_scr, td, x1_ref[...])

        def body(i, carry):
            for u in range(SEG_ALIGN):
                t = i * SEG_ALIGN + u
                row = tile_at(xs_scr, t * ROW_TILE)[...]
                tile_at(buf, p1_ref[0, t])[...] = row
                tile_at(buf, p2_ref[0, t])[...] = row
            return carry

        lax.fori_loop(0, td // SEG_ALIGN, body, 0)

    @pl.when((j >= nd) & (j < nd + nt))
    def _():
        idx = c * nt + (j - nd)
        nval = tnval_ref[idx]

        @pl.when(nval > 0)
        def _():
            start = pl.multiple_of(tstart_ref[idx], ROW_TILE * SEG_ALIGN)
            view = buf.at[pl.ds(start, MOE_R * ROW_TILE), :]
            xt = _read_rows(view, MOE_R)
            xb = xt.astype(BF16)
            g = jnp.dot(xb, wg_ref[...], preferred_element_type=F32)
            u = jnp.dot(xb, wu_ref[...], preferred_element_type=F32)
            hcat = (g / (1.0 + jnp.exp(-g))) * u
            y = jnp.dot(hcat.astype(BF16), wd_ref[...], preferred_element_type=F32)
            rows = lax.broadcasted_iota(jnp.int32, (MOE_R, 1), 0)
            _write_rows(view, MOE_R, jnp.where(rows < nval, y, xt))

    @pl.when(j >= nd + nt)
    def _():
        def body(i, carry):
            for u in range(SEG_ALIGN):
                t = i * SEG_ALIGN + u
                tile_at(as_scr, t * ROW_TILE)[...] = tile_at(buf, p1_ref[0, t])[...]
                tile_at(bs_scr, t * ROW_TILE)[...] = tile_at(buf, p2_ref[0, t])[...]
            return carry

        lax.fori_loop(0, td // SEG_ALIGN, body, 0)
        ri = ri_ref[...]
        m = ri[:, RI_W1:RI_W1 + 1] * _read_rows(as_scr, td) + ri[:, RI_W2:RI_W2 + 1] * _read_rows(bs_scr, td)
        z = DEEPNORM_ALPHA * x1_ref[...] + m
        o_ref[...] = _layer_norm(z, g2_ref[...], b2_ref[...])


def _moe(x1, pos3, ri, texp, tstart, tnval, wg, wu, wd, g2, b2, batch, seq):
    t = x1.shape[0]
    td = MOE_TD
    nd = seq // td
    nt = 2 * seq // MOE_R + N_EXPERTS
    buf_rows = 2 * seq + N_EXPERTS * SEG_ALIGN + MOE_R

    def tok_tile(j):
        return jnp.where(j < nd, j, jnp.maximum(j - nd - nt, 0))

    def tile_expert(c, j, texp_ref):
        return texp_ref[c * nt + jnp.clip(j - nd, 0, nt - 1)]

    tok_spec = lambda w: pl.BlockSpec((td, w), lambda c, j, *_: (c * nd + tok_tile(j), 0))
    pos_spec = lambda k: pl.BlockSpec((None, 1, td), lambda c, j, *_: ((c * 2 + k) * nd + tok_tile(j), 0, 0),
                                      memory_space=pltpu.SMEM)
    w_spec = lambda a, b_: pl.BlockSpec((None, a, b_), lambda c, j, te, *_: (tile_expert(c, j, te), 0, 0))
    vec_spec = pl.BlockSpec((1, D_MODEL), lambda c, j, *_: (0, 0))
    grid_spec = pltpu.PrefetchScalarGridSpec(
        num_scalar_prefetch=3,
        grid=(batch, nd + nt + nd),
        in_specs=[
            tok_spec(D_MODEL), pos_spec(0), pos_spec(1), tok_spec(LANES),
            w_spec(D_MODEL, D_EXPERT), w_spec(D_MODEL, D_EXPERT), w_spec(D_EXPERT, D_MODEL),
            vec_spec, vec_spec,
        ],
        out_specs=pl.BlockSpec((td, D_MODEL),
                               lambda c, j, *_: (c * nd + jnp.maximum(j - nd - nt, 0), 0)),
        scratch_shapes=[
            pltpu.VMEM((buf_rows * ROW_TILE, LANES), F32),
            pltpu.VMEM((td * ROW_TILE, LANES), F32),
            pltpu.VMEM((td * ROW_TILE, LANES), F32),
            pltpu.VMEM((td * ROW_TILE, LANES), F32),
        ],
    )
    return pl.pallas_call(
        functools.partial(_moe_kernel, nd, nt),
        grid_spec=grid_spec,
        out_shape=jax.ShapeDtypeStruct((t, D_MODEL), F32),
        compiler_params=_cparams(("arbitrary", "arbitrary")),
        name="moe",
    )(texp, tstart, tnval, x1, pos3, pos3, ri, wg, wu, wd, g2, b2)


def _qkv_col_scale():
    ones = lambda n: jnp.ones((n,), F32)
    att = FOX_HEADS * HEAD_DIM
    qs = lambda d: jnp.full((att,), LOG2E * d ** -0.5, F32)
    return jnp.concatenate([qs(HEAD_DIM), ones(2 * att), qs(HEAD_DIM), ones(2 * att),
                            qs(MEM_HEAD_DIM)]).reshape(1, QKV_COLS)


def kernel(x, mem, w_in, b_forget, b_gates, lambda_q1, lambda_k1, lambda_q2, lambda_k2, diff_subln_g,
           w_mem_kv, w_branch_fox, w_branch_diff, w_branch_mem, w_out, ln1_g, ln1_b, w_router_group,
           b_router_group, w_router_expert, b_router_expert, w_expert_gate, w_expert_up, w_expert_down,
           ln2_g, ln2_b):
    batch, seq, d = x.shape
    t = batch * seq
    l = 0
    x2 = x.reshape(t, d)

    w_in_t = jnp.swapaxes(w_in, 1, 2)
    w_gates = w_in_t[l, GATE_COL0:FL_COL0, :].astype(BF16)
    w_fl_t = jnp.pad(w_in_t[l, FL_COL0:, :], ((0, 16 - FOX_HEADS), (0, 0))).astype(BF16)
    w_r = jnp.concatenate([w_router_group[l], w_router_expert[l]], axis=1)
    w_r = jnp.pad(w_r, ((0, 0), (0, LANES - w_r.shape[1]))).astype(BF16)
    b_r = jnp.concatenate([b_router_group[l], b_router_expert[l]])
    b_r = jnp.pad(b_r, (0, LANES - b_r.shape[0])).reshape(1, LANES)
    lam_params = jnp.stack([lambda_q1[l], lambda_k1[l], lambda_q2[l], lambda_k2[l]])
    slopes = 2.0 ** (-8.0 * jnp.arange(1, DIFF_HEADS + 1, dtype=F32) / DIFF_HEADS)

    qkv, vt4, fl_t = _proj(x2, w_in_t, w_fl_t, _qkv_col_scale(), batch, seq)
    qkv4 = qkv.reshape(N_SLABS, batch, seq, LANES)
    c = _fscan(fl_t, b_forget[l].reshape(FOX_HEADS, 1), batch, seq)
    c4 = c.reshape(FOX_HEADS // 2, 2, t)

    y_fox, wg_b, wu_b, wd_b = _fox(qkv4, vt4, c4, w_expert_gate[l], w_expert_up[l], w_expert_down[l],
                                   batch, seq)
    y_diff = _diff(qkv4, vt4, slopes, lam_params, diff_subln_g[l].reshape(LANES, 1), batch, seq)
    mkv = _memkv(mem, w_mem_kv[l].astype(BF16))
    y_mem = _mem_attn(qkv4, mkv, batch, seq)

    x1, ri, rit = _merge(
        x2, y_fox.reshape(t, -1), y_diff.reshape(t, -1), y_mem.reshape(t, -1),
        w_gates, b_gates[l].reshape(1, -1),
        w_branch_fox[l].astype(BF16), w_branch_diff[l].astype(BF16), w_branch_mem[l].astype(BF16),
        w_out[l].astype(BF16), ln1_g[l].reshape(1, d), ln1_b[l].reshape(1, d), w_r, b_r)

    pos, tiles = _route(rit, batch, seq)
    nt = 2 * seq // MOE_R + N_EXPERTS
    texp, tstart, tnval = (tiles[:, r, :nt].reshape(-1) for r in range(3))
    pos3 = pos.reshape(batch * 2 * (seq // MOE_TD), 1, MOE_TD)
    out = _moe(x1, pos3, ri, texp, tstart, tnval,
               wg_b, wu_b, wd_b,
               ln2_g[l].reshape(1, d), ln2_b[l].reshape(1, d), batch, seq)
    return out.reshape(batch, seq, d)
```

```python
import functools
import math

import jax
import jax.numpy as jnp
from jax import lax
from jax.experimental import pallas as pl
from jax.experimental.pallas import tpu as pltpu

F32 = jnp.float32
BF16 = jnp.bfloat16

D_MODEL = 1024
HEAD_DIM = 64
FOX_HEADS = 8
DIFF_HEADS = 4
MEM_HEADS = 4
MEM_HEAD_DIM = 128
N_MEM = 256
N_BRANCH = 3
N_GROUPS = 4
EXPERTS_PER_GROUP = 8
N_EXPERTS = N_GROUPS * EXPERTS_PER_GROUP
D_EXPERT = 256
LN_EPS = 1e-5
DEPTH = 1
DEEPNORM_ALPHA = (2.0 * DEPTH) ** 0.25
LAM_INIT = 0.8 - 0.6 * math.exp(-0.3 * 0)
LOG2E = math.log2(math.e)

LANES = 128
QKV_COLS = 3584
N_SLABS = QKV_COLS // LANES
GATE_COL0 = QKV_COLS
FL_COL0 = QKV_COLS + N_BRANCH * D_MODEL
SLAB_FOX_Q, SLAB_FOX_K, SLAB_FOX_V = 0, 4, 8
SLAB_DIFF_Q, SLAB_DIFF_K, SLAB_DIFF_V = 12, 16, 20
SLAB_MEM_Q = 24
ROUTE_LANE0 = N_GROUPS

NEG_BIG = -1e30
VMEM_LIMIT = 56 * 1024 * 1024

PROJ_TM = 2048
PROJ_TN = 512
PROJ_J_FOX_V = SLAB_FOX_V * LANES // PROJ_TN
PROJ_J_DIFF_V = SLAB_DIFF_V * LANES // PROJ_TN
VT_FOX, VT_DIFF = 0, PROJ_TN // LANES
GATE_CAST_STEPS = N_BRANCH * D_MODEL // PROJ_TN
ATT_T = 256
ATT_UNROLL = 8
N_CHAINS = 4
NSL = N_CHAINS // 2
N_BIAS_LANES = 3
DIFF_ONES_ROWS = 16
SETUP_ROWS = 512
MEM_TQ = 4096
MERGE_TM = 1024
MOE_R = 256
MOE_TD = 256
SEG_ALIGN = 8
ROW_TILE = D_MODEL // LANES
RI_E1, RI_E2, RI_W1, RI_W2 = 0, 1, 2, 3


def _cparams(sem, flags=None):
    return pltpu.CompilerParams(dimension_semantics=sem, vmem_limit_bytes=VMEM_LIMIT, flags=flags)


ATT_FLAGS = None


def _proj_kernel(x_ref, w_ref, wfl_ref, sc_ref, o_ref, vt_ref, fl_ref, xb_ref):
    j = pl.program_id(1)
    is_v = (j == PROJ_J_FOX_V) | (j == PROJ_J_DIFF_V)
    spc = PROJ_TN // LANES

    @pl.when(j == 0)
    def _():
        xb = x_ref[...].astype(BF16)
        xb_ref[...] = xb
        fl_ref[...] = lax.dot_general(wfl_ref[...], xb, (((1,), (1,)), ((), ())),
                                      preferred_element_type=F32)

    @pl.when(jnp.logical_not(is_v))
    def _():
        acc = lax.dot_general(xb_ref[...], w_ref[...].astype(BF16), (((1,), (1,)), ((), ())),
                              preferred_element_type=F32) * sc_ref[...]
        for c in range(spc):
            o_ref[c] = acc[:, c * LANES:(c + 1) * LANES].astype(BF16)

    @pl.when(is_v)
    def _():
        acc_t = lax.dot_general(w_ref[...].astype(BF16), xb_ref[...], (((1,), (1,)), ((), ())),
                                preferred_element_type=F32)
        for c in range(spc):
            vt_ref[c] = acc_t[c * LANES:(c + 1) * LANES, :].astype(BF16)
        o_ref[...] = jnp.zeros(o_ref.shape, BF16)


def _proj(x2, w_in_t, w_fl_t, col_scale, batch, seq):
    t = x2.shape[0]
    spc = PROJ_TN // LANES
    per_seq = seq // PROJ_TM
    return pl.pallas_call(
        _proj_kernel,
        grid=(t // PROJ_TM, QKV_COLS // PROJ_TN),
        in_specs=[
            pl.BlockSpec((PROJ_TM, D_MODEL), lambda i, j: (i, 0)),
            pl.BlockSpec((None, PROJ_TN, D_MODEL), lambda i, j: (0, j, 0)),
            pl.BlockSpec((16, D_MODEL), lambda i, j: (0, 0)),
            pl.BlockSpec((1, PROJ_TN), lambda i, j: (0, j)),
        ],
        out_specs=[
            pl.BlockSpec((spc, PROJ_TM, LANES), lambda i, j: (j, i, 0)),
            pl.BlockSpec((spc, None, LANES, PROJ_TM),
                         lambda i, j: (jnp.where(j < PROJ_J_DIFF_V, 0, 1), i // per_seq, 0, i % per_seq)),
            pl.BlockSpec((16, PROJ_TM), lambda i, j: (0, i)),
        ],
        out_shape=[
            jax.ShapeDtypeStruct((N_SLABS, t, LANES), BF16),
            jax.ShapeDtypeStruct((2 * spc, batch, LANES, seq), BF16),
            jax.ShapeDtypeStruct((16, t), F32),
        ],
        scratch_shapes=[pltpu.VMEM((PROJ_TM, D_MODEL), BF16)],
        compiler_params=_cparams(("parallel", "arbitrary")),
        name="proj",
    )(x2, w_in_t, w_fl_t, col_scale)


def _fscan_kernel(fl_ref, b_ref, c_ref):
    s = fl_ref.shape[1]
    z = fl_ref[0:FOX_HEADS, :] + b_ref[...]
    lf = jnp.minimum(z, 0.0) - jnp.log(1.0 + jnp.exp(-jnp.abs(z)))
    lane = lax.broadcasted_iota(jnp.int32, lf.shape, 1)
    sh = 1
    while sh < s:
        r = pltpu.roll(lf, sh, axis=1)
        lf = lf + jnp.where(lane >= sh, r, 0.0)
        sh *= 2
    c_ref[...] = lf * LOG2E


def _fscan(fl_t, b_forget_col, batch, seq):
    t = fl_t.shape[1]
    return pl.pallas_call(
        _fscan_kernel,
        grid=(batch,),
        in_specs=[
            pl.BlockSpec((16, seq), lambda b: (0, b)),
            pl.BlockSpec((FOX_HEADS, 1), lambda b: (0, 0)),
        ],
        out_specs=pl.BlockSpec((FOX_HEADS, seq), lambda b: (0, b)),
        out_shape=jax.ShapeDtypeStruct((FOX_HEADS, t), F32),
        compiler_params=_cparams(("parallel",)),
        name="fscan",
    )(fl_t, b_forget_col)


def _flash_chains(qi, tq, qh_scr, k_fn, vt_fn, m_scr, acc_scr, s_scr, pm_scr):
    m_scr[...] = jnp.full(m_scr.shape, NEG_BIG, F32)
    acc_scr[...] = jnp.zeros(acc_scr.shape, F32)

    def scores(j, slot, masked):
        start = pl.multiple_of(j * tq, tq)
        for c in range(N_CHAINS):
            s = lax.dot_general(k_fn(c, start), qh_scr[c], (((1,), (1,)), ((), ())),
                                preferred_element_type=F32)
            if masked:
                kv = lax.broadcasted_iota(jnp.int32, (tq, tq), 0)
                q = lax.broadcasted_iota(jnp.int32, (tq, tq), 1)
                s = jnp.where(kv <= q, s, NEG_BIG)
            s_scr[slot, c] = s
            pm_scr[slot, c] = jnp.max(s.reshape(tq // 8, 8, tq), axis=0)

    def consume(j, slot):
        start = pl.multiple_of(j * tq, tq)
        for c in range(N_CHAINS):
            m_prev = m_scr[c]
            m_new = jnp.maximum(m_prev, jnp.max(pm_scr[slot, c], axis=0, keepdims=True))
            alpha = jnp.exp2(m_prev - m_new)
            p = jnp.exp2(s_scr[slot, c] - m_new)
            m_scr[c] = m_new
            pv = jnp.dot(vt_fn(c, start), p.astype(BF16), preferred_element_type=F32)
            acc_scr[c] = acc_scr[c] * alpha + pv

    @pl.when(qi == 0)
    def _():
        scores(0, 0, True)
        consume(0, 0)

    def run(j0, count, last_masked, issue_beyond):
        for t in range(count):
            if t + 1 < count or issue_beyond:
                scores(j0 + t + 1, (t + 1) % 2, last_masked and t + 1 == count - 1)
            consume(j0 + t, t % 2)

    @pl.when(qi > 0)
    def _():
        scores(0, 0, False)
        groups = (qi - 1) // ATT_UNROLL

        def body(g, carry):
            run(g * ATT_UNROLL, ATT_UNROLL, False, True)
            return carry

        lax.fori_loop(0, groups, body, 0)
        j0 = groups * ATT_UNROLL
        for rem in range(1, ATT_UNROLL + 1):
            @pl.when(qi - j0 == rem)
            def _(rem=rem):
                run(j0, rem + 1, True, False)


def _own_and_spare(shape, c):
    lane = lax.broadcasted_iota(jnp.int32, shape, 1)
    own = (lane < HEAD_DIM) if c % 2 == 0 else (lane >= HEAD_DIM)
    spare = HEAD_DIM if c % 2 == 0 else 0
    return lane, own, spare


def _split_q(q_ref, qh_scr):
    for c in range(N_CHAINS):
        q = q_ref[c // 2].astype(F32)
        lane, own, spare = _own_and_spare(q.shape, c)
        ones = (lane >= spare) & (lane < spare + N_BIAS_LANES)
        qh_scr[c] = jnp.where(own, q, jnp.where(ones, 1.0, 0.0)).astype(BF16)


def _augment_k(k, bias, c):
    lane, own, spare = _own_and_spare(k.shape, c)
    hi = bias.astype(BF16).astype(F32)
    r1 = bias - hi
    mid = r1.astype(BF16).astype(F32)
    lo = (r1 - mid).astype(BF16).astype(F32)
    extra = jnp.where(lane == spare, hi, jnp.where(lane == spare + 1, mid, jnp.where(lane == spare + 2, lo, 0.0)))
    return jnp.where(own, k.astype(F32), extra).astype(BF16)


def _transpose_bf16(a):
    return a.astype(F32).T.astype(BF16)


def _attn_scratch(tq, seq, acc_rows, vt_slabs):
    return [
        pltpu.VMEM((N_CHAINS, tq, LANES), BF16),
        pltpu.VMEM((N_CHAINS, 1, tq), F32),
        pltpu.VMEM((N_CHAINS, acc_rows, tq), F32),
        pltpu.VMEM((2, N_CHAINS, tq, tq), F32),
        pltpu.VMEM((2, N_CHAINS, 8, tq), F32),
        pltpu.VMEM((N_CHAINS, seq, LANES), BF16),
        pltpu.VMEM((vt_slabs, acc_rows, seq), BF16),
    ]


def _fox_kernel(q_ref, k_ref, vt_ref, c_ref, wg_ref, wu_ref, wd_ref, o_ref, wgb_ref, wub_ref, wdb_ref,
                qh_scr, m_scr, acc_scr, s_scr, pm_scr, ka_scr, vat_scr):
    tq = q_ref.shape[1]
    seq = k_ref.shape[1]
    qi = pl.program_id(2)
    _split_q(q_ref, qh_scr)

    n = (pl.program_id(0) * pl.num_programs(1) + pl.program_id(1)) * pl.num_programs(2) + qi
    for k, (src, dst) in enumerate(((wg_ref, wgb_ref), (wu_ref, wub_ref), (wd_ref, wdb_ref))):
        @pl.when((n >= k * N_EXPERTS) & (n < (k + 1) * N_EXPERTS))
        def _(src=src, dst=dst):
            dst[...] = src[...].astype(BF16)

    @pl.when(qi == 0)
    def _():
        def chunk(i, carry):
            r0 = pl.multiple_of(i * SETUP_ROWS, SETUP_ROWS)
            rows = pl.ds(r0, SETUP_ROWS)
            head_row = lax.broadcasted_iota(jnp.int32, (LANES, SETUP_ROWS), 0) < HEAD_DIM
            for sl in range(NSL):
                vt = vt_ref[sl, :, rows]
                one = jnp.ones_like(vt)
                vat_scr[2 * sl, :, rows] = jnp.where(head_row, vt, one)
                vat_scr[2 * sl + 1, :, rows] = jnp.where(head_row, one, vt)
                cc = c_ref[sl, :, rows]
                stacked = jnp.where(head_row, jnp.broadcast_to(cc[1:2, :], (LANES, SETUP_ROWS)),
                                    jnp.broadcast_to(cc[0:1, :], (LANES, SETUP_ROWS)))
                bias = -stacked.T
                k = k_ref[sl, rows, :]
                ka_scr[2 * sl, rows, :] = _augment_k(k, bias, 0)
                ka_scr[2 * sl + 1, rows, :] = _augment_k(k, bias, 1)
            return carry

        lax.fori_loop(0, seq // SETUP_ROWS, chunk, 0)

    def k_fn(c, start):
        return ka_scr[c, pl.ds(start, tq), :]

    def vt_fn(c, start):
        return vat_scr[c, :, pl.ds(start, tq)]

    _flash_chains(qi, tq, qh_scr, k_fn, vt_fn, m_scr, acc_scr, s_scr, pm_scr)
    for sl in range(NSL):
        a0 = acc_scr[2 * sl]
        a1 = acc_scr[2 * sl + 1]
        row = lax.broadcasted_iota(jnp.int32, a0.shape, 0)
        ot = jnp.where(row < HEAD_DIM, a0 * (1.0 / a0[HEAD_DIM:HEAD_DIM + 1, :]), a1 * (1.0 / a1[0:1, :]))
        o_ref[:, sl * LANES:(sl + 1) * LANES] = ot.T.astype(o_ref.dtype)


def _fox(qkv4, vt4, c4, wg, wu, wd, batch, seq):
    tq = ATT_T
    groups = FOX_HEADS // N_CHAINS
    nq = seq // tq
    assert batch * groups * nq >= 3 * N_EXPERTS

    def w_spec(k, rows, cols):
        def index(b, g, i):
            return (jnp.clip((b * groups + g) * nq + i - k * N_EXPERTS, 0, N_EXPERTS - 1), 0, 0)
        return pl.BlockSpec((None, rows, cols), index)

    w_specs = [w_spec(0, D_MODEL, D_EXPERT), w_spec(1, D_MODEL, D_EXPERT), w_spec(2, D_EXPERT, D_MODEL)]
    return pl.pallas_call(
        _fox_kernel,
        grid=(batch, groups, nq),
        in_specs=[
            pl.BlockSpec((NSL, None, tq, LANES), lambda b, g, i: (SLAB_FOX_Q // NSL + g, b, i, 0)),
            pl.BlockSpec((NSL, None, seq, LANES), lambda b, g, i: (SLAB_FOX_K // NSL + g, b, 0, 0)),
            pl.BlockSpec((NSL, None, LANES, seq), lambda b, g, i: (VT_FOX // NSL + g, b, 0, 0)),
            pl.BlockSpec((NSL, 2, seq), lambda b, g, i: (g, 0, b)),
        ] + w_specs,
        out_specs=[pl.BlockSpec((None, tq, NSL * LANES), lambda b, g, i: (b, i, g))] + w_specs,
        out_shape=[jax.ShapeDtypeStruct((batch, seq, FOX_HEADS * HEAD_DIM), BF16)]
        + [jax.ShapeDtypeStruct(w.shape, BF16) for w in (wg, wu, wd)],
        scratch_shapes=_attn_scratch(tq, seq, LANES, N_CHAINS),
        compiler_params=_cparams(("arbitrary", "arbitrary", "arbitrary"), ATT_FLAGS),
        name="fox",
    )(qkv4, qkv4, vt4, c4, wg, wu, wd)


def _diff_kernel(slopes_ref, q_ref, k_ref, vt_ref, lam_ref, g_ref, wgi_ref, o_ref, wgo_ref, qh_scr, m_scr,
                 acc_scr, s_scr, pm_scr, ka_scr, vat_scr):
    tq = q_ref.shape[1]
    seq = k_ref.shape[1]
    qi = pl.program_id(2)
    _split_q(q_ref, qh_scr)
    g = pl.program_id(1)

    n = (pl.program_id(0) * pl.num_programs(1) + g) * pl.num_programs(2) + qi

    @pl.when(n < GATE_CAST_STEPS)
    def _():
        wgo_ref[...] = wgi_ref[...].astype(BF16)

    @pl.when(qi == 0)
    def _():
        def chunk(i, carry):
            r0 = pl.multiple_of(i * SETUP_ROWS, SETUP_ROWS)
            rows = pl.ds(r0, SETUP_ROWS)
            for hd in range(NSL):
                vt = vt_ref[hd, :, rows]
                vat_scr[hd, :, rows] = jnp.concatenate([vt, jnp.ones((DIFF_ONES_ROWS, SETUP_ROWS), BF16)], axis=0)
                pos = (lax.broadcasted_iota(jnp.int32, (SETUP_ROWS, LANES), 0) + r0).astype(F32)
                bias = pos * (slopes_ref[NSL * g + hd] * LOG2E)
                k = k_ref[hd, rows, :]
                ka_scr[2 * hd, rows, :] = _augment_k(k, bias, 0)
                ka_scr[2 * hd + 1, rows, :] = _augment_k(k, bias, 1)
            return carry

        lax.fori_loop(0, seq // SETUP_ROWS, chunk, 0)

    def k_fn(c, start):
        return ka_scr[c, pl.ds(start, tq), :]

    def vt_fn(c, start):
        return vat_scr[c // 2, :, pl.ds(start, tq)]

    _flash_chains(qi, tq, qh_scr, k_fn, vt_fn, m_scr, acc_scr, s_scr, pm_scr)
    lp = lam_ref[...]
    s1 = jnp.sum(lp[0:1, :] * lp[1:2, :], axis=1, keepdims=True)
    s2 = jnp.sum(lp[2:3, :] * lp[3:4, :], axis=1, keepdims=True)
    lam = jnp.exp(s1) - jnp.exp(s2) + LAM_INIT
    for hd in range(NSL):
        a1 = acc_scr[2 * hd]
        a2 = acc_scr[2 * hd + 1]
        ot = (a1[:LANES, :] * (1.0 / a1[LANES:LANES + 1, :])
              - lam * (a2[:LANES, :] * (1.0 / a2[LANES:LANES + 1, :])))
        ms = jnp.mean(ot * ot, axis=0, keepdims=True)
        yt = ot * lax.rsqrt(ms + LN_EPS) * g_ref[...]
        o_ref[:, hd * LANES:(hd + 1) * LANES] = (yt * (1.0 - LAM_INIT)).T.astype(o_ref.dtype)


def _diff(qkv4, vt4, slopes, lam_params, subln_g_col, w_in_t, batch, seq):
    tq = ATT_T
    groups = DIFF_HEADS // NSL
    nq = seq // tq
    assert batch * groups * nq >= GATE_CAST_STEPS

    def cast_step(b, g, i):
        return jnp.minimum((b * groups + g) * nq + i, GATE_CAST_STEPS - 1)

    grid_spec = pltpu.PrefetchScalarGridSpec(
        num_scalar_prefetch=1,
        grid=(batch, groups, nq),
        in_specs=[
            pl.BlockSpec((NSL, None, tq, LANES), lambda b, g, i, s: (SLAB_DIFF_Q // NSL + g, b, i, 0)),
            pl.BlockSpec((NSL, None, seq, LANES), lambda b, g, i, s: (SLAB_DIFF_K // NSL + g, b, 0, 0)),
            pl.BlockSpec((NSL, None, LANES, seq), lambda b, g, i, s: (VT_DIFF // NSL + g, b, 0, 0)),
            pl.BlockSpec((4, HEAD_DIM), lambda b, g, i, s: (0, 0)),
            pl.BlockSpec((LANES, 1), lambda b, g, i, s: (0, 0)),
            pl.BlockSpec((None, PROJ_TN, D_MODEL),
                         lambda b, g, i, s: (0, GATE_COL0 // PROJ_TN + cast_step(b, g, i), 0)),
        ],
        out_specs=[
            pl.BlockSpec((None, tq, NSL * LANES), lambda b, g, i, s: (b, i, g)),
            pl.BlockSpec((PROJ_TN, D_MODEL), lambda b, g, i, s: (cast_step(b, g, i), 0)),
        ],
        scratch_shapes=_attn_scratch(tq, seq, LANES + DIFF_ONES_ROWS, NSL),
    )
    return pl.pallas_call(
        _diff_kernel,
        grid_spec=grid_spec,
        out_shape=[
            jax.ShapeDtypeStruct((batch, seq, DIFF_HEADS * LANES), BF16),
            jax.ShapeDtypeStruct((N_BRANCH * D_MODEL, D_MODEL), BF16),
        ],
        compiler_params=_cparams(("arbitrary", "arbitrary", "arbitrary"), ATT_FLAGS),
        name="diff",
    )(slopes, qkv4, qkv4, vt4, lam_params, subln_g_col, w_in_t)


def _memkv_kernel(mem_ref, w_ref, o_ref):
    o_ref[...] = jnp.dot(mem_ref[...].astype(BF16), w_ref[...],
                         preferred_element_type=F32).astype(o_ref.dtype)


def _memkv(mem, w_kv):
    batch = mem.shape[0]
    width = w_kv.shape[1]
    return pl.pallas_call(
        _memkv_kernel,
        grid=(batch,),
        in_specs=[
            pl.BlockSpec((None, N_MEM, D_MODEL), lambda b: (b, 0, 0)),
            pl.BlockSpec((D_MODEL, width), lambda b: (0, 0)),
        ],
        out_specs=pl.BlockSpec((None, N_MEM, width), lambda b: (b, 0, 0)),
        out_shape=jax.ShapeDtypeStruct((batch, N_MEM, width), BF16),
        compiler_params=_cparams(("parallel",)),
        name="mem_kv",
    )(mem, w_kv)


def _mem_kernel(q_ref, mk_ref, mv_ref, o_ref):
    s = lax.dot_general(q_ref[...], mk_ref[...], (((1,), (1,)), ((), ())),
                        preferred_element_type=F32)
    m = jnp.max(s, axis=1, keepdims=True)
    p = jnp.exp2(s - m)
    l = jnp.sum(p, axis=1, keepdims=True)
    o = jnp.dot(p.astype(BF16), mv_ref[...], preferred_element_type=F32)
    o_ref[...] = (o / l).astype(o_ref.dtype)


def _mem_attn(qkv4, mkv, batch, seq):
    tq = MEM_TQ
    return pl.pallas_call(
        _mem_kernel,
        grid=(batch, MEM_HEADS, seq // tq),
        in_specs=[
            pl.BlockSpec((None, None, tq, LANES), lambda b, h, i: (SLAB_MEM_Q + h, b, i, 0)),
            pl.BlockSpec((None, N_MEM, LANES), lambda b, h, i: (b, 0, h)),
            pl.BlockSpec((None, N_MEM, LANES), lambda b, h, i: (b, 0, MEM_HEADS + h)),
        ],
        out_specs=pl.BlockSpec((None, tq, LANES), lambda b, h, i: (b, i, h)),
        out_shape=jax.ShapeDtypeStruct((batch, seq, MEM_HEADS * LANES), BF16),
        compiler_params=_cparams(("parallel", "parallel", "parallel")),
        name="mem",
    )(qkv4, mkv, mkv)


def _layer_norm(z, g, b):
    mu = jnp.mean(z, axis=1, keepdims=True)
    zc = z - mu
    var = jnp.mean(zc * zc, axis=1, keepdims=True)
    return zc * lax.rsqrt(var + LN_EPS) * g + b


def _lane_max(v):
    return jnp.max(v, axis=1, keepdims=True)


def _routing_info(logits):
    lane = lax.broadcasted_iota(jnp.int32, logits.shape, 1)
    big = jnp.int32(2 * LANES)
    is_g = lane < N_GROUPS
    gl = jnp.where(is_g, logits, NEG_BIG)
    gmax = _lane_max(gl)
    g_w = 1.0 / jnp.sum(jnp.exp(gl - gmax), axis=1, keepdims=True)
    g_idx = jnp.min(jnp.where(gl == gmax, lane, big), axis=1, keepdims=True)
    lo = ROUTE_LANE0 + g_idx * EXPERTS_PER_GROUP
    in_grp = (lane >= lo) & (lane < lo + EXPERTS_PER_GROUP)
    el = jnp.where(in_grp, logits, NEG_BIG)
    e1 = _lane_max(el)
    i1 = jnp.min(jnp.where(el == e1, lane, big), axis=1, keepdims=True)
    el2 = jnp.where(lane == i1, NEG_BIG, el)
    e2 = _lane_max(el2)
    i2 = jnp.min(jnp.where(el2 == e2, lane, big), axis=1, keepdims=True)
    r = jnp.exp(e2 - e1)
    w1 = g_w / (1.0 + r)
    w2 = g_w * r / (1.0 + r)
    id1 = (i1 - ROUTE_LANE0).astype(F32)
    id2 = (i2 - ROUTE_LANE0).astype(F32)
    return jnp.where(lane == RI_E1, id1,
                     jnp.where(lane == RI_E2, id2,
                               jnp.where(lane == RI_W1, w1, jnp.where(lane == RI_W2, w2, 0.0))))


def _merge_kernel(x_ref, yf_ref, yd_ref, ym_ref, wg_ref, bg_ref, wbf_ref, wbd_ref, wbm_ref,
                  wo_ref, g1_ref, b1_ref, wr_ref, br_ref, x1_ref, ri_ref, rit_ref):
    x = x_ref[...]
    xb = x.astype(BF16)
    h = None
    for i, (y_ref, wb_ref) in enumerate(((yf_ref, wbf_ref), (yd_ref, wbd_ref), (ym_ref, wbm_ref))):
        cols = slice(i * D_MODEL, (i + 1) * D_MODEL)
        gl = lax.dot_general(xb, wg_ref[cols, :], (((1,), (1,)), ((), ())),
                             preferred_element_type=F32) + bg_ref[:, cols]
        gate = 1.0 / (1.0 + jnp.exp(-gl))
        br = jnp.dot(y_ref[...], wb_ref[...], preferred_element_type=F32)
        h = gate * br if h is None else h + gate * br
    o = jnp.dot(h.astype(BF16), wo_ref[...], preferred_element_type=F32)
    x1 = _layer_norm(DEEPNORM_ALPHA * x + o, g1_ref[...], b1_ref[...])
    x1_ref[...] = x1
    logits = jnp.dot(x1.astype(BF16), wr_ref[...], preferred_element_type=F32) + br_ref[...]
    ri = _routing_info(logits)
    ri_ref[...] = ri
    rit_ref[...] = ri.T[0:8, :]


def _const_spec(shape):
    return pl.BlockSpec(shape, lambda i: (0,) * len(shape), pipeline_mode=pl.Buffered(1))


def _merge(x2, yf, yd, ym, wg, bg, wbf, wbd, wbm, wo, g1, b1, wr, br):
    t = x2.shape[0]
    tm = MERGE_TM
    half = yf.shape[1]
    row = lambda w: pl.BlockSpec((tm, w), lambda i: (i, 0))
    return pl.pallas_call(
        _merge_kernel,
        grid=(t // tm,),
        in_specs=[
            row(D_MODEL), row(half), row(half), row(half),
            _const_spec((N_BRANCH * D_MODEL, D_MODEL)), _const_spec((1, N_BRANCH * D_MODEL)),
            _const_spec((half, D_MODEL)), _const_spec((half, D_MODEL)), _const_spec((half, D_MODEL)),
            _const_spec((D_MODEL, D_MODEL)), _const_spec((1, D_MODEL)), _const_spec((1, D_MODEL)),
            _const_spec((D_MODEL, LANES)), _const_spec((1, LANES)),
        ],
        out_specs=[row(D_MODEL), row(LANES), pl.BlockSpec((8, tm), lambda i: (0, i))],
        out_shape=[
            jax.ShapeDtypeStruct((t, D_MODEL), F32),
            jax.ShapeDtypeStruct((t, LANES), F32),
            jax.ShapeDtypeStruct((8, t), F32),
        ],
        compiler_params=_cparams(("parallel",)),
        name="merge",
    )(x2, yf, yd, ym, wg, bg, wbf, wbd, wbm, wo, g1, b1, wr, br)


def _cumsum(x, axis):
    n = x.shape[axis]
    idx = lax.broadcasted_iota(jnp.int32, x.shape, axis)
    sh = 1
    while sh < n:
        x = x + jnp.where(idx >= sh, pltpu.roll(x, sh, axis=axis), 0.0)
        sh *= 2
    return x


def _route_kernel(rit_ref, pos_ref, tiles_ref):
    s = rit_ref.shape[1]
    e1 = rit_ref[RI_E1:RI_E1 + 1, :]
    e2 = rit_ref[RI_E2:RI_E2 + 1, :]
    sub = lax.broadcasted_iota(jnp.int32, (N_EXPERTS, s), 0).astype(F32)
    oh1 = jnp.where(sub == e1, 1.0, 0.0)
    oh2 = jnp.where(sub == e2, 1.0, 0.0)
    cnt = oh1 + oh2
    incl = _cumsum(cnt, 1)
    before = incl - cnt
    total = jnp.broadcast_to(incl[:, s - 1:s], (N_EXPERTS, LANES))
    aligned = jnp.floor((total + (SEG_ALIGN - 1)) * (1.0 / SEG_ALIGN)) * SEG_ALIGN
    off = _cumsum(aligned, 0) - aligned
    slot = before + off[:, 0:1]
    pos1 = jnp.sum(oh1 * slot, axis=0, keepdims=True)
    pos2 = jnp.sum(oh2 * slot, axis=0, keepdims=True)
    pos_ref[...] = (jnp.concatenate([pos1, pos2], axis=0) * ROW_TILE).astype(jnp.int32)

    nt = jnp.floor((total + (MOE_R - 1)) * (1.0 / MOE_R))
    ct_incl = _cumsum(nt, 0)
    ct_excl = ct_incl - nt
    sub_l = lax.broadcasted_iota(jnp.int32, (N_EXPERTS, LANES), 0).astype(F32)
    tile_i = lax.broadcasted_iota(jnp.int32, (N_EXPERTS, LANES), 1).astype(F32)
    te = jnp.sum(jnp.where(ct_incl <= tile_i, 1.0, 0.0), axis=0, keepdims=True)
    te = jnp.minimum(te, N_EXPERTS - 1.0)
    sel = jnp.where(sub_l == te, 1.0, 0.0)
    k_in = tile_i - ct_excl
    start = jnp.sum(sel * (off + k_in * MOE_R), axis=0, keepdims=True)
    nval = jnp.sum(sel * jnp.clip(total - k_in * MOE_R, 0.0, float(MOE_R)), axis=0, keepdims=True)
    active = tile_i[0:1, :] < ct_incl[N_EXPERTS - 1:N_EXPERTS, :]
    start = jnp.where(active, start, 0.0)
    nval = jnp.where(active, nval, 0.0)
    rows = jnp.concatenate([te, start * ROW_TILE, nval, jnp.zeros((5, LANES), F32)], axis=0)
    tiles_ref[...] = rows.astype(jnp.int32)


def _route(rit, batch, seq):
    return pl.pallas_call(
        _route_kernel,
        grid=(batch,),
        in_specs=[pl.BlockSpec((8, seq), lambda b: (0, b))],
        out_specs=[
            pl.BlockSpec((None, 2, seq), lambda b: (b, 0, 0)),
            pl.BlockSpec((None, 8, LANES), lambda b: (b, 0, 0)),
        ],
        out_shape=[
            jax.ShapeDtypeStruct((batch, 2, seq), jnp.int32),
            jax.ShapeDtypeStruct((batch, 8, LANES), jnp.int32),
        ],
        compiler_params=_cparams(("parallel",)),
        name="route",
    )(rit)


def _read_rows(view, n_rows):
    return jnp.concatenate([view[pl.ds(k, n_rows, stride=ROW_TILE), :] for k in range(ROW_TILE)], axis=1)


def _write_rows(view, n_rows, val):
    for k in range(ROW_TILE):
        view[pl.ds(k, n_rows, stride=ROW_TILE), :] = val[:, k * LANES:(k + 1) * LANES]


def _moe_kernel(nd, nt, texp_ref, tstart_ref, tnval_ref, x1_ref, p1_ref, p2_ref, ri_ref,
                wg_ref, wu_ref, wd_ref, g2_ref, b2_ref, o_ref, buf, xs_scr, as_scr, bs_scr):
    c = pl.program_id(0)
    j = pl.program_id(1)
    td = x1_ref.shape[0]

    def tile_at(ref, off):
        return ref.at[pl.ds(pl.multiple_of(off, ROW_TILE), ROW_TILE), :]

    @pl.when(j == 0)
    def _():
        buf[...] = jnp.zeros(buf.shape, F32)

    @pl.when(j < nd)
    def _():
        _write_rows(xs_scr, td, x1_ref[...])

        def body(i, carry):
            for u in range(SEG_ALIGN):
                t = i * SEG_ALIGN + u
                row = tile_at(xs_scr, t * ROW_TILE)[...]
                tile_at(buf, p1_ref[0, t])[...] = row
                tile_at(buf, p2_ref[0, t])[...] = row
            return carry

        lax.fori_loop(0, td // SEG_ALIGN, body, 0)

    @pl.when((j >= nd) & (j < nd + nt))
    def _():
        idx = c * nt + (j - nd)
        nval = tnval_ref[idx]

        @pl.when(nval > 0)
        def _():
            start = pl.multiple_of(tstart_ref[idx], ROW_TILE * SEG_ALIGN)
            view = buf.at[pl.ds(start, MOE_R * ROW_TILE), :]
            xt = _read_rows(view, MOE_R)
            xb = xt.astype(BF16)
            g = jnp.dot(xb, wg_ref[...], preferred_element_type=F32)
            u = jnp.dot(xb, wu_ref[...], preferred_element_type=F32)
            hcat = (g / (1.0 + jnp.exp(-g))) * u
            y = jnp.dot(hcat.astype(BF16), wd_ref[...], preferred_element_type=F32)
            rows = lax.broadcasted_iota(jnp.int32, (MOE_R, 1), 0)
            _write_rows(view, MOE_R, jnp.where(rows < nval, y, xt))

    @pl.when(j >= nd + nt)
    def _():
        def body(i, carry):
            for u in range(SEG_ALIGN):
                t = i * SEG_ALIGN + u
                tile_at(as_scr, t * ROW_TILE)[...] = tile_at(buf, p1_ref[0, t])[...]
                tile_at(bs_scr, t * ROW_TILE)[...] = tile_at(buf, p2_ref[0, t])[...]
            return carry

        lax.fori_loop(0, td // SEG_ALIGN, body, 0)
        ri = ri_ref[...]
        m = ri[:, RI_W1:RI_W1 + 1] * _read_rows(as_scr, td) + ri[:, RI_W2:RI_W2 + 1] * _read_rows(bs_scr, td)
        z = DEEPNORM_ALPHA * x1_ref[...] + m
        o_ref[...] = _layer_norm(z, g2_ref[...], b2_ref[...])


def _moe(x1, pos3, ri, texp, tstart, tnval, wg, wu, wd, g2, b2, batch, seq):
    t = x1.shape[0]
    td = MOE_TD
    nd = seq // td
    nt = 2 * seq // MOE_R + N_EXPERTS
    buf_rows = 2 * seq + N_EXPERTS * SEG_ALIGN + MOE_R

    def tok_tile(j):
        return jnp.where(j < nd, j, jnp.maximum(j - nd - nt, 0))

    def tile_expert(c, j, texp_ref):
        return texp_ref[c * nt + jnp.clip(j - nd, 0, nt - 1)]

    tok_spec = lambda w: pl.BlockSpec((td, w), lambda c, j, *_: (c * nd + tok_tile(j), 0))
    pos_spec = lambda k: pl.BlockSpec((None, 1, td), lambda c, j, *_: ((c * 2 + k) * nd + tok_tile(j), 0, 0),
                                      memory_space=pltpu.SMEM)
    w_spec = lambda a, b_: pl.BlockSpec((None, a, b_), lambda c, j, te, *_: (tile_expert(c, j, te), 0, 0))
    vec_spec = pl.BlockSpec((1, D_MODEL), lambda c, j, *_: (0, 0))
    grid_spec = pltpu.PrefetchScalarGridSpec(
        num_scalar_prefetch=3,
        grid=(batch, nd + nt + nd),
        in_specs=[
            tok_spec(D_MODEL), pos_spec(0), pos_spec(1), tok_spec(LANES),
            w_spec(D_MODEL, D_EXPERT), w_spec(D_MODEL, D_EXPERT), w_spec(D_EXPERT, D_MODEL),
            vec_spec, vec_spec,
        ],
        out_specs=pl.BlockSpec((td, D_MODEL),
                               lambda c, j, *_: (c * nd + jnp.maximum(j - nd - nt, 0), 0)),
        scratch_shapes=[
            pltpu.VMEM((buf_rows * ROW_TILE, LANES), F32),
            pltpu.VMEM((td * ROW_TILE, LANES), F32),
            pltpu.VMEM((td * ROW_TILE, LANES), F32),
            pltpu.VMEM((td * ROW_TILE, LANES), F32),
        ],
    )
    return pl.pallas_call(
        functools.partial(_moe_kernel, nd, nt),
        grid_spec=grid_spec,
        out_shape=jax.ShapeDtypeStruct((t, D_MODEL), F32),
        compiler_params=_cparams(("arbitrary", "arbitrary")),
        name="moe",
    )(texp, tstart, tnval, x1, pos3, pos3, ri, wg, wu, wd, g2, b2)


def _qkv_col_scale():
    ones = lambda n: jnp.ones((n,), F32)
    att = FOX_HEADS * HEAD_DIM
    qs = lambda d: jnp.full((att,), LOG2E * d ** -0.5, F32)
    return jnp.concatenate([qs(HEAD_DIM), ones(2 * att), qs(HEAD_DIM), ones(2 * att),
                            qs(MEM_HEAD_DIM)]).reshape(1, QKV_COLS)


def kernel(x, mem, w_in, b_forget, b_gates, lambda_q1, lambda_k1, lambda_q2, lambda_k2, diff_subln_g,
           w_mem_kv, w_branch_fox, w_branch_diff, w_branch_mem, w_out, ln1_g, ln1_b, w_router_group,
           b_router_group, w_router_expert, b_router_expert, w_expert_gate, w_expert_up, w_expert_down,
           ln2_g, ln2_b):
    batch, seq, d = x.shape
    t = batch * seq
    l = 0
    x2 = x.reshape(t, d)

    w_in_t = jnp.swapaxes(w_in, 1, 2)
    w_fl_t = jnp.pad(w_in_t[l, FL_COL0:, :], ((0, 16 - FOX_HEADS), (0, 0))).astype(BF16)
    w_r = jnp.concatenate([w_router_group[l], w_router_expert[l]], axis=1)
    w_r = jnp.pad(w_r, ((0, 0), (0, LANES - w_r.shape[1]))).astype(BF16)
    b_r = jnp.concatenate([b_router_group[l], b_router_expert[l]])
    b_r = jnp.pad(b_r, (0, LANES - b_r.shape[0])).reshape(1, LANES)
    lam_params = jnp.stack([lambda_q1[l], lambda_k1[l], lambda_q2[l], lambda_k2[l]])
    slopes = 2.0 ** (-8.0 * jnp.arange(1, DIFF_HEADS + 1, dtype=F32) / DIFF_HEADS)

    qkv, vt4, fl_t = _proj(x2, w_in_t, w_fl_t, _qkv_col_scale(), batch, seq)
    qkv4 = qkv.reshape(N_SLABS, batch, seq, LANES)
    c = _fscan(fl_t, b_forget[l].reshape(FOX_HEADS, 1), batch, seq)
    c4 = c.reshape(FOX_HEADS // 2, 2, t)

    y_fox, wg_b, wu_b, wd_b = _fox(qkv4, vt4, c4, w_expert_gate[l], w_expert_up[l], w_expert_down[l],
                                   batch, seq)
    y_diff, w_gates = _diff(qkv4, vt4, slopes, lam_params, diff_subln_g[l].reshape(LANES, 1), w_in_t,
                            batch, seq)
    mkv = _memkv(mem, w_mem_kv[l].astype(BF16))
    y_mem = _mem_attn(qkv4, mkv, batch, seq)

    x1, ri, rit = _merge(
        x2, y_fox.reshape(t, -1), y_diff.reshape(t, -1), y_mem.reshape(t, -1),
        w_gates, b_gates[l].reshape(1, -1),
        w_branch_fox[l].astype(BF16), w_branch_diff[l].astype(BF16), w_branch_mem[l].astype(BF16),
        w_out[l].astype(BF16), ln1_g[l].reshape(1, d), ln1_b[l].reshape(1, d), w_r, b_r)

    pos, tiles = _route(rit, batch, seq)
    nt = 2 * seq // MOE_R + N_EXPERTS
    texp, tstart, tnval = (tiles[:, r, :nt].reshape(-1) for r in range(3))
    pos3 = pos.reshape(batch * 2 * (seq // MOE_TD), 1, MOE_TD)
    out = _moe(x1, pos3, ri, texp, tstart, tnval,
               wg_b, wu_b, wd_b,
               ln2_g[l].reshape(1, d), ln2_b[l].reshape(1, d), batch, seq)
    return out.reshape(batch, seq, d)
```

```python
import functools
import math

import jax
import jax.numpy as jnp
from jax import lax
from jax.experimental import pallas as pl
from jax.experimental.pallas import tpu as pltpu

F32 = jnp.float32
BF16 = jnp.bfloat16

D_MODEL = 1024
HEAD_DIM = 64
FOX_HEADS = 8
DIFF_HEADS = 4
MEM_HEADS = 4
MEM_HEAD_DIM = 128
N_MEM = 256
N_BRANCH = 3
N_GROUPS = 4
EXPERTS_PER_GROUP = 8
N_EXPERTS = N_GROUPS * EXPERTS_PER_GROUP
D_EXPERT = 256
LN_EPS = 1e-5
DEPTH = 1
DEEPNORM_ALPHA = (2.0 * DEPTH) ** 0.25
LAM_INIT = 0.8 - 0.6 * math.exp(-0.3 * 0)
LOG2E = math.log2(math.e)

LANES = 128
QKV_COLS = 3584
N_SLABS = QKV_COLS // LANES
GATE_COL0 = QKV_COLS
FL_COL0 = QKV_COLS + N_BRANCH * D_MODEL
SLAB_FOX_Q, SLAB_FOX_K, SLAB_FOX_V = 0, 4, 8
SLAB_DIFF_Q, SLAB_DIFF_K, SLAB_DIFF_V = 12, 16, 20
SLAB_MEM_Q = 24
ROUTE_LANE0 = N_GROUPS

NEG_BIG = -1e30
VMEM_LIMIT = 56 * 1024 * 1024

PROJ_TM = 2048
PROJ_TN = 512
PROJ_J_FOX_V = SLAB_FOX_V * LANES // PROJ_TN
PROJ_J_DIFF_V = SLAB_DIFF_V * LANES // PROJ_TN
VT_FOX, VT_DIFF = 0, PROJ_TN // LANES
GATE_CAST_STEPS = N_BRANCH * D_MODEL // PROJ_TN
ATT_T = 256
ATT_UNROLL = 8
N_CHAINS = 4
NSL = N_CHAINS // 2
N_BIAS_LANES = 3
DIFF_ONES_ROWS = 16
SETUP_ROWS = 512
MEM_TQ = 4096
MERGE_TM = 1024
MOE_R = 256
MOE_TD = 256
SEG_ALIGN = 8
ROW_TILE = D_MODEL // LANES
RI_E1, RI_E2, RI_W1, RI_W2 = 0, 1, 2, 3


def _cparams(sem, flags=None):
    return pltpu.CompilerParams(dimension_semantics=sem, vmem_limit_bytes=VMEM_LIMIT, flags=flags)


ATT_FLAGS = None


def _proj_kernel(x_ref, w_ref, wfl_ref, sc_ref, o_ref, vt_ref, fl_ref, xb_ref):
    j = pl.program_id(1)
    is_v = (j == PROJ_J_FOX_V) | (j == PROJ_J_DIFF_V)
    spc = PROJ_TN // LANES

    @pl.when(j == 0)
    def _():
        xb = x_ref[...].astype(BF16)
        xb_ref[...] = xb
        fl_ref[...] = lax.dot_general(wfl_ref[...], xb, (((1,), (1,)), ((), ())),
                                      preferred_element_type=F32)

    @pl.when(jnp.logical_not(is_v))
    def _():
        acc = lax.dot_general(xb_ref[...], w_ref[...].astype(BF16), (((1,), (1,)), ((), ())),
                              preferred_element_type=F32) * sc_ref[...]
        for c in range(spc):
            o_ref[c] = acc[:, c * LANES:(c + 1) * LANES].astype(BF16)

    @pl.when(is_v)
    def _():
        acc_t = lax.dot_general(w_ref[...].astype(BF16), xb_ref[...], (((1,), (1,)), ((), ())),
                                preferred_element_type=F32)
        for c in range(spc):
            vt_ref[c] = acc_t[c * LANES:(c + 1) * LANES, :].astype(BF16)
        o_ref[...] = jnp.zeros(o_ref.shape, BF16)


def _proj(x2, w_in_t, w_fl_t, col_scale, batch, seq):
    t = x2.shape[0]
    spc = PROJ_TN // LANES
    per_seq = seq // PROJ_TM
    return pl.pallas_call(
        _proj_kernel,
        grid=(t // PROJ_TM, QKV_COLS // PROJ_TN),
        in_specs=[
            pl.BlockSpec((PROJ_TM, D_MODEL), lambda i, j: (i, 0)),
            pl.BlockSpec((None, PROJ_TN, D_MODEL), lambda i, j: (0, j, 0)),
            pl.BlockSpec((16, D_MODEL), lambda i, j: (0, 0)),
            pl.BlockSpec((1, PROJ_TN), lambda i, j: (0, j)),
        ],
        out_specs=[
            pl.BlockSpec((spc, PROJ_TM, LANES), lambda i, j: (j, i, 0)),
            pl.BlockSpec((spc, None, LANES, PROJ_TM),
                         lambda i, j: (jnp.where(j < PROJ_J_DIFF_V, 0, 1), i // per_seq, 0, i % per_seq)),
            pl.BlockSpec((16, PROJ_TM), lambda i, j: (0, i)),
        ],
        out_shape=[
            jax.ShapeDtypeStruct((N_SLABS, t, LANES), BF16),
            jax.ShapeDtypeStruct((2 * spc, batch, LANES, seq), BF16),
            jax.ShapeDtypeStruct((16, t), F32),
        ],
        scratch_shapes=[pltpu.VMEM((PROJ_TM, D_MODEL), BF16)],
        compiler_params=_cparams(("parallel", "arbitrary")),
        name="proj",
    )(x2, w_in_t, w_fl_t, col_scale)


def _fscan_kernel(fl_ref, b_ref, c_ref):
    s = fl_ref.shape[1]
    z = fl_ref[0:FOX_HEADS, :] + b_ref[...]
    lf = jnp.minimum(z, 0.0) - jnp.log(1.0 + jnp.exp(-jnp.abs(z)))
    lane = lax.broadcasted_iota(jnp.int32, lf.shape, 1)
    sh = 1
    while sh < s:
        r = pltpu.roll(lf, sh, axis=1)
        lf = lf + jnp.where(lane >= sh, r, 0.0)
        sh *= 2
    c_ref[...] = lf * LOG2E


def _fscan(fl_t, b_forget_col, batch, seq):
    t = fl_t.shape[1]
    return pl.pallas_call(
        _fscan_kernel,
        grid=(batch,),
        in_specs=[
            pl.BlockSpec((16, seq), lambda b: (0, b)),
            pl.BlockSpec((FOX_HEADS, 1), lambda b: (0, 0)),
        ],
        out_specs=pl.BlockSpec((FOX_HEADS, seq), lambda b: (0, b)),
        out_shape=jax.ShapeDtypeStruct((FOX_HEADS, t), F32),
        compiler_params=_cparams(("parallel",)),
        name="fscan",
    )(fl_t, b_forget_col)


def _flash_chains(qi, tq, qh_scr, k_fn, vt_fn, m_scr, acc_scr, s_scr, pm_scr):
    m_scr[...] = jnp.full(m_scr.shape, NEG_BIG, F32)
    acc_scr[...] = jnp.zeros(acc_scr.shape, F32)

    def scores(j, slot, masked):
        start = pl.multiple_of(j * tq, tq)
        for c in range(N_CHAINS):
            s = lax.dot_general(k_fn(c, start), qh_scr[c], (((1,), (1,)), ((), ())),
                                preferred_element_type=F32)
            if masked:
                kv = lax.broadcasted_iota(jnp.int32, (tq, tq), 0)
                q = lax.broadcasted_iota(jnp.int32, (tq, tq), 1)
                s = jnp.where(kv <= q, s, NEG_BIG)
            s_scr[slot, c] = s
            pm_scr[slot, c] = jnp.max(s.reshape(tq // 8, 8, tq), axis=0)

    def consume(j, slot):
        start = pl.multiple_of(j * tq, tq)
        for c in range(N_CHAINS):
            m_prev = m_scr[c]
            m_new = jnp.maximum(m_prev, jnp.max(pm_scr[slot, c], axis=0, keepdims=True))
            alpha = jnp.exp2(m_prev - m_new)
            p = jnp.exp2(s_scr[slot, c] - m_new)
            m_scr[c] = m_new
            pv = jnp.dot(vt_fn(c, start), p.astype(BF16), preferred_element_type=F32)
            acc_scr[c] = acc_scr[c] * alpha + pv

    @pl.when(qi == 0)
    def _():
        scores(0, 0, True)
        consume(0, 0)

    def run(j0, count, last_masked, issue_beyond):
        for t in range(count):
            if t + 1 < count or issue_beyond:
                scores(j0 + t + 1, (t + 1) % 2, last_masked and t + 1 == count - 1)
            consume(j0 + t, t % 2)

    @pl.when(qi > 0)
    def _():
        scores(0, 0, False)
        groups = (qi - 1) // ATT_UNROLL

        def body(g, carry):
            run(g * ATT_UNROLL, ATT_UNROLL, False, True)
            return carry

        lax.fori_loop(0, groups, body, 0)
        j0 = groups * ATT_UNROLL
        for rem in range(1, ATT_UNROLL + 1):
            @pl.when(qi - j0 == rem)
            def _(rem=rem):
                run(j0, rem + 1, True, False)


def _own_and_spare(shape, c):
    lane = lax.broadcasted_iota(jnp.int32, shape, 1)
    own = (lane < HEAD_DIM) if c % 2 == 0 else (lane >= HEAD_DIM)
    spare = HEAD_DIM if c % 2 == 0 else 0
    return lane, own, spare


def _split_q(q_ref, qh_scr):
    for c in range(N_CHAINS):
        q = q_ref[c // 2].astype(F32)
        lane, own, spare = _own_and_spare(q.shape, c)
        ones = (lane >= spare) & (lane < spare + N_BIAS_LANES)
        qh_scr[c] = jnp.where(own, q, jnp.where(ones, 1.0, 0.0)).astype(BF16)


def _augment_k(k, bias, c):
    lane, own, spare = _own_and_spare(k.shape, c)
    hi = bias.astype(BF16).astype(F32)
    r1 = bias - hi
    mid = r1.astype(BF16).astype(F32)
    lo = (r1 - mid).astype(BF16).astype(F32)
    extra = jnp.where(lane == spare, hi, jnp.where(lane == spare + 1, mid, jnp.where(lane == spare + 2, lo, 0.0)))
    return jnp.where(own, k.astype(F32), extra).astype(BF16)


def _transpose_bf16(a):
    return a.astype(F32).T.astype(BF16)


def _attn_scratch(tq, seq, acc_rows, vt_slabs):
    return [
        pltpu.VMEM((N_CHAINS, tq, LANES), BF16),
        pltpu.VMEM((N_CHAINS, 1, tq), F32),
        pltpu.VMEM((N_CHAINS, acc_rows, tq), F32),
        pltpu.VMEM((2, N_CHAINS, tq, tq), F32),
        pltpu.VMEM((2, N_CHAINS, 8, tq), F32),
        pltpu.VMEM((N_CHAINS, seq, LANES), BF16),
        pltpu.VMEM((vt_slabs, acc_rows, seq), BF16),
    ]


def _fox_kernel(q_ref, k_ref, vt_ref, c_ref, wg_ref, wu_ref, wd_ref, o_ref, wgb_ref, wub_ref, wdb_ref,
                qh_scr, m_scr, acc_scr, s_scr, pm_scr, ka_scr, vat_scr):
    tq = q_ref.shape[1]
    seq = k_ref.shape[1]
    qi = pl.program_id(2)
    _split_q(q_ref, qh_scr)

    n = (pl.program_id(0) * pl.num_programs(1) + pl.program_id(1)) * pl.num_programs(2) + qi
    for k, (src, dst) in enumerate(((wg_ref, wgb_ref), (wu_ref, wub_ref), (wd_ref, wdb_ref))):
        @pl.when((n >= k * N_EXPERTS) & (n < (k + 1) * N_EXPERTS))
        def _(src=src, dst=dst):
            dst[...] = src[...].astype(BF16)

    @pl.when(qi == 0)
    def _():
        def chunk(i, carry):
            r0 = pl.multiple_of(i * SETUP_ROWS, SETUP_ROWS)
            rows = pl.ds(r0, SETUP_ROWS)
            head_row = lax.broadcasted_iota(jnp.int32, (LANES, SETUP_ROWS), 0) < HEAD_DIM
            for sl in range(NSL):
                vt = vt_ref[sl, :, rows]
                one = jnp.ones_like(vt)
                vat_scr[2 * sl, :, rows] = jnp.where(head_row, vt, one)
                vat_scr[2 * sl + 1, :, rows] = jnp.where(head_row, one, vt)
                cc = c_ref[sl, :, rows]
                stacked = jnp.where(head_row, jnp.broadcast_to(cc[1:2, :], (LANES, SETUP_ROWS)),
                                    jnp.broadcast_to(cc[0:1, :], (LANES, SETUP_ROWS)))
                bias = -stacked.T
                k = k_ref[sl, rows, :]
                ka_scr[2 * sl, rows, :] = _augment_k(k, bias, 0)
                ka_scr[2 * sl + 1, rows, :] = _augment_k(k, bias, 1)
            return carry

        lax.fori_loop(0, seq // SETUP_ROWS, chunk, 0)

    def k_fn(c, start):
        return ka_scr[c, pl.ds(start, tq), :]

    def vt_fn(c, start):
        return vat_scr[c, :, pl.ds(start, tq)]

    _flash_chains(qi, tq, qh_scr, k_fn, vt_fn, m_scr, acc_scr, s_scr, pm_scr)
    for sl in range(NSL):
        a0 = acc_scr[2 * sl]
        a1 = acc_scr[2 * sl + 1]
        row = lax.broadcasted_iota(jnp.int32, a0.shape, 0)
        ot = jnp.where(row < HEAD_DIM, a0 * (1.0 / a0[HEAD_DIM:HEAD_DIM + 1, :]), a1 * (1.0 / a1[0:1, :]))
        o_ref[:, sl * LANES:(sl + 1) * LANES] = ot.T.astype(o_ref.dtype)


def _fox(qkv4, vt4, c4, wg, wu, wd, batch, seq):
    tq = ATT_T
    groups = FOX_HEADS // N_CHAINS
    nq = seq // tq
    assert batch * groups * nq >= 3 * N_EXPERTS

    def w_spec(k, rows, cols):
        def index(b, g, i):
            return (jnp.clip((b * groups + g) * nq + i - k * N_EXPERTS, 0, N_EXPERTS - 1), 0, 0)
        return pl.BlockSpec((None, rows, cols), index)

    w_specs = [w_spec(0, D_MODEL, D_EXPERT), w_spec(1, D_MODEL, D_EXPERT), w_spec(2, D_EXPERT, D_MODEL)]
    return pl.pallas_call(
        _fox_kernel,
        grid=(batch, groups, nq),
        in_specs=[
            pl.BlockSpec((NSL, None, tq, LANES), lambda b, g, i: (SLAB_FOX_Q // NSL + g, b, i, 0)),
            pl.BlockSpec((NSL, None, seq, LANES), lambda b, g, i: (SLAB_FOX_K // NSL + g, b, 0, 0)),
            pl.BlockSpec((NSL, None, LANES, seq), lambda b, g, i: (VT_FOX // NSL + g, b, 0, 0)),
            pl.BlockSpec((NSL, 2, seq), lambda b, g, i: (g, 0, b)),
        ] + w_specs,
        out_specs=[pl.BlockSpec((None, tq, NSL * LANES), lambda b, g, i: (b, i, g))] + w_specs,
        out_shape=[jax.ShapeDtypeStruct((batch, seq, FOX_HEADS * HEAD_DIM), BF16)]
        + [jax.ShapeDtypeStruct(w.shape, BF16) for w in (wg, wu, wd)],
        scratch_shapes=_attn_scratch(tq, seq, LANES, N_CHAINS),
        compiler_params=_cparams(("arbitrary", "arbitrary", "arbitrary"), ATT_FLAGS),
        name="fox",
    )(qkv4, qkv4, vt4, c4, wg, wu, wd)


def _diff_kernel(slopes_ref, q_ref, k_ref, vt_ref, lam_ref, g_ref, wgi_ref, o_ref, wgo_ref, qh_scr, m_scr,
                 acc_scr, s_scr, pm_scr, ka_scr, vat_scr):
    tq = q_ref.shape[1]
    seq = k_ref.shape[1]
    qi = pl.program_id(2)
    _split_q(q_ref, qh_scr)
    g = pl.program_id(1)

    n = (pl.program_id(0) * pl.num_programs(1) + g) * pl.num_programs(2) + qi

    @pl.when(n < GATE_CAST_STEPS)
    def _():
        wgo_ref[...] = wgi_ref[...].astype(BF16)

    @pl.when(qi == 0)
    def _():
        def chunk(i, carry):
            r0 = pl.multiple_of(i * SETUP_ROWS, SETUP_ROWS)
            rows = pl.ds(r0, SETUP_ROWS)
            for hd in range(NSL):
                vt = vt_ref[hd, :, rows]
                vat_scr[hd, :, rows] = jnp.concatenate([vt, jnp.ones((DIFF_ONES_ROWS, SETUP_ROWS), BF16)], axis=0)
                pos = (lax.broadcasted_iota(jnp.int32, (SETUP_ROWS, LANES), 0) + r0).astype(F32)
                bias = pos * (slopes_ref[NSL * g + hd] * LOG2E)
                k = k_ref[hd, rows, :]
                ka_scr[2 * hd, rows, :] = _augment_k(k, bias, 0)
                ka_scr[2 * hd + 1, rows, :] = _augment_k(k, bias, 1)
            return carry

        lax.fori_loop(0, seq // SETUP_ROWS, chunk, 0)

    def k_fn(c, start):
        return ka_scr[c, pl.ds(start, tq), :]

    def vt_fn(c, start):
        return vat_scr[c // 2, :, pl.ds(start, tq)]

    _flash_chains(qi, tq, qh_scr, k_fn, vt_fn, m_scr, acc_scr, s_scr, pm_scr)
    lp = lam_ref[...]
    s1 = jnp.sum(lp[0:1, :] * lp[1:2, :], axis=1, keepdims=True)
    s2 = jnp.sum(lp[2:3, :] * lp[3:4, :], axis=1, keepdims=True)
    lam = jnp.exp(s1) - jnp.exp(s2) + LAM_INIT
    for hd in range(NSL):
        a1 = acc_scr[2 * hd]
        a2 = acc_scr[2 * hd + 1]
        ot = (a1[:LANES, :] * (1.0 / a1[LANES:LANES + 1, :])
              - lam * (a2[:LANES, :] * (1.0 / a2[LANES:LANES + 1, :])))
        ms = jnp.mean(ot * ot, axis=0, keepdims=True)
        yt = ot * lax.rsqrt(ms + LN_EPS) * g_ref[...]
        o_ref[:, hd * LANES:(hd + 1) * LANES] = (yt * (1.0 - LAM_INIT)).T.astype(o_ref.dtype)


def _diff(qkv4, vt4, slopes, lam_params, subln_g_col, w_in_t, batch, seq):
    tq = ATT_T
    groups = DIFF_HEADS // NSL
    nq = seq // tq
    assert batch * groups * nq >= GATE_CAST_STEPS

    def cast_step(b, g, i):
        return jnp.minimum((b * groups + g) * nq + i, GATE_CAST_STEPS - 1)

    grid_spec = pltpu.PrefetchScalarGridSpec(
        num_scalar_prefetch=1,
        grid=(batch, groups, nq),
        in_specs=[
            pl.BlockSpec((NSL, None, tq, LANES), lambda b, g, i, s: (SLAB_DIFF_Q // NSL + g, b, i, 0)),
            pl.BlockSpec((NSL, None, seq, LANES), lambda b, g, i, s: (SLAB_DIFF_K // NSL + g, b, 0, 0)),
            pl.BlockSpec((NSL, None, LANES, seq), lambda b, g, i, s: (VT_DIFF // NSL + g, b, 0, 0)),
            pl.BlockSpec((4, HEAD_DIM), lambda b, g, i, s: (0, 0)),
            pl.BlockSpec((LANES, 1), lambda b, g, i, s: (0, 0)),
            pl.BlockSpec((None, PROJ_TN, D_MODEL),
                         lambda b, g, i, s: (0, GATE_COL0 // PROJ_TN + cast_step(b, g, i), 0)),
        ],
        out_specs=[
            pl.BlockSpec((None, tq, NSL * LANES), lambda b, g, i, s: (b, i, g)),
            pl.BlockSpec((PROJ_TN, D_MODEL), lambda b, g, i, s: (cast_step(b, g, i), 0)),
        ],
        scratch_shapes=_attn_scratch(tq, seq, LANES + DIFF_ONES_ROWS, NSL),
    )
    return pl.pallas_call(
        _diff_kernel,
        grid_spec=grid_spec,
        out_shape=[
            jax.ShapeDtypeStruct((batch, seq, DIFF_HEADS * LANES), BF16),
            jax.ShapeDtypeStruct((N_BRANCH * D_MODEL, D_MODEL), BF16),
        ],
        compiler_params=_cparams(("arbitrary", "arbitrary", "arbitrary"), ATT_FLAGS),
        name="diff",
    )(slopes, qkv4, qkv4, vt4, lam_params, subln_g_col, w_in_t)


def _mem_kernel(q_ref, mem_ref, wk_ref, wv_ref, o_ref):
    memb = mem_ref[...].astype(BF16)
    mk = jnp.dot(memb, wk_ref[...].astype(BF16), preferred_element_type=F32).astype(BF16)
    mv = jnp.dot(memb, wv_ref[...].astype(BF16), preferred_element_type=F32).astype(BF16)
    s = lax.dot_general(q_ref[...], mk, (((1,), (1,)), ((), ())), preferred_element_type=F32)
    m = jnp.max(s, axis=1, keepdims=True)
    p = jnp.exp2(s - m)
    l = jnp.sum(p, axis=1, keepdims=True)
    o = jnp.dot(p.astype(BF16), mv, preferred_element_type=F32)
    o_ref[...] = (o / l).astype(o_ref.dtype)


def _mem_attn(qkv4, mem, w_mem_kv, batch, seq):
    tq = MEM_TQ
    return pl.pallas_call(
        _mem_kernel,
        grid=(batch, MEM_HEADS, seq // tq),
        in_specs=[
            pl.BlockSpec((None, None, tq, LANES), lambda b, h, i: (SLAB_MEM_Q + h, b, i, 0)),
            pl.BlockSpec((None, N_MEM, D_MODEL), lambda b, h, i: (b, 0, 0)),
            pl.BlockSpec((None, D_MODEL, LANES), lambda b, h, i: (0, 0, h)),
            pl.BlockSpec((None, D_MODEL, LANES), lambda b, h, i: (0, 0, MEM_HEADS + h)),
        ],
        out_specs=pl.BlockSpec((None, tq, LANES), lambda b, h, i: (b, i, h)),
        out_shape=jax.ShapeDtypeStruct((batch, seq, MEM_HEADS * LANES), BF16),
        compiler_params=_cparams(("parallel", "parallel", "parallel")),
        name="mem",
    )(qkv4, mem, w_mem_kv, w_mem_kv)


def _layer_norm(z, g, b):
    mu = jnp.mean(z, axis=1, keepdims=True)
    zc = z - mu
    var = jnp.mean(zc * zc, axis=1, keepdims=True)
    return zc * lax.rsqrt(var + LN_EPS) * g + b


def _lane_max(v):
    return jnp.max(v, axis=1, keepdims=True)


def _routing_info(logits):
    lane = lax.broadcasted_iota(jnp.int32, logits.shape, 1)
    big = jnp.int32(2 * LANES)
    is_g = lane < N_GROUPS
    gl = jnp.where(is_g, logits, NEG_BIG)
    gmax = _lane_max(gl)
    g_w = 1.0 / jnp.sum(jnp.exp(gl - gmax), axis=1, keepdims=True)
    g_idx = jnp.min(jnp.where(gl == gmax, lane, big), axis=1, keepdims=True)
    lo = ROUTE_LANE0 + g_idx * EXPERTS_PER_GROUP
    in_grp = (lane >= lo) & (lane < lo + EXPERTS_PER_GROUP)
    el = jnp.where(in_grp, logits, NEG_BIG)
    e1 = _lane_max(el)
    i1 = jnp.min(jnp.where(el == e1, lane, big), axis=1, keepdims=True)
    el2 = jnp.where(lane == i1, NEG_BIG, el)
    e2 = _lane_max(el2)
    i2 = jnp.min(jnp.where(el2 == e2, lane, big), axis=1, keepdims=True)
    r = jnp.exp(e2 - e1)
    w1 = g_w / (1.0 + r)
    w2 = g_w * r / (1.0 + r)
    id1 = (i1 - ROUTE_LANE0).astype(F32)
    id2 = (i2 - ROUTE_LANE0).astype(F32)
    return jnp.where(lane == RI_E1, id1,
                     jnp.where(lane == RI_E2, id2,
                               jnp.where(lane == RI_W1, w1, jnp.where(lane == RI_W2, w2, 0.0))))


def _merge_kernel(x_ref, yf_ref, yd_ref, ym_ref, wg_ref, bg_ref, wbf_ref, wbd_ref, wbm_ref,
                  wo_ref, g1_ref, b1_ref, wr_ref, br_ref, x1_ref, ri_ref, rit_ref):
    x = x_ref[...]
    xb = x.astype(BF16)
    h = None
    for i, (y_ref, wb_ref) in enumerate(((yf_ref, wbf_ref), (yd_ref, wbd_ref), (ym_ref, wbm_ref))):
        cols = slice(i * D_MODEL, (i + 1) * D_MODEL)
        gl = lax.dot_general(xb, wg_ref[cols, :], (((1,), (1,)), ((), ())),
                             preferred_element_type=F32) + bg_ref[:, cols]
        gate = 1.0 / (1.0 + jnp.exp(-gl))
        br = jnp.dot(y_ref[...], wb_ref[...], preferred_element_type=F32)
        h = gate * br if h is None else h + gate * br
    o = jnp.dot(h.astype(BF16), wo_ref[...], preferred_element_type=F32)
    x1 = _layer_norm(DEEPNORM_ALPHA * x + o, g1_ref[...], b1_ref[...])
    x1_ref[...] = x1
    logits = jnp.dot(x1.astype(BF16), wr_ref[...], preferred_element_type=F32) + br_ref[...]
    ri = _routing_info(logits)
    ri_ref[...] = ri
    rit_ref[...] = ri.T[0:8, :]


def _const_spec(shape):
    return pl.BlockSpec(shape, lambda i: (0,) * len(shape), pipeline_mode=pl.Buffered(1))


def _merge(x2, yf, yd, ym, wg, bg, wbf, wbd, wbm, wo, g1, b1, wr, br):
    t = x2.shape[0]
    tm = MERGE_TM
    half = yf.shape[1]
    row = lambda w: pl.BlockSpec((tm, w), lambda i: (i, 0))
    return pl.pallas_call(
        _merge_kernel,
        grid=(t // tm,),
        in_specs=[
            row(D_MODEL), row(half), row(half), row(half),
            _const_spec((N_BRANCH * D_MODEL, D_MODEL)), _const_spec((1, N_BRANCH * D_MODEL)),
            _const_spec((half, D_MODEL)), _const_spec((half, D_MODEL)), _const_spec((half, D_MODEL)),
            _const_spec((D_MODEL, D_MODEL)), _const_spec((1, D_MODEL)), _const_spec((1, D_MODEL)),
            _const_spec((D_MODEL, LANES)), _const_spec((1, LANES)),
        ],
        out_specs=[row(D_MODEL), row(LANES), pl.BlockSpec((8, tm), lambda i: (0, i))],
        out_shape=[
            jax.ShapeDtypeStruct((t, D_MODEL), F32),
            jax.ShapeDtypeStruct((t, LANES), F32),
            jax.ShapeDtypeStruct((8, t), F32),
        ],
        compiler_params=_cparams(("parallel",)),
        name="merge",
    )(x2, yf, yd, ym, wg, bg, wbf, wbd, wbm, wo, g1, b1, wr, br)


def _cumsum(x, axis):
    n = x.shape[axis]
    idx = lax.broadcasted_iota(jnp.int32, x.shape, axis)
    sh = 1
    while sh < n:
        x = x + jnp.where(idx >= sh, pltpu.roll(x, sh, axis=axis), 0.0)
        sh *= 2
    return x


def _route_kernel(rit_ref, pos_ref, tiles_ref):
    s = rit_ref.shape[1]
    e1 = rit_ref[RI_E1:RI_E1 + 1, :]
    e2 = rit_ref[RI_E2:RI_E2 + 1, :]
    sub = lax.broadcasted_iota(jnp.int32, (N_EXPERTS, s), 0).astype(F32)
    oh1 = jnp.where(sub == e1, 1.0, 0.0)
    oh2 = jnp.where(sub == e2, 1.0, 0.0)
    cnt = oh1 + oh2
    incl = _cumsum(cnt, 1)
    before = incl - cnt
    total = jnp.broadcast_to(incl[:, s - 1:s], (N_EXPERTS, LANES))
    aligned = jnp.floor((total + (SEG_ALIGN - 1)) * (1.0 / SEG_ALIGN)) * SEG_ALIGN
    off = _cumsum(aligned, 0) - aligned
    slot = before + off[:, 0:1]
    pos1 = jnp.sum(oh1 * slot, axis=0, keepdims=True)
    pos2 = jnp.sum(oh2 * slot, axis=0, keepdims=True)
    pos_ref[...] = (jnp.concatenate([pos1, pos2], axis=0) * ROW_TILE).astype(jnp.int32)

    nt = jnp.floor((total + (MOE_R - 1)) * (1.0 / MOE_R))
    ct_incl = _cumsum(nt, 0)
    ct_excl = ct_incl - nt
    sub_l = lax.broadcasted_iota(jnp.int32, (N_EXPERTS, LANES), 0).astype(F32)
    tile_i = lax.broadcasted_iota(jnp.int32, (N_EXPERTS, LANES), 1).astype(F32)
    te = jnp.sum(jnp.where(ct_incl <= tile_i, 1.0, 0.0), axis=0, keepdims=True)
    te = jnp.minimum(te, N_EXPERTS - 1.0)
    sel = jnp.where(sub_l == te, 1.0, 0.0)
    k_in = tile_i - ct_excl
    start = jnp.sum(sel * (off + k_in * MOE_R), axis=0, keepdims=True)
    nval = jnp.sum(sel * jnp.clip(total - k_in * MOE_R, 0.0, float(MOE_R)), axis=0, keepdims=True)
    active = tile_i[0:1, :] < ct_incl[N_EXPERTS - 1:N_EXPERTS, :]
    start = jnp.where(active, start, 0.0)
    nval = jnp.where(active, nval, 0.0)
    rows = jnp.concatenate([te, start * ROW_TILE, nval, jnp.zeros((5, LANES), F32)], axis=0)
    tiles_ref[...] = rows.astype(jnp.int32)


def _route(rit, batch, seq):
    return pl.pallas_call(
        _route_kernel,
        grid=(batch,),
        in_specs=[pl.BlockSpec((8, seq), lambda b: (0, b))],
        out_specs=[
            pl.BlockSpec((None, 2, seq), lambda b: (b, 0, 0)),
            pl.BlockSpec((None, 8, LANES), lambda b: (b, 0, 0)),
        ],
        out_shape=[
            jax.ShapeDtypeStruct((batch, 2, seq), jnp.int32),
            jax.ShapeDtypeStruct((batch, 8, LANES), jnp.int32),
        ],
        compiler_params=_cparams(("parallel",)),
        name="route",
    )(rit)


def _read_rows(view, n_rows):
    return jnp.concatenate([view[pl.ds(k, n_rows, stride=ROW_TILE), :] for k in range(ROW_TILE)], axis=1)


def _write_rows(view, n_rows, val):
    for k in range(ROW_TILE):
        view[pl.ds(k, n_rows, stride=ROW_TILE), :] = val[:, k * LANES:(k + 1) * LANES]


def _moe_kernel(nd, nt, texp_ref, tstart_ref, tnval_ref, x1_ref, p1_ref, p2_ref, ri_ref,
                wg_ref, wu_ref, wd_ref, g2_ref, b2_ref, o_ref, buf, xs_scr, as_scr, bs_scr):
    c = pl.program_id(0)
    j = pl.program_id(1)
    td = x1_ref.shape[0]

    def tile_at(ref, off):
        return ref.at[pl.ds(pl.multiple_of(off, ROW_TILE), ROW_TILE), :]

    @pl.when((j == 0) & (c == 0))
    def _():
        buf[...] = jnp.zeros(buf.shape, F32)

    @pl.when(j < nd)
    def _():
        _write_rows(xs_scr, td, x1_ref[...])

        def body(i, carry):
            for u in range(SEG_ALIGN):
                t = i * SEG_ALIGN + u
                row = tile_at(xs_scr, t * ROW_TILE)[...]
                tile_at(buf, p1_ref[0, t])[...] = row
                tile_at(buf, p2_ref[0, t])[...] = row
            return carry

        lax.fori_loop(0, td // SEG_ALIGN, body, 0)

    @pl.when((j >= nd) & (j < nd + nt))
    def _():
        idx = c * nt + (j - nd)
        nval = tnval_ref[idx]

        @pl.when(nval > 0)
        def _():
            start = pl.multiple_of(tstart_ref[idx], ROW_TILE * SEG_ALIGN)
            view = buf.at[pl.ds(start, MOE_R * ROW_TILE), :]
            xt = _read_rows(view, MOE_R)
            xb = xt.astype(BF16)
            g = jnp.dot(xb, wg_ref[...], preferred_element_type=F32)
            u = jnp.dot(xb, wu_ref[...], preferred_element_type=F32)
            hcat = (g / (1.0 + jnp.exp(-g))) * u
            y = jnp.dot(hcat.astype(BF16), wd_ref[...], preferred_element_type=F32)
            rows = lax.broadcasted_iota(jnp.int32, (MOE_R, 1), 0)
            _write_rows(view, MOE_R, jnp.where(rows < nval, y, xt))

    @pl.when(j >= nd + nt)
    def _():
        def body(i, carry):
            for u in range(SEG_ALIGN):
                t = i * SEG_ALIGN + u
                tile_at(as_scr, t * ROW_TILE)[...] = tile_at(buf, p1_ref[0, t])[...]
                tile_at(bs_scr, t * ROW_TILE)[...] = tile_at(buf, p2_ref[0, t])[...]
            return carry

        lax.fori_loop(0, td // SEG_ALIGN, body, 0)
        ri = ri_ref[...]
        m = ri[:, RI_W1:RI_W1 + 1] * _read_rows(as_scr, td) + ri[:, RI_W2:RI_W2 + 1] * _read_rows(bs_scr, td)
        z = DEEPNORM_ALPHA * x1_ref[...] + m
        o_ref[...] = _layer_norm(z, g2_ref[...], b2_ref[...])


def _moe(x1, pos3, ri, texp, tstart, tnval, wg, wu, wd, g2, b2, batch, seq):
    t = x1.shape[0]
    td = MOE_TD
    nd = seq // td
    nt = 2 * seq // MOE_R + N_EXPERTS
    buf_rows = 2 * seq + N_EXPERTS * SEG_ALIGN + MOE_R

    def tok_tile(j):
        return jnp.where(j < nd, j, jnp.maximum(j - nd - nt, 0))

    def tile_expert(c, j, texp_ref):
        return texp_ref[c * nt + jnp.clip(j - nd, 0, nt - 1)]

    tok_spec = lambda w: pl.BlockSpec((td, w), lambda c, j, *_: (c * nd + tok_tile(j), 0))
    pos_spec = lambda k: pl.BlockSpec((None, 1, td), lambda c, j, *_: ((c * 2 + k) * nd + tok_tile(j), 0, 0),
                                      memory_space=pltpu.SMEM)
    w_spec = lambda a, b_: pl.BlockSpec((None, a, b_), lambda c, j, te, *_: (tile_expert(c, j, te), 0, 0))
    vec_spec = pl.BlockSpec((1, D_MODEL), lambda c, j, *_: (0, 0))
    grid_spec = pltpu.PrefetchScalarGridSpec(
        num_scalar_prefetch=3,
        grid=(batch, nd + nt + nd),
        in_specs=[
            tok_spec(D_MODEL), pos_spec(0), pos_spec(1), tok_spec(LANES),
            w_spec(D_MODEL, D_EXPERT), w_spec(D_MODEL, D_EXPERT), w_spec(D_EXPERT, D_MODEL),
            vec_spec, vec_spec,
        ],
        out_specs=pl.BlockSpec((td, D_MODEL),
                               lambda c, j, *_: (c * nd + jnp.maximum(j - nd - nt, 0), 0)),
        scratch_shapes=[
            pltpu.VMEM((buf_rows * ROW_TILE, LANES), F32),
            pltpu.VMEM((td * ROW_TILE, LANES), F32),
            pltpu.VMEM((td * ROW_TILE, LANES), F32),
            pltpu.VMEM((td * ROW_TILE, LANES), F32),
        ],
    )
    return pl.pallas_call(
        functools.partial(_moe_kernel, nd, nt),
        grid_spec=grid_spec,
        out_shape=jax.ShapeDtypeStruct((t, D_MODEL), F32),
        compiler_params=_cparams(("arbitrary", "arbitrary")),
        name="moe",
    )(texp, tstart, tnval, x1, pos3, pos3, ri, wg, wu, wd, g2, b2)


def _qkv_col_scale():
    ones = lambda n: jnp.ones((n,), F32)
    att = FOX_HEADS * HEAD_DIM
    qs = lambda d: jnp.full((att,), LOG2E * d ** -0.5, F32)
    return jnp.concatenate([qs(HEAD_DIM), ones(2 * att), qs(HEAD_DIM), ones(2 * att),
                            qs(MEM_HEAD_DIM)]).reshape(1, QKV_COLS)


def kernel(x, mem, w_in, b_forget, b_gates, lambda_q1, lambda_k1, lambda_q2, lambda_k2, diff_subln_g,
           w_mem_kv, w_branch_fox, w_branch_diff, w_branch_mem, w_out, ln1_g, ln1_b, w_router_group,
           b_router_group, w_router_expert, b_router_expert, w_expert_gate, w_expert_up, w_expert_down,
           ln2_g, ln2_b):
    batch, seq, d = x.shape
    t = batch * seq
    l = 0
    x2 = x.reshape(t, d)

    w_in_t = jnp.swapaxes(w_in, 1, 2)
    w_fl_t = jnp.pad(w_in_t[l, FL_COL0:, :], ((0, 16 - FOX_HEADS), (0, 0))).astype(BF16)
    w_r = jnp.concatenate([w_router_group[l], w_router_expert[l]], axis=1)
    w_r = jnp.pad(w_r, ((0, 0), (0, LANES - w_r.shape[1]))).astype(BF16)
    b_r = jnp.concatenate([b_router_group[l], b_router_expert[l]])
    b_r = jnp.pad(b_r, (0, LANES - b_r.shape[0])).reshape(1, LANES)
    lam_params = jnp.stack([lambda_q1[l], lambda_k1[l], lambda_q2[l], lambda_k2[l]])
    slopes = 2.0 ** (-8.0 * jnp.arange(1, DIFF_HEADS + 1, dtype=F32) / DIFF_HEADS)

    qkv, vt4, fl_t = _proj(x2, w_in_t, w_fl_t, _qkv_col_scale(), batch, seq)
    qkv4 = qkv.reshape(N_SLABS, batch, seq, LANES)
    c = _fscan(fl_t, b_forget[l].reshape(FOX_HEADS, 1), batch, seq)
    c4 = c.reshape(FOX_HEADS // 2, 2, t)

    y_fox, wg_b, wu_b, wd_b = _fox(qkv4, vt4, c4, w_expert_gate[l], w_expert_up[l], w_expert_down[l],
                                   batch, seq)
    y_diff, w_gates = _diff(qkv4, vt4, slopes, lam_params, diff_subln_g[l].reshape(LANES, 1), w_in_t,
                            batch, seq)
    y_mem = _mem_attn(qkv4, mem, w_mem_kv, batch, seq)

    x1, ri, rit = _merge(
        x2, y_fox.reshape(t, -1), y_diff.reshape(t, -1), y_mem.reshape(t, -1),
        w_gates, b_gates[l].reshape(1, -1),
        w_branch_fox[l].astype(BF16), w_branch_diff[l].astype(BF16), w_branch_mem[l].astype(BF16),
        w_out[l].astype(BF16), ln1_g[l].reshape(1, d), ln1_b[l].reshape(1, d), w_r, b_r)

    pos, tiles = _route(rit, batch, seq)
    nt = 2 * seq // MOE_R + N_EXPERTS
    texp, tstart, tnval = (tiles[:, r, :nt].reshape(-1) for r in range(3))
    pos3 = pos.reshape(batch * 2 * (seq // MOE_TD), 1, MOE_TD)
    out = _moe(x1, pos3, ri, texp, tstart, tnval,
               wg_b, wu_b, wd_b,
               ln2_g[l].reshape(1, d), ln2_b[l].reshape(1, d), batch, seq)
    return out.reshape(batch, seq, d)
```

```python
import functools
import math

import jax
import jax.numpy as jnp
from jax import lax
from jax.experimental import pallas as pl
from jax.experimental.pallas import tpu as pltpu

F32 = jnp.float32
BF16 = jnp.bfloat16

D_MODEL = 1024
HEAD_DIM = 64
FOX_HEADS = 8
DIFF_HEADS = 4
MEM_HEADS = 4
MEM_HEAD_DIM = 128
N_MEM = 256
N_BRANCH = 3
N_GROUPS = 4
EXPERTS_PER_GROUP = 8
N_EXPERTS = N_GROUPS * EXPERTS_PER_GROUP
D_EXPERT = 256
LN_EPS = 1e-5
DEPTH = 1
DEEPNORM_ALPHA = (2.0 * DEPTH) ** 0.25
LAM_INIT = 0.8 - 0.6 * math.exp(-0.3 * 0)
LOG2E = math.log2(math.e)

LANES = 128
QKV_COLS = 3584
N_SLABS = QKV_COLS // LANES
GATE_COL0 = QKV_COLS
FL_COL0 = QKV_COLS + N_BRANCH * D_MODEL
SLAB_FOX_Q, SLAB_FOX_K, SLAB_FOX_V = 0, 4, 8
SLAB_DIFF_Q, SLAB_DIFF_K, SLAB_DIFF_V = 12, 16, 20
SLAB_MEM_Q = 24
ROUTE_LANE0 = N_GROUPS

NEG_BIG = -1e30
VMEM_LIMIT = 56 * 1024 * 1024

PROJ_TM = 2048
PROJ_TN = 512
PROJ_J_FOX_V = SLAB_FOX_V * LANES // PROJ_TN
PROJ_J_DIFF_V = SLAB_DIFF_V * LANES // PROJ_TN
VT_FOX, VT_DIFF = 0, PROJ_TN // LANES
GATE_CAST_STEPS = N_BRANCH * D_MODEL // PROJ_TN
ATT_T = 256
ATT_UNROLL = 8
N_CHAINS = 4
NSL = N_CHAINS // 2
N_BIAS_LANES = 3
DIFF_ONES_ROWS = 16
SETUP_ROWS = 512
MEM_TQ = 4096
MERGE_TM = 1024
MOE_R = 256
MOE_TD = 256
SEG_ALIGN = 8
ROW_TILE = D_MODEL // LANES
RI_E1, RI_E2, RI_W1, RI_W2 = 0, 1, 2, 3


def _cparams(sem, flags=None):
    return pltpu.CompilerParams(dimension_semantics=sem, vmem_limit_bytes=VMEM_LIMIT, flags=flags)


ATT_FLAGS = None


def _proj_kernel(x_ref, w_ref, wfl_ref, sc_ref, o_ref, vt_ref, fl_ref, xb_ref):
    j = pl.program_id(1)
    is_v = (j == PROJ_J_FOX_V) | (j == PROJ_J_DIFF_V)
    spc = PROJ_TN // LANES

    @pl.when(j == 0)
    def _():
        xb = x_ref[...].astype(BF16)
        xb_ref[...] = xb
        fl_ref[...] = lax.dot_general(wfl_ref[...], xb, (((1,), (1,)), ((), ())),
                                      preferred_element_type=F32)

    @pl.when(jnp.logical_not(is_v))
    def _():
        acc = lax.dot_general(xb_ref[...], w_ref[...].astype(BF16), (((1,), (1,)), ((), ())),
                              preferred_element_type=F32) * sc_ref[...]
        for c in range(spc):
            o_ref[c] = acc[:, c * LANES:(c + 1) * LANES].astype(BF16)

    @pl.when(is_v)
    def _():
        acc_t = lax.dot_general(w_ref[...].astype(BF16), xb_ref[...], (((1,), (1,)), ((), ())),
                                preferred_element_type=F32)
        for c in range(spc):
            vt_ref[c] = acc_t[c * LANES:(c + 1) * LANES, :].astype(BF16)
        o_ref[...] = jnp.zeros(o_ref.shape, BF16)


def _proj(x2, w_in_t, w_fl_t, col_scale, batch, seq):
    t = x2.shape[0]
    spc = PROJ_TN // LANES
    per_seq = seq // PROJ_TM
    return pl.pallas_call(
        _proj_kernel,
        grid=(t // PROJ_TM, QKV_COLS // PROJ_TN),
        in_specs=[
            pl.BlockSpec((PROJ_TM, D_MODEL), lambda i, j: (i, 0)),
            pl.BlockSpec((None, PROJ_TN, D_MODEL), lambda i, j: (0, j, 0)),
            pl.BlockSpec((16, D_MODEL), lambda i, j: (0, 0)),
            pl.BlockSpec((1, PROJ_TN), lambda i, j: (0, j)),
        ],
        out_specs=[
            pl.BlockSpec((spc, PROJ_TM, LANES), lambda i, j: (j, i, 0)),
            pl.BlockSpec((spc, None, LANES, PROJ_TM),
                         lambda i, j: (jnp.where(j < PROJ_J_DIFF_V, 0, 1), i // per_seq, 0, i % per_seq)),
            pl.BlockSpec((16, PROJ_TM), lambda i, j: (0, i)),
        ],
        out_shape=[
            jax.ShapeDtypeStruct((N_SLABS, t, LANES), BF16),
            jax.ShapeDtypeStruct((2 * spc, batch, LANES, seq), BF16),
            jax.ShapeDtypeStruct((16, t), F32),
        ],
        scratch_shapes=[pltpu.VMEM((PROJ_TM, D_MODEL), BF16)],
        compiler_params=_cparams(("parallel", "arbitrary")),
        name="proj",
    )(x2, w_in_t, w_fl_t, col_scale)


def _fscan_kernel(fl_ref, b_ref, c_ref):
    s = fl_ref.shape[1]
    z = fl_ref[0:FOX_HEADS, :] + b_ref[...]
    lf = jnp.minimum(z, 0.0) - jnp.log(1.0 + jnp.exp(-jnp.abs(z)))
    lane = lax.broadcasted_iota(jnp.int32, lf.shape, 1)
    sh = 1
    while sh < s:
        r = pltpu.roll(lf, sh, axis=1)
        lf = lf + jnp.where(lane >= sh, r, 0.0)
        sh *= 2
    c_ref[...] = lf * LOG2E


def _fscan(fl_t, b_forget_col, batch, seq):
    t = fl_t.shape[1]
    return pl.pallas_call(
        _fscan_kernel,
        grid=(batch,),
        in_specs=[
            pl.BlockSpec((16, seq), lambda b: (0, b)),
            pl.BlockSpec((FOX_HEADS, 1), lambda b: (0, 0)),
        ],
        out_specs=pl.BlockSpec((FOX_HEADS, seq), lambda b: (0, b)),
        out_shape=jax.ShapeDtypeStruct((FOX_HEADS, t), F32),
        compiler_params=_cparams(("parallel",)),
        name="fscan",
    )(fl_t, b_forget_col)


def _flash_chains(qi, tq, q_ref, qh_scr, k_fn, vt_fn, m_scr, acc_scr, s_scr, pm_scr):
    def causal(s):
        kv = lax.broadcasted_iota(jnp.int32, (tq, tq), 0)
        q = lax.broadcasted_iota(jnp.int32, (tq, tq), 1)
        return jnp.where(kv <= q, s, NEG_BIG)

    def store_scores(slot, c, s):
        s_scr[slot, c] = s
        pm_scr[slot, c] = jnp.max(s.reshape(tq // 8, 8, tq), axis=0)

    def scores(j, slot, masked):
        start = pl.multiple_of(j * tq, tq)
        for c in range(N_CHAINS):
            s = lax.dot_general(k_fn(c, start), qh_scr[c], (((1,), (1,)), ((), ())),
                                preferred_element_type=F32)
            store_scores(slot, c, causal(s) if masked else s)

    _split_q(q_ref, qh_scr)
    m_scr[...] = jnp.full(m_scr.shape, NEG_BIG, F32)
    acc_scr[...] = jnp.zeros(acc_scr.shape, F32)
    scores(0, 0, False)

    def consume(j, slot):
        start = pl.multiple_of(j * tq, tq)
        for c in range(N_CHAINS):
            m_prev = m_scr[c]
            m_new = jnp.maximum(m_prev, jnp.max(pm_scr[slot, c], axis=0, keepdims=True))
            alpha = jnp.exp2(m_prev - m_new)
            p = jnp.exp2(s_scr[slot, c] - m_new)
            m_scr[c] = m_new
            pv = jnp.dot(vt_fn(c, start), p.astype(BF16), preferred_element_type=F32)
            acc_scr[c] = acc_scr[c] * alpha + pv

    @pl.when(qi == 0)
    def _():
        for c in range(N_CHAINS):
            store_scores(0, c, causal(s_scr[0, c]))
        consume(0, 0)

    def run(j0, count, last_masked, issue_beyond):
        for t in range(count):
            if t + 1 < count or issue_beyond:
                scores(j0 + t + 1, (t + 1) % 2, last_masked and t + 1 == count - 1)
            consume(j0 + t, t % 2)

    @pl.when(qi > 0)
    def _():
        groups = (qi - 1) // ATT_UNROLL

        def body(g, carry):
            run(g * ATT_UNROLL, ATT_UNROLL, False, True)
            return carry

        lax.fori_loop(0, groups, body, 0)
        j0 = groups * ATT_UNROLL
        for rem in range(1, ATT_UNROLL + 1):
            @pl.when(qi - j0 == rem)
            def _(rem=rem):
                run(j0, rem + 1, True, False)


def _own_and_spare(shape, c):
    lane = lax.broadcasted_iota(jnp.int32, shape, 1)
    own = (lane < HEAD_DIM) if c % 2 == 0 else (lane >= HEAD_DIM)
    spare = HEAD_DIM if c % 2 == 0 else 0
    return lane, own, spare


def _split_q(q_ref, qh_scr):
    for c in range(N_CHAINS):
        q = q_ref[c // 2].astype(F32)
        lane, own, spare = _own_and_spare(q.shape, c)
        ones = (lane >= spare) & (lane < spare + N_BIAS_LANES)
        qh_scr[c] = jnp.where(own, q, jnp.where(ones, 1.0, 0.0)).astype(BF16)


def _augment_k(k, bias, c):
    lane, own, spare = _own_and_spare(k.shape, c)
    hi = bias.astype(BF16).astype(F32)
    r1 = bias - hi
    mid = r1.astype(BF16).astype(F32)
    lo = (r1 - mid).astype(BF16).astype(F32)
    extra = jnp.where(lane == spare, hi, jnp.where(lane == spare + 1, mid, jnp.where(lane == spare + 2, lo, 0.0)))
    return jnp.where(own, k.astype(F32), extra).astype(BF16)


def _transpose_bf16(a):
    return a.astype(F32).T.astype(BF16)


def _attn_scratch(tq, seq, acc_rows, vt_slabs):
    return [
        pltpu.VMEM((N_CHAINS, tq, LANES), BF16),
        pltpu.VMEM((N_CHAINS, 1, tq), F32),
        pltpu.VMEM((N_CHAINS, acc_rows, tq), F32),
        pltpu.VMEM((2, N_CHAINS, tq, tq), F32),
        pltpu.VMEM((2, N_CHAINS, 8, tq), F32),
        pltpu.VMEM((N_CHAINS, seq, LANES), BF16),
        pltpu.VMEM((vt_slabs, acc_rows, seq), BF16),
    ]


def _fox_kernel(q_ref, k_ref, vt_ref, c_ref, wg_ref, wu_ref, wd_ref, o_ref, wgb_ref, wub_ref, wdb_ref,
                qh_scr, m_scr, acc_scr, s_scr, pm_scr, ka_scr, vat_scr):
    tq = q_ref.shape[1]
    seq = k_ref.shape[1]
    qi = pl.program_id(2)

    n = (pl.program_id(0) * pl.num_programs(1) + pl.program_id(1)) * pl.num_programs(2) + qi
    for k, (src, dst) in enumerate(((wg_ref, wgb_ref), (wu_ref, wub_ref), (wd_ref, wdb_ref))):
        @pl.when((n >= k * N_EXPERTS) & (n < (k + 1) * N_EXPERTS))
        def _(src=src, dst=dst):
            dst[...] = src[...].astype(BF16)

    @pl.when(qi == 0)
    def _():
        def chunk(i, carry):
            r0 = pl.multiple_of(i * SETUP_ROWS, SETUP_ROWS)
            rows = pl.ds(r0, SETUP_ROWS)
            head_row = lax.broadcasted_iota(jnp.int32, (LANES, SETUP_ROWS), 0) < HEAD_DIM
            for sl in range(NSL):
                vt = vt_ref[sl, :, rows]
                one = jnp.ones_like(vt)
                vat_scr[2 * sl, :, rows] = jnp.where(head_row, vt, one)
                vat_scr[2 * sl + 1, :, rows] = jnp.where(head_row, one, vt)
                cc = c_ref[sl, :, rows]
                stacked = jnp.where(head_row, jnp.broadcast_to(cc[1:2, :], (LANES, SETUP_ROWS)),
                                    jnp.broadcast_to(cc[0:1, :], (LANES, SETUP_ROWS)))
                bias = -stacked.T
                k = k_ref[sl, rows, :]
                ka_scr[2 * sl, rows, :] = _augment_k(k, bias, 0)
                ka_scr[2 * sl + 1, rows, :] = _augment_k(k, bias, 1)
            return carry

        lax.fori_loop(0, seq // SETUP_ROWS, chunk, 0)

    def k_fn(c, start):
        return ka_scr[c, pl.ds(start, tq), :]

    def vt_fn(c, start):
        return vat_scr[c, :, pl.ds(start, tq)]

    _flash_chains(qi, tq, q_ref, qh_scr, k_fn, vt_fn, m_scr, acc_scr, s_scr, pm_scr)
    for sl in range(NSL):
        a0 = acc_scr[2 * sl]
        a1 = acc_scr[2 * sl + 1]
        row = lax.broadcasted_iota(jnp.int32, a0.shape, 0)
        ot = jnp.where(row < HEAD_DIM, a0 * (1.0 / a0[HEAD_DIM:HEAD_DIM + 1, :]), a1 * (1.0 / a1[0:1, :]))
        o_ref[:, sl * LANES:(sl + 1) * LANES] = ot.T.astype(o_ref.dtype)


def _fox(qkv4, vt4, c4, wg, wu, wd, batch, seq):
    tq = ATT_T
    groups = FOX_HEADS // N_CHAINS
    nq = seq // tq
    assert batch * groups * nq >= 3 * N_EXPERTS

    def w_spec(k, rows, cols):
        def index(b, g, i):
            return (jnp.clip((b * groups + g) * nq + i - k * N_EXPERTS, 0, N_EXPERTS - 1), 0, 0)
        return pl.BlockSpec((None, rows, cols), index)

    w_specs = [w_spec(0, D_MODEL, D_EXPERT), w_spec(1, D_MODEL, D_EXPERT), w_spec(2, D_EXPERT, D_MODEL)]
    return pl.pallas_call(
        _fox_kernel,
        grid=(batch, groups, nq),
        in_specs=[
            pl.BlockSpec((NSL, None, tq, LANES), lambda b, g, i: (SLAB_FOX_Q // NSL + g, b, i, 0)),
            pl.BlockSpec((NSL, None, seq, LANES), lambda b, g, i: (SLAB_FOX_K // NSL + g, b, 0, 0)),
            pl.BlockSpec((NSL, None, LANES, seq), lambda b, g, i: (VT_FOX // NSL + g, b, 0, 0)),
            pl.BlockSpec((NSL, 2, seq), lambda b, g, i: (g, 0, b)),
        ] + w_specs,
        out_specs=[pl.BlockSpec((None, tq, NSL * LANES), lambda b, g, i: (b, i, g))] + w_specs,
        out_shape=[jax.ShapeDtypeStruct((batch, seq, FOX_HEADS * HEAD_DIM), BF16)]
        + [jax.ShapeDtypeStruct(w.shape, BF16) for w in (wg, wu, wd)],
        scratch_shapes=_attn_scratch(tq, seq, LANES, N_CHAINS),
        compiler_params=_cparams(("arbitrary", "arbitrary", "arbitrary"), ATT_FLAGS),
        name="fox",
    )(qkv4, qkv4, vt4, c4, wg, wu, wd)


def _diff_kernel(slopes_ref, q_ref, k_ref, vt_ref, lam_ref, g_ref, wgi_ref, o_ref, wgo_ref, qh_scr, m_scr,
                 acc_scr, s_scr, pm_scr, ka_scr, vat_scr):
    tq = q_ref.shape[1]
    seq = k_ref.shape[1]
    qi = pl.program_id(2)
    g = pl.program_id(1)

    n = (pl.program_id(0) * pl.num_programs(1) + g) * pl.num_programs(2) + qi

    @pl.when(n < GATE_CAST_STEPS)
    def _():
        wgo_ref[...] = wgi_ref[...].astype(BF16)

    @pl.when(qi == 0)
    def _():
        def chunk(i, carry):
            r0 = pl.multiple_of(i * SETUP_ROWS, SETUP_ROWS)
            rows = pl.ds(r0, SETUP_ROWS)
            for hd in range(NSL):
                vt = vt_ref[hd, :, rows]
                vat_scr[hd, :, rows] = jnp.concatenate([vt, jnp.ones((DIFF_ONES_ROWS, SETUP_ROWS), BF16)], axis=0)
                pos = (lax.broadcasted_iota(jnp.int32, (SETUP_ROWS, LANES), 0) + r0).astype(F32)
                bias = pos * (slopes_ref[NSL * g + hd] * LOG2E)
                k = k_ref[hd, rows, :]
                ka_scr[2 * hd, rows, :] = _augment_k(k, bias, 0)
                ka_scr[2 * hd + 1, rows, :] = _augment_k(k, bias, 1)
            return carry

        lax.fori_loop(0, seq // SETUP_ROWS, chunk, 0)

    def k_fn(c, start):
        return ka_scr[c, pl.ds(start, tq), :]

    def vt_fn(c, start):
        return vat_scr[c // 2, :, pl.ds(start, tq)]

    _flash_chains(qi, tq, q_ref, qh_scr, k_fn, vt_fn, m_scr, acc_scr, s_scr, pm_scr)
    lp = lam_ref[...]
    s1 = jnp.sum(lp[0:1, :] * lp[1:2, :], axis=1, keepdims=True)
    s2 = jnp.sum(lp[2:3, :] * lp[3:4, :], axis=1, keepdims=True)
    lam = jnp.exp(s1) - jnp.exp(s2) + LAM_INIT
    for hd in range(NSL):
        a1 = acc_scr[2 * hd]
        a2 = acc_scr[2 * hd + 1]
        ot = (a1[:LANES, :] * (1.0 / a1[LANES:LANES + 1, :])
              - lam * (a2[:LANES, :] * (1.0 / a2[LANES:LANES + 1, :])))
        ms = jnp.mean(ot * ot, axis=0, keepdims=True)
        yt = ot * lax.rsqrt(ms + LN_EPS) * g_ref[...]
        o_ref[:, hd * LANES:(hd + 1) * LANES] = (yt * (1.0 - LAM_INIT)).T.astype(o_ref.dtype)


def _diff(qkv4, vt4, slopes, lam_params, subln_g_col, w_in_t, batch, seq):
    tq = ATT_T
    groups = DIFF_HEADS // NSL
    nq = seq // tq
    assert batch * groups * nq >= GATE_CAST_STEPS

    def cast_step(b, g, i):
        return jnp.minimum((b * groups + g) * nq + i, GATE_CAST_STEPS - 1)

    grid_spec = pltpu.PrefetchScalarGridSpec(
        num_scalar_prefetch=1,
        grid=(batch, groups, nq),
        in_specs=[
            pl.BlockSpec((NSL, None, tq, LANES), lambda b, g, i, s: (SLAB_DIFF_Q // NSL + g, b, i, 0)),
            pl.BlockSpec((NSL, None, seq, LANES), lambda b, g, i, s: (SLAB_DIFF_K // NSL + g, b, 0, 0)),
            pl.BlockSpec((NSL, None, LANES, seq), lambda b, g, i, s: (VT_DIFF // NSL + g, b, 0, 0)),
            pl.BlockSpec((4, HEAD_DIM), lambda b, g, i, s: (0, 0)),
            pl.BlockSpec((LANES, 1), lambda b, g, i, s: (0, 0)),
            pl.BlockSpec((None, PROJ_TN, D_MODEL),
                         lambda b, g, i, s: (0, GATE_COL0 // PROJ_TN + cast_step(b, g, i), 0)),
        ],
        out_specs=[
            pl.BlockSpec((None, tq, NSL * LANES), lambda b, g, i, s: (b, i, g)),
            pl.BlockSpec((PROJ_TN, D_MODEL), lambda b, g, i, s: (cast_step(b, g, i), 0)),
        ],
        scratch_shapes=_attn_scratch(tq, seq, LANES + DIFF_ONES_ROWS, NSL),
    )
    return pl.pallas_call(
        _diff_kernel,
        grid_spec=grid_spec,
        out_shape=[
            jax.ShapeDtypeStruct((batch, seq, DIFF_HEADS * LANES), BF16),
            jax.ShapeDtypeStruct((N_BRANCH * D_MODEL, D_MODEL), BF16),
        ],
        compiler_params=_cparams(("arbitrary", "arbitrary", "arbitrary"), ATT_FLAGS),
        name="diff",
    )(slopes, qkv4, qkv4, vt4, lam_params, subln_g_col, w_in_t)


def _mem_kernel(q_ref, mem_ref, wk_ref, wv_ref, o_ref):
    memb = mem_ref[...].astype(BF16)
    mk = jnp.dot(memb, wk_ref[...].astype(BF16), preferred_element_type=F32).astype(BF16)
    mv = jnp.dot(memb, wv_ref[...].astype(BF16), preferred_element_type=F32).astype(BF16)
    s = lax.dot_general(q_ref[...], mk, (((1,), (1,)), ((), ())), preferred_element_type=F32)
    m = jnp.max(s, axis=1, keepdims=True)
    p = jnp.exp2(s - m)
    l = jnp.sum(p, axis=1, keepdims=True)
    o = jnp.dot(p.astype(BF16), mv, preferred_element_type=F32)
    o_ref[...] = (o / l).astype(o_ref.dtype)


def _mem_attn(qkv4, mem, w_mem_kv, batch, seq):
    tq = MEM_TQ
    return pl.pallas_call(
        _mem_kernel,
        grid=(batch, MEM_HEADS, seq // tq),
        in_specs=[
            pl.BlockSpec((None, None, tq, LANES), lambda b, h, i: (SLAB_MEM_Q + h, b, i, 0)),
            pl.BlockSpec((None, N_MEM, D_MODEL), lambda b, h, i: (b, 0, 0)),
            pl.BlockSpec((None, D_MODEL, LANES), lambda b, h, i: (0, 0, h)),
            pl.BlockSpec((None, D_MODEL, LANES), lambda b, h, i: (0, 0, MEM_HEADS + h)),
        ],
        out_specs=pl.BlockSpec((None, tq, LANES), lambda b, h, i: (b, i, h)),
        out_shape=jax.ShapeDtypeStruct((batch, seq, MEM_HEADS * LANES), BF16),
        compiler_params=_cparams(("parallel", "parallel", "parallel")),
        name="mem",
    )(qkv4, mem, w_mem_kv, w_mem_kv)


def _layer_norm(z, g, b):
    mu = jnp.mean(z, axis=1, keepdims=True)
    zc = z - mu
    var = jnp.mean(zc * zc, axis=1, keepdims=True)
    return zc * lax.rsqrt(var + LN_EPS) * g + b


def _lane_max(v):
    return jnp.max(v, axis=1, keepdims=True)


def _routing_info(logits):
    lane = lax.broadcasted_iota(jnp.int32, logits.shape, 1)
    big = jnp.int32(2 * LANES)
    is_g = lane < N_GROUPS
    gl = jnp.where(is_g, logits, NEG_BIG)
    gmax = _lane_max(gl)
    g_w = 1.0 / jnp.sum(jnp.exp(gl - gmax), axis=1, keepdims=True)
    g_idx = jnp.min(jnp.where(gl == gmax, lane, big), axis=1, keepdims=True)
    lo = ROUTE_LANE0 + g_idx * EXPERTS_PER_GROUP
    in_grp = (lane >= lo) & (lane < lo + EXPERTS_PER_GROUP)
    el = jnp.where(in_grp, logits, NEG_BIG)
    e1 = _lane_max(el)
    i1 = jnp.min(jnp.where(el == e1, lane, big), axis=1, keepdims=True)
    el2 = jnp.where(lane == i1, NEG_BIG, el)
    e2 = _lane_max(el2)
    i2 = jnp.min(jnp.where(el2 == e2, lane, big), axis=1, keepdims=True)
    r = jnp.exp(e2 - e1)
    w1 = g_w / (1.0 + r)
    w2 = g_w * r / (1.0 + r)
    id1 = (i1 - ROUTE_LANE0).astype(F32)
    id2 = (i2 - ROUTE_LANE0).astype(F32)
    return jnp.where(lane == RI_E1, id1,
                     jnp.where(lane == RI_E2, id2,
                               jnp.where(lane == RI_W1, w1, jnp.where(lane == RI_W2, w2, 0.0))))


def _merge_kernel(x_ref, yf_ref, yd_ref, ym_ref, wg_ref, bg_ref, wbf_ref, wbd_ref, wbm_ref,
                  wo_ref, g1_ref, b1_ref, wr_ref, br_ref, x1_ref, ri_ref, rit_ref):
    x = x_ref[...]
    xb = x.astype(BF16)
    h = None
    for i, (y_ref, wb_ref) in enumerate(((yf_ref, wbf_ref), (yd_ref, wbd_ref), (ym_ref, wbm_ref))):
        cols = slice(i * D_MODEL, (i + 1) * D_MODEL)
        gl = lax.dot_general(xb, wg_ref[cols, :], (((1,), (1,)), ((), ())),
                             preferred_element_type=F32) + bg_ref[:, cols]
        gate = 1.0 / (1.0 + jnp.exp(-gl))
        br = jnp.dot(y_ref[...], wb_ref[...], preferred_element_type=F32)
        h = gate * br if h is None else h + gate * br
    o = jnp.dot(h.astype(BF16), wo_ref[...], preferred_element_type=F32)
    x1 = _layer_norm(DEEPNORM_ALPHA * x + o, g1_ref[...], b1_ref[...])
    x1_ref[...] = x1
    logits = jnp.dot(x1.astype(BF16), wr_ref[...], preferred_element_type=F32) + br_ref[...]
    ri = _routing_info(logits)
    ri_ref[...] = ri
    rit_ref[...] = ri.T[0:8, :]


def _const_spec(shape):
    return pl.BlockSpec(shape, lambda i: (0,) * len(shape), pipeline_mode=pl.Buffered(1))


def _merge(x2, yf, yd, ym, wg, bg, wbf, wbd, wbm, wo, g1, b1, wr, br):
    t = x2.shape[0]
    tm = MERGE_TM
    half = yf.shape[1]
    row = lambda w: pl.BlockSpec((tm, w), lambda i: (i, 0))
    return pl.pallas_call(
        _merge_kernel,
        grid=(t // tm,),
        in_specs=[
            row(D_MODEL), row(half), row(half), row(half),
            _const_spec((N_BRANCH * D_MODEL, D_MODEL)), _const_spec((1, N_BRANCH * D_MODEL)),
            _const_spec((half, D_MODEL)), _const_spec((half, D_MODEL)), _const_spec((half, D_MODEL)),
            _const_spec((D_MODEL, D_MODEL)), _const_spec((1, D_MODEL)), _const_spec((1, D_MODEL)),
            _const_spec((D_MODEL, LANES)), _const_spec((1, LANES)),
        ],
        out_specs=[row(D_MODEL), row(LANES), pl.BlockSpec((8, tm), lambda i: (0, i))],
        out_shape=[
            jax.ShapeDtypeStruct((t, D_MODEL), F32),
            jax.ShapeDtypeStruct((t, LANES), F32),
            jax.ShapeDtypeStruct((8, t), F32),
        ],
        compiler_params=_cparams(("parallel",)),
        name="merge",
    )(x2, yf, yd, ym, wg, bg, wbf, wbd, wbm, wo, g1, b1, wr, br)


def _cumsum(x, axis):
    n = x.shape[axis]
    idx = lax.broadcasted_iota(jnp.int32, x.shape, axis)
    sh = 1
    while sh < n:
        x = x + jnp.where(idx >= sh, pltpu.roll(x, sh, axis=axis), 0.0)
        sh *= 2
    return x


def _route_kernel(rit_ref, pos_ref, tiles_ref):
    s = rit_ref.shape[1]
    e1 = rit_ref[RI_E1:RI_E1 + 1, :]
    e2 = rit_ref[RI_E2:RI_E2 + 1, :]
    sub = lax.broadcasted_iota(jnp.int32, (N_EXPERTS, s), 0).astype(F32)
    oh1 = jnp.where(sub == e1, 1.0, 0.0)
    oh2 = jnp.where(sub == e2, 1.0, 0.0)
    cnt = oh1 + oh2
    incl = _cumsum(cnt, 1)
    before = incl - cnt
    total = jnp.broadcast_to(incl[:, s - 1:s], (N_EXPERTS, LANES))
    aligned = jnp.floor((total + (SEG_ALIGN - 1)) * (1.0 / SEG_ALIGN)) * SEG_ALIGN
    off = _cumsum(aligned, 0) - aligned
    slot = before + off[:, 0:1]
    pos1 = jnp.sum(oh1 * slot, axis=0, keepdims=True)
    pos2 = jnp.sum(oh2 * slot, axis=0, keepdims=True)
    pos_ref[...] = (jnp.concatenate([pos1, pos2], axis=0) * ROW_TILE).astype(jnp.int32)

    nt = jnp.floor((total + (MOE_R - 1)) * (1.0 / MOE_R))
    ct_incl = _cumsum(nt, 0)
    ct_excl = ct_incl - nt
    sub_l = lax.broadcasted_iota(jnp.int32, (N_EXPERTS, LANES), 0).astype(F32)
    tile_i = lax.broadcasted_iota(jnp.int32, (N_EXPERTS, LANES), 1).astype(F32)
    te = jnp.sum(jnp.where(ct_incl <= tile_i, 1.0, 0.0), axis=0, keepdims=True)
    te = jnp.minimum(te, N_EXPERTS - 1.0)
    sel = jnp.where(sub_l == te, 1.0, 0.0)
    k_in = tile_i - ct_excl
    start = jnp.sum(sel * (off + k_in * MOE_R), axis=0, keepdims=True)
    nval = jnp.sum(sel * jnp.clip(total - k_in * MOE_R, 0.0, float(MOE_R)), axis=0, keepdims=True)
    active = tile_i[0:1, :] < ct_incl[N_EXPERTS - 1:N_EXPERTS, :]
    start = jnp.where(active, start, 0.0)
    nval = jnp.where(active, nval, 0.0)
    rows = jnp.concatenate([te, start * ROW_TILE, nval, jnp.zeros((5, LANES), F32)], axis=0)
    tiles_ref[...] = rows.astype(jnp.int32)


def _route(rit, batch, seq):
    return pl.pallas_call(
        _route_kernel,
        grid=(batch,),
        in_specs=[pl.BlockSpec((8, seq), lambda b: (0, b))],
        out_specs=[
            pl.BlockSpec((None, 2, seq), lambda b: (b, 0, 0)),
            pl.BlockSpec((None, 8, LANES), lambda b: (b, 0, 0)),
        ],
        out_shape=[
            jax.ShapeDtypeStruct((batch, 2, seq), jnp.int32),
            jax.ShapeDtypeStruct((batch, 8, LANES), jnp.int32),
        ],
        compiler_params=_cparams(("parallel",)),
        name="route",
    )(rit)


def _read_rows(view, n_rows):
    return jnp.concatenate([view[pl.ds(k, n_rows, stride=ROW_TILE), :] for k in range(ROW_TILE)], axis=1)


def _write_rows(view, n_rows, val):
    for k in range(ROW_TILE):
        view[pl.ds(k, n_rows, stride=ROW_TILE), :] = val[:, k * LANES:(k + 1) * LANES]


def _moe_kernel(nd, nt, texp_ref, tstart_ref, tnval_ref, x1_ref, p1_ref, p2_ref, ri_ref,
                wg_ref, wu_ref, wd_ref, g2_ref, b2_ref, o_ref, buf, xs_scr, as_scr, bs_scr):
    c = pl.program_id(0)
    j = pl.program_id(1)
    td = x1_ref.shape[0]

    def tile_at(ref, off):
        return ref.at[pl.ds(pl.multiple_of(off, ROW_TILE), ROW_TILE), :]

    @pl.when((j == 0) & (c == 0))
    def _():
        buf[...] = jnp.zeros(buf.shape, F32)

    @pl.when(j < nd)
    def _():
        _write_rows(xs_scr, td, x1_ref[...])

        def body(i, carry):
            for u in range(SEG_ALIGN):
                t = i * SEG_ALIGN + u
                row = tile_at(xs_scr, t * ROW_TILE)[...]
                tile_at(buf, p1_ref[0, t])[...] = row
                tile_at(buf, p2_ref[0, t])[...] = row
            return carry

        lax.fori_loop(0, td // SEG_ALIGN, body, 0)

    @pl.when((j >= nd) & (j < nd + nt))
    def _():
        idx = c * nt + (j - nd)
        nval = tnval_ref[idx]

        @pl.when(nval > 0)
        def _():
            start = pl.multiple_of(tstart_ref[idx], ROW_TILE * SEG_ALIGN)
            view = buf.at[pl.ds(start, MOE_R * ROW_TILE), :]
            xt = _read_rows(view, MOE_R)
            xb = xt.astype(BF16)
            g = jnp.dot(xb, wg_ref[...], preferred_element_type=F32)
            u = jnp.dot(xb, wu_ref[...], preferred_element_type=F32)
            hcat = (g / (1.0 + jnp.exp(-g))) * u
            y = jnp.dot(hcat.astype(BF16), wd_ref[...], preferred_element_type=F32)
            rows = lax.broadcasted_iota(jnp.int32, (MOE_R, 1), 0)
            _write_rows(view, MOE_R, jnp.where(rows < nval, y, xt))

    @pl.when(j >= nd + nt)
    def _():
        def body(i, carry):
            for u in range(SEG_ALIGN):
                t = i * SEG_ALIGN + u
                tile_at(as_scr, t * ROW_TILE)[...] = tile_at(buf, p1_ref[0, t])[...]
                tile_at(bs_scr, t * ROW_TILE)[...] = tile_at(buf, p2_ref[0, t])[...]
            return carry

        lax.fori_loop(0, td // SEG_ALIGN, body, 0)
        ri = ri_ref[...]
        m = ri[:, RI_W1:RI_W1 + 1] * _read_rows(as_scr, td) + ri[:, RI_W2:RI_W2 + 1] * _read_rows(bs_scr, td)
        z = DEEPNORM_ALPHA * x1_ref[...] + m
        o_ref[...] = _layer_norm(z, g2_ref[...], b2_ref[...])


def _moe(x1, pos3, ri, texp, tstart, tnval, wg, wu, wd, g2, b2, batch, seq):
    t = x1.shape[0]
    td = MOE_TD
    nd = seq // td
    nt = 2 * seq // MOE_R + N_EXPERTS
    buf_rows = 2 * seq + N_EXPERTS * SEG_ALIGN + MOE_R

    def tok_tile(j):
        return jnp.where(j < nd, j, jnp.maximum(j - nd - nt, 0))

    def tile_expert(c, j, texp_ref):
        return texp_ref[c * nt + jnp.clip(j - nd, 0, nt - 1)]

    tok_spec = lambda w: pl.BlockSpec((td, w), lambda c, j, *_: (c * nd + tok_tile(j), 0))
    pos_spec = lambda k: pl.BlockSpec((None, 1, td), lambda c, j, *_: ((c * 2 + k) * nd + tok_tile(j), 0, 0),
                                      memory_space=pltpu.SMEM)
    w_spec = lambda a, b_: pl.BlockSpec((None, a, b_), lambda c, j, te, *_: (tile_expert(c, j, te), 0, 0))
    vec_spec = pl.BlockSpec((1, D_MODEL), lambda c, j, *_: (0, 0))
    grid_spec = pltpu.PrefetchScalarGridSpec(
        num_scalar_prefetch=3,
        grid=(batch, nd + nt + nd),
        in_specs=[
            tok_spec(D_MODEL), pos_spec(0), pos_spec(1), tok_spec(LANES),
            w_spec(D_MODEL, D_EXPERT), w_spec(D_MODEL, D_EXPERT), w_spec(D_EXPERT, D_MODEL),
            vec_spec, vec_spec,
        ],
        out_specs=pl.BlockSpec((td, D_MODEL),
                               lambda c, j, *_: (c * nd + jnp.maximum(j - nd - nt, 0), 0)),
        scratch_shapes=[
            pltpu.VMEM((buf_rows * ROW_TILE, LANES), F32),
            pltpu.VMEM((td * ROW_TILE, LANES), F32),
            pltpu.VMEM((td * ROW_TILE, LANES), F32),
            pltpu.VMEM((td * ROW_TILE, LANES), F32),
        ],
    )
    return pl.pallas_call(
        functools.partial(_moe_kernel, nd, nt),
        grid_spec=grid_spec,
        out_shape=jax.ShapeDtypeStruct((t, D_MODEL), F32),
        compiler_params=_cparams(("arbitrary", "arbitrary")),
        name="moe",
    )(texp, tstart, tnval, x1, pos3, pos3, ri, wg, wu, wd, g2, b2)


def _qkv_col_scale():
    ones = lambda n: jnp.ones((n,), F32)
    att = FOX_HEADS * HEAD_DIM
    qs = lambda d: jnp.full((att,), LOG2E * d ** -0.5, F32)
    return jnp.concatenate([qs(HEAD_DIM), ones(2 * att), qs(HEAD_DIM), ones(2 * att),
                            qs(MEM_HEAD_DIM)]).reshape(1, QKV_COLS)


def kernel(x, mem, w_in, b_forget, b_gates, lambda_q1, lambda_k1, lambda_q2, lambda_k2, diff_subln_g,
           w_mem_kv, w_branch_fox, w_branch_diff, w_branch_mem, w_out, ln1_g, ln1_b, w_router_group,
           b_router_group, w_router_expert, b_router_expert, w_expert_gate, w_expert_up, w_expert_down,
           ln2_g, ln2_b):
    batch, seq, d = x.shape
    t = batch * seq
    l = 0
    x2 = x.reshape(t, d)

    w_in_t = jnp.swapaxes(w_in, 1, 2)
    w_fl_t = jnp.pad(w_in_t[l, FL_COL0:, :], ((0, 16 - FOX_HEADS), (0, 0))).astype(BF16)
    w_r = jnp.concatenate([w_router_group[l], w_router_expert[l]], axis=1)
    w_r = jnp.pad(w_r, ((0, 0), (0, LANES - w_r.shape[1]))).astype(BF16)
    b_r = jnp.concatenate([b_router_group[l], b_router_expert[l]])
    b_r = jnp.pad(b_r, (0, LANES - b_r.shape[0])).reshape(1, LANES)
    lam_params = jnp.stack([lambda_q1[l], lambda_k1[l], lambda_q2[l], lambda_k2[l]])
    slopes = 2.0 ** (-8.0 * jnp.arange(1, DIFF_HEADS + 1, dtype=F32) / DIFF_HEADS)

    qkv, vt4, fl_t = _proj(x2, w_in_t, w_fl_t, _qkv_col_scale(), batch, seq)
    qkv4 = qkv.reshape(N_SLABS, batch, seq, LANES)
    c = _fscan(fl_t, b_forget[l].reshape(FOX_HEADS, 1), batch, seq)
    c4 = c.reshape(FOX_HEADS // 2, 2, t)

    y_fox, wg_b, wu_b, wd_b = _fox(qkv4, vt4, c4, w_expert_gate[l], w_expert_up[l], w_expert_down[l],
                                   batch, seq)
    y_diff, w_gates = _diff(qkv4, vt4, slopes, lam_params, diff_subln_g[l].reshape(LANES, 1), w_in_t,
                            batch, seq)
    y_mem = _mem_attn(qkv4, mem, w_mem_kv, batch, seq)

    x1, ri, rit = _merge(
        x2, y_fox.reshape(t, -1), y_diff.reshape(t, -1), y_mem.reshape(t, -1),
        w_gates, b_gates[l].reshape(1, -1),
        w_branch_fox[l].astype(BF16), w_branch_diff[l].astype(BF16), w_branch_mem[l].astype(BF16),
        w_out[l].astype(BF16), ln1_g[l].reshape(1, d), ln1_b[l].reshape(1, d), w_r, b_r)

    pos, tiles = _route(rit, batch, seq)
    nt = 2 * seq // MOE_R + N_EXPERTS
    texp, tstart, tnval = (tiles[:, r, :nt].reshape(-1) for r in range(3))
    pos3 = pos.reshape(batch * 2 * (seq // MOE_TD), 1, MOE_TD)
    out = _moe(x1, pos3, ri, texp, tstart, tnval,
               wg_b, wu_b, wd_b,
               ln2_g[l].reshape(1, d), ln2_b[l].reshape(1, d), batch, seq)
    return out.reshape(batch, seq, d)
```

```python
import functools
import math

import jax
import jax.numpy as jnp
from jax import lax
from jax.experimental import pallas as pl
from jax.experimental.pallas import tpu as pltpu

F32 = jnp.float32
BF16 = jnp.bfloat16

D_MODEL = 1024
HEAD_DIM = 64
FOX_HEADS = 8
DIFF_HEADS = 4
MEM_HEADS = 4
MEM_HEAD_DIM = 128
N_MEM = 256
N_BRANCH = 3
N_GROUPS = 4
EXPERTS_PER_GROUP = 8
N_EXPERTS = N_GROUPS * EXPERTS_PER_GROUP
D_EXPERT = 256
LN_EPS = 1e-5
DEPTH = 1
DEEPNORM_ALPHA = (2.0 * DEPTH) ** 0.25
LAM_INIT = 0.8 - 0.6 * math.exp(-0.3 * 0)
LOG2E = math.log2(math.e)

LANES = 128
QKV_COLS = 3584
N_SLABS = QKV_COLS // LANES
GATE_COL0 = QKV_COLS
FL_COL0 = QKV_COLS + N_BRANCH * D_MODEL
SLAB_FOX_Q, SLAB_FOX_K, SLAB_FOX_V = 0, 4, 8
SLAB_DIFF_Q, SLAB_DIFF_K, SLAB_DIFF_V = 12, 16, 20
SLAB_MEM_Q = 24
ROUTE_LANE0 = N_GROUPS

NEG_BIG = -1e30
VMEM_LIMIT = 56 * 1024 * 1024

PROJ_TM = 2048
PROJ_TN = 512
PROJ_J_FOX_V = SLAB_FOX_V * LANES // PROJ_TN
PROJ_J_DIFF_V = SLAB_DIFF_V * LANES // PROJ_TN
VT_FOX, VT_DIFF = 0, PROJ_TN // LANES
GATE_CAST_STEPS = N_BRANCH * D_MODEL // PROJ_TN
EXPERT_CAST_BLOCK = 2
EXPERT_CAST_STEPS = N_EXPERTS // EXPERT_CAST_BLOCK
ATT_T = 256
ATT_UNROLL = 8
N_CHAINS = 4
NSL = N_CHAINS // 2
N_BIAS_LANES = 3
DIFF_ONES_ROWS = 16
SETUP_ROWS = 512
MEM_TQ = 4096
MERGE_TM = 1024
MOE_R = 256
MOE_TD = 256
SEG_ALIGN = 8
MOE_ROWS_PER_TRIP = 16
ROW_TILE = D_MODEL // LANES
RI_E1, RI_E2, RI_W1, RI_W2 = 0, 1, 2, 3


def _cparams(sem, flags=None):
    return pltpu.CompilerParams(dimension_semantics=sem, vmem_limit_bytes=VMEM_LIMIT, flags=flags)


ATT_FLAGS = None


def _proj_kernel(x_ref, w_ref, wfl_ref, sc_ref, o_ref, vt_ref, fl_ref, xb_ref):
    j = pl.program_id(1)
    is_v = (j == PROJ_J_FOX_V) | (j == PROJ_J_DIFF_V)
    spc = PROJ_TN // LANES

    @pl.when(j == 0)
    def _():
        xb = x_ref[...].astype(BF16)
        xb_ref[...] = xb
        fl_ref[...] = lax.dot_general(wfl_ref[...], xb, (((1,), (1,)), ((), ())),
                                      preferred_element_type=F32)

    @pl.when(jnp.logical_not(is_v))
    def _():
        acc = lax.dot_general(xb_ref[...], w_ref[...].astype(BF16), (((1,), (1,)), ((), ())),
                              preferred_element_type=F32) * sc_ref[...]
        for c in range(spc):
            o_ref[c] = acc[:, c * LANES:(c + 1) * LANES].astype(BF16)

    @pl.when(is_v)
    def _():
        acc_t = lax.dot_general(w_ref[...].astype(BF16), xb_ref[...], (((1,), (1,)), ((), ())),
                                preferred_element_type=F32)
        for c in range(spc):
            vt_ref[c] = acc_t[c * LANES:(c + 1) * LANES, :].astype(BF16)
        o_ref[...] = jnp.zeros(o_ref.shape, BF16)


def _proj(x2, w_in_t, w_fl_t, col_scale, batch, seq):
    t = x2.shape[0]
    spc = PROJ_TN // LANES
    per_seq = seq // PROJ_TM
    return pl.pallas_call(
        _proj_kernel,
        grid=(t // PROJ_TM, QKV_COLS // PROJ_TN),
        in_specs=[
            pl.BlockSpec((PROJ_TM, D_MODEL), lambda i, j: (i, 0)),
            pl.BlockSpec((None, PROJ_TN, D_MODEL), lambda i, j: (0, j, 0)),
            pl.BlockSpec((16, D_MODEL), lambda i, j: (0, 0)),
            pl.BlockSpec((1, PROJ_TN), lambda i, j: (0, j)),
        ],
        out_specs=[
            pl.BlockSpec((spc, PROJ_TM, LANES), lambda i, j: (j, i, 0)),
            pl.BlockSpec((spc, None, LANES, PROJ_TM),
                         lambda i, j: (jnp.where(j < PROJ_J_DIFF_V, 0, 1), i // per_seq, 0, i % per_seq)),
            pl.BlockSpec((16, PROJ_TM), lambda i, j: (0, i)),
        ],
        out_shape=[
            jax.ShapeDtypeStruct((N_SLABS, t, LANES), BF16),
            jax.ShapeDtypeStruct((2 * spc, batch, LANES, seq), BF16),
            jax.ShapeDtypeStruct((16, t), F32),
        ],
        scratch_shapes=[pltpu.VMEM((PROJ_TM, D_MODEL), BF16)],
        compiler_params=_cparams(("parallel", "arbitrary")),
        name="proj",
    )(x2, w_in_t, w_fl_t, col_scale)


def _fscan_kernel(fl_ref, b_ref, c_ref):
    s = fl_ref.shape[1]
    z = fl_ref[0:FOX_HEADS, :] + b_ref[...]
    lf = jnp.minimum(z, 0.0) - jnp.log(1.0 + jnp.exp(-jnp.abs(z)))
    lane = lax.broadcasted_iota(jnp.int32, lf.shape, 1)
    sh = 1
    while sh < s:
        r = pltpu.roll(lf, sh, axis=1)
        lf = lf + jnp.where(lane >= sh, r, 0.0)
        sh *= 2
    c_ref[...] = lf * LOG2E


def _fscan(fl_t, b_forget_col, batch, seq):
    t = fl_t.shape[1]
    return pl.pallas_call(
        _fscan_kernel,
        grid=(batch,),
        in_specs=[
            pl.BlockSpec((16, seq), lambda b: (0, b)),
            pl.BlockSpec((FOX_HEADS, 1), lambda b: (0, 0)),
        ],
        out_specs=pl.BlockSpec((FOX_HEADS, seq), lambda b: (0, b)),
        out_shape=jax.ShapeDtypeStruct((FOX_HEADS, t), F32),
        compiler_params=_cparams(("parallel",)),
        name="fscan",
    )(fl_t, b_forget_col)


def _flash_chains(qi, tq, qh_scr, k_fn, vt_fn, m_scr, acc_scr, s_scr, pm_scr):
    m_scr[...] = jnp.full(m_scr.shape, NEG_BIG, F32)
    acc_scr[...] = jnp.zeros(acc_scr.shape, F32)

    def scores(j, slot, masked):
        start = pl.multiple_of(j * tq, tq)
        for c in range(N_CHAINS):
            s = lax.dot_general(k_fn(c, start), qh_scr[c], (((1,), (1,)), ((), ())),
                                preferred_element_type=F32)
            if masked:
                kv = lax.broadcasted_iota(jnp.int32, (tq, tq), 0)
                q = lax.broadcasted_iota(jnp.int32, (tq, tq), 1)
                s = jnp.where(kv <= q, s, NEG_BIG)
            s_scr[slot, c] = s
            pm_scr[slot, c] = jnp.max(s.reshape(tq // 8, 8, tq), axis=0)

    def consume(j, slot):
        start = pl.multiple_of(j * tq, tq)
        for c in range(N_CHAINS):
            m_prev = m_scr[c]
            m_new = jnp.maximum(m_prev, jnp.max(pm_scr[slot, c], axis=0, keepdims=True))
            alpha = jnp.exp2(m_prev - m_new)
            p = jnp.exp2(s_scr[slot, c] - m_new)
            m_scr[c] = m_new
            pv = jnp.dot(vt_fn(c, start), p.astype(BF16), preferred_element_type=F32)
            acc_scr[c] = acc_scr[c] * alpha + pv

    @pl.when(qi == 0)
    def _():
        scores(0, 0, True)
        consume(0, 0)

    def run(j0, count, last_masked, issue_beyond):
        for t in range(count):
            if t + 1 < count or issue_beyond:
                scores(j0 + t + 1, (t + 1) % 2, last_masked and t + 1 == count - 1)
            consume(j0 + t, t % 2)

    @pl.when(qi > 0)
    def _():
        scores(0, 0, False)
        groups = (qi - 1) // ATT_UNROLL

        def body(g, carry):
            run(g * ATT_UNROLL, ATT_UNROLL, False, True)
            return carry

        lax.fori_loop(0, groups, body, 0)
        j0 = groups * ATT_UNROLL
        for rem in range(1, ATT_UNROLL + 1):
            @pl.when(qi - j0 == rem)
            def _(rem=rem):
                run(j0, rem + 1, True, False)


def _own_and_spare(shape, c):
    lane = lax.broadcasted_iota(jnp.int32, shape, 1)
    own = (lane < HEAD_DIM) if c % 2 == 0 else (lane >= HEAD_DIM)
    spare = HEAD_DIM if c % 2 == 0 else 0
    return lane, own, spare


def _split_q(q_ref, qh_scr):
    for c in range(N_CHAINS):
        q = q_ref[c // 2].astype(F32)
        lane, own, spare = _own_and_spare(q.shape, c)
        ones = (lane >= spare) & (lane < spare + N_BIAS_LANES)
        qh_scr[c] = jnp.where(own, q, jnp.where(ones, 1.0, 0.0)).astype(BF16)


def _augment_k(k, bias, c):
    lane, own, spare = _own_and_spare(k.shape, c)
    hi = bias.astype(BF16).astype(F32)
    r1 = bias - hi
    mid = r1.astype(BF16).astype(F32)
    lo = (r1 - mid).astype(BF16).astype(F32)
    extra = jnp.where(lane == spare, hi, jnp.where(lane == spare + 1, mid, jnp.where(lane == spare + 2, lo, 0.0)))
    return jnp.where(own, k.astype(F32), extra).astype(BF16)


def _transpose_bf16(a):
    return a.astype(F32).T.astype(BF16)


def _attn_scratch(tq, seq, acc_rows, vt_slabs):
    return [
        pltpu.VMEM((N_CHAINS, tq, LANES), BF16),
        pltpu.VMEM((N_CHAINS, 1, tq), F32),
        pltpu.VMEM((N_CHAINS, acc_rows, tq), F32),
        pltpu.VMEM((2, N_CHAINS, tq, tq), F32),
        pltpu.VMEM((2, N_CHAINS, 8, tq), F32),
        pltpu.VMEM((N_CHAINS, seq, LANES), BF16),
        pltpu.VMEM((vt_slabs, acc_rows, seq), BF16),
    ]


def _fox_kernel(q_ref, k_ref, vt_ref, c_ref, wg_ref, wu_ref, wd_ref, o_ref, wgb_ref, wub_ref, wdb_ref,
                qh_scr, m_scr, acc_scr, s_scr, pm_scr, ka_scr, vat_scr):
    tq = q_ref.shape[1]
    seq = k_ref.shape[1]
    qi = pl.program_id(2)
    _split_q(q_ref, qh_scr)

    n = (pl.program_id(0) * pl.num_programs(1) + pl.program_id(1)) * pl.num_programs(2) + qi
    for k, (src, dst) in enumerate(((wg_ref, wgb_ref), (wu_ref, wub_ref), (wd_ref, wdb_ref))):
        @pl.when((n >= k * EXPERT_CAST_STEPS) & (n < (k + 1) * EXPERT_CAST_STEPS))
        def _(src=src, dst=dst):
            dst[...] = src[...].astype(BF16)

    @pl.when(qi == 0)
    def _():
        def chunk(i, carry):
            r0 = pl.multiple_of(i * SETUP_ROWS, SETUP_ROWS)
            rows = pl.ds(r0, SETUP_ROWS)
            head_row = lax.broadcasted_iota(jnp.int32, (LANES, SETUP_ROWS), 0) < HEAD_DIM
            for sl in range(NSL):
                vt = vt_ref[sl, :, rows]
                one = jnp.ones_like(vt)
                vat_scr[2 * sl, :, rows] = jnp.where(head_row, vt, one)
                vat_scr[2 * sl + 1, :, rows] = jnp.where(head_row, one, vt)
                cc = c_ref[sl, :, rows]
                stacked = jnp.where(head_row, jnp.broadcast_to(cc[1:2, :], (LANES, SETUP_ROWS)),
                                    jnp.broadcast_to(cc[0:1, :], (LANES, SETUP_ROWS)))
                bias = -stacked.T
                k = k_ref[sl, rows, :]
                ka_scr[2 * sl, rows, :] = _augment_k(k, bias, 0)
                ka_scr[2 * sl + 1, rows, :] = _augment_k(k, bias, 1)
            return carry

        lax.fori_loop(0, seq // SETUP_ROWS, chunk, 0)

    def k_fn(c, start):
        return ka_scr[c, pl.ds(start, tq), :]

    def vt_fn(c, start):
        return vat_scr[c, :, pl.ds(start, tq)]

    _flash_chains(qi, tq, qh_scr, k_fn, vt_fn, m_scr, acc_scr, s_scr, pm_scr)
    for sl in range(NSL):
        a0 = acc_scr[2 * sl]
        a1 = acc_scr[2 * sl + 1]
        row = lax.broadcasted_iota(jnp.int32, a0.shape, 0)
        ot = jnp.where(row < HEAD_DIM, a0 * (1.0 / a0[HEAD_DIM:HEAD_DIM + 1, :]), a1 * (1.0 / a1[0:1, :]))
        o_ref[:, sl * LANES:(sl + 1) * LANES] = ot.T.astype(o_ref.dtype)


def _fox(qkv4, vt4, c4, wg, wu, wd, batch, seq):
    tq = ATT_T
    groups = FOX_HEADS // N_CHAINS
    nq = seq // tq
    assert batch * groups * nq >= 3 * EXPERT_CAST_STEPS

    def w_spec(k, rows, cols):
        def index(b, g, i):
            step = (b * groups + g) * nq + i - k * EXPERT_CAST_STEPS
            return (jnp.clip(step, 0, EXPERT_CAST_STEPS - 1), 0, 0)
        return pl.BlockSpec((EXPERT_CAST_BLOCK, rows, cols), index)

    w_specs = [w_spec(0, D_MODEL, D_EXPERT), w_spec(1, D_MODEL, D_EXPERT), w_spec(2, D_EXPERT, D_MODEL)]
    return pl.pallas_call(
        _fox_kernel,
        grid=(batch, groups, nq),
        in_specs=[
            pl.BlockSpec((NSL, None, tq, LANES), lambda b, g, i: (SLAB_FOX_Q // NSL + g, b, i, 0)),
            pl.BlockSpec((NSL, None, seq, LANES), lambda b, g, i: (SLAB_FOX_K // NSL + g, b, 0, 0)),
            pl.BlockSpec((NSL, None, LANES, seq), lambda b, g, i: (VT_FOX // NSL + g, b, 0, 0)),
            pl.BlockSpec((NSL, 2, seq), lambda b, g, i: (g, 0, b)),
        ] + w_specs,
        out_specs=[pl.BlockSpec((None, tq, NSL * LANES), lambda b, g, i: (b, i, g))] + w_specs,
        out_shape=[jax.ShapeDtypeStruct((batch, seq, FOX_HEADS * HEAD_DIM), BF16)]
        + [jax.ShapeDtypeStruct(w.shape, BF16) for w in (wg, wu, wd)],
        scratch_shapes=_attn_scratch(tq, seq, LANES, N_CHAINS),
        compiler_params=_cparams(("arbitrary", "arbitrary", "arbitrary"), ATT_FLAGS),
        name="fox",
    )(qkv4, qkv4, vt4, c4, wg, wu, wd)


def _diff_kernel(slopes_ref, q_ref, k_ref, vt_ref, lam_ref, g_ref, wgi_ref, o_ref, wgo_ref, qh_scr, m_scr,
                 acc_scr, s_scr, pm_scr, ka_scr, vat_scr):
    tq = q_ref.shape[1]
    seq = k_ref.shape[1]
    qi = pl.program_id(2)
    _split_q(q_ref, qh_scr)
    g = pl.program_id(1)

    n = (pl.program_id(0) * pl.num_programs(1) + g) * pl.num_programs(2) + qi

    @pl.when(n < GATE_CAST_STEPS)
    def _():
        wgo_ref[...] = wgi_ref[...].astype(BF16)

    @pl.when(qi == 0)
    def _():
        def chunk(i, carry):
            r0 = pl.multiple_of(i * SETUP_ROWS, SETUP_ROWS)
            rows = pl.ds(r0, SETUP_ROWS)
            for hd in range(NSL):
                vt = vt_ref[hd, :, rows]
                vat_scr[hd, :, rows] = jnp.concatenate([vt, jnp.ones((DIFF_ONES_ROWS, SETUP_ROWS), BF16)], axis=0)
                pos = (lax.broadcasted_iota(jnp.int32, (SETUP_ROWS, LANES), 0) + r0).astype(F32)
                bias = pos * (slopes_ref[NSL * g + hd] * LOG2E)
                k = k_ref[hd, rows, :]
                ka_scr[2 * hd, rows, :] = _augment_k(k, bias, 0)
                ka_scr[2 * hd + 1, rows, :] = _augment_k(k, bias, 1)
            return carry

        lax.fori_loop(0, seq // SETUP_ROWS, chunk, 0)

    def k_fn(c, start):
        return ka_scr[c, pl.ds(start, tq), :]

    def vt_fn(c, start):
        return vat_scr[c // 2, :, pl.ds(start, tq)]

    _flash_chains(qi, tq, qh_scr, k_fn, vt_fn, m_scr, acc_scr, s_scr, pm_scr)
    lp = lam_ref[...]
    s1 = jnp.sum(lp[0:1, :] * lp[1:2, :], axis=1, keepdims=True)
    s2 = jnp.sum(lp[2:3, :] * lp[3:4, :], axis=1, keepdims=True)
    lam = jnp.exp(s1) - jnp.exp(s2) + LAM_INIT
    for hd in range(NSL):
        a1 = acc_scr[2 * hd]
        a2 = acc_scr[2 * hd + 1]
        ot = (a1[:LANES, :] * (1.0 / a1[LANES:LANES + 1, :])
              - lam * (a2[:LANES, :] * (1.0 / a2[LANES:LANES + 1, :])))
        ms = jnp.mean(ot * ot, axis=0, keepdims=True)
        yt = ot * lax.rsqrt(ms + LN_EPS) * g_ref[...]
        o_ref[:, hd * LANES:(hd + 1) * LANES] = (yt * (1.0 - LAM_INIT)).T.astype(o_ref.dtype)


def _diff(qkv4, vt4, slopes, lam_params, subln_g_col, w_in_t, batch, seq):
    tq = ATT_T
    groups = DIFF_HEADS // NSL
    nq = seq // tq
    assert batch * groups * nq >= GATE_CAST_STEPS

    def cast_step(b, g, i):
        return jnp.minimum((b * groups + g) * nq + i, GATE_CAST_STEPS - 1)

    grid_spec = pltpu.PrefetchScalarGridSpec(
        num_scalar_prefetch=1,
        grid=(batch, groups, nq),
        in_specs=[
            pl.BlockSpec((NSL, None, tq, LANES), lambda b, g, i, s: (SLAB_DIFF_Q // NSL + g, b, i, 0)),
            pl.BlockSpec((NSL, None, seq, LANES), lambda b, g, i, s: (SLAB_DIFF_K // NSL + g, b, 0, 0)),
            pl.BlockSpec((NSL, None, LANES, seq), lambda b, g, i, s: (VT_DIFF // NSL + g, b, 0, 0)),
            pl.BlockSpec((4, HEAD_DIM), lambda b, g, i, s: (0, 0)),
            pl.BlockSpec((LANES, 1), lambda b, g, i, s: (0, 0)),
            pl.BlockSpec((None, PROJ_TN, D_MODEL),
                         lambda b, g, i, s: (0, GATE_COL0 // PROJ_TN + cast_step(b, g, i), 0)),
        ],
        out_specs=[
            pl.BlockSpec((None, tq, NSL * LANES), lambda b, g, i, s: (b, i, g)),
            pl.BlockSpec((PROJ_TN, D_MODEL), lambda b, g, i, s: (cast_step(b, g, i), 0)),
        ],
        scratch_shapes=_attn_scratch(tq, seq, LANES + DIFF_ONES_ROWS, NSL),
    )
    return pl.pallas_call(
        _diff_kernel,
        grid_spec=grid_spec,
        out_shape=[
            jax.ShapeDtypeStruct((batch, seq, DIFF_HEADS * LANES), BF16),
            jax.ShapeDtypeStruct((N_BRANCH * D_MODEL, D_MODEL), BF16),
        ],
        compiler_params=_cparams(("arbitrary", "arbitrary", "arbitrary"), ATT_FLAGS),
        name="diff",
    )(slopes, qkv4, qkv4, vt4, lam_params, subln_g_col, w_in_t)


def _mem_kernel(q_ref, mem_ref, wk_ref, wv_ref, o_ref):
    memb = mem_ref[...].astype(BF16)
    mk = jnp.dot(memb, wk_ref[...].astype(BF16), preferred_element_type=F32).astype(BF16)
    mv = jnp.dot(memb, wv_ref[...].astype(BF16), preferred_element_type=F32).astype(BF16)
    s = lax.dot_general(q_ref[...], mk, (((1,), (1,)), ((), ())), preferred_element_type=F32)
    m = jnp.max(s, axis=1, keepdims=True)
    p = jnp.exp2(s - m)
    l = jnp.sum(p, axis=1, keepdims=True)
    o = jnp.dot(p.astype(BF16), mv, preferred_element_type=F32)
    o_ref[...] = (o / l).astype(o_ref.dtype)


def _mem_attn(qkv4, mem, w_mem_kv, batch, seq):
    tq = MEM_TQ
    return pl.pallas_call(
        _mem_kernel,
        grid=(batch, MEM_HEADS, seq // tq),
        in_specs=[
            pl.BlockSpec((None, None, tq, LANES), lambda b, h, i: (SLAB_MEM_Q + h, b, i, 0)),
            pl.BlockSpec((None, N_MEM, D_MODEL), lambda b, h, i: (b, 0, 0)),
            pl.BlockSpec((None, D_MODEL, LANES), lambda b, h, i: (0, 0, h)),
            pl.BlockSpec((None, D_MODEL, LANES), lambda b, h, i: (0, 0, MEM_HEADS + h)),
        ],
        out_specs=pl.BlockSpec((None, tq, LANES), lambda b, h, i: (b, i, h)),
        out_shape=jax.ShapeDtypeStruct((batch, seq, MEM_HEADS * LANES), BF16),
        compiler_params=_cparams(("parallel", "parallel", "parallel")),
        name="mem",
    )(qkv4, mem, w_mem_kv, w_mem_kv)


def _layer_norm(z, g, b):
    mu = jnp.mean(z, axis=1, keepdims=True)
    zc = z - mu
    var = jnp.mean(zc * zc, axis=1, keepdims=True)
    return zc * lax.rsqrt(var + LN_EPS) * g + b


def _lane_max(v):
    return jnp.max(v, axis=1, keepdims=True)


def _routing_info(logits):
    lane = lax.broadcasted_iota(jnp.int32, logits.shape, 1)
    big = jnp.int32(2 * LANES)
    is_g = lane < N_GROUPS
    gl = jnp.where(is_g, logits, NEG_BIG)
    gmax = _lane_max(gl)
    g_w = 1.0 / jnp.sum(jnp.exp(gl - gmax), axis=1, keepdims=True)
    g_idx = jnp.min(jnp.where(gl == gmax, lane, big), axis=1, keepdims=True)
    lo = ROUTE_LANE0 + g_idx * EXPERTS_PER_GROUP
    in_grp = (lane >= lo) & (lane < lo + EXPERTS_PER_GROUP)
    el = jnp.where(in_grp, logits, NEG_BIG)
    e1 = _lane_max(el)
    i1 = jnp.min(jnp.where(el == e1, lane, big), axis=1, keepdims=True)
    el2 = jnp.where(lane == i1, NEG_BIG, el)
    e2 = _lane_max(el2)
    i2 = jnp.min(jnp.where(el2 == e2, lane, big), axis=1, keepdims=True)
    r = jnp.exp(e2 - e1)
    w1 = g_w / (1.0 + r)
    w2 = g_w * r / (1.0 + r)
    id1 = (i1 - ROUTE_LANE0).astype(F32)
    id2 = (i2 - ROUTE_LANE0).astype(F32)
    return jnp.where(lane == RI_E1, id1,
                     jnp.where(lane == RI_E2, id2,
                               jnp.where(lane == RI_W1, w1, jnp.where(lane == RI_W2, w2, 0.0))))


def _merge_kernel(x_ref, yf_ref, yd_ref, ym_ref, wg_ref, bg_ref, wbf_ref, wbd_ref, wbm_ref,
                  wo_ref, g1_ref, b1_ref, wr_ref, br_ref, x1_ref, ri_ref, rit_ref):
    x = x_ref[...]
    xb = x.astype(BF16)
    h = None
    for i, (y_ref, wb_ref) in enumerate(((yf_ref, wbf_ref), (yd_ref, wbd_ref), (ym_ref, wbm_ref))):
        cols = slice(i * D_MODEL, (i + 1) * D_MODEL)
        gl = lax.dot_general(xb, wg_ref[cols, :], (((1,), (1,)), ((), ())),
                             preferred_element_type=F32) + bg_ref[:, cols]
        gate = 1.0 / (1.0 + jnp.exp(-gl))
        br = jnp.dot(y_ref[...], wb_ref[...], preferred_element_type=F32)
        h = gate * br if h is None else h + gate * br
    o = jnp.dot(h.astype(BF16), wo_ref[...], preferred_element_type=F32)
    x1 = _layer_norm(DEEPNORM_ALPHA * x + o, g1_ref[...], b1_ref[...])
    x1_ref[...] = x1
    logits = jnp.dot(x1.astype(BF16), wr_ref[...], preferred_element_type=F32) + br_ref[...]
    ri = _routing_info(logits)
    ri_ref[...] = ri
    rit_ref[...] = ri.T[0:8, :]


def _const_spec(shape):
    return pl.BlockSpec(shape, lambda i: (0,) * len(shape), pipeline_mode=pl.Buffered(1))


def _merge(x2, yf, yd, ym, wg, bg, wbf, wbd, wbm, wo, g1, b1, wr, br):
    t = x2.shape[0]
    tm = MERGE_TM
    half = yf.shape[1]
    row = lambda w: pl.BlockSpec((tm, w), lambda i: (i, 0))
    return pl.pallas_call(
        _merge_kernel,
        grid=(t // tm,),
        in_specs=[
            row(D_MODEL), row(half), row(half), row(half),
            _const_spec((N_BRANCH * D_MODEL, D_MODEL)), _const_spec((1, N_BRANCH * D_MODEL)),
            _const_spec((half, D_MODEL)), _const_spec((half, D_MODEL)), _const_spec((half, D_MODEL)),
            _const_spec((D_MODEL, D_MODEL)), _const_spec((1, D_MODEL)), _const_spec((1, D_MODEL)),
            _const_spec((D_MODEL, LANES)), _const_spec((1, LANES)),
        ],
        out_specs=[row(D_MODEL), row(LANES), pl.BlockSpec((8, tm), lambda i: (0, i))],
        out_shape=[
            jax.ShapeDtypeStruct((t, D_MODEL), F32),
            jax.ShapeDtypeStruct((t, LANES), F32),
            jax.ShapeDtypeStruct((8, t), F32),
        ],
        compiler_params=_cparams(("parallel",)),
        name="merge",
    )(x2, yf, yd, ym, wg, bg, wbf, wbd, wbm, wo, g1, b1, wr, br)


def _cumsum(x, axis):
    n = x.shape[axis]
    idx = lax.broadcasted_iota(jnp.int32, x.shape, axis)
    sh = 1
    while sh < n:
        x = x + jnp.where(idx >= sh, pltpu.roll(x, sh, axis=axis), 0.0)
        sh *= 2
    return x


def _route_kernel(rit_ref, pos_ref, tiles_ref):
    s = rit_ref.shape[1]
    e1 = rit_ref[RI_E1:RI_E1 + 1, :]
    e2 = rit_ref[RI_E2:RI_E2 + 1, :]
    sub = lax.broadcasted_iota(jnp.int32, (N_EXPERTS, s), 0).astype(F32)
    oh1 = jnp.where(sub == e1, 1.0, 0.0)
    oh2 = jnp.where(sub == e2, 1.0, 0.0)
    cnt = oh1 + oh2
    incl = _cumsum(cnt, 1)
    before = incl - cnt
    total = jnp.broadcast_to(incl[:, s - 1:s], (N_EXPERTS, LANES))
    aligned = jnp.floor((total + (SEG_ALIGN - 1)) * (1.0 / SEG_ALIGN)) * SEG_ALIGN
    off = _cumsum(aligned, 0) - aligned
    slot = before + off[:, 0:1]
    pos1 = jnp.sum(oh1 * slot, axis=0, keepdims=True)
    pos2 = jnp.sum(oh2 * slot, axis=0, keepdims=True)
    pos_ref[...] = (jnp.concatenate([pos1, pos2], axis=0) * ROW_TILE).astype(jnp.int32)

    nt = jnp.floor((total + (MOE_R - 1)) * (1.0 / MOE_R))
    ct_incl = _cumsum(nt, 0)
    ct_excl = ct_incl - nt
    sub_l = lax.broadcasted_iota(jnp.int32, (N_EXPERTS, LANES), 0).astype(F32)
    tile_i = lax.broadcasted_iota(jnp.int32, (N_EXPERTS, LANES), 1).astype(F32)
    te = jnp.sum(jnp.where(ct_incl <= tile_i, 1.0, 0.0), axis=0, keepdims=True)
    te = jnp.minimum(te, N_EXPERTS - 1.0)
    sel = jnp.where(sub_l == te, 1.0, 0.0)
    k_in = tile_i - ct_excl
    start = jnp.sum(sel * (off + k_in * MOE_R), axis=0, keepdims=True)
    nval = jnp.sum(sel * jnp.clip(total - k_in * MOE_R, 0.0, float(MOE_R)), axis=0, keepdims=True)
    active = tile_i[0:1, :] < ct_incl[N_EXPERTS - 1:N_EXPERTS, :]
    start = jnp.where(active, start, 0.0)
    nval = jnp.where(active, nval, 0.0)
    rows = jnp.concatenate([te, start * ROW_TILE, nval, jnp.zeros((5, LANES), F32)], axis=0)
    tiles_ref[...] = rows.astype(jnp.int32)


def _route(rit, batch, seq):
    return pl.pallas_call(
        _route_kernel,
        grid=(batch,),
        in_specs=[pl.BlockSpec((8, seq), lambda b: (0, b))],
        out_specs=[
            pl.BlockSpec((None, 2, seq), lambda b: (b, 0, 0)),
            pl.BlockSpec((None, 8, LANES), lambda b: (b, 0, 0)),
        ],
        out_shape=[
            jax.ShapeDtypeStruct((batch, 2, seq), jnp.int32),
            jax.ShapeDtypeStruct((batch, 8, LANES), jnp.int32),
        ],
        compiler_params=_cparams(("parallel",)),
        name="route",
    )(rit)


def _read_rows(view, n_rows):
    return jnp.concatenate([view[pl.ds(k, n_rows, stride=ROW_TILE), :] for k in range(ROW_TILE)], axis=1)


def _write_rows(view, n_rows, val):
    for k in range(ROW_TILE):
        view[pl.ds(k, n_rows, stride=ROW_TILE), :] = val[:, k * LANES:(k + 1) * LANES]


def _moe_kernel(nd, nt, texp_ref, tstart_ref, tnval_ref, x1_ref, p1_ref, p2_ref, ri_ref,
                wg_ref, wu_ref, wd_ref, g2_ref, b2_ref, o_ref, buf, xs_scr, as_scr, bs_scr):
    c = pl.program_id(0)
    j = pl.program_id(1)
    td = x1_ref.shape[0]

    def tile_at(ref, off):
        return ref.at[pl.ds(pl.multiple_of(off, ROW_TILE), ROW_TILE), :]

    @pl.when((j == 0) & (c == 0))
    def _():
        buf[...] = jnp.zeros(buf.shape, F32)

    @pl.when(j < nd)
    def _():
        _write_rows(xs_scr, td, x1_ref[...])

        def body(i, carry):
            for u in range(MOE_ROWS_PER_TRIP):
                t = i * MOE_ROWS_PER_TRIP + u
                row = tile_at(xs_scr, t * ROW_TILE)[...]
                tile_at(buf, p1_ref[0, t])[...] = row
                tile_at(buf, p2_ref[0, t])[...] = row
            return carry

        lax.fori_loop(0, td // MOE_ROWS_PER_TRIP, body, 0)

    @pl.when((j >= nd) & (j < nd + nt))
    def _():
        idx = c * nt + (j - nd)
        nval = tnval_ref[idx]

        @pl.when(nval > 0)
        def _():
            start = pl.multiple_of(tstart_ref[idx], ROW_TILE * SEG_ALIGN)
            view = buf.at[pl.ds(start, MOE_R * ROW_TILE), :]
            xt = _read_rows(view, MOE_R)
            xb = xt.astype(BF16)
            g = jnp.dot(xb, wg_ref[...], preferred_element_type=F32)
            u = jnp.dot(xb, wu_ref[...], preferred_element_type=F32)
            hcat = (g / (1.0 + jnp.exp(-g))) * u
            y = jnp.dot(hcat.astype(BF16), wd_ref[...], preferred_element_type=F32)
            rows = lax.broadcasted_iota(jnp.int32, (MOE_R, 1), 0)
            _write_rows(view, MOE_R, jnp.where(rows < nval, y, xt))

    @pl.when(j >= nd + nt)
    def _():
        def body(i, carry):
            for u in range(MOE_ROWS_PER_TRIP):
                t = i * MOE_ROWS_PER_TRIP + u
                tile_at(as_scr, t * ROW_TILE)[...] = tile_at(buf, p1_ref[0, t])[...]
                tile_at(bs_scr, t * ROW_TILE)[...] = tile_at(buf, p2_ref[0, t])[...]
            return carry

        lax.fori_loop(0, td // MOE_ROWS_PER_TRIP, body, 0)
        ri = ri_ref[...]
        m = ri[:, RI_W1:RI_W1 + 1] * _read_rows(as_scr, td) + ri[:, RI_W2:RI_W2 + 1] * _read_rows(bs_scr, td)
        z = DEEPNORM_ALPHA * x1_ref[...] + m
        o_ref[...] = _layer_norm(z, g2_ref[...], b2_ref[...])


def _moe(x1, pos3, ri, texp, tstart, tnval, wg, wu, wd, g2, b2, batch, seq):
    t = x1.shape[0]
    td = MOE_TD
    nd = seq // td
    nt = 2 * seq // MOE_R + N_EXPERTS
    buf_rows = 2 * seq + N_EXPERTS * SEG_ALIGN + MOE_R

    def tok_tile(j):
        return jnp.where(j < nd, j, jnp.maximum(j - nd - nt, 0))

    def tile_expert(c, j, texp_ref):
        return texp_ref[c * nt + jnp.clip(j - nd, 0, nt - 1)]

    tok_spec = lambda w: pl.BlockSpec((td, w), lambda c, j, *_: (c * nd + tok_tile(j), 0))
    pos_spec = lambda k: pl.BlockSpec((None, 1, td), lambda c, j, *_: ((c * 2 + k) * nd + tok_tile(j), 0, 0),
                                      memory_space=pltpu.SMEM)
    w_spec = lambda a, b_: pl.BlockSpec((None, a, b_), lambda c, j, te, *_: (tile_expert(c, j, te), 0, 0))
    vec_spec = pl.BlockSpec((1, D_MODEL), lambda c, j, *_: (0, 0))
    grid_spec = pltpu.PrefetchScalarGridSpec(
        num_scalar_prefetch=3,
        grid=(batch, nd + nt + nd),
        in_specs=[
            tok_spec(D_MODEL), pos_spec(0), pos_spec(1), tok_spec(LANES),
            w_spec(D_MODEL, D_EXPERT), w_spec(D_MODEL, D_EXPERT), w_spec(D_EXPERT, D_MODEL),
            vec_spec, vec_spec,
        ],
        out_specs=pl.BlockSpec((td, D_MODEL),
                               lambda c, j, *_: (c * nd + jnp.maximum(j - nd - nt, 0), 0)),
        scratch_shapes=[
            pltpu.VMEM((buf_rows * ROW_TILE, LANES), F32),
            pltpu.VMEM((td * ROW_TILE, LANES), F32),
            pltpu.VMEM((td * ROW_TILE, LANES), F32),
            pltpu.VMEM((td * ROW_TILE, LANES), F32),
        ],
    )
    return pl.pallas_call(
        functools.partial(_moe_kernel, nd, nt),
        grid_spec=grid_spec,
        out_shape=jax.ShapeDtypeStruct((t, D_MODEL), F32),
        compiler_params=_cparams(("arbitrary", "arbitrary")),
        name="moe",
    )(texp, tstart, tnval, x1, pos3, pos3, ri, wg, wu, wd, g2, b2)


def _qkv_col_scale():
    ones = lambda n: jnp.ones((n,), F32)
    att = FOX_HEADS * HEAD_DIM
    qs = lambda d: jnp.full((att,), LOG2E * d ** -0.5, F32)
    return jnp.concatenate([qs(HEAD_DIM), ones(2 * att), qs(HEAD_DIM), ones(2 * att),
                            qs(MEM_HEAD_DIM)]).reshape(1, QKV_COLS)


def kernel(x, mem, w_in, b_forget, b_gates, lambda_q1, lambda_k1, lambda_q2, lambda_k2, diff_subln_g,
           w_mem_kv, w_branch_fox, w_branch_diff, w_branch_mem, w_out, ln1_g, ln1_b, w_router_group,
           b_router_group, w_router_expert, b_router_expert, w_expert_gate, w_expert_up, w_expert_down,
           ln2_g, ln2_b):
    batch, seq, d = x.shape
    t = batch * seq
    l = 0
    x2 = x.reshape(t, d)

    w_in_t = jnp.swapaxes(w_in, 1, 2)
    w_fl_t = jnp.pad(w_in_t[l, FL_COL0:, :], ((0, 16 - FOX_HEADS), (0, 0))).astype(BF16)
    w_r = jnp.concatenate([w_router_group[l], w_router_expert[l]], axis=1)
    w_r = jnp.pad(w_r, ((0, 0), (0, LANES - w_r.shape[1]))).astype(BF16)
    b_r = jnp.concatenate([b_router_group[l], b_router_expert[l]])
    b_r = jnp.pad(b_r, (0, LANES - b_r.shape[0])).reshape(1, LANES)
    lam_params = jnp.stack([lambda_q1[l], lambda_k1[l], lambda_q2[l], lambda_k2[l]])
    slopes = 2.0 ** (-8.0 * jnp.arange(1, DIFF_HEADS + 1, dtype=F32) / DIFF_HEADS)

    qkv, vt4, fl_t = _proj(x2, w_in_t, w_fl_t, _qkv_col_scale(), batch, seq)
    qkv4 = qkv.reshape(N_SLABS, batch, seq, LANES)
    c = _fscan(fl_t, b_forget[l].reshape(FOX_HEADS, 1), batch, seq)
    c4 = c.reshape(FOX_HEADS // 2, 2, t)

    y_fox, wg_b, wu_b, wd_b = _fox(qkv4, vt4, c4, w_expert_gate[l], w_expert_up[l], w_expert_down[l],
                                   batch, seq)
    y_diff, w_gates = _diff(qkv4, vt4, slopes, lam_params, diff_subln_g[l].reshape(LANES, 1), w_in_t,
                            batch, seq)
    y_mem = _mem_attn(qkv4, mem, w_mem_kv, batch, seq)

    x1, ri, rit = _merge(
        x2, y_fox.reshape(t, -1), y_diff.reshape(t, -1), y_mem.reshape(t, -1),
        w_gates, b_gates[l].reshape(1, -1),
        w_branch_fox[l].astype(BF16), w_branch_diff[l].astype(BF16), w_branch_mem[l].astype(BF16),
        w_out[l].astype(BF16), ln1_g[l].reshape(1, d), ln1_b[l].reshape(1, d), w_r, b_r)

    pos, tiles = _route(rit, batch, seq)
    nt = 2 * seq // MOE_R + N_EXPERTS
    texp, tstart, tnval = (tiles[:, r, :nt].reshape(-1) for r in range(3))
    pos3 = pos.reshape(batch * 2 * (seq // MOE_TD), 1, MOE_TD)
    out = _moe(x1, pos3, ri, texp, tstart, tnval,
               wg_b, wu_b, wd_b,
               ln2_g[l].reshape(1, d), ln2_b[l].reshape(1, d), batch, seq)
    return out.reshape(batch, seq, d)
```

```python
import functools
import math

import jax
import jax.numpy as jnp
from jax import lax
from jax.experimental import pallas as pl
from jax.experimental.pallas import tpu as pltpu

F32 = jnp.float32
BF16 = jnp.bfloat16

D_MODEL = 1024
HEAD_DIM = 64
FOX_HEADS = 8
DIFF_HEADS = 4
MEM_HEADS = 4
MEM_HEAD_DIM = 128
N_MEM = 256
N_BRANCH = 3
N_GROUPS = 4
EXPERTS_PER_GROUP = 8
N_EXPERTS = N_GROUPS * EXPERTS_PER_GROUP
D_EXPERT = 256
LN_EPS = 1e-5
DEPTH = 1
DEEPNORM_ALPHA = (2.0 * DEPTH) ** 0.25
LAM_INIT = 0.8 - 0.6 * math.exp(-0.3 * 0)
LOG2E = math.log2(math.e)

LANES = 128
QKV_COLS = 3584
N_SLABS = QKV_COLS // LANES
GATE_COL0 = QKV_COLS
FL_COL0 = QKV_COLS + N_BRANCH * D_MODEL
SLAB_FOX_Q, SLAB_FOX_K, SLAB_FOX_V = 0, 4, 8
SLAB_DIFF_Q, SLAB_DIFF_K, SLAB_DIFF_V = 12, 16, 20
SLAB_MEM_Q = 24
ROUTE_LANE0 = N_GROUPS

NEG_BIG = -1e30
VMEM_LIMIT = 56 * 1024 * 1024

PROJ_TM = 2048
PROJ_TN = 512
PROJ_J_FOX_V = SLAB_FOX_V * LANES // PROJ_TN
PROJ_J_DIFF_V = SLAB_DIFF_V * LANES // PROJ_TN
VT_FOX, VT_DIFF = 0, PROJ_TN // LANES
GATE_CAST_STEPS = N_BRANCH * D_MODEL // PROJ_TN
ATT_T = 256
ATT_UNROLL = 8
N_CHAINS = 4
NSL = N_CHAINS // 2
N_BIAS_LANES = 3
DIFF_ONES_ROWS = 16
SETUP_ROWS = 512
MEM_TQ = 4096
MERGE_TM = 1024
MOE_R = 256
MOE_TD = 256
SEG_ALIGN = 8
ROW_TILE = D_MODEL // LANES
RI_E1, RI_E2, RI_W1, RI_W2 = 0, 1, 2, 3


def _cparams(sem, flags=None):
    return pltpu.CompilerParams(dimension_semantics=sem, vmem_limit_bytes=VMEM_LIMIT, flags=flags)


ATT_FLAGS = None


def _proj_kernel(x_ref, w_ref, wfl_ref, sc_ref, o_ref, vt_ref, fl_ref, xb_ref):
    j = pl.program_id(1)
    is_v = (j == PROJ_J_FOX_V) | (j == PROJ_J_DIFF_V)
    spc = PROJ_TN // LANES

    @pl.when(j == 0)
    def _():
        xb = x_ref[...].astype(BF16)
        xb_ref[...] = xb
        fl_ref[...] = lax.dot_general(wfl_ref[...], xb, (((1,), (1,)), ((), ())),
                                      preferred_element_type=F32)

    @pl.when(jnp.logical_not(is_v))
    def _():
        acc = lax.dot_general(xb_ref[...], w_ref[...].astype(BF16), (((1,), (1,)), ((), ())),
                              preferred_element_type=F32) * sc_ref[...]
        for c in range(spc):
            o_ref[c] = acc[:, c * LANES:(c + 1) * LANES].astype(BF16)

    @pl.when(is_v)
    def _():
        acc_t = lax.dot_general(w_ref[...].astype(BF16), xb_ref[...], (((1,), (1,)), ((), ())),
                                preferred_element_type=F32)
        for c in range(spc):
            vt_ref[c] = acc_t[c * LANES:(c + 1) * LANES, :].astype(BF16)
        o_ref[...] = jnp.zeros(o_ref.shape, BF16)


def _proj(x2, w_in_t, w_fl_t, col_scale, batch, seq):
    t = x2.shape[0]
    spc = PROJ_TN // LANES
    per_seq = seq // PROJ_TM
    return pl.pallas_call(
        _proj_kernel,
        grid=(t // PROJ_TM, QKV_COLS // PROJ_TN),
        in_specs=[
            pl.BlockSpec((PROJ_TM, D_MODEL), lambda i, j: (i, 0)),
            pl.BlockSpec((None, PROJ_TN, D_MODEL), lambda i, j: (0, j, 0)),
            pl.BlockSpec((16, D_MODEL), lambda i, j: (0, 0)),
            pl.BlockSpec((1, PROJ_TN), lambda i, j: (0, j)),
        ],
        out_specs=[
            pl.BlockSpec((spc, PROJ_TM, LANES), lambda i, j: (j, i, 0)),
            pl.BlockSpec((spc, None, LANES, PROJ_TM),
                         lambda i, j: (jnp.where(j < PROJ_J_DIFF_V, 0, 1), i // per_seq, 0, i % per_seq)),
            pl.BlockSpec((16, PROJ_TM), lambda i, j: (0, i)),
        ],
        out_shape=[
            jax.ShapeDtypeStruct((N_SLABS, t, LANES), BF16),
            jax.ShapeDtypeStruct((2 * spc, batch, LANES, seq), BF16),
            jax.ShapeDtypeStruct((16, t), F32),
        ],
        scratch_shapes=[pltpu.VMEM((PROJ_TM, D_MODEL), BF16)],
        compiler_params=_cparams(("parallel", "arbitrary")),
        name="proj",
    )(x2, w_in_t, w_fl_t, col_scale)


def _fscan_kernel(fl_ref, b_ref, c_ref):
    s = fl_ref.shape[1]
    z = fl_ref[0:FOX_HEADS, :] + b_ref[...]
    lf = jnp.minimum(z, 0.0) - jnp.log(1.0 + jnp.exp(-jnp.abs(z)))
    lane = lax.broadcasted_iota(jnp.int32, lf.shape, 1)
    sh = 1
    while sh < s:
        r = pltpu.roll(lf, sh, axis=1)
        lf = lf + jnp.where(lane >= sh, r, 0.0)
        sh *= 2
    c_ref[...] = lf * LOG2E


def _fscan(fl_t, b_forget_col, batch, seq):
    t = fl_t.shape[1]
    return pl.pallas_call(
        _fscan_kernel,
        grid=(batch,),
        in_specs=[
            pl.BlockSpec((16, seq), lambda b: (0, b)),
            pl.BlockSpec((FOX_HEADS, 1), lambda b: (0, 0)),
        ],
        out_specs=pl.BlockSpec((FOX_HEADS, seq), lambda b: (0, b)),
        out_shape=jax.ShapeDtypeStruct((FOX_HEADS, t), F32),
        compiler_params=_cparams(("parallel",)),
        name="fscan",
    )(fl_t, b_forget_col)


def _flash_chains(qi, tq, qh_scr, k_fn, vt_fn, m_scr, acc_scr, s_scr, pm_scr):
    m_scr[...] = jnp.full(m_scr.shape, NEG_BIG, F32)
    acc_scr[...] = jnp.zeros(acc_scr.shape, F32)

    def scores(j, slot, masked):
        start = pl.multiple_of(j * tq, tq)
        for c in range(N_CHAINS):
            s = lax.dot_general(k_fn(c, start), qh_scr[c], (((1,), (1,)), ((), ())),
                                preferred_element_type=F32)
            if masked:
                kv = lax.broadcasted_iota(jnp.int32, (tq, tq), 0)
                q = lax.broadcasted_iota(jnp.int32, (tq, tq), 1)
                s = jnp.where(kv <= q, s, NEG_BIG)
            s_scr[slot, c] = s
            pm_scr[slot, c] = jnp.max(s.reshape(tq // 8, 8, tq), axis=0)

    def consume(j, slot):
        start = pl.multiple_of(j * tq, tq)
        for c in range(N_CHAINS):
            m_prev = m_scr[c]
            m_new = jnp.maximum(m_prev, jnp.max(pm_scr[slot, c], axis=0, keepdims=True))
            alpha = jnp.exp2(m_prev - m_new)
            p = jnp.exp2(s_scr[slot, c] - m_new)
            m_scr[c] = m_new
            pv = jnp.dot(vt_fn(c, start), p.astype(BF16), preferred_element_type=F32)
            acc_scr[c] = acc_scr[c] * alpha + pv

    @pl.when(qi == 0)
    def _():
        scores(0, 0, True)
        consume(0, 0)

    def run(j0, count, last_masked, issue_beyond):
        for t in range(count):
            if t + 1 < count or issue_beyond:
                scores(j0 + t + 1, (t + 1) % 2, last_masked and t + 1 == count - 1)
            consume(j0 + t, t % 2)

    @pl.when(qi > 0)
    def _():
        scores(0, 0, False)
        groups = (qi - 1) // ATT_UNROLL

        def body(g, carry):
            run(g * ATT_UNROLL, ATT_UNROLL, False, True)
            return carry

        lax.fori_loop(0, groups, body, 0)
        j0 = groups * ATT_UNROLL
        for rem in range(1, ATT_UNROLL + 1):
            @pl.when(qi - j0 == rem)
            def _(rem=rem):
                run(j0, rem + 1, True, False)


def _own_and_spare(shape, c):
    lane = lax.broadcasted_iota(jnp.int32, shape, 1)
    own = (lane < HEAD_DIM) if c % 2 == 0 else (lane >= HEAD_DIM)
    spare = HEAD_DIM if c % 2 == 0 else 0
    return lane, own, spare


def _split_q(q_ref, qh_scr):
    for c in range(N_CHAINS):
        q = q_ref[c // 2].astype(F32)
        lane, own, spare = _own_and_spare(q.shape, c)
        ones = (lane >= spare) & (lane < spare + N_BIAS_LANES)
        qh_scr[c] = jnp.where(own, q, jnp.where(ones, 1.0, 0.0)).astype(BF16)


def _augment_k_pair(k, bias):
    hi = bias.astype(BF16).astype(F32)
    r1 = bias - hi
    mid = r1.astype(BF16).astype(F32)
    lo = (r1 - mid).astype(BF16).astype(F32)
    lane = lax.broadcasted_iota(jnp.int32, k.shape, 1)
    half_lane = lane & (HEAD_DIM - 1)
    pieces = jnp.where(half_lane == 0, hi, jnp.where(half_lane == 1, mid, lo))
    kf = k.astype(F32)
    out = []
    for c in range(2):
        _, own, spare = _own_and_spare(k.shape, c)
        bias_lanes = (lane >= spare) & (lane < spare + N_BIAS_LANES)
        out.append(jnp.where(own, kf, jnp.where(bias_lanes, pieces, 0.0)).astype(BF16))
    return out


def _transpose_bf16(a):
    return a.astype(F32).T.astype(BF16)


def _attn_scratch(tq, seq, acc_rows, vt_slabs):
    return [
        pltpu.VMEM((N_CHAINS, tq, LANES), BF16),
        pltpu.VMEM((N_CHAINS, 1, tq), F32),
        pltpu.VMEM((N_CHAINS, acc_rows, tq), F32),
        pltpu.VMEM((2, N_CHAINS, tq, tq), F32),
        pltpu.VMEM((2, N_CHAINS, 8, tq), F32),
        pltpu.VMEM((N_CHAINS, seq, LANES), BF16),
        pltpu.VMEM((vt_slabs, acc_rows, seq), BF16),
    ]


def _fox_kernel(q_ref, k_ref, vt_ref, c_ref, wg_ref, wu_ref, wd_ref, o_ref, wgb_ref, wub_ref, wdb_ref,
                qh_scr, m_scr, acc_scr, s_scr, pm_scr, ka_scr, vat_scr):
    tq = q_ref.shape[1]
    seq = k_ref.shape[1]
    qi = pl.program_id(2)
    _split_q(q_ref, qh_scr)

    n = (pl.program_id(0) * pl.num_programs(1) + pl.program_id(1)) * pl.num_programs(2) + qi
    for k, (src, dst) in enumerate(((wg_ref, wgb_ref), (wu_ref, wub_ref), (wd_ref, wdb_ref))):
        @pl.when((n >= k * N_EXPERTS) & (n < (k + 1) * N_EXPERTS))
        def _(src=src, dst=dst):
            dst[...] = src[...].astype(BF16)

    @pl.when(qi == 0)
    def _():
        def chunk(i, carry):
            r0 = pl.multiple_of(i * SETUP_ROWS, SETUP_ROWS)
            rows = pl.ds(r0, SETUP_ROWS)
            head_row = lax.broadcasted_iota(jnp.int32, (LANES, SETUP_ROWS), 0) < HEAD_DIM
            for sl in range(NSL):
                vt = vt_ref[sl, :, rows]
                one = jnp.ones_like(vt)
                vat_scr[2 * sl, :, rows] = jnp.where(head_row, vt, one)
                vat_scr[2 * sl + 1, :, rows] = jnp.where(head_row, one, vt)
                cc = c_ref[sl, :, rows]
                stacked = jnp.where(head_row, jnp.broadcast_to(cc[1:2, :], (LANES, SETUP_ROWS)),
                                    jnp.broadcast_to(cc[0:1, :], (LANES, SETUP_ROWS)))
                bias = -stacked.T
                k = k_ref[sl, rows, :]
                ka_scr[2 * sl, rows, :], ka_scr[2 * sl + 1, rows, :] = _augment_k_pair(k, bias)
            return carry

        lax.fori_loop(0, seq // SETUP_ROWS, chunk, 0)

    def k_fn(c, start):
        return ka_scr[c, pl.ds(start, tq), :]

    def vt_fn(c, start):
        return vat_scr[c, :, pl.ds(start, tq)]

    _flash_chains(qi, tq, qh_scr, k_fn, vt_fn, m_scr, acc_scr, s_scr, pm_scr)
    for sl in range(NSL):
        a0 = acc_scr[2 * sl]
        a1 = acc_scr[2 * sl + 1]
        row = lax.broadcasted_iota(jnp.int32, a0.shape, 0)
        ot = jnp.where(row < HEAD_DIM, a0 * (1.0 / a0[HEAD_DIM:HEAD_DIM + 1, :]), a1 * (1.0 / a1[0:1, :]))
        o_ref[:, sl * LANES:(sl + 1) * LANES] = ot.T.astype(o_ref.dtype)


def _fox(qkv4, vt4, c4, wg, wu, wd, batch, seq):
    tq = ATT_T
    groups = FOX_HEADS // N_CHAINS
    nq = seq // tq
    assert batch * groups * nq >= 3 * N_EXPERTS

    def w_spec(k, rows, cols):
        def index(b, g, i):
            return (jnp.clip((b * groups + g) * nq + i - k * N_EXPERTS, 0, N_EXPERTS - 1), 0, 0)
        return pl.BlockSpec((None, rows, cols), index)

    w_specs = [w_spec(0, D_MODEL, D_EXPERT), w_spec(1, D_MODEL, D_EXPERT), w_spec(2, D_EXPERT, D_MODEL)]
    return pl.pallas_call(
        _fox_kernel,
        grid=(batch, groups, nq),
        in_specs=[
            pl.BlockSpec((NSL, None, tq, LANES), lambda b, g, i: (SLAB_FOX_Q // NSL + g, b, i, 0)),
            pl.BlockSpec((NSL, None, seq, LANES), lambda b, g, i: (SLAB_FOX_K // NSL + g, b, 0, 0)),
            pl.BlockSpec((NSL, None, LANES, seq), lambda b, g, i: (VT_FOX // NSL + g, b, 0, 0)),
            pl.BlockSpec((NSL, 2, seq), lambda b, g, i: (g, 0, b)),
        ] + w_specs,
        out_specs=[pl.BlockSpec((None, tq, NSL * LANES), lambda b, g, i: (b, i, g))] + w_specs,
        out_shape=[jax.ShapeDtypeStruct((batch, seq, FOX_HEADS * HEAD_DIM), BF16)]
        + [jax.ShapeDtypeStruct(w.shape, BF16) for w in (wg, wu, wd)],
        scratch_shapes=_attn_scratch(tq, seq, LANES, N_CHAINS),
        compiler_params=_cparams(("arbitrary", "arbitrary", "arbitrary"), ATT_FLAGS),
        name="fox",
    )(qkv4, qkv4, vt4, c4, wg, wu, wd)


def _diff_kernel(slopes_ref, q_ref, k_ref, vt_ref, lam_ref, g_ref, wgi_ref, o_ref, wgo_ref, qh_scr, m_scr,
                 acc_scr, s_scr, pm_scr, ka_scr, vat_scr):
    tq = q_ref.shape[1]
    seq = k_ref.shape[1]
    qi = pl.program_id(2)
    _split_q(q_ref, qh_scr)
    g = pl.program_id(1)

    n = (pl.program_id(0) * pl.num_programs(1) + g) * pl.num_programs(2) + qi

    @pl.when(n < GATE_CAST_STEPS)
    def _():
        wgo_ref[...] = wgi_ref[...].astype(BF16)

    @pl.when(qi == 0)
    def _():
        def chunk(i, carry):
            r0 = pl.multiple_of(i * SETUP_ROWS, SETUP_ROWS)
            rows = pl.ds(r0, SETUP_ROWS)
            for hd in range(NSL):
                vt = vt_ref[hd, :, rows]
                vat_scr[hd, :, rows] = jnp.concatenate([vt, jnp.ones((DIFF_ONES_ROWS, SETUP_ROWS), BF16)], axis=0)
                pos = (lax.broadcasted_iota(jnp.int32, (SETUP_ROWS, LANES), 0) + r0).astype(F32)
                bias = pos * (slopes_ref[NSL * g + hd] * LOG2E)
                k = k_ref[hd, rows, :]
                ka_scr[2 * hd, rows, :], ka_scr[2 * hd + 1, rows, :] = _augment_k_pair(k, bias)
            return carry

        lax.fori_loop(0, seq // SETUP_ROWS, chunk, 0)

    def k_fn(c, start):
        return ka_scr[c, pl.ds(start, tq), :]

    def vt_fn(c, start):
        return vat_scr[c // 2, :, pl.ds(start, tq)]

    _flash_chains(qi, tq, qh_scr, k_fn, vt_fn, m_scr, acc_scr, s_scr, pm_scr)
    lp = lam_ref[...]
    s1 = jnp.sum(lp[0:1, :] * lp[1:2, :], axis=1, keepdims=True)
    s2 = jnp.sum(lp[2:3, :] * lp[3:4, :], axis=1, keepdims=True)
    lam = jnp.exp(s1) - jnp.exp(s2) + LAM_INIT
    for hd in range(NSL):
        a1 = acc_scr[2 * hd]
        a2 = acc_scr[2 * hd + 1]
        ot = (a1[:LANES, :] * (1.0 / a1[LANES:LANES + 1, :])
              - lam * (a2[:LANES, :] * (1.0 / a2[LANES:LANES + 1, :])))
        ms = jnp.mean(ot * ot, axis=0, keepdims=True)
        yt = ot * lax.rsqrt(ms + LN_EPS) * g_ref[...]
        o_ref[:, hd * LANES:(hd + 1) * LANES] = (yt * (1.0 - LAM_INIT)).T.astype(o_ref.dtype)


def _diff(qkv4, vt4, slopes, lam_params, subln_g_col, w_in_t, batch, seq):
    tq = ATT_T
    groups = DIFF_HEADS // NSL
    nq = seq // tq
    assert batch * groups * nq >= GATE_CAST_STEPS

    def cast_step(b, g, i):
        return jnp.minimum((b * groups + g) * nq + i, GATE_CAST_STEPS - 1)

    grid_spec = pltpu.PrefetchScalarGridSpec(
        num_scalar_prefetch=1,
        grid=(batch, groups, nq),
        in_specs=[
            pl.BlockSpec((NSL, None, tq, LANES), lambda b, g, i, s: (SLAB_DIFF_Q // NSL + g, b, i, 0)),
            pl.BlockSpec((NSL, None, seq, LANES), lambda b, g, i, s: (SLAB_DIFF_K // NSL + g, b, 0, 0)),
            pl.BlockSpec((NSL, None, LANES, seq), lambda b, g, i, s: (VT_DIFF // NSL + g, b, 0, 0)),
            pl.BlockSpec((4, HEAD_DIM), lambda b, g, i, s: (0, 0)),
            pl.BlockSpec((LANES, 1), lambda b, g, i, s: (0, 0)),
            pl.BlockSpec((None, PROJ_TN, D_MODEL),
                         lambda b, g, i, s: (0, GATE_COL0 // PROJ_TN + cast_step(b, g, i), 0)),
        ],
        out_specs=[
            pl.BlockSpec((None, tq, NSL * LANES), lambda b, g, i, s: (b, i, g)),
            pl.BlockSpec((PROJ_TN, D_MODEL), lambda b, g, i, s: (cast_step(b, g, i), 0)),
        ],
        scratch_shapes=_attn_scratch(tq, seq, LANES + DIFF_ONES_ROWS, NSL),
    )
    return pl.pallas_call(
        _diff_kernel,
        grid_spec=grid_spec,
        out_shape=[
            jax.ShapeDtypeStruct((batch, seq, DIFF_HEADS * LANES), BF16),
            jax.ShapeDtypeStruct((N_BRANCH * D_MODEL, D_MODEL), BF16),
        ],
        compiler_params=_cparams(("arbitrary", "arbitrary", "arbitrary"), ATT_FLAGS),
        name="diff",
    )(slopes, qkv4, qkv4, vt4, lam_params, subln_g_col, w_in_t)


def _mem_kernel(q_ref, mem_ref, wk_ref, wv_ref, o_ref):
    memb = mem_ref[...].astype(BF16)
    mk = jnp.dot(memb, wk_ref[...].astype(BF16), preferred_element_type=F32).astype(BF16)
    mv = jnp.dot(memb, wv_ref[...].astype(BF16), preferred_element_type=F32).astype(BF16)
    s = lax.dot_general(q_ref[...], mk, (((1,), (1,)), ((), ())), preferred_element_type=F32)
    m = jnp.max(s, axis=1, keepdims=True)
    p = jnp.exp2(s - m)
    l = jnp.sum(p, axis=1, keepdims=True)
    o = jnp.dot(p.astype(BF16), mv, preferred_element_type=F32)
    o_ref[...] = (o / l).astype(o_ref.dtype)


def _mem_attn(qkv4, mem, w_mem_kv, batch, seq):
    tq = MEM_TQ
    return pl.pallas_call(
        _mem_kernel,
        grid=(batch, MEM_HEADS, seq // tq),
        in_specs=[
            pl.BlockSpec((None, None, tq, LANES), lambda b, h, i: (SLAB_MEM_Q + h, b, i, 0)),
            pl.BlockSpec((None, N_MEM, D_MODEL), lambda b, h, i: (b, 0, 0)),
            pl.BlockSpec((None, D_MODEL, LANES), lambda b, h, i: (0, 0, h)),
            pl.BlockSpec((None, D_MODEL, LANES), lambda b, h, i: (0, 0, MEM_HEADS + h)),
        ],
        out_specs=pl.BlockSpec((None, tq, LANES), lambda b, h, i: (b, i, h)),
        out_shape=jax.ShapeDtypeStruct((batch, seq, MEM_HEADS * LANES), BF16),
        compiler_params=_cparams(("parallel", "parallel", "parallel")),
        name="mem",
    )(qkv4, mem, w_mem_kv, w_mem_kv)


def _layer_norm(z, g, b):
    mu = jnp.mean(z, axis=1, keepdims=True)
    zc = z - mu
    var = jnp.mean(zc * zc, axis=1, keepdims=True)
    return zc * lax.rsqrt(var + LN_EPS) * g + b


def _lane_max(v):
    return jnp.max(v, axis=1, keepdims=True)


def _routing_info(logits):
    lane = lax.broadcasted_iota(jnp.int32, logits.shape, 1)
    big = jnp.int32(2 * LANES)
    is_g = lane < N_GROUPS
    gl = jnp.where(is_g, logits, NEG_BIG)
    gmax = _lane_max(gl)
    g_w = 1.0 / jnp.sum(jnp.exp(gl - gmax), axis=1, keepdims=True)
    g_idx = jnp.min(jnp.where(gl == gmax, lane, big), axis=1, keepdims=True)
    lo = ROUTE_LANE0 + g_idx * EXPERTS_PER_GROUP
    in_grp = (lane >= lo) & (lane < lo + EXPERTS_PER_GROUP)
    el = jnp.where(in_grp, logits, NEG_BIG)
    e1 = _lane_max(el)
    i1 = jnp.min(jnp.where(el == e1, lane, big), axis=1, keepdims=True)
    el2 = jnp.where(lane == i1, NEG_BIG, el)
    e2 = _lane_max(el2)
    i2 = jnp.min(jnp.where(el2 == e2, lane, big), axis=1, keepdims=True)
    r = jnp.exp(e2 - e1)
    w1 = g_w / (1.0 + r)
    w2 = g_w * r / (1.0 + r)
    id1 = (i1 - ROUTE_LANE0).astype(F32)
    id2 = (i2 - ROUTE_LANE0).astype(F32)
    return jnp.where(lane == RI_E1, id1,
                     jnp.where(lane == RI_E2, id2,
                               jnp.where(lane == RI_W1, w1, jnp.where(lane == RI_W2, w2, 0.0))))


def _merge_kernel(x_ref, yf_ref, yd_ref, ym_ref, wg_ref, bg_ref, wbf_ref, wbd_ref, wbm_ref,
                  wo_ref, g1_ref, b1_ref, wr_ref, br_ref, x1_ref, ri_ref, rit_ref):
    x = x_ref[...]
    xb = x.astype(BF16)
    h = None
    for i, (y_ref, wb_ref) in enumerate(((yf_ref, wbf_ref), (yd_ref, wbd_ref), (ym_ref, wbm_ref))):
        cols = slice(i * D_MODEL, (i + 1) * D_MODEL)
        gl = lax.dot_general(xb, wg_ref[cols, :], (((1,), (1,)), ((), ())),
                             preferred_element_type=F32) + bg_ref[:, cols]
        gate = 1.0 / (1.0 + jnp.exp(-gl))
        br = jnp.dot(y_ref[...], wb_ref[...], preferred_element_type=F32)
        h = gate * br if h is None else h + gate * br
    o = jnp.dot(h.astype(BF16), wo_ref[...], preferred_element_type=F32)
    x1 = _layer_norm(DEEPNORM_ALPHA * x + o, g1_ref[...], b1_ref[...])
    x1_ref[...] = x1
    logits = jnp.dot(x1.astype(BF16), wr_ref[...], preferred_element_type=F32) + br_ref[...]
    ri = _routing_info(logits)
    ri_ref[...] = ri
    rit_ref[...] = ri.T[0:8, :]


def _const_spec(shape):
    return pl.BlockSpec(shape, lambda i: (0,) * len(shape), pipeline_mode=pl.Buffered(1))


def _merge(x2, yf, yd, ym, wg, bg, wbf, wbd, wbm, wo, g1, b1, wr, br):
    t = x2.shape[0]
    tm = MERGE_TM
    half = yf.shape[1]
    row = lambda w: pl.BlockSpec((tm, w), lambda i: (i, 0))
    return pl.pallas_call(
        _merge_kernel,
        grid=(t // tm,),
        in_specs=[
            row(D_MODEL), row(half), row(half), row(half),
            _const_spec((N_BRANCH * D_MODEL, D_MODEL)), _const_spec((1, N_BRANCH * D_MODEL)),
            _const_spec((half, D_MODEL)), _const_spec((half, D_MODEL)), _const_spec((half, D_MODEL)),
            _const_spec((D_MODEL, D_MODEL)), _const_spec((1, D_MODEL)), _const_spec((1, D_MODEL)),
            _const_spec((D_MODEL, LANES)), _const_spec((1, LANES)),
        ],
        out_specs=[row(D_MODEL), row(LANES), pl.BlockSpec((8, tm), lambda i: (0, i))],
        out_shape=[
            jax.ShapeDtypeStruct((t, D_MODEL), F32),
            jax.ShapeDtypeStruct((t, LANES), F32),
            jax.ShapeDtypeStruct((8, t), F32),
        ],
        compiler_params=_cparams(("parallel",)),
        name="merge",
    )(x2, yf, yd, ym, wg, bg, wbf, wbd, wbm, wo, g1, b1, wr, br)


def _cumsum(x, axis):
    n = x.shape[axis]
    idx = lax.broadcasted_iota(jnp.int32, x.shape, axis)
    sh = 1
    while sh < n:
        x = x + jnp.where(idx >= sh, pltpu.roll(x, sh, axis=axis), 0.0)
        sh *= 2
    return x


def _route_kernel(rit_ref, pos_ref, tiles_ref):
    s = rit_ref.shape[1]
    e1 = rit_ref[RI_E1:RI_E1 + 1, :]
    e2 = rit_ref[RI_E2:RI_E2 + 1, :]
    sub = lax.broadcasted_iota(jnp.int32, (N_EXPERTS, s), 0).astype(F32)
    oh1 = jnp.where(sub == e1, 1.0, 0.0)
    oh2 = jnp.where(sub == e2, 1.0, 0.0)
    cnt = oh1 + oh2
    incl = _cumsum(cnt, 1)
    before = incl - cnt
    total = jnp.broadcast_to(incl[:, s - 1:s], (N_EXPERTS, LANES))
    aligned = jnp.floor((total + (SEG_ALIGN - 1)) * (1.0 / SEG_ALIGN)) * SEG_ALIGN
    off = _cumsum(aligned, 0) - aligned
    slot = before + off[:, 0:1]
    pos1 = jnp.sum(oh1 * slot, axis=0, keepdims=True)
    pos2 = jnp.sum(oh2 * slot, axis=0, keepdims=True)
    pos_ref[...] = (jnp.concatenate([pos1, pos2], axis=0) * ROW_TILE).astype(jnp.int32)

    nt = jnp.floor((total + (MOE_R - 1)) * (1.0 / MOE_R))
    ct_incl = _cumsum(nt, 0)
    ct_excl = ct_incl - nt
    sub_l = lax.broadcasted_iota(jnp.int32, (N_EXPERTS, LANES), 0).astype(F32)
    tile_i = lax.broadcasted_iota(jnp.int32, (N_EXPERTS, LANES), 1).astype(F32)
    te = jnp.sum(jnp.where(ct_incl <= tile_i, 1.0, 0.0), axis=0, keepdims=True)
    te = jnp.minimum(te, N_EXPERTS - 1.0)
    sel = jnp.where(sub_l == te, 1.0, 0.0)
    k_in = tile_i - ct_excl
    start = jnp.sum(sel * (off + k_in * MOE_R), axis=0, keepdims=True)
    nval = jnp.sum(sel * jnp.clip(total - k_in * MOE_R, 0.0, float(MOE_R)), axis=0, keepdims=True)
    active = tile_i[0:1, :] < ct_incl[N_EXPERTS - 1:N_EXPERTS, :]
    start = jnp.where(active, start, 0.0)
    nval = jnp.where(active, nval, 0.0)
    rows = jnp.concatenate([te, start * ROW_TILE, nval, jnp.zeros((5, LANES), F32)], axis=0)
    tiles_ref[...] = rows.astype(jnp.int32)


def _route(rit, batch, seq):
    return pl.pallas_call(
        _route_kernel,
        grid=(batch,),
        in_specs=[pl.BlockSpec((8, seq), lambda b: (0, b))],
        out_specs=[
            pl.BlockSpec((None, 2, seq), lambda b: (b, 0, 0)),
            pl.BlockSpec((None, 8, LANES), lambda b: (b, 0, 0)),
        ],
        out_shape=[
            jax.ShapeDtypeStruct((batch, 2, seq), jnp.int32),
            jax.ShapeDtypeStruct((batch, 8, LANES), jnp.int32),
        ],
        compiler_params=_cparams(("parallel",)),
        name="route",
    )(rit)


def _read_rows(view, n_rows):
    return jnp.concatenate([view[pl.ds(k, n_rows, stride=ROW_TILE), :] for k in range(ROW_TILE)], axis=1)


def _write_rows(view, n_rows, val):
    for k in range(ROW_TILE):
        view[pl.ds(k, n_rows, stride=ROW_TILE), :] = val[:, k * LANES:(k + 1) * LANES]


def _moe_kernel(nd, nt, texp_ref, tstart_ref, tnval_ref, x1_ref, p1_ref, p2_ref, ri_ref,
                wg_ref, wu_ref, wd_ref, g2_ref, b2_ref, o_ref, buf, xs_scr, as_scr, bs_scr):
    c = pl.program_id(0)
    j = pl.program_id(1)
    td = x1_ref.shape[0]

    def tile_at(ref, off):
        return ref.at[pl.ds(pl.multiple_of(off, ROW_TILE), ROW_TILE), :]

    @pl.when((j == 0) & (c == 0))
    def _():
        buf[...] = jnp.zeros(buf.shape, F32)

    @pl.when(j < nd)
    def _():
        _write_rows(xs_scr, td, x1_ref[...])

        def body(i, carry):
            for u in range(SEG_ALIGN):
                t = i * SEG_ALIGN + u
                row = tile_at(xs_scr, t * ROW_TILE)[...]
                tile_at(buf, p1_ref[0, t])[...] = row
                tile_at(buf, p2_ref[0, t])[...] = row
            return carry

        lax.fori_loop(0, td // SEG_ALIGN, body, 0)

    @pl.when((j >= nd) & (j < nd + nt))
    def _():
        idx = c * nt + (j - nd)
        nval = tnval_ref[idx]

        @pl.when(nval > 0)
        def _():
            start = pl.multiple_of(tstart_ref[idx], ROW_TILE * SEG_ALIGN)
            view = buf.at[pl.ds(start, MOE_R * ROW_TILE), :]
            xt = _read_rows(view, MOE_R)
            xb = xt.astype(BF16)
            g = jnp.dot(xb, wg_ref[...], preferred_element_type=F32)
            u = jnp.dot(xb, wu_ref[...], preferred_element_type=F32)
            hcat = (g / (1.0 + jnp.exp(-g))) * u
            y = jnp.dot(hcat.astype(BF16), wd_ref[...], preferred_element_type=F32)
            rows = lax.broadcasted_iota(jnp.int32, (MOE_R, 1), 0)
            _write_rows(view, MOE_R, jnp.where(rows < nval, y, xt))

    @pl.when(j >= nd + nt)
    def _():
        def body(i, carry):
            for u in range(SEG_ALIGN):
                t = i * SEG_ALIGN + u
                tile_at(as_scr, t * ROW_TILE)[...] = tile_at(buf, p1_ref[0, t])[...]
                tile_at(bs_scr, t * ROW_TILE)[...] = tile_at(buf, p2_ref[0, t])[...]
            return carry

        lax.fori_loop(0, td // SEG_ALIGN, body, 0)
        ri = ri_ref[...]
        m = ri[:, RI_W1:RI_W1 + 1] * _read_rows(as_scr, td) + ri[:, RI_W2:RI_W2 + 1] * _read_rows(bs_scr, td)
        z = DEEPNORM_ALPHA * x1_ref[...] + m
        o_ref[...] = _layer_norm(z, g2_ref[...], b2_ref[...])


def _moe(x1, pos3, ri, texp, tstart, tnval, wg, wu, wd, g2, b2, batch, seq):
    t = x1.shape[0]
    td = MOE_TD
    nd = seq // td
    nt = 2 * seq // MOE_R + N_EXPERTS
    buf_rows = 2 * seq + N_EXPERTS * SEG_ALIGN + MOE_R

    def tok_tile(j):
        return jnp.where(j < nd, j, jnp.maximum(j - nd - nt, 0))

    def tile_expert(c, j, texp_ref):
        return texp_ref[c * nt + jnp.clip(j - nd, 0, nt - 1)]

    tok_spec = lambda w: pl.BlockSpec((td, w), lambda c, j, *_: (c * nd + tok_tile(j), 0))
    pos_spec = lambda k: pl.BlockSpec((None, 1, td), lambda c, j, *_: ((c * 2 + k) * nd + tok_tile(j), 0, 0),
                                      memory_space=pltpu.SMEM)
    w_spec = lambda a, b_: pl.BlockSpec((None, a, b_), lambda c, j, te, *_: (tile_expert(c, j, te), 0, 0))
    vec_spec = pl.BlockSpec((1, D_MODEL), lambda c, j, *_: (0, 0))
    grid_spec = pltpu.PrefetchScalarGridSpec(
        num_scalar_prefetch=3,
        grid=(batch, nd + nt + nd),
        in_specs=[
            tok_spec(D_MODEL), pos_spec(0), pos_spec(1), tok_spec(LANES),
            w_spec(D_MODEL, D_EXPERT), w_spec(D_MODEL, D_EXPERT), w_spec(D_EXPERT, D_MODEL),
            vec_spec, vec_spec,
        ],
        out_specs=pl.BlockSpec((td, D_MODEL),
                               lambda c, j, *_: (c * nd + jnp.maximum(j - nd - nt, 0), 0)),
        scratch_shapes=[
            pltpu.VMEM((buf_rows * ROW_TILE, LANES), F32),
            pltpu.VMEM((td * ROW_TILE, LANES), F32),
            pltpu.VMEM((td * ROW_TILE, LANES), F32),
            pltpu.VMEM((td * ROW_TILE, LANES), F32),
        ],
    )
    return pl.pallas_call(
        functools.partial(_moe_kernel, nd, nt),
        grid_spec=grid_spec,
        out_shape=jax.ShapeDtypeStruct((t, D_MODEL), F32),
        compiler_params=_cparams(("arbitrary", "arbitrary")),
        name="moe",
    )(texp, tstart, tnval, x1, pos3, pos3, ri, wg, wu, wd, g2, b2)


def _qkv_col_scale():
    ones = lambda n: jnp.ones((n,), F32)
    att = FOX_HEADS * HEAD_DIM
    qs = lambda d: jnp.full((att,), LOG2E * d ** -0.5, F32)
    return jnp.concatenate([qs(HEAD_DIM), ones(2 * att), qs(HEAD_DIM), ones(2 * att),
                            qs(MEM_HEAD_DIM)]).reshape(1, QKV_COLS)


def kernel(x, mem, w_in, b_forget, b_gates, lambda_q1, lambda_k1, lambda_q2, lambda_k2, diff_subln_g,
           w_mem_kv, w_branch_fox, w_branch_diff, w_branch_mem, w_out, ln1_g, ln1_b, w_router_group,
           b_router_group, w_router_expert, b_router_expert, w_expert_gate, w_expert_up, w_expert_down,
           ln2_g, ln2_b):
    batch, seq, d = x.shape
    t = batch * seq
    l = 0
    x2 = x.reshape(t, d)

    w_in_t = jnp.swapaxes(w_in, 1, 2)
    w_fl_t = jnp.pad(w_in_t[l, FL_COL0:, :], ((0, 16 - FOX_HEADS), (0, 0))).astype(BF16)
    w_r = jnp.concatenate([w_router_group[l], w_router_expert[l]], axis=1)
    w_r = jnp.pad(w_r, ((0, 0), (0, LANES - w_r.shape[1]))).astype(BF16)
    b_r = jnp.concatenate([b_router_group[l], b_router_expert[l]])
    b_r = jnp.pad(b_r, (0, LANES - b_r.shape[0])).reshape(1, LANES)
    lam_params = jnp.stack([lambda_q1[l], lambda_k1[l], lambda_q2[l], lambda_k2[l]])
    slopes = 2.0 ** (-8.0 * jnp.arange(1, DIFF_HEADS + 1, dtype=F32) / DIFF_HEADS)

    qkv, vt4, fl_t = _proj(x2, w_in_t, w_fl_t, _qkv_col_scale(), batch, seq)
    qkv4 = qkv.reshape(N_SLABS, batch, seq, LANES)
    c = _fscan(fl_t, b_forget[l].reshape(FOX_HEADS, 1), batch, seq)
    c4 = c.reshape(FOX_HEADS // 2, 2, t)

    y_fox, wg_b, wu_b, wd_b = _fox(qkv4, vt4, c4, w_expert_gate[l], w_expert_up[l], w_expert_down[l],
                                   batch, seq)
    y_diff, w_gates = _diff(qkv4, vt4, slopes, lam_params, diff_subln_g[l].reshape(LANES, 1), w_in_t,
                            batch, seq)
    y_mem = _mem_attn(qkv4, mem, w_mem_kv, batch, seq)

    x1, ri, rit = _merge(
        x2, y_fox.reshape(t, -1), y_diff.reshape(t, -1), y_mem.reshape(t, -1),
        w_gates, b_gates[l].reshape(1, -1),
        w_branch_fox[l].astype(BF16), w_branch_diff[l].astype(BF16), w_branch_mem[l].astype(BF16),
        w_out[l].astype(BF16), ln1_g[l].reshape(1, d), ln1_b[l].reshape(1, d), w_r, b_r)

    pos, tiles = _route(rit, batch, seq)
    nt = 2 * seq // MOE_R + N_EXPERTS
    texp, tstart, tnval = (tiles[:, r, :nt].reshape(-1) for r in range(3))
    pos3 = pos.reshape(batch * 2 * (seq // MOE_TD), 1, MOE_TD)
    out = _moe(x1, pos3, ri, texp, tstart, tnval,
               wg_b, wu_b, wd_b,
               ln2_g[l].reshape(1, d), ln2_b[l].reshape(1, d), batch, seq)
    return out.reshape(batch, seq, d)
```

```python
import functools
import math

import jax
import jax.numpy as jnp
from jax import lax
from jax.experimental import pallas as pl
from jax.experimental.pallas import tpu as pltpu

F32 = jnp.float32
BF16 = jnp.bfloat16

D_MODEL = 1024
HEAD_DIM = 64
FOX_HEADS = 8
DIFF_HEADS = 4
MEM_HEADS = 4
MEM_HEAD_DIM = 128
N_MEM = 256
N_BRANCH = 3
N_GROUPS = 4
EXPERTS_PER_GROUP = 8
N_EXPERTS = N_GROUPS * EXPERTS_PER_GROUP
D_EXPERT = 256
LN_EPS = 1e-5
DEPTH = 1
DEEPNORM_ALPHA = (2.0 * DEPTH) ** 0.25
LAM_INIT = 0.8 - 0.6 * math.exp(-0.3 * 0)
LOG2E = math.log2(math.e)

LANES = 128
QKV_COLS = 3584
N_SLABS = QKV_COLS // LANES
GATE_COL0 = QKV_COLS
FL_COL0 = QKV_COLS + N_BRANCH * D_MODEL
SLAB_FOX_Q, SLAB_FOX_K, SLAB_FOX_V = 0, 4, 8
SLAB_DIFF_Q, SLAB_DIFF_K, SLAB_DIFF_V = 12, 16, 20
SLAB_MEM_Q = 24
ROUTE_LANE0 = N_GROUPS

NEG_BIG = -1e30
VMEM_LIMIT = 56 * 1024 * 1024

PROJ_TM = 2048
PROJ_TN = 512
PROJ_J_FOX_V = SLAB_FOX_V * LANES // PROJ_TN
PROJ_J_DIFF_V = SLAB_DIFF_V * LANES // PROJ_TN
VT_FOX, VT_DIFF = 0, PROJ_TN // LANES
GATE_CAST_STEPS = N_BRANCH * D_MODEL // PROJ_TN
ATT_T = 256
ATT_UNROLL = 8
N_CHAINS = 4
NSL = N_CHAINS // 2
N_BIAS_LANES = 3
DIFF_ONES_ROWS = 16
SETUP_ROWS = 512
MEM_TQ = 4096
MERGE_TM = 1024
MOE_R = 256
MOE_TD = 256
SEG_ALIGN = 8
ROW_TILE = D_MODEL // LANES
RI_E1, RI_E2, RI_W1, RI_W2 = 0, 1, 2, 3


def _cparams(sem, flags=None):
    return pltpu.CompilerParams(dimension_semantics=sem, vmem_limit_bytes=VMEM_LIMIT, flags=flags)


ATT_FLAGS = None


def _proj_kernel(x_ref, w_ref, wfl_ref, sc_ref, o_ref, vt_ref, fl_ref, xb_ref):
    j = pl.program_id(1)
    is_v = (j == PROJ_J_FOX_V) | (j == PROJ_J_DIFF_V)
    spc = PROJ_TN // LANES

    @pl.when(j == 0)
    def _():
        xb = x_ref[...].astype(BF16)
        xb_ref[...] = xb
        fl_ref[...] = lax.dot_general(wfl_ref[...], xb, (((1,), (1,)), ((), ())),
                                      preferred_element_type=F32)

    @pl.when(jnp.logical_not(is_v))
    def _():
        acc = lax.dot_general(xb_ref[...], w_ref[...].astype(BF16), (((1,), (1,)), ((), ())),
                              preferred_element_type=F32) * sc_ref[...]
        for c in range(spc):
            o_ref[c] = acc[:, c * LANES:(c + 1) * LANES].astype(BF16)

    @pl.when(is_v)
    def _():
        acc_t = lax.dot_general(w_ref[...].astype(BF16), xb_ref[...], (((1,), (1,)), ((), ())),
                                preferred_element_type=F32)
        for c in range(spc):
            vt_ref[c] = acc_t[c * LANES:(c + 1) * LANES, :].astype(BF16)
        o_ref[...] = jnp.zeros(o_ref.shape, BF16)


def _proj(x2, w_in_t, w_fl_t, col_scale, batch, seq):
    t = x2.shape[0]
    spc = PROJ_TN // LANES
    per_seq = seq // PROJ_TM
    return pl.pallas_call(
        _proj_kernel,
        grid=(t // PROJ_TM, QKV_COLS // PROJ_TN),
        in_specs=[
            pl.BlockSpec((PROJ_TM, D_MODEL), lambda i, j: (i, 0)),
            pl.BlockSpec((None, PROJ_TN, D_MODEL), lambda i, j: (0, j, 0)),
            pl.BlockSpec((16, D_MODEL), lambda i, j: (0, 0)),
            pl.BlockSpec((1, PROJ_TN), lambda i, j: (0, j)),
        ],
        out_specs=[
            pl.BlockSpec((spc, PROJ_TM, LANES), lambda i, j: (j, i, 0)),
            pl.BlockSpec((spc, None, LANES, PROJ_TM),
                         lambda i, j: (jnp.where(j < PROJ_J_DIFF_V, 0, 1), i // per_seq, 0, i % per_seq)),
            pl.BlockSpec((16, PROJ_TM), lambda i, j: (0, i)),
        ],
        out_shape=[
            jax.ShapeDtypeStruct((N_SLABS, t, LANES), BF16),
            jax.ShapeDtypeStruct((2 * spc, batch, LANES, seq), BF16),
            jax.ShapeDtypeStruct((16, t), F32),
        ],
        scratch_shapes=[pltpu.VMEM((PROJ_TM, D_MODEL), BF16)],
        compiler_params=_cparams(("parallel", "arbitrary")),
        name="proj",
    )(x2, w_in_t, w_fl_t, col_scale)


def _fscan_kernel(fl_ref, b_ref, c_ref):
    s = fl_ref.shape[1]
    z = fl_ref[0:FOX_HEADS, :] + b_ref[...]
    lf = jnp.minimum(z, 0.0) - jnp.log(1.0 + jnp.exp(-jnp.abs(z)))
    lane = lax.broadcasted_iota(jnp.int32, lf.shape, 1)
    sh = 1
    while sh < s:
        r = pltpu.roll(lf, sh, axis=1)
        lf = lf + jnp.where(lane >= sh, r, 0.0)
        sh *= 2
    c_ref[...] = lf * LOG2E


def _fscan(fl_t, b_forget_col, batch, seq):
    t = fl_t.shape[1]
    return pl.pallas_call(
        _fscan_kernel,
        grid=(batch,),
        in_specs=[
            pl.BlockSpec((16, seq), lambda b: (0, b)),
            pl.BlockSpec((FOX_HEADS, 1), lambda b: (0, 0)),
        ],
        out_specs=pl.BlockSpec((FOX_HEADS, seq), lambda b: (0, b)),
        out_shape=jax.ShapeDtypeStruct((FOX_HEADS, t), F32),
        compiler_params=_cparams(("parallel",)),
        name="fscan",
    )(fl_t, b_forget_col)


def _flash_chains(qi, tq, qh_scr, k_fn, vt_fn, m_scr, acc_scr, s_scr, pm_scr):
    m_scr[...] = jnp.full(m_scr.shape, NEG_BIG, F32)
    acc_scr[...] = jnp.zeros(acc_scr.shape, F32)

    def scores(j, slot, masked):
        start = pl.multiple_of(j * tq, tq)
        for c in range(N_CHAINS):
            s = lax.dot_general(k_fn(c, start), qh_scr[c], (((1,), (1,)), ((), ())),
                                preferred_element_type=F32)
            if masked:
                kv = lax.broadcasted_iota(jnp.int32, (tq, tq), 0)
                q = lax.broadcasted_iota(jnp.int32, (tq, tq), 1)
                s = jnp.where(kv <= q, s, NEG_BIG)
            s_scr[slot, c] = s
            pm_scr[slot, c] = jnp.max(s.reshape(tq // 8, 8, tq), axis=0)

    def consume(j, slot):
        start = pl.multiple_of(j * tq, tq)
        for c in range(N_CHAINS):
            m_prev = m_scr[c]
            m_new = jnp.maximum(m_prev, jnp.max(pm_scr[slot, c], axis=0, keepdims=True))
            alpha = jnp.exp2(m_prev - m_new)
            p = jnp.exp2(s_scr[slot, c] - m_new)
            m_scr[c] = m_new
            pv = jnp.dot(vt_fn(c, start), p.astype(BF16), preferred_element_type=F32)
            acc_scr[c] = acc_scr[c] * alpha + pv

    @pl.when(qi == 0)
    def _():
        scores(0, 0, True)
        consume(0, 0)

    def run(j0, count, last_masked, issue_beyond):
        for t in range(count):
            if t + 1 < count or issue_beyond:
                scores(j0 + t + 1, (t + 1) % 2, last_masked and t + 1 == count - 1)
            consume(j0 + t, t % 2)

    @pl.when(qi > 0)
    def _():
        scores(0, 0, False)
        groups = (qi - 1) // ATT_UNROLL

        def body(g, carry):
            run(g * ATT_UNROLL, ATT_UNROLL, False, True)
            return carry

        lax.fori_loop(0, groups, body, 0)
        j0 = groups * ATT_UNROLL
        for rem in range(1, ATT_UNROLL + 1):
            @pl.when(qi - j0 == rem)
            def _(rem=rem):
                run(j0, rem + 1, True, False)


def _own_and_spare(shape, c):
    lane = lax.broadcasted_iota(jnp.int32, shape, 1)
    own = (lane < HEAD_DIM) if c % 2 == 0 else (lane >= HEAD_DIM)
    spare = HEAD_DIM if c % 2 == 0 else 0
    return lane, own, spare


def _split_q(q_ref, qh_scr):
    for c in range(N_CHAINS):
        q = q_ref[c // 2].astype(F32)
        lane, own, spare = _own_and_spare(q.shape, c)
        ones = (lane >= spare) & (lane < spare + N_BIAS_LANES)
        qh_scr[c] = jnp.where(own, q, jnp.where(ones, 1.0, 0.0)).astype(BF16)


def _augment_k_pair(k, bias):
    hi = bias.astype(BF16).astype(F32)
    r1 = bias - hi
    mid = r1.astype(BF16).astype(F32)
    lo = (r1 - mid).astype(BF16).astype(F32)
    lane = lax.broadcasted_iota(jnp.int32, k.shape, 1)
    half_lane = lane & (HEAD_DIM - 1)
    pieces = jnp.where(half_lane == 0, hi, jnp.where(half_lane == 1, mid, lo))
    kf = k.astype(F32)
    out = []
    for c in range(2):
        _, own, spare = _own_and_spare(k.shape, c)
        bias_lanes = (lane >= spare) & (lane < spare + N_BIAS_LANES)
        out.append(jnp.where(own, kf, jnp.where(bias_lanes, pieces, 0.0)).astype(BF16))
    return out


def _attn_scratch(tq, seq, acc_rows, vt_slabs):
    return [
        pltpu.VMEM((N_CHAINS, tq, LANES), BF16),
        pltpu.VMEM((N_CHAINS, 1, tq), F32),
        pltpu.VMEM((N_CHAINS, acc_rows, tq), F32),
        pltpu.VMEM((2, N_CHAINS, tq, tq), F32),
        pltpu.VMEM((2, N_CHAINS, 8, tq), F32),
        pltpu.VMEM((N_CHAINS, seq, LANES), BF16),
        pltpu.VMEM((vt_slabs, acc_rows, seq), BF16),
    ]


def _fox_kernel(q_ref, k_ref, vt_ref, c_ref, wg_ref, wu_ref, wd_ref, o_ref, wgb_ref, wub_ref, wdb_ref,
                qh_scr, m_scr, acc_scr, s_scr, pm_scr, ka_scr, vat_scr):
    tq = q_ref.shape[1]
    seq = k_ref.shape[1]
    qi = pl.program_id(2)
    _split_q(q_ref, qh_scr)

    n = (pl.program_id(0) * pl.num_programs(1) + pl.program_id(1)) * pl.num_programs(2) + qi
    for k, (src, dst) in enumerate(((wg_ref, wgb_ref), (wu_ref, wub_ref), (wd_ref, wdb_ref))):
        @pl.when((n >= k * N_EXPERTS) & (n < (k + 1) * N_EXPERTS))
        def _(src=src, dst=dst):
            dst[...] = src[...].astype(BF16)

    @pl.when(qi == 0)
    def _():
        def chunk(i, carry):
            r0 = pl.multiple_of(i * SETUP_ROWS, SETUP_ROWS)
            rows = pl.ds(r0, SETUP_ROWS)
            head_row = lax.broadcasted_iota(jnp.int32, (LANES, SETUP_ROWS), 0) < HEAD_DIM
            for sl in range(NSL):
                vt = vt_ref[sl, :, rows]
                one = jnp.ones_like(vt)
                vat_scr[2 * sl, :, rows] = jnp.where(head_row, vt, one)
                vat_scr[2 * sl + 1, :, rows] = jnp.where(head_row, one, vt)
                cc = c_ref[sl, :, rows]
                stacked = jnp.where(head_row, jnp.broadcast_to(cc[1:2, :], (LANES, SETUP_ROWS)),
                                    jnp.broadcast_to(cc[0:1, :], (LANES, SETUP_ROWS)))
                bias = -stacked.T
                k = k_ref[sl, rows, :]
                ka_scr[2 * sl, rows, :], ka_scr[2 * sl + 1, rows, :] = _augment_k_pair(k, bias)
            return carry

        lax.fori_loop(0, seq // SETUP_ROWS, chunk, 0)

    def k_fn(c, start):
        return ka_scr[c, pl.ds(start, tq), :]

    def vt_fn(c, start):
        return vat_scr[c, :, pl.ds(start, tq)]

    _flash_chains(qi, tq, qh_scr, k_fn, vt_fn, m_scr, acc_scr, s_scr, pm_scr)
    for sl in range(NSL):
        a0 = acc_scr[2 * sl]
        a1 = acc_scr[2 * sl + 1]
        row = lax.broadcasted_iota(jnp.int32, a0.shape, 0)
        ot = jnp.where(row < HEAD_DIM, a0 * (1.0 / a0[HEAD_DIM:HEAD_DIM + 1, :]), a1 * (1.0 / a1[0:1, :]))
        o_ref[:, sl * LANES:(sl + 1) * LANES] = ot.T.astype(o_ref.dtype)


def _fox(qkv4, vt4, c4, wg, wu, wd, batch, seq):
    tq = ATT_T
    groups = FOX_HEADS // N_CHAINS
    nq = seq // tq
    assert batch * groups * nq >= 3 * N_EXPERTS

    def w_spec(k, rows, cols):
        def index(b, g, i):
            return (jnp.clip((b * groups + g) * nq + i - k * N_EXPERTS, 0, N_EXPERTS - 1), 0, 0)
        return pl.BlockSpec((None, rows, cols), index)

    w_specs = [w_spec(0, D_MODEL, D_EXPERT), w_spec(1, D_MODEL, D_EXPERT), w_spec(2, D_EXPERT, D_MODEL)]
    return pl.pallas_call(
        _fox_kernel,
        grid=(batch, groups, nq),
        in_specs=[
            pl.BlockSpec((NSL, None, tq, LANES), lambda b, g, i: (SLAB_FOX_Q // NSL + g, b, i, 0)),
            pl.BlockSpec((NSL, None, seq, LANES), lambda b, g, i: (SLAB_FOX_K // NSL + g, b, 0, 0)),
            pl.BlockSpec((NSL, None, LANES, seq), lambda b, g, i: (VT_FOX // NSL + g, b, 0, 0)),
            pl.BlockSpec((NSL, 2, seq), lambda b, g, i: (g, 0, b)),
        ] + w_specs,
        out_specs=[pl.BlockSpec((None, tq, NSL * LANES), lambda b, g, i: (b, i, g))] + w_specs,
        out_shape=[jax.ShapeDtypeStruct((batch, seq, FOX_HEADS * HEAD_DIM), BF16)]
        + [jax.ShapeDtypeStruct(w.shape, BF16) for w in (wg, wu, wd)],
        scratch_shapes=_attn_scratch(tq, seq, LANES, N_CHAINS),
        compiler_params=_cparams(("arbitrary", "arbitrary", "arbitrary"), ATT_FLAGS),
        name="fox",
    )(qkv4, qkv4, vt4, c4, wg, wu, wd)


def _diff_kernel(slopes_ref, q_ref, k_ref, vt_ref, lam_ref, g_ref, wgi_ref, o_ref, wgo_ref, qh_scr, m_scr,
                 acc_scr, s_scr, pm_scr, ka_scr, vat_scr):
    tq = q_ref.shape[1]
    seq = k_ref.shape[1]
    qi = pl.program_id(2)
    _split_q(q_ref, qh_scr)
    g = pl.program_id(1)

    n = (pl.program_id(0) * pl.num_programs(1) + g) * pl.num_programs(2) + qi

    @pl.when(n < GATE_CAST_STEPS)
    def _():
        wgo_ref[...] = wgi_ref[...].astype(BF16)

    @pl.when(qi == 0)
    def _():
        def chunk(i, carry):
            r0 = pl.multiple_of(i * SETUP_ROWS, SETUP_ROWS)
            rows = pl.ds(r0, SETUP_ROWS)
            for hd in range(NSL):
                vt = vt_ref[hd, :, rows]
                vat_scr[hd, :, rows] = jnp.concatenate([vt, jnp.ones((DIFF_ONES_ROWS, SETUP_ROWS), BF16)], axis=0)
                pos = (lax.broadcasted_iota(jnp.int32, (SETUP_ROWS, LANES), 0) + r0).astype(F32)
                bias = pos * (slopes_ref[NSL * g + hd] * LOG2E)
                k = k_ref[hd, rows, :]
                ka_scr[2 * hd, rows, :], ka_scr[2 * hd + 1, rows, :] = _augment_k_pair(k, bias)
            return carry

        lax.fori_loop(0, seq // SETUP_ROWS, chunk, 0)

    def k_fn(c, start):
        return ka_scr[c, pl.ds(start, tq), :]

    def vt_fn(c, start):
        return vat_scr[c // 2, :, pl.ds(start, tq)]

    _flash_chains(qi, tq, qh_scr, k_fn, vt_fn, m_scr, acc_scr, s_scr, pm_scr)
    lp = lam_ref[...]
    s1 = jnp.sum(lp[0:1, :] * lp[1:2, :], axis=1, keepdims=True)
    s2 = jnp.sum(lp[2:3, :] * lp[3:4, :], axis=1, keepdims=True)
    lam = jnp.exp(s1) - jnp.exp(s2) + LAM_INIT
    for hd in range(NSL):
        a1 = acc_scr[2 * hd]
        a2 = acc_scr[2 * hd + 1]
        ot = (a1[:LANES, :] * (1.0 / a1[LANES:LANES + 1, :])
              - lam * (a2[:LANES, :] * (1.0 / a2[LANES:LANES + 1, :])))
        ms = jnp.mean(ot * ot, axis=0, keepdims=True)
        yt = ot * lax.rsqrt(ms + LN_EPS) * g_ref[...]
        o_ref[:, hd * LANES:(hd + 1) * LANES] = (yt * (1.0 - LAM_INIT)).T.astype(o_ref.dtype)


def _diff(qkv4, vt4, slopes, lam_params, subln_g_col, w_in_t, batch, seq):
    tq = ATT_T
    groups = DIFF_HEADS // NSL
    nq = seq // tq
    assert batch * groups * nq >= GATE_CAST_STEPS

    def cast_step(b, g, i):
        return jnp.minimum((b * groups + g) * nq + i, GATE_CAST_STEPS - 1)

    grid_spec = pltpu.PrefetchScalarGridSpec(
        num_scalar_prefetch=1,
        grid=(batch, groups, nq),
        in_specs=[
            pl.BlockSpec((NSL, None, tq, LANES), lambda b, g, i, s: (SLAB_DIFF_Q // NSL + g, b, i, 0)),
            pl.BlockSpec((NSL, None, seq, LANES), lambda b, g, i, s: (SLAB_DIFF_K // NSL + g, b, 0, 0)),
            pl.BlockSpec((NSL, None, LANES, seq), lambda b, g, i, s: (VT_DIFF // NSL + g, b, 0, 0)),
            pl.BlockSpec((4, HEAD_DIM), lambda b, g, i, s: (0, 0)),
            pl.BlockSpec((LANES, 1), lambda b, g, i, s: (0, 0)),
            pl.BlockSpec((None, PROJ_TN, D_MODEL),
                         lambda b, g, i, s: (0, GATE_COL0 // PROJ_TN + cast_step(b, g, i), 0)),
        ],
        out_specs=[
            pl.BlockSpec((None, tq, NSL * LANES), lambda b, g, i, s: (b, i, g)),
            pl.BlockSpec((PROJ_TN, D_MODEL), lambda b, g, i, s: (cast_step(b, g, i), 0)),
        ],
        scratch_shapes=_attn_scratch(tq, seq, LANES + DIFF_ONES_ROWS, NSL),
    )
    return pl.pallas_call(
        _diff_kernel,
        grid_spec=grid_spec,
        out_shape=[
            jax.ShapeDtypeStruct((batch, seq, DIFF_HEADS * LANES), BF16),
            jax.ShapeDtypeStruct((N_BRANCH * D_MODEL, D_MODEL), BF16),
        ],
        compiler_params=_cparams(("arbitrary", "arbitrary", "arbitrary"), ATT_FLAGS),
        name="diff",
    )(slopes, qkv4, qkv4, vt4, lam_params, subln_g_col, w_in_t)


def _mem_kernel(q_ref, mem_ref, wk_ref, wv_ref, o_ref):
    memb = mem_ref[...].astype(BF16)
    mk = jnp.dot(memb, wk_ref[...].astype(BF16), preferred_element_type=F32).astype(BF16)
    mv = jnp.dot(memb, wv_ref[...].astype(BF16), preferred_element_type=F32).astype(BF16)
    s = lax.dot_general(q_ref[...], mk, (((1,), (1,)), ((), ())), preferred_element_type=F32)
    m = jnp.max(s, axis=1, keepdims=True)
    p = jnp.exp2(s - m)
    l = jnp.sum(p, axis=1, keepdims=True)
    o = jnp.dot(p.astype(BF16), mv, preferred_element_type=F32)
    o_ref[...] = (o / l).astype(o_ref.dtype)


def _mem_attn(qkv4, mem, w_mem_kv, batch, seq):
    tq = MEM_TQ
    return pl.pallas_call(
        _mem_kernel,
        grid=(batch, MEM_HEADS, seq // tq),
        in_specs=[
            pl.BlockSpec((None, None, tq, LANES), lambda b, h, i: (SLAB_MEM_Q + h, b, i, 0)),
            pl.BlockSpec((None, N_MEM, D_MODEL), lambda b, h, i: (b, 0, 0)),
            pl.BlockSpec((None, D_MODEL, LANES), lambda b, h, i: (0, 0, h)),
            pl.BlockSpec((None, D_MODEL, LANES), lambda b, h, i: (0, 0, MEM_HEADS + h)),
        ],
        out_specs=pl.BlockSpec((None, tq, LANES), lambda b, h, i: (b, i, h)),
        out_shape=jax.ShapeDtypeStruct((batch, seq, MEM_HEADS * LANES), BF16),
        compiler_params=_cparams(("parallel", "parallel", "parallel")),
        name="mem",
    )(qkv4, mem, w_mem_kv, w_mem_kv)


def _layer_norm(z, g, b):
    mu = jnp.mean(z, axis=1, keepdims=True)
    zc = z - mu
    var = jnp.mean(zc * zc, axis=1, keepdims=True)
    return zc * lax.rsqrt(var + LN_EPS) * g + b


def _lane_max(v):
    return jnp.max(v, axis=1, keepdims=True)


def _routing_info(logits):
    lane = lax.broadcasted_iota(jnp.int32, logits.shape, 1)
    big = jnp.int32(2 * LANES)
    is_g = lane < N_GROUPS
    gl = jnp.where(is_g, logits, NEG_BIG)
    gmax = _lane_max(gl)
    g_w = 1.0 / jnp.sum(jnp.exp(gl - gmax), axis=1, keepdims=True)
    g_idx = jnp.min(jnp.where(gl == gmax, lane, big), axis=1, keepdims=True)
    lo = ROUTE_LANE0 + g_idx * EXPERTS_PER_GROUP
    in_grp = (lane >= lo) & (lane < lo + EXPERTS_PER_GROUP)
    el = jnp.where(in_grp, logits, NEG_BIG)
    e1 = _lane_max(el)
    i1 = jnp.min(jnp.where(el == e1, lane, big), axis=1, keepdims=True)
    el2 = jnp.where(lane == i1, NEG_BIG, el)
    e2 = _lane_max(el2)
    i2 = jnp.min(jnp.where(el2 == e2, lane, big), axis=1, keepdims=True)
    r = jnp.exp(e2 - e1)
    w1 = g_w / (1.0 + r)
    w2 = g_w * r / (1.0 + r)
    id1 = (i1 - ROUTE_LANE0).astype(F32)
    id2 = (i2 - ROUTE_LANE0).astype(F32)
    return jnp.where(lane == RI_E1, id1,
                     jnp.where(lane == RI_E2, id2,
                               jnp.where(lane == RI_W1, w1, jnp.where(lane == RI_W2, w2, 0.0))))


def _merge_kernel(x_ref, yf_ref, yd_ref, ym_ref, wg_ref, bg_ref, wbf_ref, wbd_ref, wbm_ref,
                  wo_ref, g1_ref, b1_ref, wr_ref, br_ref, x1_ref, ri_ref, rit_ref):
    x = x_ref[...]
    xb = x.astype(BF16)
    h = None
    for i, (y_ref, wb_ref) in enumerate(((yf_ref, wbf_ref), (yd_ref, wbd_ref), (ym_ref, wbm_ref))):
        cols = slice(i * D_MODEL, (i + 1) * D_MODEL)
        gl = lax.dot_general(xb, wg_ref[cols, :], (((1,), (1,)), ((), ())),
                             preferred_element_type=F32) + bg_ref[:, cols]
        gate = 1.0 / (1.0 + jnp.exp(-gl))
        br = jnp.dot(y_ref[...], wb_ref[...], preferred_element_type=F32)
        h = gate * br if h is None else h + gate * br
    o = jnp.dot(h.astype(BF16), wo_ref[...], preferred_element_type=F32)
    x1 = _layer_norm(DEEPNORM_ALPHA * x + o, g1_ref[...], b1_ref[...])
    x1_ref[...] = x1
    logits = jnp.dot(x1.astype(BF16), wr_ref[...], preferred_element_type=F32) + br_ref[...]
    ri = _routing_info(logits)
    ri_ref[...] = ri
    rit_ref[...] = ri.T[0:8, :]


def _const_spec(shape):
    return pl.BlockSpec(shape, lambda i: (0,) * len(shape), pipeline_mode=pl.Buffered(1))


def _merge(x2, yf, yd, ym, wg, bg, wbf, wbd, wbm, wo, g1, b1, wr, br):
    t = x2.shape[0]
    tm = MERGE_TM
    half = yf.shape[1]
    row = lambda w: pl.BlockSpec((tm, w), lambda i: (i, 0))
    return pl.pallas_call(
        _merge_kernel,
        grid=(t // tm,),
        in_specs=[
            row(D_MODEL), row(half), row(half), row(half),
            _const_spec((N_BRANCH * D_MODEL, D_MODEL)), _const_spec((1, N_BRANCH * D_MODEL)),
            _const_spec((half, D_MODEL)), _const_spec((half, D_MODEL)), _const_spec((half, D_MODEL)),
            _const_spec((D_MODEL, D_MODEL)), _const_spec((1, D_MODEL)), _const_spec((1, D_MODEL)),
            _const_spec((D_MODEL, LANES)), _const_spec((1, LANES)),
        ],
        out_specs=[row(D_MODEL), row(LANES), pl.BlockSpec((8, tm), lambda i: (0, i))],
        out_shape=[
            jax.ShapeDtypeStruct((t, D_MODEL), F32),
            jax.ShapeDtypeStruct((t, LANES), F32),
            jax.ShapeDtypeStruct((8, t), F32),
        ],
        compiler_params=_cparams(("parallel",)),
        name="merge",
    )(x2, yf, yd, ym, wg, bg, wbf, wbd, wbm, wo, g1, b1, wr, br)


def _cumsum(x, axis):
    n = x.shape[axis]
    idx = lax.broadcasted_iota(jnp.int32, x.shape, axis)
    sh = 1
    while sh < n:
        x = x + jnp.where(idx >= sh, pltpu.roll(x, sh, axis=axis), 0.0)
        sh *= 2
    return x


def _route_kernel(rit_ref, pos_ref, tiles_ref):
    s = rit_ref.shape[1]
    e1 = rit_ref[RI_E1:RI_E1 + 1, :]
    e2 = rit_ref[RI_E2:RI_E2 + 1, :]
    sub = lax.broadcasted_iota(jnp.int32, (N_EXPERTS, s), 0).astype(F32)
    oh1 = jnp.where(sub == e1, 1.0, 0.0)
    oh2 = jnp.where(sub == e2, 1.0, 0.0)
    cnt = oh1 + oh2
    incl = _cumsum(cnt, 1)
    before = incl - cnt
    total = jnp.broadcast_to(incl[:, s - 1:s], (N_EXPERTS, LANES))
    aligned = jnp.floor((total + (SEG_ALIGN - 1)) * (1.0 / SEG_ALIGN)) * SEG_ALIGN
    off = _cumsum(aligned, 0) - aligned
    slot = before + off[:, 0:1]
    pos1 = jnp.sum(oh1 * slot, axis=0, keepdims=True)
    pos2 = jnp.sum(oh2 * slot, axis=0, keepdims=True)
    pos_ref[...] = (jnp.concatenate([pos1, pos2], axis=0) * ROW_TILE).astype(jnp.int32)

    nt = jnp.floor((total + (MOE_R - 1)) * (1.0 / MOE_R))
    ct_incl = _cumsum(nt, 0)
    ct_excl = ct_incl - nt
    sub_l = lax.broadcasted_iota(jnp.int32, (N_EXPERTS, LANES), 0).astype(F32)
    tile_i = lax.broadcasted_iota(jnp.int32, (N_EXPERTS, LANES), 1).astype(F32)
    te = jnp.sum(jnp.where(ct_incl <= tile_i, 1.0, 0.0), axis=0, keepdims=True)
    te = jnp.minimum(te, N_EXPERTS - 1.0)
    sel = jnp.where(sub_l == te, 1.0, 0.0)
    k_in = tile_i - ct_excl
    start = jnp.sum(sel * (off + k_in * MOE_R), axis=0, keepdims=True)
    nval = jnp.sum(sel * jnp.clip(total - k_in * MOE_R, 0.0, float(MOE_R)), axis=0, keepdims=True)
    active = tile_i[0:1, :] < ct_incl[N_EXPERTS - 1:N_EXPERTS, :]
    start = jnp.where(active, start, 0.0)
    nval = jnp.where(active, nval, 0.0)
    rows = jnp.concatenate([te, start * ROW_TILE, nval, jnp.zeros((5, LANES), F32)], axis=0)
    tiles_ref[...] = rows.astype(jnp.int32)


def _route(rit, batch, seq):
    return pl.pallas_call(
        _route_kernel,
        grid=(batch,),
        in_specs=[pl.BlockSpec((8, seq), lambda b: (0, b))],
        out_specs=[
            pl.BlockSpec((None, 2, seq), lambda b: (b, 0, 0)),
            pl.BlockSpec((None, 8, LANES), lambda b: (b, 0, 0)),
        ],
        out_shape=[
            jax.ShapeDtypeStruct((batch, 2, seq), jnp.int32),
            jax.ShapeDtypeStruct((batch, 8, LANES), jnp.int32),
        ],
        compiler_params=_cparams(("parallel",)),
        name="route",
    )(rit)


def _read_rows(view, n_rows):
    return jnp.concatenate([view[pl.ds(k, n_rows, stride=ROW_TILE), :] for k in range(ROW_TILE)], axis=1)


def _write_rows(view, n_rows, val):
    for k in range(ROW_TILE):
        view[pl.ds(k, n_rows, stride=ROW_TILE), :] = val[:, k * LANES:(k + 1) * LANES]


def _moe_kernel(nd, nt, texp_ref, tstart_ref, tnval_ref, x1_ref, p1_ref, p2_ref, ri_ref,
                wg_ref, wu_ref, wd_ref, g2_ref, b2_ref, o_ref, buf, xs_scr, as_scr, bs_scr):
    c = pl.program_id(0)
    j = pl.program_id(1)
    td = x1_ref.shape[0]

    def tile_at(ref, off):
        return ref.at[pl.ds(pl.multiple_of(off, ROW_TILE), ROW_TILE), :]

    @pl.when((j == 0) & (c == 0))
    def _():
        buf[...] = jnp.zeros(buf.shape, F32)

    @pl.when(j < nd)
    def _():
        _write_rows(xs_scr, td, x1_ref[...])

        def body(i, carry):
            for u in range(SEG_ALIGN):
                t = i * SEG_ALIGN + u
                row = tile_at(xs_scr, t * ROW_TILE)[...]
                tile_at(buf, p1_ref[0, t])[...] = row
                tile_at(buf, p2_ref[0, t])[...] = row
            return carry

        lax.fori_loop(0, td // SEG_ALIGN, body, 0)

    @pl.when((j >= nd) & (j < nd + nt))
    def _():
        idx = c * nt + (j - nd)
        nval = tnval_ref[idx]

        @pl.when(nval > 0)
        def _():
            start = pl.multiple_of(tstart_ref[idx], ROW_TILE * SEG_ALIGN)
            view = buf.at[pl.ds(start, MOE_R * ROW_TILE), :]
            xt = _read_rows(view, MOE_R)
            xb = xt.astype(BF16)
            g = jnp.dot(xb, wg_ref[...], preferred_element_type=F32)
            u = jnp.dot(xb, wu_ref[...], preferred_element_type=F32)
            hcat = (g / (1.0 + jnp.exp(-g))) * u
            y = jnp.dot(hcat.astype(BF16), wd_ref[...], preferred_element_type=F32)
            rows = lax.broadcasted_iota(jnp.int32, (MOE_R, 1), 0)
            _write_rows(view, MOE_R, jnp.where(rows < nval, y, xt))

    @pl.when(j >= nd + nt)
    def _():
        def body(i, carry):
            for u in range(SEG_ALIGN):
                t = i * SEG_ALIGN + u
                tile_at(as_scr, t * ROW_TILE)[...] = tile_at(buf, p1_ref[0, t])[...]
                tile_at(bs_scr, t * ROW_TILE)[...] = tile_at(buf, p2_ref[0, t])[...]
            return carry

        lax.fori_loop(0, td // SEG_ALIGN, body, 0)
        ri = ri_ref[...]
        m = ri[:, RI_W1:RI_W1 + 1] * _read_rows(as_scr, td) + ri[:, RI_W2:RI_W2 + 1] * _read_rows(bs_scr, td)
        z = DEEPNORM_ALPHA * x1_ref[...] + m
        o_ref[...] = _layer_norm(z, g2_ref[...], b2_ref[...])


def _moe(x1, pos3, ri, texp, tstart, tnval, wg, wu, wd, g2, b2, batch, seq):
    t = x1.shape[0]
    td = MOE_TD
    nd = seq // td
    nt = 2 * seq // MOE_R + N_EXPERTS
    buf_rows = 2 * seq + N_EXPERTS * SEG_ALIGN + MOE_R

    def tok_tile(j):
        return jnp.where(j < nd, j, jnp.maximum(j - nd - nt, 0))

    def tile_expert(c, j, texp_ref):
        return texp_ref[c * nt + jnp.clip(j - nd, 0, nt - 1)]

    tok_spec = lambda w: pl.BlockSpec((td, w), lambda c, j, *_: (c * nd + tok_tile(j), 0))
    pos_spec = lambda k: pl.BlockSpec((None, 1, td), lambda c, j, *_: ((c * 2 + k) * nd + tok_tile(j), 0, 0),
                                      memory_space=pltpu.SMEM)
    w_spec = lambda a, b_: pl.BlockSpec((None, a, b_), lambda c, j, te, *_: (tile_expert(c, j, te), 0, 0))
    vec_spec = pl.BlockSpec((1, D_MODEL), lambda c, j, *_: (0, 0))
    grid_spec = pltpu.PrefetchScalarGridSpec(
        num_scalar_prefetch=3,
        grid=(batch, nd + nt + nd),
        in_specs=[
            tok_spec(D_MODEL), pos_spec(0), pos_spec(1), tok_spec(LANES),
            w_spec(D_MODEL, D_EXPERT), w_spec(D_MODEL, D_EXPERT), w_spec(D_EXPERT, D_MODEL),
            vec_spec, vec_spec,
        ],
        out_specs=pl.BlockSpec((td, D_MODEL),
                               lambda c, j, *_: (c * nd + jnp.maximum(j - nd - nt, 0), 0)),
        scratch_shapes=[
            pltpu.VMEM((buf_rows * ROW_TILE, LANES), F32),
            pltpu.VMEM((td * ROW_TILE, LANES), F32),
            pltpu.VMEM((td * ROW_TILE, LANES), F32),
            pltpu.VMEM((td * ROW_TILE, LANES), F32),
        ],
    )
    return pl.pallas_call(
        functools.partial(_moe_kernel, nd, nt),
        grid_spec=grid_spec,
        out_shape=jax.ShapeDtypeStruct((t, D_MODEL), F32),
        compiler_params=_cparams(("arbitrary", "arbitrary")),
        name="moe",
    )(texp, tstart, tnval, x1, pos3, pos3, ri, wg, wu, wd, g2, b2)


def _qkv_col_scale():
    ones = lambda n: jnp.ones((n,), F32)
    att = FOX_HEADS * HEAD_DIM
    qs = lambda d: jnp.full((att,), LOG2E * d ** -0.5, F32)
    return jnp.concatenate([qs(HEAD_DIM), ones(2 * att), qs(HEAD_DIM), ones(2 * att),
                            qs(MEM_HEAD_DIM)]).reshape(1, QKV_COLS)


def kernel(x, mem, w_in, b_forget, b_gates, lambda_q1, lambda_k1, lambda_q2, lambda_k2, diff_subln_g,
           w_mem_kv, w_branch_fox, w_branch_diff, w_branch_mem, w_out, ln1_g, ln1_b, w_router_group,
           b_router_group, w_router_expert, b_router_expert, w_expert_gate, w_expert_up, w_expert_down,
           ln2_g, ln2_b):
    batch, seq, d = x.shape
    t = batch * seq
    l = 0
    x2 = x.reshape(t, d)

    w_in_t = jnp.swapaxes(w_in, 1, 2)
    w_fl_t = jnp.pad(w_in_t[l, FL_COL0:, :], ((0, 16 - FOX_HEADS), (0, 0))).astype(BF16)
    w_r = jnp.concatenate([w_router_group[l], w_router_expert[l]], axis=1)
    w_r = jnp.pad(w_r, ((0, 0), (0, LANES - w_r.shape[1]))).astype(BF16)
    b_r = jnp.concatenate([b_router_group[l], b_router_expert[l]])
    b_r = jnp.pad(b_r, (0, LANES - b_r.shape[0])).reshape(1, LANES)
    lam_params = jnp.stack([lambda_q1[l], lambda_k1[l], lambda_q2[l], lambda_k2[l]])
    slopes = 2.0 ** (-8.0 * jnp.arange(1, DIFF_HEADS + 1, dtype=F32) / DIFF_HEADS)

    qkv, vt4, fl_t = _proj(x2, w_in_t, w_fl_t, _qkv_col_scale(), batch, seq)
    qkv4 = qkv.reshape(N_SLABS, batch, seq, LANES)
    c = _fscan(fl_t, b_forget[l].reshape(FOX_HEADS, 1), batch, seq)
    c4 = c.reshape(FOX_HEADS // 2, 2, t)

    y_fox, wg_b, wu_b, wd_b = _fox(qkv4, vt4, c4, w_expert_gate[l], w_expert_up[l], w_expert_down[l],
                                   batch, seq)
    y_diff, w_gates = _diff(qkv4, vt4, slopes, lam_params, diff_subln_g[l].reshape(LANES, 1), w_in_t,
                            batch, seq)
    y_mem = _mem_attn(qkv4, mem, w_mem_kv, batch, seq)

    x1, ri, rit = _merge(
        x2, y_fox.reshape(t, -1), y_diff.reshape(t, -1), y_mem.reshape(t, -1),
        w_gates, b_gates[l].reshape(1, -1),
        w_branch_fox[l].astype(BF16), w_branch_diff[l].astype(BF16), w_branch_mem[l].astype(BF16),
        w_out[l].astype(BF16), ln1_g[l].reshape(1, d), ln1_b[l].reshape(1, d), w_r, b_r)

    pos, tiles = _route(rit, batch, seq)
    nt = 2 * seq // MOE_R + N_EXPERTS
    texp, tstart, tnval = (tiles[:, r, :nt].reshape(-1) for r in range(3))
    pos3 = pos.reshape(batch * 2 * (seq // MOE_TD), 1, MOE_TD)
    out = _moe(x1, pos3, ri, texp, tstart, tnval,
               wg_b, wu_b, wd_b,
               ln2_g[l].reshape(1, d), ln2_b[l].reshape(1, d), batch, seq)
    return out.reshape(batch, seq, d)
```

```python
import functools
import math

import jax
import jax.numpy as jnp
from jax import lax
from jax.experimental import pallas as pl
from jax.experimental.pallas import tpu as pltpu

F32 = jnp.float32
BF16 = jnp.bfloat16

D_MODEL = 1024
HEAD_DIM = 64
FOX_HEADS = 8
DIFF_HEADS = 4
MEM_HEADS = 4
MEM_HEAD_DIM = 128
N_MEM = 256
N_BRANCH = 3
N_GROUPS = 4
EXPERTS_PER_GROUP = 8
N_EXPERTS = N_GROUPS * EXPERTS_PER_GROUP
D_EXPERT = 256
LN_EPS = 1e-5
DEPTH = 1
DEEPNORM_ALPHA = (2.0 * DEPTH) ** 0.25
LAM_INIT = 0.8 - 0.6 * math.exp(-0.3 * 0)
LOG2E = math.log2(math.e)

LANES = 128
QKV_COLS = 3584
N_SLABS = QKV_COLS // LANES
GATE_COL0 = QKV_COLS
FL_COL0 = QKV_COLS + N_BRANCH * D_MODEL
SLAB_FOX_Q, SLAB_FOX_K, SLAB_FOX_V = 0, 4, 8
SLAB_DIFF_Q, SLAB_DIFF_K, SLAB_DIFF_V = 12, 16, 20
SLAB_MEM_Q = 24
ROUTE_LANE0 = N_GROUPS

NEG_BIG = -1e30
VMEM_LIMIT = 56 * 1024 * 1024

PROJ_TM = 2048
PROJ_TN = 512
PROJ_W_SLOTS = 3
PROJ_J_FOX_V = SLAB_FOX_V * LANES // PROJ_TN
PROJ_J_DIFF_V = SLAB_DIFF_V * LANES // PROJ_TN
VT_FOX, VT_DIFF = 0, PROJ_TN // LANES
GATE_CAST_STEPS = N_BRANCH * D_MODEL // PROJ_TN
ATT_T = 256
ATT_UNROLL = 8
N_CHAINS = 4
NSL = N_CHAINS // 2
N_BIAS_LANES = 3
DIFF_ONES_ROWS = 16
SETUP_ROWS = 512
MEM_TQ = 4096
MERGE_TM = 1024
MOE_R = 256
MOE_TD = 256
SEG_ALIGN = 8
ROW_TILE = D_MODEL // LANES
RI_E1, RI_E2, RI_W1, RI_W2 = 0, 1, 2, 3


def _cparams(sem, flags=None):
    return pltpu.CompilerParams(dimension_semantics=sem, vmem_limit_bytes=VMEM_LIMIT, flags=flags)


ATT_FLAGS = None


def _proj_kernel(x_ref, w_hbm, wfl_ref, sc_ref, o_ref, vt_ref, fl_ref, xb_ref, wbuf, wsem):
    j = pl.program_id(1)
    nj = pl.num_programs(1)
    is_v = (j == PROJ_J_FOX_V) | (j == PROJ_J_DIFF_V)
    spc = PROJ_TN // LANES

    n = pl.program_id(0) * nj + j
    total = pl.num_programs(0) * nj

    def w_copy(step):
        rows = pl.ds(pl.multiple_of(lax.rem(step, nj) * PROJ_TN, PROJ_TN), PROJ_TN)
        slot = lax.rem(step, PROJ_W_SLOTS)
        return pltpu.make_async_copy(w_hbm.at[0, rows, :], wbuf.at[slot], wsem.at[slot])

    @pl.when(n == 0)
    def _():
        w_copy(0).start()
        w_copy(1).start()

    @pl.when(n + 2 < total)
    def _():
        w_copy(n + 2).start()

    w_copy(n).wait()
    w_ref = wbuf.at[lax.rem(n, PROJ_W_SLOTS)]

    @pl.when(j == 0)
    def _():
        xb = x_ref[...].astype(BF16)
        xb_ref[...] = xb
        fl_ref[...] = lax.dot_general(wfl_ref[...], xb, (((1,), (1,)), ((), ())),
                                      preferred_element_type=F32)

    @pl.when(jnp.logical_not(is_v))
    def _():
        acc = lax.dot_general(xb_ref[...], w_ref[...].astype(BF16), (((1,), (1,)), ((), ())),
                              preferred_element_type=F32) * sc_ref[...]
        for c in range(spc):
            o_ref[c] = acc[:, c * LANES:(c + 1) * LANES].astype(BF16)

    @pl.when(is_v)
    def _():
        acc_t = lax.dot_general(w_ref[...].astype(BF16), xb_ref[...], (((1,), (1,)), ((), ())),
                                preferred_element_type=F32)
        for c in range(spc):
            vt_ref[c] = acc_t[c * LANES:(c + 1) * LANES, :].astype(BF16)
        o_ref[...] = jnp.zeros(o_ref.shape, BF16)


def _proj(x2, w_in_t, w_fl_t, col_scale, batch, seq):
    t = x2.shape[0]
    spc = PROJ_TN // LANES
    per_seq = seq // PROJ_TM
    return pl.pallas_call(
        _proj_kernel,
        grid=(t // PROJ_TM, QKV_COLS // PROJ_TN),
        in_specs=[
            pl.BlockSpec((PROJ_TM, D_MODEL), lambda i, j: (i, 0)),
            pl.BlockSpec(memory_space=pl.ANY),
            pl.BlockSpec((16, D_MODEL), lambda i, j: (0, 0)),
            pl.BlockSpec((1, PROJ_TN), lambda i, j: (0, j)),
        ],
        out_specs=[
            pl.BlockSpec((spc, PROJ_TM, LANES), lambda i, j: (j, i, 0)),
            pl.BlockSpec((spc, None, LANES, PROJ_TM),
                         lambda i, j: (jnp.where(j < PROJ_J_DIFF_V, 0, 1), i // per_seq, 0, i % per_seq)),
            pl.BlockSpec((16, PROJ_TM), lambda i, j: (0, i)),
        ],
        out_shape=[
            jax.ShapeDtypeStruct((N_SLABS, t, LANES), BF16),
            jax.ShapeDtypeStruct((2 * spc, batch, LANES, seq), BF16),
            jax.ShapeDtypeStruct((16, t), F32),
        ],
        scratch_shapes=[
            pltpu.VMEM((PROJ_TM, D_MODEL), BF16),
            pltpu.VMEM((PROJ_W_SLOTS, PROJ_TN, D_MODEL), F32),
            pltpu.SemaphoreType.DMA((PROJ_W_SLOTS,)),
        ],
        compiler_params=_cparams(("arbitrary", "arbitrary")),
        name="proj",
    )(x2, w_in_t, w_fl_t, col_scale)


def _fscan_kernel(fl_ref, b_ref, c_ref):
    s = fl_ref.shape[1]
    z = fl_ref[0:FOX_HEADS, :] + b_ref[...]
    lf = jnp.minimum(z, 0.0) - jnp.log(1.0 + jnp.exp(-jnp.abs(z)))
    lane = lax.broadcasted_iota(jnp.int32, lf.shape, 1)
    sh = 1
    while sh < s:
        r = pltpu.roll(lf, sh, axis=1)
        lf = lf + jnp.where(lane >= sh, r, 0.0)
        sh *= 2
    c_ref[...] = lf * LOG2E


def _fscan(fl_t, b_forget_col, batch, seq):
    t = fl_t.shape[1]
    return pl.pallas_call(
        _fscan_kernel,
        grid=(batch,),
        in_specs=[
            pl.BlockSpec((16, seq), lambda b: (0, b)),
            pl.BlockSpec((FOX_HEADS, 1), lambda b: (0, 0)),
        ],
        out_specs=pl.BlockSpec((FOX_HEADS, seq), lambda b: (0, b)),
        out_shape=jax.ShapeDtypeStruct((FOX_HEADS, t), F32),
        compiler_params=_cparams(("parallel",)),
        name="fscan",
    )(fl_t, b_forget_col)


def _flash_chains(qi, tq, qh_scr, k_fn, vt_fn, m_scr, acc_scr, s_scr, pm_scr):
    m_scr[...] = jnp.full(m_scr.shape, NEG_BIG, F32)
    acc_scr[...] = jnp.zeros(acc_scr.shape, F32)

    def scores(j, slot, masked):
        start = pl.multiple_of(j * tq, tq)
        for c in range(N_CHAINS):
            s = lax.dot_general(k_fn(c, start), qh_scr[c], (((1,), (1,)), ((), ())),
                                preferred_element_type=F32)
            if masked:
                kv = lax.broadcasted_iota(jnp.int32, (tq, tq), 0)
                q = lax.broadcasted_iota(jnp.int32, (tq, tq), 1)
                s = jnp.where(kv <= q, s, NEG_BIG)
            s_scr[slot, c] = s
            pm_scr[slot, c] = jnp.max(s.reshape(tq // 8, 8, tq), axis=0)

    def consume(j, slot):
        start = pl.multiple_of(j * tq, tq)
        for c in range(N_CHAINS):
            m_prev = m_scr[c]
            m_new = jnp.maximum(m_prev, jnp.max(pm_scr[slot, c], axis=0, keepdims=True))
            alpha = jnp.exp2(m_prev - m_new)
            p = jnp.exp2(s_scr[slot, c] - m_new)
            m_scr[c] = m_new
            pv = jnp.dot(vt_fn(c, start), p.astype(BF16), preferred_element_type=F32)
            acc_scr[c] = acc_scr[c] * alpha + pv

    @pl.when(qi == 0)
    def _():
        scores(0, 0, True)
        consume(0, 0)

    def run(j0, count, last_masked, issue_beyond):
        for t in range(count):
            if t + 1 < count or issue_beyond:
                scores(j0 + t + 1, (t + 1) % 2, last_masked and t + 1 == count - 1)
            consume(j0 + t, t % 2)

    @pl.when(qi > 0)
    def _():
        scores(0, 0, False)
        groups = (qi - 1) // ATT_UNROLL

        def body(g, carry):
            run(g * ATT_UNROLL, ATT_UNROLL, False, True)
            return carry

        lax.fori_loop(0, groups, body, 0)
        j0 = groups * ATT_UNROLL
        for rem in range(1, ATT_UNROLL + 1):
            @pl.when(qi - j0 == rem)
            def _(rem=rem):
                run(j0, rem + 1, True, False)


def _own_and_spare(shape, c):
    lane = lax.broadcasted_iota(jnp.int32, shape, 1)
    own = (lane < HEAD_DIM) if c % 2 == 0 else (lane >= HEAD_DIM)
    spare = HEAD_DIM if c % 2 == 0 else 0
    return lane, own, spare


def _split_q(q_ref, qh_scr):
    for c in range(N_CHAINS):
        q = q_ref[c // 2].astype(F32)
        lane, own, spare = _own_and_spare(q.shape, c)
        ones = (lane >= spare) & (lane < spare + N_BIAS_LANES)
        qh_scr[c] = jnp.where(own, q, jnp.where(ones, 1.0, 0.0)).astype(BF16)


def _augment_k_pair(k, bias):
    hi = bias.astype(BF16).astype(F32)
    r1 = bias - hi
    mid = r1.astype(BF16).astype(F32)
    lo = (r1 - mid).astype(BF16).astype(F32)
    lane = lax.broadcasted_iota(jnp.int32, k.shape, 1)
    half_lane = lane & (HEAD_DIM - 1)
    pieces = jnp.where(half_lane == 0, hi, jnp.where(half_lane == 1, mid, lo))
    kf = k.astype(F32)
    out = []
    for c in range(2):
        _, own, spare = _own_and_spare(k.shape, c)
        bias_lanes = (lane >= spare) & (lane < spare + N_BIAS_LANES)
        out.append(jnp.where(own, kf, jnp.where(bias_lanes, pieces, 0.0)).astype(BF16))
    return out


def _attn_scratch(tq, seq, acc_rows, vt_slabs):
    return [
        pltpu.VMEM((N_CHAINS, tq, LANES), BF16),
        pltpu.VMEM((N_CHAINS, 1, tq), F32),
        pltpu.VMEM((N_CHAINS, acc_rows, tq), F32),
        pltpu.VMEM((2, N_CHAINS, tq, tq), F32),
        pltpu.VMEM((2, N_CHAINS, 8, tq), F32),
        pltpu.VMEM((N_CHAINS, seq, LANES), BF16),
        pltpu.VMEM((vt_slabs, acc_rows, seq), BF16),
    ]


def _fox_kernel(q_ref, k_ref, vt_ref, c_ref, wg_ref, wu_ref, wd_ref, o_ref, wgb_ref, wub_ref, wdb_ref,
                qh_scr, m_scr, acc_scr, s_scr, pm_scr, ka_scr, vat_scr):
    tq = q_ref.shape[1]
    seq = k_ref.shape[1]
    qi = pl.program_id(2)
    _split_q(q_ref, qh_scr)

    n = (pl.program_id(0) * pl.num_programs(1) + pl.program_id(1)) * pl.num_programs(2) + qi
    for k, (src, dst) in enumerate(((wg_ref, wgb_ref), (wu_ref, wub_ref), (wd_ref, wdb_ref))):
        @pl.when((n >= k * N_EXPERTS) & (n < (k + 1) * N_EXPERTS))
        def _(src=src, dst=dst):
            dst[...] = src[...].astype(BF16)

    @pl.when(qi == 0)
    def _():
        def chunk(i, carry):
            r0 = pl.multiple_of(i * SETUP_ROWS, SETUP_ROWS)
            rows = pl.ds(r0, SETUP_ROWS)
            head_row = lax.broadcasted_iota(jnp.int32, (LANES, SETUP_ROWS), 0) < HEAD_DIM
            for sl in range(NSL):
                vt = vt_ref[sl, :, rows]
                one = jnp.ones_like(vt)
                vat_scr[2 * sl, :, rows] = jnp.where(head_row, vt, one)
                vat_scr[2 * sl + 1, :, rows] = jnp.where(head_row, one, vt)
                cc = c_ref[sl, :, rows]
                stacked = jnp.where(head_row, jnp.broadcast_to(cc[1:2, :], (LANES, SETUP_ROWS)),
                                    jnp.broadcast_to(cc[0:1, :], (LANES, SETUP_ROWS)))
                bias = -stacked.T
                k = k_ref[sl, rows, :]
                ka_scr[2 * sl, rows, :], ka_scr[2 * sl + 1, rows, :] = _augment_k_pair(k, bias)
            return carry

        lax.fori_loop(0, seq // SETUP_ROWS, chunk, 0)

    def k_fn(c, start):
        return ka_scr[c, pl.ds(start, tq), :]

    def vt_fn(c, start):
        return vat_scr[c, :, pl.ds(start, tq)]

    _flash_chains(qi, tq, qh_scr, k_fn, vt_fn, m_scr, acc_scr, s_scr, pm_scr)
    for sl in range(NSL):
        a0 = acc_scr[2 * sl]
        a1 = acc_scr[2 * sl + 1]
        row = lax.broadcasted_iota(jnp.int32, a0.shape, 0)
        ot = jnp.where(row < HEAD_DIM, a0 * (1.0 / a0[HEAD_DIM:HEAD_DIM + 1, :]), a1 * (1.0 / a1[0:1, :]))
        o_ref[:, sl * LANES:(sl + 1) * LANES] = ot.T.astype(o_ref.dtype)


def _fox(qkv4, vt4, c4, wg, wu, wd, batch, seq):
    tq = ATT_T
    groups = FOX_HEADS // N_CHAINS
    nq = seq // tq
    assert batch * groups * nq >= 3 * N_EXPERTS

    def w_spec(k, rows, cols):
        def index(b, g, i):
            return (jnp.clip((b * groups + g) * nq + i - k * N_EXPERTS, 0, N_EXPERTS - 1), 0, 0)
        return pl.BlockSpec((None, rows, cols), index)

    w_specs = [w_spec(0, D_MODEL, D_EXPERT), w_spec(1, D_MODEL, D_EXPERT), w_spec(2, D_EXPERT, D_MODEL)]
    return pl.pallas_call(
        _fox_kernel,
        grid=(batch, groups, nq),
        in_specs=[
            pl.BlockSpec((NSL, None, tq, LANES), lambda b, g, i: (SLAB_FOX_Q // NSL + g, b, i, 0)),
            pl.BlockSpec((NSL, None, seq, LANES), lambda b, g, i: (SLAB_FOX_K // NSL + g, b, 0, 0)),
            pl.BlockSpec((NSL, None, LANES, seq), lambda b, g, i: (VT_FOX // NSL + g, b, 0, 0)),
            pl.BlockSpec((NSL, 2, seq), lambda b, g, i: (g, 0, b)),
        ] + w_specs,
        out_specs=[pl.BlockSpec((None, tq, NSL * LANES), lambda b, g, i: (b, i, g))] + w_specs,
        out_shape=[jax.ShapeDtypeStruct((batch, seq, FOX_HEADS * HEAD_DIM), BF16)]
        + [jax.ShapeDtypeStruct(w.shape, BF16) for w in (wg, wu, wd)],
        scratch_shapes=_attn_scratch(tq, seq, LANES, N_CHAINS),
        compiler_params=_cparams(("arbitrary", "arbitrary", "arbitrary"), ATT_FLAGS),
        name="fox",
    )(qkv4, qkv4, vt4, c4, wg, wu, wd)


def _diff_kernel(slopes_ref, q_ref, k_ref, vt_ref, lam_ref, g_ref, wgi_ref, o_ref, wgo_ref, qh_scr, m_scr,
                 acc_scr, s_scr, pm_scr, ka_scr, vat_scr):
    tq = q_ref.shape[1]
    seq = k_ref.shape[1]
    qi = pl.program_id(2)
    _split_q(q_ref, qh_scr)
    g = pl.program_id(1)

    n = (pl.program_id(0) * pl.num_programs(1) + g) * pl.num_programs(2) + qi

    @pl.when(n < GATE_CAST_STEPS)
    def _():
        wgo_ref[...] = wgi_ref[...].astype(BF16)

    @pl.when(qi == 0)
    def _():
        def chunk(i, carry):
            r0 = pl.multiple_of(i * SETUP_ROWS, SETUP_ROWS)
            rows = pl.ds(r0, SETUP_ROWS)
            for hd in range(NSL):
                vt = vt_ref[hd, :, rows]
                vat_scr[hd, :, rows] = jnp.concatenate([vt, jnp.ones((DIFF_ONES_ROWS, SETUP_ROWS), BF16)], axis=0)
                pos = (lax.broadcasted_iota(jnp.int32, (SETUP_ROWS, LANES), 0) + r0).astype(F32)
                bias = pos * (slopes_ref[NSL * g + hd] * LOG2E)
                k = k_ref[hd, rows, :]
                ka_scr[2 * hd, rows, :], ka_scr[2 * hd + 1, rows, :] = _augment_k_pair(k, bias)
            return carry

        lax.fori_loop(0, seq // SETUP_ROWS, chunk, 0)

    def k_fn(c, start):
        return ka_scr[c, pl.ds(start, tq), :]

    def vt_fn(c, start):
        return vat_scr[c // 2, :, pl.ds(start, tq)]

    _flash_chains(qi, tq, qh_scr, k_fn, vt_fn, m_scr, acc_scr, s_scr, pm_scr)
    lp = lam_ref[...]
    s1 = jnp.sum(lp[0:1, :] * lp[1:2, :], axis=1, keepdims=True)
    s2 = jnp.sum(lp[2:3, :] * lp[3:4, :], axis=1, keepdims=True)
    lam = jnp.exp(s1) - jnp.exp(s2) + LAM_INIT
    for hd in range(NSL):
        a1 = acc_scr[2 * hd]
        a2 = acc_scr[2 * hd + 1]
        ot = (a1[:LANES, :] * (1.0 / a1[LANES:LANES + 1, :])
              - lam * (a2[:LANES, :] * (1.0 / a2[LANES:LANES + 1, :])))
        ms = jnp.mean(ot * ot, axis=0, keepdims=True)
        yt = ot * lax.rsqrt(ms + LN_EPS) * g_ref[...]
        o_ref[:, hd * LANES:(hd + 1) * LANES] = (yt * (1.0 - LAM_INIT)).T.astype(o_ref.dtype)


def _diff(qkv4, vt4, slopes, lam_params, subln_g_col, w_in_t, batch, seq):
    tq = ATT_T
    groups = DIFF_HEADS // NSL
    nq = seq // tq
    assert batch * groups * nq >= GATE_CAST_STEPS

    def cast_step(b, g, i):
        return jnp.minimum((b * groups + g) * nq + i, GATE_CAST_STEPS - 1)

    grid_spec = pltpu.PrefetchScalarGridSpec(
        num_scalar_prefetch=1,
        grid=(batch, groups, nq),
        in_specs=[
            pl.BlockSpec((NSL, None, tq, LANES), lambda b, g, i, s: (SLAB_DIFF_Q // NSL + g, b, i, 0)),
            pl.BlockSpec((NSL, None, seq, LANES), lambda b, g, i, s: (SLAB_DIFF_K // NSL + g, b, 0, 0)),
            pl.BlockSpec((NSL, None, LANES, seq), lambda b, g, i, s: (VT_DIFF // NSL + g, b, 0, 0)),
            pl.BlockSpec((4, HEAD_DIM), lambda b, g, i, s: (0, 0)),
            pl.BlockSpec((LANES, 1), lambda b, g, i, s: (0, 0)),
            pl.BlockSpec((None, PROJ_TN, D_MODEL),
                         lambda b, g, i, s: (0, GATE_COL0 // PROJ_TN + cast_step(b, g, i), 0)),
        ],
        out_specs=[
            pl.BlockSpec((None, tq, NSL * LANES), lambda b, g, i, s: (b, i, g)),
            pl.BlockSpec((PROJ_TN, D_MODEL), lambda b, g, i, s: (cast_step(b, g, i), 0)),
        ],
        scratch_shapes=_attn_scratch(tq, seq, LANES + DIFF_ONES_ROWS, NSL),
    )
    return pl.pallas_call(
        _diff_kernel,
        grid_spec=grid_spec,
        out_shape=[
            jax.ShapeDtypeStruct((batch, seq, DIFF_HEADS * LANES), BF16),
            jax.ShapeDtypeStruct((N_BRANCH * D_MODEL, D_MODEL), BF16),
        ],
        compiler_params=_cparams(("arbitrary", "arbitrary", "arbitrary"), ATT_FLAGS),
        name="diff",
    )(slopes, qkv4, qkv4, vt4, lam_params, subln_g_col, w_in_t)


def _mem_kernel(q_ref, mem_ref, wk_ref, wv_ref, o_ref):
    memb = mem_ref[...].astype(BF16)
    mk = jnp.dot(memb, wk_ref[...].astype(BF16), preferred_element_type=F32).astype(BF16)
    mv = jnp.dot(memb, wv_ref[...].astype(BF16), preferred_element_type=F32).astype(BF16)
    s = lax.dot_general(q_ref[...], mk, (((1,), (1,)), ((), ())), preferred_element_type=F32)
    m = jnp.max(s, axis=1, keepdims=True)
    p = jnp.exp2(s - m)
    l = jnp.sum(p, axis=1, keepdims=True)
    o = jnp.dot(p.astype(BF16), mv, preferred_element_type=F32)
    o_ref[...] = (o / l).astype(o_ref.dtype)


def _mem_attn(qkv4, mem, w_mem_kv, batch, seq):
    tq = MEM_TQ
    return pl.pallas_call(
        _mem_kernel,
        grid=(batch, MEM_HEADS, seq // tq),
        in_specs=[
            pl.BlockSpec((None, None, tq, LANES), lambda b, h, i: (SLAB_MEM_Q + h, b, i, 0)),
            pl.BlockSpec((None, N_MEM, D_MODEL), lambda b, h, i: (b, 0, 0)),
            pl.BlockSpec((None, D_MODEL, LANES), lambda b, h, i: (0, 0, h)),
            pl.BlockSpec((None, D_MODEL, LANES), lambda b, h, i: (0, 0, MEM_HEADS + h)),
        ],
        out_specs=pl.BlockSpec((None, tq, LANES), lambda b, h, i: (b, i, h)),
        out_shape=jax.ShapeDtypeStruct((batch, seq, MEM_HEADS * LANES), BF16),
        compiler_params=_cparams(("parallel", "parallel", "parallel")),
        name="mem",
    )(qkv4, mem, w_mem_kv, w_mem_kv)


def _layer_norm(z, g, b):
    mu = jnp.mean(z, axis=1, keepdims=True)
    zc = z - mu
    var = jnp.mean(zc * zc, axis=1, keepdims=True)
    return zc * lax.rsqrt(var + LN_EPS) * g + b


def _lane_max(v):
    return jnp.max(v, axis=1, keepdims=True)


def _routing_info(logits):
    lane = lax.broadcasted_iota(jnp.int32, logits.shape, 1)
    big = jnp.int32(2 * LANES)
    is_g = lane < N_GROUPS
    gl = jnp.where(is_g, logits, NEG_BIG)
    gmax = _lane_max(gl)
    g_w = 1.0 / jnp.sum(jnp.exp(gl - gmax), axis=1, keepdims=True)
    g_idx = jnp.min(jnp.where(gl == gmax, lane, big), axis=1, keepdims=True)
    lo = ROUTE_LANE0 + g_idx * EXPERTS_PER_GROUP
    in_grp = (lane >= lo) & (lane < lo + EXPERTS_PER_GROUP)
    el = jnp.where(in_grp, logits, NEG_BIG)
    e1 = _lane_max(el)
    i1 = jnp.min(jnp.where(el == e1, lane, big), axis=1, keepdims=True)
    el2 = jnp.where(lane == i1, NEG_BIG, el)
    e2 = _lane_max(el2)
    i2 = jnp.min(jnp.where(el2 == e2, lane, big), axis=1, keepdims=True)
    r = jnp.exp(e2 - e1)
    w1 = g_w / (1.0 + r)
    w2 = g_w * r / (1.0 + r)
    id1 = (i1 - ROUTE_LANE0).astype(F32)
    id2 = (i2 - ROUTE_LANE0).astype(F32)
    return jnp.where(lane == RI_E1, id1,
                     jnp.where(lane == RI_E2, id2,
                               jnp.where(lane == RI_W1, w1, jnp.where(lane == RI_W2, w2, 0.0))))


def _merge_kernel(x_ref, yf_ref, yd_ref, ym_ref, wg_ref, bg_ref, wbf_ref, wbd_ref, wbm_ref,
                  wo_ref, g1_ref, b1_ref, wr_ref, br_ref, x1_ref, ri_ref, rit_ref):
    x = x_ref[...]
    xb = x.astype(BF16)
    h = None
    for i, (y_ref, wb_ref) in enumerate(((yf_ref, wbf_ref), (yd_ref, wbd_ref), (ym_ref, wbm_ref))):
        cols = slice(i * D_MODEL, (i + 1) * D_MODEL)
        gl = lax.dot_general(xb, wg_ref[cols, :], (((1,), (1,)), ((), ())),
                             preferred_element_type=F32) + bg_ref[:, cols]
        gate = 1.0 / (1.0 + jnp.exp(-gl))
        br = jnp.dot(y_ref[...], wb_ref[...], preferred_element_type=F32)
        h = gate * br if h is None else h + gate * br
    o = jnp.dot(h.astype(BF16), wo_ref[...], preferred_element_type=F32)
    x1 = _layer_norm(DEEPNORM_ALPHA * x + o, g1_ref[...], b1_ref[...])
    x1_ref[...] = x1
    logits = jnp.dot(x1.astype(BF16), wr_ref[...], preferred_element_type=F32) + br_ref[...]
    ri = _routing_info(logits)
    ri_ref[...] = ri
    rit_ref[...] = ri.T[0:8, :]


def _const_spec(shape):
    return pl.BlockSpec(shape, lambda i: (0,) * len(shape), pipeline_mode=pl.Buffered(1))


def _merge(x2, yf, yd, ym, wg, bg, wbf, wbd, wbm, wo, g1, b1, wr, br):
    t = x2.shape[0]
    tm = MERGE_TM
    half = yf.shape[1]
    row = lambda w: pl.BlockSpec((tm, w), lambda i: (i, 0))
    return pl.pallas_call(
        _merge_kernel,
        grid=(t // tm,),
        in_specs=[
            row(D_MODEL), row(half), row(half), row(half),
            _const_spec((N_BRANCH * D_MODEL, D_MODEL)), _const_spec((1, N_BRANCH * D_MODEL)),
            _const_spec((half, D_MODEL)), _const_spec((half, D_MODEL)), _const_spec((half, D_MODEL)),
            _const_spec((D_MODEL, D_MODEL)), _const_spec((1, D_MODEL)), _const_spec((1, D_MODEL)),
            _const_spec((D_MODEL, LANES)), _const_spec((1, LANES)),
        ],
        out_specs=[row(D_MODEL), row(LANES), pl.BlockSpec((8, tm), lambda i: (0, i))],
        out_shape=[
            jax.ShapeDtypeStruct((t, D_MODEL), F32),
            jax.ShapeDtypeStruct((t, LANES), F32),
            jax.ShapeDtypeStruct((8, t), F32),
        ],
        compiler_params=_cparams(("parallel",)),
        name="merge",
    )(x2, yf, yd, ym, wg, bg, wbf, wbd, wbm, wo, g1, b1, wr, br)


def _cumsum(x, axis):
    n = x.shape[axis]
    idx = lax.broadcasted_iota(jnp.int32, x.shape, axis)
    sh = 1
    while sh < n:
        x = x + jnp.where(idx >= sh, pltpu.roll(x, sh, axis=axis), 0.0)
        sh *= 2
    return x


def _route_kernel(rit_ref, pos_ref, tiles_ref):
    s = rit_ref.shape[1]
    e1 = rit_ref[RI_E1:RI_E1 + 1, :]
    e2 = rit_ref[RI_E2:RI_E2 + 1, :]
    sub = lax.broadcasted_iota(jnp.int32, (N_EXPERTS, s), 0).astype(F32)
    oh1 = jnp.where(sub == e1, 1.0, 0.0)
    oh2 = jnp.where(sub == e2, 1.0, 0.0)
    cnt = oh1 + oh2
    incl = _cumsum(cnt, 1)
    before = incl - cnt
    total = jnp.broadcast_to(incl[:, s - 1:s], (N_EXPERTS, LANES))
    aligned = jnp.floor((total + (SEG_ALIGN - 1)) * (1.0 / SEG_ALIGN)) * SEG_ALIGN
    off = _cumsum(aligned, 0) - aligned
    slot = before + off[:, 0:1]
    pos1 = jnp.sum(oh1 * slot, axis=0, keepdims=True)
    pos2 = jnp.sum(oh2 * slot, axis=0, keepdims=True)
    pos_ref[...] = (jnp.concatenate([pos1, pos2], axis=0) * ROW_TILE).astype(jnp.int32)

    nt = jnp.floor((total + (MOE_R - 1)) * (1.0 / MOE_R))
    ct_incl = _cumsum(nt, 0)
    ct_excl = ct_incl - nt
    sub_l = lax.broadcasted_iota(jnp.int32, (N_EXPERTS, LANES), 0).astype(F32)
    tile_i = lax.broadcasted_iota(jnp.int32, (N_EXPERTS, LANES), 1).astype(F32)
    te = jnp.sum(jnp.where(ct_incl <= tile_i, 1.0, 0.0), axis=0, keepdims=True)
    te = jnp.minimum(te, N_EXPERTS - 1.0)
    sel = jnp.where(sub_l == te, 1.0, 0.0)
    k_in = tile_i - ct_excl
    start = jnp.sum(sel * (off + k_in * MOE_R), axis=0, keepdims=True)
    nval = jnp.sum(sel * jnp.clip(total - k_in * MOE_R, 0.0, float(MOE_R)), axis=0, keepdims=True)
    active = tile_i[0:1, :] < ct_incl[N_EXPERTS - 1:N_EXPERTS, :]
    start = jnp.where(active, start, 0.0)
    nval = jnp.where(active, nval, 0.0)
    rows = jnp.concatenate([te, start * ROW_TILE, nval, jnp.zeros((5, LANES), F32)], axis=0)
    tiles_ref[...] = rows.astype(jnp.int32)


def _route(rit, batch, seq):
    return pl.pallas_call(
        _route_kernel,
        grid=(batch,),
        in_specs=[pl.BlockSpec((8, seq), lambda b: (0, b))],
        out_specs=[
            pl.BlockSpec((None, 2, seq), lambda b: (b, 0, 0)),
            pl.BlockSpec((None, 8, LANES), lambda b: (b, 0, 0)),
        ],
        out_shape=[
            jax.ShapeDtypeStruct((batch, 2, seq), jnp.int32),
            jax.ShapeDtypeStruct((batch, 8, LANES), jnp.int32),
        ],
        compiler_params=_cparams(("parallel",)),
        name="route",
    )(rit)


def _read_rows(view, n_rows):
    return jnp.concatenate([view[pl.ds(k, n_rows, stride=ROW_TILE), :] for k in range(ROW_TILE)], axis=1)


def _write_rows(view, n_rows, val):
    for k in range(ROW_TILE):
        view[pl.ds(k, n_rows, stride=ROW_TILE), :] = val[:, k * LANES:(k + 1) * LANES]


def _moe_kernel(nd, nt, texp_ref, tstart_ref, tnval_ref, x1_ref, p1_ref, p2_ref, ri_ref,
                wg_ref, wu_ref, wd_ref, g2_ref, b2_ref, o_ref, buf, xs_scr, as_scr, bs_scr):
    c = pl.program_id(0)
    j = pl.program_id(1)
    td = x1_ref.shape[0]

    def tile_at(ref, off):
        return ref.at[pl.ds(pl.multiple_of(off, ROW_TILE), ROW_TILE), :]

    @pl.when((j == 0) & (c == 0))
    def _():
        buf[...] = jnp.zeros(buf.shape, F32)

    @pl.when(j < nd)
    def _():
        _write_rows(xs_scr, td, x1_ref[...])

        def body(i, carry):
            for u in range(SEG_ALIGN):
                t = i * SEG_ALIGN + u
                row = tile_at(xs_scr, t * ROW_TILE)[...]
                tile_at(buf, p1_ref[0, t])[...] = row
                tile_at(buf, p2_ref[0, t])[...] = row
            return carry

        lax.fori_loop(0, td // SEG_ALIGN, body, 0)

    @pl.when((j >= nd) & (j < nd + nt))
    def _():
        idx = c * nt + (j - nd)
        nval = tnval_ref[idx]

        @pl.when(nval > 0)
        def _():
            start = pl.multiple_of(tstart_ref[idx], ROW_TILE * SEG_ALIGN)
            view = buf.at[pl.ds(start, MOE_R * ROW_TILE), :]
            xt = _read_rows(view, MOE_R)
            xb = xt.astype(BF16)
            g = jnp.dot(xb, wg_ref[...], preferred_element_type=F32)
            u = jnp.dot(xb, wu_ref[...], preferred_element_type=F32)
            hcat = (g / (1.0 + jnp.exp(-g))) * u
            y = jnp.dot(hcat.astype(BF16), wd_ref[...], preferred_element_type=F32)
            rows = lax.broadcasted_iota(jnp.int32, (MOE_R, 1), 0)
            _write_rows(view, MOE_R, jnp.where(rows < nval, y, xt))

    @pl.when(j >= nd + nt)
    def _():
        def body(i, carry):
            for u in range(SEG_ALIGN):
                t = i * SEG_ALIGN + u
                tile_at(as_scr, t * ROW_TILE)[...] = tile_at(buf, p1_ref[0, t])[...]
                tile_at(bs_scr, t * ROW_TILE)[...] = tile_at(buf, p2_ref[0, t])[...]
            return carry

        lax.fori_loop(0, td // SEG_ALIGN, body, 0)
        ri = ri_ref[...]
        m = ri[:, RI_W1:RI_W1 + 1] * _read_rows(as_scr, td) + ri[:, RI_W2:RI_W2 + 1] * _read_rows(bs_scr, td)
        z = DEEPNORM_ALPHA * x1_ref[...] + m
        o_ref[...] = _layer_norm(z, g2_ref[...], b2_ref[...])


def _moe(x1, pos3, ri, texp, tstart, tnval, wg, wu, wd, g2, b2, batch, seq):
    t = x1.shape[0]
    td = MOE_TD
    nd = seq // td
    nt = 2 * seq // MOE_R + N_EXPERTS
    buf_rows = 2 * seq + N_EXPERTS * SEG_ALIGN + MOE_R

    def tok_tile(j):
        return jnp.where(j < nd, j, jnp.maximum(j - nd - nt, 0))

    def tile_expert(c, j, texp_ref):
        return texp_ref[c * nt + jnp.clip(j - nd, 0, nt - 1)]

    tok_spec = lambda w: pl.BlockSpec((td, w), lambda c, j, *_: (c * nd + tok_tile(j), 0))
    pos_spec = lambda k: pl.BlockSpec((None, 1, td), lambda c, j, *_: ((c * 2 + k) * nd + tok_tile(j), 0, 0),
                                      memory_space=pltpu.SMEM)
    w_spec = lambda a, b_: pl.BlockSpec((None, a, b_), lambda c, j, te, *_: (tile_expert(c, j, te), 0, 0))
    vec_spec = pl.BlockSpec((1, D_MODEL), lambda c, j, *_: (0, 0))
    grid_spec = pltpu.PrefetchScalarGridSpec(
        num_scalar_prefetch=3,
        grid=(batch, nd + nt + nd),
        in_specs=[
            tok_spec(D_MODEL), pos_spec(0), pos_spec(1), tok_spec(LANES),
            w_spec(D_MODEL, D_EXPERT), w_spec(D_MODEL, D_EXPERT), w_spec(D_EXPERT, D_MODEL),
            vec_spec, vec_spec,
        ],
        out_specs=pl.BlockSpec((td, D_MODEL),
                               lambda c, j, *_: (c * nd + jnp.maximum(j - nd - nt, 0), 0)),
        scratch_shapes=[
            pltpu.VMEM((buf_rows * ROW_TILE, LANES), F32),
            pltpu.VMEM((td * ROW_TILE, LANES), F32),
            pltpu.VMEM((td * ROW_TILE, LANES), F32),
            pltpu.VMEM((td * ROW_TILE, LANES), F32),
        ],
    )
    return pl.pallas_call(
        functools.partial(_moe_kernel, nd, nt),
        grid_spec=grid_spec,
        out_shape=jax.ShapeDtypeStruct((t, D_MODEL), F32),
        compiler_params=_cparams(("arbitrary", "arbitrary")),
        name="moe",
    )(texp, tstart, tnval, x1, pos3, pos3, ri, wg, wu, wd, g2, b2)


def _qkv_col_scale():
    ones = lambda n: jnp.ones((n,), F32)
    att = FOX_HEADS * HEAD_DIM
    qs = lambda d: jnp.full((att,), LOG2E * d ** -0.5, F32)
    return jnp.concatenate([qs(HEAD_DIM), ones(2 * att), qs(HEAD_DIM), ones(2 * att),
                            qs(MEM_HEAD_DIM)]).reshape(1, QKV_COLS)


def kernel(x, mem, w_in, b_forget, b_gates, lambda_q1, lambda_k1, lambda_q2, lambda_k2, diff_subln_g,
           w_mem_kv, w_branch_fox, w_branch_diff, w_branch_mem, w_out, ln1_g, ln1_b, w_router_group,
           b_router_group, w_router_expert, b_router_expert, w_expert_gate, w_expert_up, w_expert_down,
           ln2_g, ln2_b):
    batch, seq, d = x.shape
    t = batch * seq
    l = 0
    x2 = x.reshape(t, d)

    w_in_t = jnp.swapaxes(w_in, 1, 2)
    w_fl_t = jnp.pad(w_in_t[l, FL_COL0:, :], ((0, 16 - FOX_HEADS), (0, 0))).astype(BF16)
    w_r = jnp.concatenate([w_router_group[l], w_router_expert[l]], axis=1)
    w_r = jnp.pad(w_r, ((0, 0), (0, LANES - w_r.shape[1]))).astype(BF16)
    b_r = jnp.concatenate([b_router_group[l], b_router_expert[l]])
    b_r = jnp.pad(b_r, (0, LANES - b_r.shape[0])).reshape(1, LANES)
    lam_params = jnp.stack([lambda_q1[l], lambda_k1[l], lambda_q2[l], lambda_k2[l]])
    slopes = 2.0 ** (-8.0 * jnp.arange(1, DIFF_HEADS + 1, dtype=F32) / DIFF_HEADS)

    qkv, vt4, fl_t = _proj(x2, w_in_t, w_fl_t, _qkv_col_scale(), batch, seq)
    qkv4 = qkv.reshape(N_SLABS, batch, seq, LANES)
    c = _fscan(fl_t, b_forget[l].reshape(FOX_HEADS, 1), batch, seq)
    c4 = c.reshape(FOX_HEADS // 2, 2, t)

    y_fox, wg_b, wu_b, wd_b = _fox(qkv4, vt4, c4, w_expert_gate[l], w_expert_up[l], w_expert_down[l],
                                   batch, seq)
    y_diff, w_gates = _diff(qkv4, vt4, slopes, lam_params, diff_subln_g[l].reshape(LANES, 1), w_in_t,
                            batch, seq)
    y_mem = _mem_attn(qkv4, mem, w_mem_kv, batch, seq)

    x1, ri, rit = _merge(
        x2, y_fox.reshape(t, -1), y_diff.reshape(t, -1), y_mem.reshape(t, -1),
        w_gates, b_gates[l].reshape(1, -1),
        w_branch_fox[l].astype(BF16), w_branch_diff[l].astype(BF16), w_branch_mem[l].astype(BF16),
        w_out[l].astype(BF16), ln1_g[l].reshape(1, d), ln1_b[l].reshape(1, d), w_r, b_r)

    pos, tiles = _route(rit, batch, seq)
    nt = 2 * seq // MOE_R + N_EXPERTS
    texp, tstart, tnval = (tiles[:, r, :nt].reshape(-1) for r in range(3))
    pos3 = pos.reshape(batch * 2 * (seq // MOE_TD), 1, MOE_TD)
    out = _moe(x1, pos3, ri, texp, tstart, tnval,
               wg_b, wu_b, wd_b,
               ln2_g[l].reshape(1, d), ln2_b[l].reshape(1, d), batch, seq)
    return out.reshape(batch, seq, d)
```
